```python
import math
import jax
import jax.numpy as jnp
from jax import lax
import numpy as np

D_MODEL = 1024
BATCH = 16
SEQ = 2048
DEPTH = 2

N_MIXERS = 2
N_NSA_LAYERS = (DEPTH + 1) // 2
N_HGRN_LAYERS = DEPTH // 2

NSA_HEADS = 16
NSA_KV_GROUPS = 4
NSA_GROUP_SIZE = NSA_HEADS // NSA_KV_GROUPS
NSA_HEAD_DIM = D_MODEL // NSA_HEADS
NSA_Q_DIM = NSA_HEADS * NSA_HEAD_DIM
NSA_KV_DIM = NSA_KV_GROUPS * NSA_HEAD_DIM
NSA_IN_DIM = NSA_Q_DIM + 6 * NSA_KV_DIM + 3 * NSA_HEADS
CMP_BLOCK = 32
CMP_STRIDE = 16
SEL_BLOCK = 64
N_SELECT = 8
WINDOW = 512
Q_BLOCK = 32
FORCED_SCORE = 1.0e4

REL_BUCKETS = 32
REL_MAX_DISTANCE = 1024

HGRN_EXPAND = 128
HGRN_HEADS = D_MODEL // HGRN_EXPAND
HGRN_KEY_DIM = HGRN_EXPAND
HGRN_VALUE_DIM = D_MODEL // HGRN_HEADS
HGRN_F_DIM = HGRN_HEADS * HGRN_KEY_DIM
HGRN_IN_DIM = 2 * HGRN_F_DIM + 2 * D_MODEL
HGRN_CHUNK = 32

MOE_GROUPS = 4
MOE_EXPERTS_PER_GROUP = 4
MOE_EXPERTS = MOE_GROUPS * MOE_EXPERTS_PER_GROUP
MOE_TOP_K = 2
MOE_FF = 512
MOE_ROW_BLOCK = 128

NORM_EPS = 1e-6

kernel_name = 'hybrid_nsa_hgrn2_hier_moe_adaln'


def rms_norm(x, gain):
    xf = x.astype(jnp.float32)
    y = xf * lax.rsqrt(jnp.mean(xf * xf, axis=-1, keepdims=True) + NORM_EPS)
    return (y * gain.astype(jnp.float32)).astype(x.dtype)


def masked_softmax(logits, mask):
    logits = jnp.where(mask, logits, -jnp.inf)
    m = jnp.max(logits, axis=-1, keepdims=True)
    m = jnp.where(jnp.isfinite(m), m, 0.0)
    e = jnp.exp(logits - m)
    return e / jnp.maximum(jnp.sum(e, axis=-1, keepdims=True), 1e-30)


def t5_bucket(dist):
    n = jnp.maximum(dist, 0)
    max_exact = REL_BUCKETS // 2
    nf = jnp.maximum(n, 1).astype(jnp.float32)
    large = max_exact + (jnp.log(nf / max_exact) / math.log(REL_MAX_DISTANCE / max_exact)
                         * (REL_BUCKETS - max_exact)).astype(jnp.int32)
    large = jnp.minimum(large, REL_BUCKETS - 1)
    return jnp.where(n < max_exact, n, large)


def compress_blocks(kv, pe, w1, w2):
    b, t, g, d = kv.shape
    n_cmp = (t - CMP_BLOCK) // CMP_STRIDE + 1
    idx = np.arange(n_cmp)[:, None] * CMP_STRIDE + np.arange(CMP_BLOCK)[None, :]
    blocks = kv[:, idx] + pe[None, None, :, None, :]
    flat = blocks.transpose(0, 1, 3, 2, 4).reshape(b, n_cmp, g, CMP_BLOCK * d)
    return jax.nn.silu(flat @ w1) @ w2


def cmp_to_sel_weights(n_cmp, n_sel):
    cs = np.arange(n_cmp) * CMP_STRIDE
    ss = np.arange(n_sel) * SEL_BLOCK
    shared = (np.minimum(cs[:, None] + CMP_BLOCK, ss[None, :] + SEL_BLOCK)
              - np.maximum(cs[:, None], ss[None, :]))
    return jnp.asarray(np.clip(shared, 0, None) / CMP_BLOCK, dtype=jnp.float32)


def nsa_mixer(h, rel_bias, w_in, q_gain, k_gain, cmp_pe, cmp_w1, cmp_w2, w_out):
    b, t, _ = h.shape
    G, R, Dh = NSA_KV_GROUPS, NSA_GROUP_SIZE, NSA_HEAD_DIM
    cuts = [NSA_Q_DIM + i * NSA_KV_DIM for i in range(7)]
    q, kc, vc, ks, vs, kw, vw, gl = jnp.split(h @ w_in, cuts, axis=-1)
    kvs = (b, t, G, Dh)
    q = rms_norm(q.reshape(b, t, G, R, Dh), q_gain)
    kc = rms_norm(compress_blocks(kc.reshape(kvs), cmp_pe[0], cmp_w1[0], cmp_w2[0]), k_gain[0])
    vc = compress_blocks(vc.reshape(kvs), cmp_pe[1], cmp_w1[1], cmp_w2[1])
    ks = rms_norm(ks.reshape(kvs), k_gain[1])
    vs = vs.reshape(kvs)
    kw = rms_norm(kw.reshape(kvs), k_gain[2])
    vw = vw.reshape(kvs)
    gates = jax.nn.sigmoid(gl.astype(jnp.float32)).reshape(b, t, G, R, 3)

    n_cmp = kc.shape[1]
    n_sel_blocks = t // SEL_BLOCK
    n_select = min(N_SELECT, n_sel_blocks)
    n_keys_sel = n_select * SEL_BLOCK
    cmp_end = jnp.arange(n_cmp) * CMP_STRIDE + (CMP_BLOCK - 1)
    cmp_to_sel = cmp_to_sel_weights(n_cmp, n_sel_blocks)
    ks_blk = ks.transpose(0, 2, 1, 3).reshape(b, G, n_sel_blocks, SEL_BLOCK, Dh)
    vs_blk = vs.transpose(0, 2, 1, 3).reshape(b, G, n_sel_blocks, SEL_BLOCK, Dh)
    kw_pad = jnp.pad(kw, ((0, 0), (WINDOW, 0), (0, 0), (0, 0)))
    vw_pad = jnp.pad(vw, ((0, 0), (WINDOW, 0), (0, 0), (0, 0)))
    bias_tbl = rel_bias.astype(jnp.float32).T.reshape(G, R, REL_BUCKETS)
    scale = NSA_HEAD_DIM ** -0.5
    blk_ids = jnp.arange(n_sel_blocks)
    b_ix = jnp.arange(b)[:, None, None, None]
    g_ix = jnp.arange(G)[None, :, None, None]
    g_ix5 = jnp.arange(G)[None, :, None, None, None]
    r_ix5 = jnp.arange(R)[None, None, :, None, None]

    def query_block(qb):
        t0 = qb * Q_BLOCK
        tq = t0 + jnp.arange(Q_BLOCK)
        q_b = lax.dynamic_slice_in_dim(q, t0, Q_BLOCK, axis=1)
        gate_b = lax.dynamic_slice_in_dim(gates, t0, Q_BLOCK, axis=1)

        dist_c = tq[:, None] - cmp_end[None, :]
        lg = (jnp.einsum('bqgrd,bngd->bgrqn', q_b, kc).astype(jnp.float32) * scale
              + bias_tbl[:, :, t5_bucket(dist_c)])
        p_c = masked_softmax(lg, dist_c >= 0)
        o_c = jnp.einsum('bgrqn,bngd->bqgrd', p_c.astype(vc.dtype), vc)

        imp = jnp.einsum('bgrqn,ns->bgqs', p_c, cmp_to_sel)
        cur = tq // SEL_BLOCK
        forced = ((blk_ids[None, :] == 0) | (blk_ids[None, :] == cur[:, None])
                  | (blk_ids[None, :] == cur[:, None] - 1))
        causal_blk = blk_ids[None, :] <= cur[:, None]
        score = jnp.where(forced, FORCED_SCORE, jnp.where(causal_blk, imp, -1.0))
        _, sel = lax.top_k(score, n_select)
        k_sel = ks_blk[b_ix, g_ix, sel].reshape(b, G, Q_BLOCK, n_keys_sel, Dh)
        v_sel = vs_blk[b_ix, g_ix, sel].reshape(b, G, Q_BLOCK, n_keys_sel, Dh)
        pos = (sel[..., None] * SEL_BLOCK + jnp.arange(SEL_BLOCK)).reshape(b, G, Q_BLOCK, n_keys_sel)
        dist_s = tq[None, None, :, None] - pos
        lg = (jnp.einsum('bqgrd,bgqkd->bgrqk', q_b, k_sel).astype(jnp.float32) * scale
              + bias_tbl[g_ix5, r_ix5, t5_bucket(dist_s)[:, :, None]])
        p_s = masked_softmax(lg, (dist_s >= 0)[:, :, None])
        o_s = jnp.einsum('bgrqk,bgqkd->bqgrd', p_s.astype(v_sel.dtype), v_sel)

        k_w = lax.dynamic_slice_in_dim(kw_pad, t0, Q_BLOCK + WINDOW, axis=1)
        v_w = lax.dynamic_slice_in_dim(vw_pad, t0, Q_BLOCK + WINDOW, axis=1)
        kpos = t0 - WINDOW + jnp.arange(Q_BLOCK + WINDOW)
        dist_w = tq[:, None] - kpos[None, :]
        mask_w = (dist_w >= 0) & (dist_w < WINDOW) & (kpos[None, :] >= 0)
        lg = (jnp.einsum('bqgrd,bkgd->bgrqk', q_b, k_w).astype(jnp.float32) * scale
              + bias_tbl[:, :, t5_bucket(dist_w)])
        p_w = masked_softmax(lg, mask_w)
        o_w = jnp.einsum('bgrqk,bkgd->bqgrd', p_w.astype(v_w.dtype), v_w)

        o = gate_b[..., 0:1] * o_c + gate_b[..., 1:2] * o_s + gate_b[..., 2:3] * o_w
        return o.astype(h.dtype).reshape(b, Q_BLOCK, NSA_Q_DIM)

    o = lax.map(query_block, jnp.arange(t // Q_BLOCK))
    o = o.transpose(1, 0, 2, 3).reshape(b, t, NSA_Q_DIM)
    return o @ w_out


def hgrn2_mixer(h, lower_bound, w_in, out_gain, w_out):
    b, t, _ = h.shape
    NH, dk, dv, C = HGRN_HEADS, HGRN_KEY_DIM, HGRN_VALUE_DIM, HGRN_CHUNK
    q, f, i, g = jnp.split(h @ w_in, [HGRN_F_DIM, 2 * HGRN_F_DIM, 2 * HGRN_F_DIM + D_MODEL], axis=-1)
    q = jax.nn.silu(q.astype(jnp.float32)).reshape(b, t, NH, dk)
    lb = lower_bound.astype(jnp.float32).reshape(NH, dk)
    log_f = jnp.logaddexp(jnp.log(lb), jnp.log1p(-lb)
                          + jax.nn.log_sigmoid(f.astype(jnp.float32).reshape(b, t, NH, dk)))
    k = -jnp.expm1(log_f)
    v = i.astype(jnp.float32).reshape(b, t, NH, dv)
    n_chunks = t // C

    def to_chunks(a):
        return a.reshape(b, n_chunks, C, NH, a.shape[-1]).transpose(1, 0, 3, 2, 4)

    causal = jnp.tril(jnp.ones((C, C), dtype=bool))

    def chunk_step(S, xs):
        q_c, k_c, v_c, lf_c = xs
        cum = jnp.cumsum(lf_c, axis=2)
        rel = jnp.where(causal[:, :, None], cum[:, :, :, None, :] - cum[:, :, None, :, :], -jnp.inf)
        attn = jnp.einsum('bhtd,bhsd,bhtsd->bhts', q_c, k_c, jnp.exp(rel))
        o = attn @ v_c + jnp.einsum('bhtd,bhde->bhte', q_c * jnp.exp(cum), S)
        last = cum[:, :, -1:, :]
        S = (jnp.exp(last[:, :, 0, :, None]) * S
             + jnp.einsum('bhsd,bhse->bhde', k_c * jnp.exp(last - cum), v_c))
        return S, o

    S0 = jnp.zeros((b, NH, dk, dv), jnp.float32)
    _, o = lax.scan(chunk_step, S0, (to_chunks(q), to_chunks(k), to_chunks(v), to_chunks(log_f)))
    o = o.transpose(1, 0, 3, 2, 4).reshape(b, t, NH, dv)
    o = rms_norm(o, out_gain) * jax.nn.silu(g.astype(jnp.float32).reshape(b, t, NH, dv))
    return o.reshape(b, t, D_MODEL).astype(h.dtype) @ w_out


def grouped_expert_ffn(tok, expert, weight, w1, w3, w2):
    n, d = tok.shape
    a = n * MOE_TOP_K
    flat_e = expert.reshape(a)
    flat_t = jnp.repeat(jnp.arange(n, dtype=jnp.int32), MOE_TOP_K)
    flat_w = weight.reshape(a)
    order = jnp.argsort(flat_e)
    e_sorted, t_sorted, w_sorted = flat_e[order], flat_t[order], flat_w[order]
    counts = jnp.bincount(flat_e, length=MOE_EXPERTS)
    padded = (counts + MOE_ROW_BLOCK - 1) // MOE_ROW_BLOCK * MOE_ROW_BLOCK
    pad_end = jnp.cumsum(padded)
    pad_start = pad_end - padded
    start = jnp.cumsum(counts) - counts
    dest = pad_start[e_sorted] + (jnp.arange(a) - start[e_sorted])
    n_blk = -(-(a + MOE_EXPERTS * (MOE_ROW_BLOCK - 1)) // MOE_ROW_BLOCK)
    p = n_blk * MOE_ROW_BLOCK
    buf_t = jnp.full((p,), n, dtype=jnp.int32).at[dest].set(t_sorted)
    buf_w = jnp.zeros((p,), tok.dtype).at[dest].set(w_sorted)
    blk_e = jnp.minimum(jnp.searchsorted(pad_end, jnp.arange(n_blk) * MOE_ROW_BLOCK, side='right'),
                        MOE_EXPERTS - 1)
    tok_pad = jnp.concatenate([tok, jnp.zeros((1, d), tok.dtype)], axis=0)
    xb = tok_pad[buf_t].reshape(n_blk, MOE_ROW_BLOCK, d)

    def expert_block(args):
        xblk, e = args
        return (jax.nn.silu(xblk @ w1[e]) * (xblk @ w3[e])) @ w2[e]

    yb = lax.map(expert_block, (xb, blk_e)).reshape(p, d)
    out = jnp.zeros((n + 1, d), tok.dtype).at[buf_t].add(yb * buf_w[:, None])
    return out[:n]


def hier_moe(h, w_router_group, w_router_expert, w1, w3, w2):
    b, t, d = h.shape
    n = b * t
    tok = h.reshape(n, d)
    p_group = jax.nn.softmax((tok @ w_router_group).astype(jnp.float32), axis=-1)
    grp = jnp.argmax(p_group, axis=-1).astype(jnp.int32)
    p_grp_sel = jnp.take_along_axis(p_group, grp[:, None], axis=-1)
    e_logits = (tok @ w_router_expert).astype(jnp.float32).reshape(n, MOE_GROUPS, MOE_EXPERTS_PER_GROUP)
    e_logits = jnp.take_along_axis(e_logits, grp[:, None, None], axis=1)[:, 0]
    top_p, top_i = lax.top_k(jax.nn.softmax(e_logits, axis=-1), MOE_TOP_K)
    weight = p_grp_sel * top_p / jnp.sum(top_p, axis=-1, keepdims=True)
    expert = grp[:, None] * MOE_EXPERTS_PER_GROUP + top_i.astype(jnp.int32)
    y = grouped_expert_ffn(tok, expert, weight.astype(h.dtype), w1, w3, w2)
    return y.reshape(b, t, d)


def setup_inputs(seed: int = 0) -> dict:
    key = jax.random.key(seed)
    ks = jax.random.split(key, 24)
    f32 = jnp.float32
    D = D_MODEL

    def nrm(k, shape, s):
        return jax.random.normal(k, shape, f32) * s

    return {
        'x': nrm(ks[0], (BATCH, SEQ, D), 1.0),
        'c': nrm(ks[1], (BATCH, D), 1.0),
        'ada_w': nrm(ks[2], (DEPTH, 2, D, 3 * D), 0.5 * D ** -0.5),
        'ada_b': nrm(ks[3], (DEPTH, 2, 3 * D), 0.01),
        'norm_g': 1.0 + nrm(ks[4], (DEPTH, 2, D), 0.01),
        'rel_bias': nrm(ks[5], (REL_BUCKETS, NSA_HEADS), 0.5),
        'nsa_w_in': nrm(ks[6], (N_NSA_LAYERS, D, NSA_IN_DIM), D ** -0.5),
        'nsa_q_gain': 1.0 + nrm(ks[7], (N_NSA_LAYERS, NSA_HEAD_DIM), 0.01),
        'nsa_k_gain': 1.0 + nrm(ks[8], (N_NSA_LAYERS, 3, NSA_HEAD_DIM), 0.01),
        'nsa_cmp_pe': nrm(ks[9], (N_NSA_LAYERS, 2, CMP_BLOCK, NSA_HEAD_DIM), 0.02),
        'nsa_cmp_w1': nrm(ks[10], (N_NSA_LAYERS, 2, CMP_BLOCK * NSA_HEAD_DIM, NSA_HEAD_DIM),
                          (CMP_BLOCK * NSA_HEAD_DIM) ** -0.5),
        'nsa_cmp_w2': nrm(ks[11], (N_NSA_LAYERS, 2, NSA_HEAD_DIM, NSA_HEAD_DIM), NSA_HEAD_DIM ** -0.5),
        'nsa_w_out': nrm(ks[12], (N_NSA_LAYERS, NSA_Q_DIM, D), NSA_Q_DIM ** -0.5),
        'hgrn_w_in': nrm(ks[13], (N_HGRN_LAYERS, D, HGRN_IN_DIM), D ** -0.5),
        'hgrn_lower_bounds': 1.0 + nrm(ks[14], (DEPTH, HGRN_F_DIM), 0.1),
        'hgrn_out_gain': 1.0 + nrm(ks[15], (N_HGRN_LAYERS, HGRN_VALUE_DIM), 0.01),
        'hgrn_w_out': nrm(ks[16], (N_HGRN_LAYERS, D, D), D ** -0.5),
        'moe_router_group': nrm(ks[17], (DEPTH, D, MOE_GROUPS), D ** -0.5),
        'moe_router_expert': nrm(ks[18], (DEPTH, D, MOE_EXPERTS), D ** -0.5),
        'moe_w1': nrm(ks[19], (DEPTH, MOE_EXPERTS, D, MOE_FF), D ** -0.5),
        'moe_w3': nrm(ks[20], (DEPTH, MOE_EXPERTS, D, MOE_FF), D ** -0.5),
        'moe_w2': nrm(ks[21], (DEPTH, MOE_EXPERTS, MOE_FF, D), MOE_FF ** -0.5),
    }


def reference(x, c, ada_w, ada_b, norm_g, rel_bias, nsa_w_in, nsa_q_gain, nsa_k_gain,
              nsa_cmp_pe, nsa_cmp_w1, nsa_cmp_w2, nsa_w_out, hgrn_w_in, hgrn_lower_bounds,
              hgrn_out_gain, hgrn_w_out, moe_router_group, moe_router_expert,
              moe_w1, moe_w3, moe_w2):
    lb_soft = jax.nn.softmax(hgrn_lower_bounds.astype(jnp.float32), axis=0)
    lb_all = jnp.cumsum(lb_soft, axis=0) - lb_soft[0]
    cond = jax.nn.silu(c)
    for layer in range(DEPTH):
        mod = jnp.einsum('bd,sde->sbe', cond, ada_w[layer]) + ada_b[layer][:, None, :]
        j = layer // N_MIXERS

        shift, scale, gate = jnp.split(mod[0], 3, axis=-1)
        h = rms_norm(x, norm_g[layer, 0]) * (1.0 + scale[:, None, :]) + shift[:, None, :]
        if layer % N_MIXERS == 0:
            y = nsa_mixer(h, rel_bias, nsa_w_in[j], nsa_q_gain[j], nsa_k_gain[j],
                          nsa_cmp_pe[j], nsa_cmp_w1[j], nsa_cmp_w2[j], nsa_w_out[j])
        else:
            y = hgrn2_mixer(h, lb_all[layer], hgrn_w_in[j], hgrn_out_gain[j], hgrn_w_out[j])
        x = x + gate[:, None, :] * y

        shift, scale, gate = jnp.split(mod[1], 3, axis=-1)
        h = rms_norm(x, norm_g[layer, 1]) * (1.0 + scale[:, None, :]) + shift[:, None, :]
        y = hier_moe(h, moe_router_group[layer], moe_router_expert[layer],
                     moe_w1[layer], moe_w3[layer], moe_w2[layer])
        x = x + gate[:, None, :] * y
    return x
```

```python
import functools
import math

import numpy as np
import jax
import jax.numpy as jnp
from jax import lax
from jax.experimental import pallas as pl
from jax.experimental.pallas import tpu as pltpu

F32 = jnp.float32
BF16 = jnp.bfloat16
HIGHEST = lax.Precision.HIGHEST

NSA_HEADS = 16
NSA_KV_GROUPS = 4
NSA_GROUP_SIZE = NSA_HEADS // NSA_KV_GROUPS
NSA_HEAD_DIM = 64
CMP_BLOCK = 32
CMP_STRIDE = 16
SEL_BLOCK = 64
N_SELECT = 8
WINDOW = 512
FORCED_SCORE = 1.0e4
REL_BUCKETS = 32
REL_MAX_DISTANCE = 1024
HGRN_DK = 128
MOE_GROUPS = 4
MOE_EPG = 4
MOE_EXPERTS = MOE_GROUPS * MOE_EPG
MOE_TOP_K = 2
MOE_ROW_BLOCK = 128
NORM_EPS = 1e-6

LANES = 128
NEG = -1.0e30
VMEM_LIMIT = 48 * 1024 * 1024

ATT_TILE = 128
HG_CHUNK = 128
HG_SUB = 16


def _cparams(*sem):
    return pltpu.CompilerParams(dimension_semantics=sem, vmem_limit_bytes=VMEM_LIMIT)


def _dot(a, b):
    return jnp.dot(a, b, preferred_element_type=F32)


def _dot_exact(a, b):
    return jnp.dot(a, b, preferred_element_type=F32, precision=HIGHEST)


def _dot_nt(a, b):
    return lax.dot_general(a, b, (((1,), (1,)), ((), ())), preferred_element_type=F32)


def _sigmoid(x):
    return 1.0 / (1.0 + jnp.exp(-x))


def _silu(x):
    return x * _sigmoid(x)


def _adaln_kernel(c_ref, w_ref, b_ref, o_ref):
    cond = _silu(c_ref[...])
    o_ref[0] = _dot_exact(cond, w_ref[0]) + b_ref[0]


def adaln_mod(c, ada_w, ada_b):
    depth, two, d, d3 = ada_w.shape
    b = c.shape[0]
    ls = depth * two
    tn = 1024
    out = pl.pallas_call(
        _adaln_kernel,
        grid=(ls, d3 // tn),
        in_specs=[
            pl.BlockSpec((b, d), lambda i, j: (0, 0)),
            pl.BlockSpec((1, d, tn), lambda i, j: (i, 0, j)),
            pl.BlockSpec((1, 1, tn), lambda i, j: (i, 0, j)),
        ],
        out_specs=pl.BlockSpec((1, b, tn), lambda i, j: (i, 0, j)),
        out_shape=jax.ShapeDtypeStruct((ls, b, d3), F32),
        compiler_params=_cparams("parallel", "parallel"),
        name="adaln_mod",
    )(c, ada_w.reshape(ls, d, d3), ada_b.reshape(ls, 1, d3))
    return out.reshape(depth, two, b, d3)


def _modulated_norm(x, gain, shift, scale):
    ms = jnp.mean(x * x, axis=-1, keepdims=True)
    y = x * lax.rsqrt(ms + NORM_EPS) * gain
    return y * (1.0 + scale) + shift


def _norm_matmul_kernel(x_ref, g_ref, sh_ref, sc_ref, w_ref, o_ref, h_sc):
    @pl.when(pl.program_id(2) == 0)
    def _():
        h = _modulated_norm(x_ref[0], g_ref[...], sh_ref[0], sc_ref[0])
        h_sc[...] = h.astype(BF16)

    o_ref[0] = _dot(h_sc[...], w_ref[...])


def norm_matmul(x, gain, shift, scale, w, tm=512, tn=1024):
    b, t, d = x.shape
    n = w.shape[1]
    tm = min(tm, t)
    return pl.pallas_call(
        _norm_matmul_kernel,
        grid=(b, t // tm, n // tn),
        in_specs=[
            pl.BlockSpec((1, tm, d), lambda bi, i, j: (bi, i, 0)),
            pl.BlockSpec((1, d), lambda bi, i, j: (0, 0)),
            pl.BlockSpec((1, 1, d), lambda bi, i, j: (bi, 0, 0)),
            pl.BlockSpec((1, 1, d), lambda bi, i, j: (bi, 0, 0)),
            pl.BlockSpec((d, tn), lambda bi, i, j: (0, j)),
        ],
        out_specs=pl.BlockSpec((1, tm, tn), lambda bi, i, j: (bi, i, j)),
        out_shape=jax.ShapeDtypeStruct((b, t, n), F32),
        scratch_shapes=[pltpu.VMEM((tm, d), BF16)],
        compiler_params=_cparams("parallel", "parallel", "arbitrary"),
        name="norm_matmul",
    )(x, gain.reshape(1, d), shift.reshape(b, 1, d), scale.reshape(b, 1, d), w)


def _out_proj_kernel(o_ref, w_ref, x_ref, gt_ref, y_ref):
    y = _dot(o_ref[0], w_ref[...])
    y_ref[0] = x_ref[0] + gt_ref[0] * y


def out_proj_residual(o, w, x, gate, tm=512):
    b, t, d = x.shape
    k = o.shape[-1]
    tm = min(tm, t)
    return pl.pallas_call(
        _out_proj_kernel,
        grid=(b, t // tm),
        in_specs=[
            pl.BlockSpec((1, tm, k), lambda bi, i: (bi, i, 0)),
            pl.BlockSpec((k, d), lambda bi, i: (0, 0)),
            pl.BlockSpec((1, tm, d), lambda bi, i: (bi, i, 0)),
            pl.BlockSpec((1, 1, d), lambda bi, i: (bi, 0, 0)),
        ],
        out_specs=pl.BlockSpec((1, tm, d), lambda bi, i: (bi, i, 0)),
        out_shape=jax.ShapeDtypeStruct((b, t, d), F32),
        compiler_params=_cparams("parallel", "parallel"),
        name="out_proj_residual",
    )(o, w, x, gate.reshape(b, 1, d))


def _pair_rms(x, gain2):
    lane = lax.broadcasted_iota(jnp.int32, x.shape, 1)
    lo = lane < NSA_HEAD_DIM
    x2 = x * x
    s_lo = jnp.sum(jnp.where(lo, x2, 0.0), axis=-1, keepdims=True)
    s_hi = jnp.sum(jnp.where(lo, 0.0, x2), axis=-1, keepdims=True)
    inv = jnp.where(lo, lax.rsqrt(s_lo / NSA_HEAD_DIM + NORM_EPS),
                    lax.rsqrt(s_hi / NSA_HEAD_DIM + NORM_EPS))
    return x * inv * gain2


def _nsa_prep_kernel(kc0_ref, kc1_ref, vc0_ref, vc1_ref, ks_ref, vs_ref, kw_ref, vw_ref,
                     kg_ref, pe_ref, w1_ref, w1big_ref, w2big_ref,
                     kct_ref, vco_ref, kst_ref, vso_ref, kwt_ref, vwo_ref):
    t = ks_ref.shape[1]
    nblk = t // CMP_STRIDE
    dh = NSA_HEAD_DIM
    G = NSA_KV_GROUPS

    for src, dst, gi in ((ks_ref, kst_ref, 1), (kw_ref, kwt_ref, 2)):
        g2 = kg_ref[gi:gi + 1, :]
        for gp in range(G // 2):
            slab = src[0, :, gp * LANES:(gp + 1) * LANES]
            kt = _pair_rms(slab, g2).T.astype(BF16)
            dst[0, 2 * gp] = kt[:dh]
            dst[0, 2 * gp + 1] = kt[dh:]
    for src, dst in ((vs_ref, vso_ref), (vw_ref, vwo_ref)):
        for g in range(G):
            dst[0, g] = src[0, :, g * dh:(g + 1) * dh].astype(BF16)

    for ci, (srcs, is_key) in enumerate((((kc0_ref, kc1_ref), True), ((vc0_ref, vc1_ref), False))):
        pe1 = _dot(pe_ref[ci].astype(BF16), w1_ref[ci])[0:1]
        pe2 = jnp.concatenate([pe1, pe1], axis=1)
        for gp, src in enumerate(srcs):
            parts = [src[0, pl.ds(l, nblk, stride=CMP_STRIDE), :] for l in range(CMP_STRIDE)]
            r = jnp.concatenate(parts, axis=1).astype(BF16)
            ab = _dot(r, w1big_ref[ci])
            second = ab[:, LANES:]
            shifted = jnp.concatenate([second[1:], jnp.zeros((1, LANES), F32)], axis=0)
            pre = ab[:, :LANES] + shifted + pe2
            hid = _dot(_silu(pre).astype(BF16), w2big_ref[ci])
            if is_key:
                kt = _pair_rms(hid, kg_ref[0:1, :]).T.astype(BF16)
                kct_ref[0, 2 * gp] = kt[:dh]
                kct_ref[0, 2 * gp + 1] = kt[dh:]
            else:
                vco_ref[0, 2 * gp] = hid[:, :dh].astype(BF16)
                vco_ref[0, 2 * gp + 1] = hid[:, dh:].astype(BF16)


def nsa_prep(proj, k_gain, cmp_pe, cmp_w1, cmp_w2):
    b, t, _ = proj.shape
    G, dh = NSA_KV_GROUPS, NSA_HEAD_DIM
    kvw = G * dh
    q_blocks = (NSA_HEADS * dh) // kvw
    nblk = t // CMP_STRIDE
    kg2 = jnp.concatenate([k_gain, k_gain], axis=1)
    pe_flat = jnp.broadcast_to(cmp_pe.reshape(2, 1, CMP_BLOCK * dh), (2, 8, CMP_BLOCK * dh))

    w1r = cmp_w1.reshape(2, 2, CMP_STRIDE, dh, dh)
    eye2 = jnp.eye(2, dtype=cmp_w1.dtype)
    w1big = jnp.einsum('chlde,gk->clgdhke', w1r, eye2).reshape(2, CMP_STRIDE * 2 * dh, 4 * dh)
    w2big = jnp.einsum('cde,gk->cgdke', cmp_w2, eye2).reshape(2, 2 * dh, 2 * dh)

    def col(i):
        return pl.BlockSpec((1, t, kvw), lambda bi, i=i: (bi, 0, q_blocks + i))

    def col_pair(i, gp):
        return pl.BlockSpec((1, t, LANES), lambda bi: (bi, 0, (q_blocks + i) * (kvw // LANES) + gp))

    def full(shape):
        return pl.BlockSpec(shape, lambda bi: (0,) * len(shape))

    def per_b(shape):
        return pl.BlockSpec((1,) + shape, lambda bi: (bi,) + (0,) * len(shape))

    return pl.pallas_call(
        _nsa_prep_kernel,
        grid=(b,),
        in_specs=[col_pair(0, 0), col_pair(0, 1), col_pair(1, 0), col_pair(1, 1),
                  col(2), col(3), col(4), col(5),
                  full((3, 2 * dh)), full((2, 8, CMP_BLOCK * dh)),
                  full((2, CMP_BLOCK * dh, dh)), full((2, 2 * CMP_STRIDE * dh, 4 * dh)),
                  full((2, 2 * dh, 2 * dh))],
        out_specs=[per_b((G, dh, nblk)), per_b((G, nblk, dh)),
                   per_b((G, dh, t)), per_b((G, t, dh)),
                   per_b((G, dh, t)), per_b((G, t, dh))],
        out_shape=[jax.ShapeDtypeStruct((b, G, dh, nblk), BF16),
                   jax.ShapeDtypeStruct((b, G, nblk, dh), BF16),
                   jax.ShapeDtypeStruct((b, G, dh, t), BF16),
                   jax.ShapeDtypeStruct((b, G, t, dh), BF16),
                   jax.ShapeDtypeStruct((b, G, dh, t), BF16),
                   jax.ShapeDtypeStruct((b, G, t, dh), BF16)],
        compiler_params=_cparams("parallel"),
        name="nsa_prep",
    )(proj, proj, proj, proj, proj, proj, proj, proj, kg2, pe_flat,
      cmp_w1.astype(BF16), w1big.astype(BF16), w2big.astype(BF16))


def _nsa_attn_kernel(q_ref, gl_ref, pg_ref, qg_ref, kct_ref, vc_ref, kst_ref, vs_ref,
                     kwt_ref, vw_ref, bc_ref, bt_ref, c2s_ref, ex_ref,
                     o_ref, p_sc, m_sc, l_sc, acc_sc, *, n_diag, n_sel_blocks, n_select):
    tq = ATT_TILE
    R, dh = NSA_GROUP_SIZE, NSA_HEAD_DIM
    qi = pl.program_id(2)
    scale = dh ** -0.5

    qraw = q_ref[0]
    qs = []
    for r in range(R):
        qr = qraw[:, r * dh:(r + 1) * dh]
        ms = jnp.mean(qr * qr, axis=-1, keepdims=True)
        qs.append((qr * lax.rsqrt(ms + NORM_EPS) * qg_ref[...] * scale).astype(BF16))
    qall = jnp.concatenate(qs, axis=0)

    ncp = kct_ref.shape[3]
    sc = _dot(qall, kct_ref[0, 0])
    t_idx = qi * tq + lax.broadcasted_iota(jnp.int32, (tq, ncp), 0)
    n_idx = lax.broadcasted_iota(jnp.int32, (tq, ncp), 1)
    vis = (n_idx * CMP_STRIDE + (CMP_BLOCK - 1)) <= t_idx
    psum = jnp.zeros((tq, ncp), F32)
    for r in range(R):
        s = jnp.where(vis, sc[r * tq:(r + 1) * tq] + bc_ref[r], NEG)
        m = jnp.max(s, axis=-1, keepdims=True)
        e = jnp.where(vis, jnp.exp(s - m), 0.0)
        p = e / jnp.maximum(jnp.sum(e, axis=-1, keepdims=True), 1e-30)
        psum = psum + p
        p_sc[r * tq:(r + 1) * tq, :ncp] = p.astype(BF16)
    o_cmp = _dot(p_sc[:, :ncp], vc_ref[0, 0])

    imp_t = _dot_exact(psum, c2s_ref[...]).T[:n_sel_blocks]
    blk = lax.broadcasted_iota(jnp.int32, (n_sel_blocks, tq), 0)
    tpos = qi * tq + lax.broadcasted_iota(jnp.int32, (n_sel_blocks, tq), 1)
    cur = tpos // SEL_BLOCK
    forced = (blk == 0) | (blk == cur) | (blk == cur - 1)
    score = jnp.where(forced, FORCED_SCORE, jnp.where(blk <= cur, imp_t, -1.0))
    rank = jnp.zeros((n_sel_blocks, tq), F32)
    for s2 in range(n_sel_blocks):
        row = score[s2:s2 + 1, :]
        beats = (row > score) | ((row == score) & (blk > s2))
        rank = rank + jnp.where(beats, 1.0, 0.0)
    sel_t = jnp.where(rank < n_select, 1.0, 0.0)
    if n_sel_blocks < LANES:
        sel_t = jnp.concatenate([sel_t, jnp.zeros((LANES - n_sel_blocks, tq), F32)], axis=0)
    sel = sel_t.T.astype(BF16)

    def sweep(kt_ref, v_ref, lo, hi, use_sel, window_edge):
        m_sc[...] = jnp.full(m_sc.shape, NEG, F32)
        l_sc[...] = jnp.zeros(l_sc.shape, F32)
        acc_sc[...] = jnp.zeros(acc_sc.shape, F32)

        def body(d, carry):
            jt = qi - d
            col = pl.multiple_of(jt * tq, tq)
            s_all = _dot(qall, kt_ref[0, 0, :, pl.ds(col, tq)])
            if window_edge is None:
                bi = jnp.minimum(d, n_diag)
            else:
                bi = jnp.where(d == window_edge, n_diag + 1, d)
            if use_sel:
                keep = _dot(sel, ex_ref[:, pl.ds(col, tq)]) > 0.5
            for r in range(R):
                rows = slice(r * tq, (r + 1) * tq)
                s = s_all[rows] + bt_ref[r, bi]
                if use_sel:
                    s = jnp.where(keep, s, NEG)
                m_old = m_sc[rows]
                m_new = jnp.maximum(m_old, jnp.max(s, axis=-1, keepdims=True))
                alpha = jnp.exp(m_old - m_new)
                p = jnp.exp(s - m_new)
                l_sc[rows] = alpha * l_sc[rows] + jnp.sum(p, axis=-1, keepdims=True)
                m_sc[rows] = m_new
                acc_sc[rows] = acc_sc[rows] * alpha
                p_sc[rows] = p.astype(BF16)
            acc_sc[...] += _dot(p_sc[...], v_ref[0, 0, pl.ds(col, tq), :])
            return carry

        lax.fori_loop(lo, hi, body, 0)
        return acc_sc[...] / l_sc[...]

    o_sel = sweep(kst_ref, vs_ref, 0, qi + 1, True, None)
    wt = WINDOW // tq
    o_win = sweep(kwt_ref, vw_ref, 0, jnp.minimum(qi, wt) + 1, False, wt)

    gates = _sigmoid(_dot_exact(gl_ref[0], pg_ref[0]))
    outs = []
    for r in range(R):
        rows = slice(r * tq, (r + 1) * tq)
        outs.append(gates[:, 3 * r:3 * r + 1] * o_cmp[rows]
                    + gates[:, 3 * r + 1:3 * r + 2] * o_sel[rows]
                    + gates[:, 3 * r + 2:3 * r + 3] * o_win[rows])
    o_ref[0] = jnp.concatenate(outs, axis=1).astype(o_ref.dtype)


def _t5_bucket(dist):
    n = jnp.maximum(dist, 0)
    max_exact = REL_BUCKETS // 2
    nf = jnp.maximum(n, 1).astype(F32)
    large = max_exact + (jnp.log(nf / max_exact) / math.log(REL_MAX_DISTANCE / max_exact)
                         * (REL_BUCKETS - max_exact)).astype(jnp.int32)
    large = jnp.minimum(large, REL_BUCKETS - 1)
    return jnp.where(n < max_exact, n, large)


def _bias_of_dist(dist, rel_bias):
    onehot = (_t5_bucket(dist)[..., None] == jnp.arange(REL_BUCKETS)).astype(F32)
    out = jnp.einsum('...k,kh->...h', onehot, rel_bias.astype(F32), precision=HIGHEST)
    return jnp.moveaxis(out, -1, 0)


def _saturated_diag():
    max_exact = REL_BUCKETS // 2
    steps = REL_BUCKETS - max_exact
    n_sat = max_exact * (REL_MAX_DISTANCE / max_exact) ** ((steps - 1) / steps)
    return int(math.ceil((n_sat + 2 + ATT_TILE) / ATT_TILE))


def nsa_attention(proj, kct, vc, kst, vs, kwt, vw, rel_bias, q_gain):
    b, t, _ = proj.shape
    G, R, dh = NSA_KV_GROUPS, NSA_GROUP_SIZE, NSA_HEAD_DIM
    H = NSA_HEADS
    tq = ATT_TILE
    ncp = kct.shape[3]
    n_sel_blocks = t // SEL_BLOCK
    n_select = min(N_SELECT, n_sel_blocks)
    assert n_sel_blocks <= LANES and ncp <= LANES and WINDOW % tq == 0 and t % tq == 0
    wt = WINDOW // tq
    n_diag = max(_saturated_diag(), wt + 1)

    ii = np.arange(tq)[:, None]
    jj = np.arange(tq)[None, :]
    dist = np.arange(n_diag + 1)[:, None, None] * tq + (ii - jj)[None]
    bt = jnp.where(dist >= 0, _bias_of_dist(jnp.asarray(dist), rel_bias), NEG)
    dwin = WINDOW + (ii - jj)
    bwin = jnp.where(dwin < WINDOW, _bias_of_dist(jnp.asarray(dwin), rel_bias), NEG)
    bias_tiles = jnp.concatenate([bt, bwin[:, None]], axis=1)
    dc = np.arange(t)[:, None] - (np.arange(ncp)[None, :] * CMP_STRIDE + CMP_BLOCK - 1)
    bias_c = _bias_of_dist(jnp.asarray(dc), rel_bias)

    cs = np.arange(ncp) * CMP_STRIDE
    ss = np.arange(LANES) * SEL_BLOCK
    shared = (np.minimum(cs[:, None] + CMP_BLOCK, ss[None, :] + SEL_BLOCK)
              - np.maximum(cs[:, None], ss[None, :]))
    c2s = np.clip(shared, 0, None) / CMP_BLOCK
    c2s[:, n_sel_blocks:] = 0.0
    c2s = jnp.asarray(c2s, F32)
    expand = jnp.asarray(np.arange(LANES)[:, None] == (np.arange(t)[None, :] // SEL_BLOCK), BF16)
    pg = np.zeros((G, LANES, LANES), np.float32)
    for g in range(G):
        for k in range(3 * R):
            pg[g, 3 * R * g + k, k] = 1.0
    pg = jnp.asarray(pg)
    gate_blk = (H * dh + 6 * G * dh) // LANES

    kernel = functools.partial(_nsa_attn_kernel, n_diag=n_diag, n_sel_blocks=n_sel_blocks,
                               n_select=n_select)
    return pl.pallas_call(
        kernel,
        grid=(b, G, t // tq),
        in_specs=[
            pl.BlockSpec((1, tq, R * dh), lambda bi, g, i: (bi, i, g)),
            pl.BlockSpec((1, tq, LANES), lambda bi, g, i: (bi, i, gate_blk)),
            pl.BlockSpec((1, LANES, LANES), lambda bi, g, i: (g, 0, 0)),
            pl.BlockSpec((1, dh), lambda bi, g, i: (0, 0)),
            pl.BlockSpec((1, 1, dh, ncp), lambda bi, g, i: (bi, g, 0, 0)),
            pl.BlockSpec((1, 1, ncp, dh), lambda bi, g, i: (bi, g, 0, 0)),
            pl.BlockSpec((1, 1, dh, t), lambda bi, g, i: (bi, g, 0, 0)),
            pl.BlockSpec((1, 1, t, dh), lambda bi, g, i: (bi, g, 0, 0)),
            pl.BlockSpec((1, 1, dh, t), lambda bi, g, i: (bi, g, 0, 0)),
            pl.BlockSpec((1, 1, t, dh), lambda bi, g, i: (bi, g, 0, 0)),
            pl.BlockSpec((R, tq, ncp), lambda bi, g, i: (g, i, 0)),
            pl.BlockSpec((R, n_diag + 2, tq, tq), lambda bi, g, i: (g, 0, 0, 0)),
            pl.BlockSpec((ncp, LANES), lambda bi, g, i: (0, 0)),
            pl.BlockSpec((LANES, t), lambda bi, g, i: (0, 0)),
        ],
        out_specs=pl.BlockSpec((1, tq, R * dh), lambda bi, g, i: (bi, i, g)),
        out_shape=jax.ShapeDtypeStruct((b, t, H * dh), BF16),
        scratch_shapes=[
            pltpu.VMEM((R * tq, tq), BF16),
            pltpu.VMEM((R * tq, 1), F32),
            pltpu.VMEM((R * tq, 1), F32),
            pltpu.VMEM((R * tq, dh), F32),
        ],
        compiler_params=_cparams("parallel", "parallel", "arbitrary"),
        name="nsa_attention",
    )(proj, proj, pg, q_gain.reshape(1, dh), kct, vc, kst, vs, kwt, vw,
      bias_c, bias_tiles, c2s, expand)


def nsa_layer(x, shift, scale, gate, norm_g, rel_bias, w_in, q_gain, k_gain,
              cmp_pe, cmp_w1, cmp_w2, w_out):
    d = x.shape[-1]
    n_in = w_in.shape[1]
    n_pad = -(-n_in // 1024) * 1024
    w_in_p = jnp.pad(w_in, ((0, 0), (0, n_pad - n_in))).astype(BF16)
    proj = norm_matmul(x, norm_g, shift, scale, w_in_p)
    kct, vc, kst, vs, kwt, vw = nsa_prep(proj, k_gain, cmp_pe, cmp_w1, cmp_w2)
    o = nsa_attention(proj, kct, vc, kst, vs, kwt, vw, rel_bias, q_gain)
    return out_proj_residual(o, w_out.astype(BF16), x, gate)


def _hgrn_kernel(q_ref, f_ref, v_ref, g_ref, lb_ref, og_ref, tri_ref, ones_ref,
                 o_ref, st_sc):
    C = HG_CHUNK
    dk = HGRN_DK

    @pl.when(pl.program_id(2) == 0)
    def _():
        st_sc[...] = jnp.zeros(st_sc.shape, F32)

    q = _silu(q_ref[0])
    fl = f_ref[0]
    v = v_ref[0]
    lb = lb_ref[0]
    log_sig = jnp.minimum(fl, 0.0) - jnp.log(1.0 + jnp.exp(-jnp.abs(fl)))
    ta = jnp.log(lb)
    tb = jnp.log1p(-lb) + log_sig
    lf = jnp.maximum(ta, tb) + jnp.log(1.0 + jnp.exp(-jnp.abs(ta - tb)))
    k = 1.0 - jnp.exp(lf)
    cum = _dot_exact(tri_ref[...], lf)

    row = lax.broadcasted_iota(jnp.int32, (C, C), 0)
    colm = lax.broadcasted_iota(jnp.int32, (C, C), 1)
    rloc = lax.broadcasted_iota(jnp.int32, (C, dk), 0)

    n_sub = C // HG_SUB
    k3 = k.reshape(n_sub, HG_SUB, dk)
    c3 = cum.reshape(n_sub, HG_SUB, dk)
    tloc = rloc % HG_SUB
    pieces = []
    for j in range(HG_SUB):
        kj = jnp.broadcast_to(k3[:, j:j + 1, :], (n_sub, HG_SUB, dk)).reshape(C, dk)
        cj = jnp.broadcast_to(c3[:, j:j + 1, :], (n_sub, HG_SUB, dk)).reshape(C, dk)
        arg = jnp.where(tloc >= j, cum - cj, NEG)
        pieces.append((q * kj * jnp.exp(arg)).astype(BF16))
    xcat = jnp.concatenate(pieces, axis=1)
    attn = jnp.where((row // HG_SUB) == (colm // HG_SUB), _dot(xcat, ones_ref[...]), 0.0)

    m = HG_SUB
    while m < C:
        nb = C // (2 * m)
        anchor = jnp.broadcast_to(cum.reshape(nb, 2 * m, dk)[:, m - 1:m, :],
                                  (nb, 2 * m, dk)).reshape(C, dk)
        e = jnp.exp(-jnp.abs(cum - anchor))
        upper = (rloc // m) % 2 == 1
        qm = jnp.where(upper, q * e, 0.0).astype(BF16)
        km = jnp.where(upper, 0.0, k * e).astype(BF16)
        attn = attn + jnp.where((row // (2 * m)) == (colm // (2 * m)), _dot_nt(qm, km), 0.0)
        m *= 2

    st = st_sc[...]
    o = _dot(attn.astype(BF16), v.astype(BF16))
    o = o + _dot_nt((q * jnp.exp(cum)).astype(BF16), st.astype(BF16))
    total = cum[C - 1:C, :]
    kd = (k * jnp.exp(total - cum)).astype(BF16)
    st_sc[...] = st * jnp.exp(total) + lax.dot_general(
        v.astype(BF16), kd, (((0,), (0,)), ((), ())), preferred_element_type=F32)

    ms = jnp.mean(o * o, axis=-1, keepdims=True)
    o = o * lax.rsqrt(ms + NORM_EPS) * og_ref[...]
    o_ref[0] = (o * _silu(g_ref[0])).astype(o_ref.dtype)


def hgrn_recurrence(proj, lb, out_gain):
    b, t, four_d = proj.shape
    d = four_d // 4
    dk = HGRN_DK
    nh = d // dk
    C = HG_CHUNK
    tri = jnp.asarray(np.tril(np.ones((C, C), np.float32)))
    ones = jnp.asarray(np.arange(HG_SUB * dk)[:, None] // dk == (np.arange(C)[None, :] % HG_SUB), BF16)
    return pl.pallas_call(
        _hgrn_kernel,
        grid=(b, nh, t // C),
        in_specs=[
            pl.BlockSpec((1, C, dk), lambda bi, h, c: (bi, c, h)),
            pl.BlockSpec((1, C, dk), lambda bi, h, c: (bi, c, nh + h)),
            pl.BlockSpec((1, C, dk), lambda bi, h, c: (bi, c, 2 * nh + h)),
            pl.BlockSpec((1, C, dk), lambda bi, h, c: (bi, c, 3 * nh + h)),
            pl.BlockSpec((1, 1, dk), lambda bi, h, c: (h, 0, 0)),
            pl.BlockSpec((1, dk), lambda bi, h, c: (0, 0)),
            pl.BlockSpec((C, C), lambda bi, h, c: (0, 0)),
            pl.BlockSpec((HG_SUB * dk, C), lambda bi, h, c: (0, 0)),
        ],
        out_specs=pl.BlockSpec((1, C, dk), lambda bi, h, c: (bi, c, h)),
        out_shape=jax.ShapeDtypeStruct((b, t, d), BF16),
        scratch_shapes=[pltpu.VMEM((dk, dk), F32)],
        compiler_params=_cparams("parallel", "parallel", "arbitrary"),
        name="hgrn_recurrence",
    )(proj, proj, proj, proj, lb.reshape(nh, 1, dk), out_gain.reshape(1, dk), tri, ones)


def hgrn_layer(x, shift, scale, gate, norm_g, lb, w_in, out_gain, w_out):
    proj = norm_matmul(x, norm_g, shift, scale, w_in.astype(BF16))
    o = hgrn_recurrence(proj, lb, out_gain)
    return out_proj_residual(o, w_out.astype(BF16), x, gate)


def _router_kernel(x_ref, g_ref, sh_ref, sc_ref, wr_ref, h_ref, eid_ref, ew_ref):
    h = _modulated_norm(x_ref[0], g_ref[...], sh_ref[0], sc_ref[0])
    h_ref[0] = h.astype(BF16)
    lt = _dot_exact(h, wr_ref[...]).T
    NG, EPG = MOE_GROUPS, MOE_EPG

    def softmax_rows(rows):
        mx = functools.reduce(jnp.maximum, rows)
        es = [jnp.exp(r - mx) for r in rows]
        tot = functools.reduce(lambda a, c: a + c, es)
        return [e / tot for e in es]

    def argmax_rows(rows):
        best, idx = rows[0], jnp.zeros(rows[0].shape, jnp.int32)
        for i in range(1, len(rows)):
            better = rows[i] > best
            best = jnp.where(better, rows[i], best)
            idx = jnp.where(better, i, idx)
        return best, idx

    pg = softmax_rows([lt[i:i + 1] for i in range(NG)])
    p_grp, grp = argmax_rows(pg)
    el = []
    for j in range(EPG):
        acc = lt[NG + j:NG + j + 1]
        for gi in range(1, NG):
            acc = jnp.where(grp == gi, lt[NG + gi * EPG + j:NG + gi * EPG + j + 1], acc)
        el.append(acc)
    pe = softmax_rows(el)
    p1, i1 = argmax_rows(pe)
    p2, i2 = argmax_rows([jnp.where(i1 == j, -1.0, pe[j]) for j in range(EPG)])
    den = p1 + p2
    eid_ref[0:1, :] = grp * EPG + i1
    eid_ref[1:2, :] = grp * EPG + i2
    ew_ref[0:1, :] = p_grp * p1 / den
    ew_ref[1:2, :] = p_grp * p2 / den


def moe_router(x, gain, shift, scale, w_group, w_expert, tm=512):
    b, t, d = x.shape
    tm = min(tm, t)
    nt = t // tm
    wr = jnp.concatenate([w_group, w_expert], axis=1)
    wr = jnp.pad(wr, ((0, 0), (0, LANES - wr.shape[1])))
    return pl.pallas_call(
        _router_kernel,
        grid=(b, nt),
        in_specs=[
            pl.BlockSpec((1, tm, d), lambda bi, i: (bi, i, 0)),
            pl.BlockSpec((1, d), lambda bi, i: (0, 0)),
            pl.BlockSpec((1, 1, d), lambda bi, i: (bi, 0, 0)),
            pl.BlockSpec((1, 1, d), lambda bi, i: (bi, 0, 0)),
            pl.BlockSpec((d, LANES), lambda bi, i: (0, 0)),
        ],
        out_specs=[
            pl.BlockSpec((1, tm, d), lambda bi, i: (bi, i, 0)),
            pl.BlockSpec((MOE_TOP_K, tm), lambda bi, i: (0, bi * nt + i)),
            pl.BlockSpec((MOE_TOP_K, tm), lambda bi, i: (0, bi * nt + i)),
        ],
        out_shape=[
            jax.ShapeDtypeStruct((b, t, d), BF16),
            jax.ShapeDtypeStruct((MOE_TOP_K, b * t), jnp.int32),
            jax.ShapeDtypeStruct((MOE_TOP_K, b * t), F32),
        ],
        compiler_params=_cparams("parallel", "parallel"),
        name="moe_router",
    )(x, gain.reshape(1, d), shift.reshape(b, 1, d), scale.reshape(b, 1, d), wr)


def _expert_ffn_kernel(be_ref, xb_ref, w1_ref, w3_ref, w2_ref, yb_ref):
    del be_ref
    xb = xb_ref[...]
    a = _dot(xb, w1_ref[0])
    g = _dot(xb, w3_ref[0])
    yb_ref[...] = _dot((_silu(a) * g).astype(BF16), w2_ref[0])


def expert_ffn(xb, blk_e, w1, w3, w2):
    p, d = xb.shape
    ff = w1.shape[2]
    rb = MOE_ROW_BLOCK
    grid_spec = pltpu.PrefetchScalarGridSpec(
        num_scalar_prefetch=1,
        grid=(p // rb,),
        in_specs=[
            pl.BlockSpec((rb, d), lambda i, be: (i, 0)),
            pl.BlockSpec((1, d, ff), lambda i, be: (be[i], 0, 0)),
            pl.BlockSpec((1, d, ff), lambda i, be: (be[i], 0, 0)),
            pl.BlockSpec((1, ff, d), lambda i, be: (be[i], 0, 0)),
        ],
        out_specs=pl.BlockSpec((rb, d), lambda i, be: (i, 0)),
    )
    return pl.pallas_call(
        _expert_ffn_kernel,
        grid_spec=grid_spec,
        out_shape=jax.ShapeDtypeStruct((p, d), F32),
        compiler_params=_cparams("arbitrary"),
        name="expert_ffn",
    )(blk_e, xb, w1, w3, w2)


def _combine_kernel(x_ref, gt_ref, y1_ref, y2_ref, w_ref, o_ref):
    w = w_ref[0]
    y = w[:, 0:1] * y1_ref[0] + w[:, 1:2] * y2_ref[0]
    o_ref[0] = x_ref[0] + gt_ref[0] * y


def moe_combine(x, gate, y1, y2, w, tm=512):
    b, t, d = x.shape
    tm = min(tm, t)
    spec = pl.BlockSpec((1, tm, d), lambda bi, i: (bi, i, 0))
    return pl.pallas_call(
        _combine_kernel,
        grid=(b, t // tm),
        in_specs=[spec, pl.BlockSpec((1, 1, d), lambda bi, i: (bi, 0, 0)), spec, spec,
                  pl.BlockSpec((1, tm, MOE_TOP_K), lambda bi, i: (bi, i, 0))],
        out_specs=spec,
        out_shape=jax.ShapeDtypeStruct((b, t, d), F32),
        compiler_params=_cparams("parallel", "parallel"),
        name="moe_combine",
    )(x, gate.reshape(b, 1, d), y1, y2, w)


def moe_layer(x, shift, scale, gate, norm_g, w_group, w_expert, w1, w3, w2):
    b, t, d = x.shape
    n = b * t
    a = n * MOE_TOP_K
    rb = MOE_ROW_BLOCK
    h, eid, ew = moe_router(x, norm_g, shift, scale, w_group, w_expert)
    flat_e = eid.T.reshape(a)
    onehot = (flat_e[:, None] == jnp.arange(MOE_EXPERTS, dtype=jnp.int32)).astype(jnp.int32)
    incl = jnp.cumsum(onehot, axis=0)
    counts = incl[-1]
    pos = jnp.sum((incl - onehot) * onehot, axis=1)
    padded = (counts + rb - 1) // rb * rb
    pad_end = jnp.cumsum(padded)
    pad_start = pad_end - padded
    dest = pad_start[flat_e] + pos
    n_blk = -(-(a + MOE_EXPERTS * (rb - 1)) // rb)
    p = n_blk * rb
    flat_t = jnp.arange(a, dtype=jnp.int32) // MOE_TOP_K
    buf_t = jnp.full((p,), n, jnp.int32).at[dest].set(flat_t)
    blk_e = jnp.minimum(jnp.searchsorted(pad_end, jnp.arange(n_blk) * rb, side='right'),
                        MOE_EXPERTS - 1).astype(jnp.int32)
    h_pad = jnp.concatenate([h.reshape(n, d), jnp.zeros((1, d), BF16)], axis=0)
    xb = h_pad[buf_t]
    yb = expert_ffn(xb, blk_e, w1.astype(BF16), w3.astype(BF16), w2.astype(BF16))
    ysel = yb[dest].reshape(b, t, MOE_TOP_K, d)
    return moe_combine(x, gate, ysel[:, :, 0], ysel[:, :, 1], ew.T.reshape(b, t, MOE_TOP_K))


def kernel(x, c, ada_w, ada_b, norm_g, rel_bias, nsa_w_in, nsa_q_gain, nsa_k_gain, nsa_cmp_pe,
           nsa_cmp_w1, nsa_cmp_w2, nsa_w_out, hgrn_w_in, hgrn_lower_bounds, hgrn_out_gain,
           hgrn_w_out, moe_router_group, moe_router_expert, moe_w1, moe_w3, moe_w2):
    depth = ada_w.shape[0]
    d = x.shape[-1]
    lb_soft = jax.nn.softmax(hgrn_lower_bounds.astype(F32), axis=0)
    lb_all = jnp.cumsum(lb_soft, axis=0) - lb_soft[0]
    mod = adaln_mod(c, ada_w, ada_b)
    for layer in range(depth):
        j = layer // 2
        shift, scale, gate = (mod[layer, 0, :, i * d:(i + 1) * d] for i in range(3))
        if layer % 2 == 0:
            x = nsa_layer(x, shift, scale, gate, norm_g[layer, 0], rel_bias, nsa_w_in[j],
                          nsa_q_gain[j], nsa_k_gain[j], nsa_cmp_pe[j], nsa_cmp_w1[j],
                          nsa_cmp_w2[j], nsa_w_out[j])
        else:
            x = hgrn_layer(x, shift, scale, gate, norm_g[layer, 0], lb_all[layer],
                           hgrn_w_in[j], hgrn_out_gain[j], hgrn_w_out[j])
        shift, scale, gate = (mod[layer, 1, :, i * d:(i + 1) * d] for i in range(3))
        x = moe_layer(x, shift, scale, gate, norm_g[layer, 1], moe_router_group[layer],
                      moe_router_expert[layer], moe_w1[layer], moe_w3[layer], moe_w2[layer])
    return x
```

```python
import functools
import math

import numpy as np
import jax
import jax.numpy as jnp
from jax import lax
from jax.experimental import pallas as pl
from jax.experimental.pallas import tpu as pltpu

F32 = jnp.float32
BF16 = jnp.bfloat16
HIGHEST = lax.Precision.HIGHEST

NSA_HEADS = 16
NSA_KV_GROUPS = 4
NSA_GROUP_SIZE = NSA_HEADS // NSA_KV_GROUPS
NSA_HEAD_DIM = 64
CMP_BLOCK = 32
CMP_STRIDE = 16
SEL_BLOCK = 64
N_SELECT = 8
WINDOW = 512
FORCED_SCORE = 1.0e4
REL_BUCKETS = 32
REL_MAX_DISTANCE = 1024
HGRN_DK = 128
MOE_GROUPS = 4
MOE_EPG = 4
MOE_EXPERTS = MOE_GROUPS * MOE_EPG
MOE_TOP_K = 2
MOE_ROW_BLOCK = 128
NORM_EPS = 1e-6

LANES = 128
NEG = -1.0e30
VMEM_LIMIT = 48 * 1024 * 1024

ATT_TILE = 128
HG_CHUNK = 128
HG_SUB = 16


def _cparams(*sem):
    return pltpu.CompilerParams(dimension_semantics=sem, vmem_limit_bytes=VMEM_LIMIT)


def _dot(a, b):
    return jnp.dot(a, b, preferred_element_type=F32)


def _dot_exact(a, b):
    return jnp.dot(a, b, preferred_element_type=F32, precision=HIGHEST)


def _dot_nt(a, b):
    return lax.dot_general(a, b, (((1,), (1,)), ((), ())), preferred_element_type=F32)


def _sigmoid(x):
    return 1.0 / (1.0 + jnp.exp(-x))


def _silu(x):
    return x * _sigmoid(x)


def _adaln_kernel(c_ref, w_ref, b_ref, o_ref):
    cond = _silu(c_ref[...])
    o_ref[0] = _dot_exact(cond, w_ref[0]) + b_ref[0]


def adaln_mod(c, ada_w, ada_b):
    depth, two, d, d3 = ada_w.shape
    b = c.shape[0]
    ls = depth * two
    tn = 1024
    out = pl.pallas_call(
        _adaln_kernel,
        grid=(ls, d3 // tn),
        in_specs=[
            pl.BlockSpec((b, d), lambda i, j: (0, 0)),
            pl.BlockSpec((1, d, tn), lambda i, j: (i, 0, j)),
            pl.BlockSpec((1, 1, tn), lambda i, j: (i, 0, j)),
        ],
        out_specs=pl.BlockSpec((1, b, tn), lambda i, j: (i, 0, j)),
        out_shape=jax.ShapeDtypeStruct((ls, b, d3), F32),
        compiler_params=_cparams("parallel", "parallel"),
        name="adaln_mod",
    )(c, ada_w.reshape(ls, d, d3), ada_b.reshape(ls, 1, d3))
    return out.reshape(depth, two, b, d3)


def _modulated_norm(x, gain, shift, scale):
    ms = jnp.mean(x * x, axis=-1, keepdims=True)
    y = x * lax.rsqrt(ms + NORM_EPS) * gain
    return y * (1.0 + scale) + shift


def _norm_matmul_kernel(x_ref, g_ref, sh_ref, sc_ref, w_ref, o_ref, h_sc):
    @pl.when(pl.program_id(2) == 0)
    def _():
        h = _modulated_norm(x_ref[0], g_ref[...], sh_ref[0], sc_ref[0])
        h_sc[...] = h.astype(BF16)

    o_ref[0] = _dot(h_sc[...], w_ref[...])


def norm_matmul(x, gain, shift, scale, w, tm=512, tn=1024):
    b, t, d = x.shape
    n = w.shape[1]
    tm = min(tm, t)
    return pl.pallas_call(
        _norm_matmul_kernel,
        grid=(b, t // tm, n // tn),
        in_specs=[
            pl.BlockSpec((1, tm, d), lambda bi, i, j: (bi, i, 0)),
            pl.BlockSpec((1, d), lambda bi, i, j: (0, 0)),
            pl.BlockSpec((1, 1, d), lambda bi, i, j: (bi, 0, 0)),
            pl.BlockSpec((1, 1, d), lambda bi, i, j: (bi, 0, 0)),
            pl.BlockSpec((d, tn), lambda bi, i, j: (0, j)),
        ],
        out_specs=pl.BlockSpec((1, tm, tn), lambda bi, i, j: (bi, i, j)),
        out_shape=jax.ShapeDtypeStruct((b, t, n), F32),
        scratch_shapes=[pltpu.VMEM((tm, d), BF16)],
        compiler_params=_cparams("parallel", "parallel", "arbitrary"),
        name="norm_matmul",
    )(x, gain.reshape(1, d), shift.reshape(b, 1, d), scale.reshape(b, 1, d), w)


def _out_proj_kernel(o_ref, w_ref, x_ref, gt_ref, y_ref):
    y = _dot(o_ref[0], w_ref[...])
    y_ref[0] = x_ref[0] + gt_ref[0] * y


def out_proj_residual(o, w, x, gate, tm=512):
    b, t, d = x.shape
    k = o.shape[-1]
    tm = min(tm, t)
    return pl.pallas_call(
        _out_proj_kernel,
        grid=(b, t // tm),
        in_specs=[
            pl.BlockSpec((1, tm, k), lambda bi, i: (bi, i, 0)),
            pl.BlockSpec((k, d), lambda bi, i: (0, 0)),
            pl.BlockSpec((1, tm, d), lambda bi, i: (bi, i, 0)),
            pl.BlockSpec((1, 1, d), lambda bi, i: (bi, 0, 0)),
        ],
        out_specs=pl.BlockSpec((1, tm, d), lambda bi, i: (bi, i, 0)),
        out_shape=jax.ShapeDtypeStruct((b, t, d), F32),
        compiler_params=_cparams("parallel", "parallel"),
        name="out_proj_residual",
    )(o, w, x, gate.reshape(b, 1, d))


def _pair_rms(x, gain2):
    lane = lax.broadcasted_iota(jnp.int32, x.shape, 1)
    lo = lane < NSA_HEAD_DIM
    x2 = x * x
    s_lo = jnp.sum(jnp.where(lo, x2, 0.0), axis=-1, keepdims=True)
    s_hi = jnp.sum(jnp.where(lo, 0.0, x2), axis=-1, keepdims=True)
    inv = jnp.where(lo, lax.rsqrt(s_lo / NSA_HEAD_DIM + NORM_EPS),
                    lax.rsqrt(s_hi / NSA_HEAD_DIM + NORM_EPS))
    return x * inv * gain2


def _nsa_prep_kernel(kc0_ref, kc1_ref, vc0_ref, vc1_ref, ks_ref, vs_ref, kw_ref, vw_ref,
                     kg_ref, pe_ref, w1_ref, w1big_ref, w2big_ref,
                     kco_ref, vct_ref, kso_ref, vst_ref, kwo_ref, vwt_ref):
    t = ks_ref.shape[1]
    nblk = t // CMP_STRIDE
    dh = NSA_HEAD_DIM
    G = NSA_KV_GROUPS

    for src, dst, gi in ((ks_ref, kso_ref, 1), (kw_ref, kwo_ref, 2)):
        g2 = kg_ref[gi:gi + 1, :]
        for gp in range(G // 2):
            kn = _pair_rms(src[0, :, gp * LANES:(gp + 1) * LANES], g2).astype(BF16)
            dst[0, 2 * gp] = kn[:, :dh]
            dst[0, 2 * gp + 1] = kn[:, dh:]
    for src, dst in ((vs_ref, vst_ref), (vw_ref, vwt_ref)):
        for gp in range(G // 2):
            vt = src[0, :, gp * LANES:(gp + 1) * LANES].T.astype(BF16)
            dst[0, 2 * gp] = vt[:dh]
            dst[0, 2 * gp + 1] = vt[dh:]

    for ci, (srcs, is_key) in enumerate((((kc0_ref, kc1_ref), True), ((vc0_ref, vc1_ref), False))):
        pe1 = _dot(pe_ref[ci].astype(BF16), w1_ref[ci])[0:1]
        pe2 = jnp.concatenate([pe1, pe1], axis=1)
        for gp, src in enumerate(srcs):
            parts = [src[0, pl.ds(l, nblk, stride=CMP_STRIDE), :] for l in range(CMP_STRIDE)]
            r = jnp.concatenate(parts, axis=1).astype(BF16)
            ab = _dot(r, w1big_ref[ci])
            second = ab[:, LANES:]
            shifted = jnp.concatenate([second[1:], jnp.zeros((1, LANES), F32)], axis=0)
            pre = ab[:, :LANES] + shifted + pe2
            hid = _dot(_silu(pre).astype(BF16), w2big_ref[ci])
            if is_key:
                kn = _pair_rms(hid, kg_ref[0:1, :]).astype(BF16)
                kco_ref[0, 2 * gp] = kn[:, :dh]
                kco_ref[0, 2 * gp + 1] = kn[:, dh:]
            else:
                vt = hid.T.astype(BF16)
                vct_ref[0, 2 * gp] = vt[:dh]
                vct_ref[0, 2 * gp + 1] = vt[dh:]


def nsa_prep(proj, k_gain, cmp_pe, cmp_w1, cmp_w2):
    b, t, _ = proj.shape
    G, dh = NSA_KV_GROUPS, NSA_HEAD_DIM
    kvw = G * dh
    q_blocks = (NSA_HEADS * dh) // kvw
    nblk = t // CMP_STRIDE
    kg2 = jnp.concatenate([k_gain, k_gain], axis=1)
    pe_flat = jnp.broadcast_to(cmp_pe.reshape(2, 1, CMP_BLOCK * dh), (2, 8, CMP_BLOCK * dh))

    w1r = cmp_w1.reshape(2, 2, CMP_STRIDE, dh, dh)
    eye2 = jnp.eye(2, dtype=cmp_w1.dtype)
    w1big = jnp.einsum('chlde,gk->clgdhke', w1r, eye2).reshape(2, CMP_STRIDE * 2 * dh, 4 * dh)
    w2big = jnp.einsum('cde,gk->cgdke', cmp_w2, eye2).reshape(2, 2 * dh, 2 * dh)

    def col(i):
        return pl.BlockSpec((1, t, kvw), lambda bi, i=i: (bi, 0, q_blocks + i))

    def col_pair(i, gp):
        return pl.BlockSpec((1, t, LANES), lambda bi: (bi, 0, (q_blocks + i) * (kvw // LANES) + gp))

    def full(shape):
        return pl.BlockSpec(shape, lambda bi: (0,) * len(shape))

    def per_b(shape):
        return pl.BlockSpec((1,) + shape, lambda bi: (bi,) + (0,) * len(shape))

    return pl.pallas_call(
        _nsa_prep_kernel,
        grid=(b,),
        in_specs=[col_pair(0, 0), col_pair(0, 1), col_pair(1, 0), col_pair(1, 1),
                  col(2), col(3), col(4), col(5),
                  full((3, 2 * dh)), full((2, 8, CMP_BLOCK * dh)),
                  full((2, CMP_BLOCK * dh, dh)), full((2, 2 * CMP_STRIDE * dh, 4 * dh)),
                  full((2, 2 * dh, 2 * dh))],
        out_specs=[per_b((G, nblk, dh)), per_b((G, dh, nblk)),
                   per_b((G, t, dh)), per_b((G, dh, t)),
                   per_b((G, t, dh)), per_b((G, dh, t))],
        out_shape=[jax.ShapeDtypeStruct((b, G, nblk, dh), BF16),
                   jax.ShapeDtypeStruct((b, G, dh, nblk), BF16),
                   jax.ShapeDtypeStruct((b, G, t, dh), BF16),
                   jax.ShapeDtypeStruct((b, G, dh, t), BF16),
                   jax.ShapeDtypeStruct((b, G, t, dh), BF16),
                   jax.ShapeDtypeStruct((b, G, dh, t), BF16)],
        compiler_params=_cparams("parallel"),
        name="nsa_prep",
    )(proj, proj, proj, proj, proj, proj, proj, proj, kg2, pe_flat,
      cmp_w1.astype(BF16), w1big.astype(BF16), w2big.astype(BF16))


def _nsa_attn_kernel(q_ref, gl_ref, pgt_ref, qg_ref, kc_ref, vct_ref, ks_ref, vst_ref,
                     kw_ref, vwt_ref, bc_ref, bt_ref, c2st_ref, ext_ref,
                     o_ref, acc_sc, *, n_diag, n_sel_blocks, n_select):
    tq = ATT_TILE
    R, dh = NSA_GROUP_SIZE, NSA_HEAD_DIM
    qi = pl.program_id(2)
    scale = dh ** -0.5

    qt = q_ref[0].T
    qs = []
    for r in range(R):
        qr = qt[r * dh:(r + 1) * dh]
        ms = jnp.mean(qr * qr, axis=0, keepdims=True)
        qs.append((qr * lax.rsqrt(ms + NORM_EPS) * qg_ref[...] * scale).astype(BF16))
    qall = jnp.concatenate(qs, axis=1)

    ncp = kc_ref.shape[2]
    sc = _dot(kc_ref[0, 0], qall)
    n_idx = lax.broadcasted_iota(jnp.int32, (ncp, tq), 0)
    t_idx = qi * tq + lax.broadcasted_iota(jnp.int32, (ncp, tq), 1)
    vis = (n_idx * CMP_STRIDE + (CMP_BLOCK - 1)) <= t_idx
    psum = jnp.zeros((ncp, tq), F32)
    o_cmp = []
    for r in range(R):
        s = jnp.where(vis, sc[:, r * tq:(r + 1) * tq] + bc_ref[r], NEG)
        m = jnp.max(s, axis=0, keepdims=True)
        e = jnp.where(vis, jnp.exp(s - m), 0.0)
        p = e * (1.0 / jnp.maximum(jnp.sum(e, axis=0, keepdims=True), 1e-30))
        psum = psum + p
        o_cmp.append(_dot(vct_ref[0, 0], p.astype(BF16)))

    imp_t = _dot_exact(c2st_ref[...], psum)
    blk = lax.broadcasted_iota(jnp.int32, (n_sel_blocks, tq), 0)
    tpos = qi * tq + lax.broadcasted_iota(jnp.int32, (n_sel_blocks, tq), 1)
    cur = tpos // SEL_BLOCK
    forced = (blk == 0) | (blk == cur) | (blk == cur - 1)
    score = jnp.where(forced, FORCED_SCORE, jnp.where(blk <= cur, imp_t, -1.0))
    rank = jnp.zeros((n_sel_blocks, tq), F32)
    for s2 in range(n_sel_blocks):
        row = score[s2:s2 + 1, :]
        beats = (row > score) | ((row == score) & (blk > s2))
        rank = rank + jnp.where(beats, 1.0, 0.0)
    negsel = jnp.where(rank < n_select, 0.0, NEG)
    if n_sel_blocks < LANES:
        negsel = jnp.concatenate([negsel, jnp.zeros((LANES - n_sel_blocks, tq), F32)], axis=0)
    negsel = negsel.astype(BF16)

    ones_rows = jnp.ones((16, tq), BF16)

    def sweep(k_ref, vt_ref, hi, use_sel, window_edge):
        acc_sc[...] = jnp.zeros(acc_sc.shape, F32)

        def body(d, ms):
            jt = qi - d
            col = pl.multiple_of(jt * tq, tq)
            s_all = _dot(k_ref[0, 0, pl.ds(col, tq), :], qall)
            if window_edge is None:
                bi = jnp.minimum(d, n_diag)
            else:
                bi = jnp.where(d == window_edge, n_diag + 1, d)
            if use_sel:
                mb = _dot(ext_ref[pl.ds(col, tq), :], negsel)
            vaug = jnp.concatenate([vt_ref[0, 0, :, pl.ds(col, tq)], ones_rows], axis=0)
            new_ms = []
            for r in range(R):
                s = s_all[:, r * tq:(r + 1) * tq] + bt_ref[r, bi]
                if use_sel:
                    s = s + mb
                m_new = jnp.maximum(ms[r], jnp.max(s, axis=0, keepdims=True))
                alpha = jnp.exp(ms[r] - m_new)
                p = jnp.exp(s - m_new).astype(BF16)
                acc_sc[r] = acc_sc[r] * alpha + _dot(vaug, p)
                new_ms.append(m_new)
            return tuple(new_ms)

        lax.fori_loop(0, hi, body, tuple(jnp.full((1, tq), NEG, F32) for _ in range(R)))
        outs = []
        for r in range(R):
            acc = acc_sc[r]
            outs.append(acc[:dh] * (1.0 / acc[dh:dh + 1]))
        return outs

    o_sel = sweep(ks_ref, vst_ref, qi + 1, True, None)
    wt = WINDOW // tq
    o_win = sweep(kw_ref, vwt_ref, jnp.minimum(qi, wt) + 1, False, wt)

    gates = _sigmoid(lax.dot_general(pgt_ref[0], gl_ref[0], (((1,), (1,)), ((), ())),
                                     preferred_element_type=F32, precision=HIGHEST))
    outs = []
    for r in range(R):
        outs.append(gates[3 * r:3 * r + 1] * o_cmp[r]
                    + gates[3 * r + 1:3 * r + 2] * o_sel[r]
                    + gates[3 * r + 2:3 * r + 3] * o_win[r])
    o_ref[0] = jnp.concatenate(outs, axis=0).T.astype(o_ref.dtype)


def _t5_bucket(dist):
    n = jnp.maximum(dist, 0)
    max_exact = REL_BUCKETS // 2
    nf = jnp.maximum(n, 1).astype(F32)
    large = max_exact + (jnp.log(nf / max_exact) / math.log(REL_MAX_DISTANCE / max_exact)
                         * (REL_BUCKETS - max_exact)).astype(jnp.int32)
    large = jnp.minimum(large, REL_BUCKETS - 1)
    return jnp.where(n < max_exact, n, large)


def _bias_of_dist(dist, rel_bias):
    onehot = (_t5_bucket(dist)[..., None] == jnp.arange(REL_BUCKETS)).astype(F32)
    out = jnp.einsum('...k,kh->...h', onehot, rel_bias.astype(F32), precision=HIGHEST)
    return jnp.moveaxis(out, -1, 0)


def _saturated_diag():
    max_exact = REL_BUCKETS // 2
    steps = REL_BUCKETS - max_exact
    n_sat = max_exact * (REL_MAX_DISTANCE / max_exact) ** ((steps - 1) / steps)
    return int(math.ceil((n_sat + 2 + ATT_TILE) / ATT_TILE))


def nsa_attention(proj, kc, vct, ks, vst, kw, vwt, rel_bias, q_gain):
    b, t, _ = proj.shape
    G, R, dh = NSA_KV_GROUPS, NSA_GROUP_SIZE, NSA_HEAD_DIM
    H = NSA_HEADS
    tq = ATT_TILE
    ncp = kc.shape[2]
    n_sel_blocks = t // SEL_BLOCK
    n_select = min(N_SELECT, n_sel_blocks)
    assert n_sel_blocks <= LANES and n_sel_blocks % 8 == 0 and WINDOW % tq == 0 and t % tq == 0
    wt = WINDOW // tq
    n_diag = max(_saturated_diag(), wt + 1)

    jj = np.arange(tq)[:, None]
    ii = np.arange(tq)[None, :]
    dist = np.arange(n_diag + 1)[:, None, None] * tq + (ii - jj)[None]
    bt = jnp.where(dist >= 0, _bias_of_dist(jnp.asarray(dist), rel_bias), NEG)
    dwin = WINDOW + (ii - jj)
    bwin = jnp.where(dwin < WINDOW, _bias_of_dist(jnp.asarray(dwin), rel_bias), NEG)
    bias_tiles = jnp.concatenate([bt, bwin[:, None]], axis=1)
    dc = np.arange(t)[None, :] - (np.arange(ncp)[:, None] * CMP_STRIDE + CMP_BLOCK - 1)
    bias_c = _bias_of_dist(jnp.asarray(dc), rel_bias)

    cs = np.arange(ncp) * CMP_STRIDE
    ss = np.arange(n_sel_blocks) * SEL_BLOCK
    shared = (np.minimum(cs[None, :] + CMP_BLOCK, ss[:, None] + SEL_BLOCK)
              - np.maximum(cs[None, :], ss[:, None]))
    c2st = jnp.asarray(np.clip(shared, 0, None) / CMP_BLOCK, F32)
    expand = jnp.asarray((np.arange(t)[:, None] // SEL_BLOCK) == np.arange(LANES)[None, :], BF16)
    pgt = np.zeros((G, LANES, LANES), np.float32)
    for g in range(G):
        for k in range(3 * R):
            pgt[g, k, 3 * R * g + k] = 1.0
    pgt = jnp.asarray(pgt)
    gate_blk = (H * dh + 6 * G * dh) // LANES
    qg = jnp.broadcast_to(q_gain.reshape(dh, 1), (dh, tq))

    kernel = functools.partial(_nsa_attn_kernel, n_diag=n_diag, n_sel_blocks=n_sel_blocks,
                               n_select=n_select)
    return pl.pallas_call(
        kernel,
        grid=(b, G, t // tq),
        in_specs=[
            pl.BlockSpec((1, tq, R * dh), lambda bi, g, i: (bi, i, g)),
            pl.BlockSpec((1, tq, LANES), lambda bi, g, i: (bi, i, gate_blk)),
            pl.BlockSpec((1, LANES, LANES), lambda bi, g, i: (g, 0, 0)),
            pl.BlockSpec((dh, tq), lambda bi, g, i: (0, 0)),
            pl.BlockSpec((1, 1, ncp, dh), lambda bi, g, i: (bi, g, 0, 0)),
            pl.BlockSpec((1, 1, dh, ncp), lambda bi, g, i: (bi, g, 0, 0)),
            pl.BlockSpec((1, 1, t, dh), lambda bi, g, i: (bi, g, 0, 0)),
            pl.BlockSpec((1, 1, dh, t), lambda bi, g, i: (bi, g, 0, 0)),
            pl.BlockSpec((1, 1, t, dh), lambda bi, g, i: (bi, g, 0, 0)),
            pl.BlockSpec((1, 1, dh, t), lambda bi, g, i: (bi, g, 0, 0)),
            pl.BlockSpec((R, ncp, tq), lambda bi, g, i: (g, 0, i)),
            pl.BlockSpec((R, n_diag + 2, tq, tq), lambda bi, g, i: (g, 0, 0, 0)),
            pl.BlockSpec((n_sel_blocks, ncp), lambda bi, g, i: (0, 0)),
            pl.BlockSpec((t, LANES), lambda bi, g, i: (0, 0)),
        ],
        out_specs=pl.BlockSpec((1, tq, R * dh), lambda bi, g, i: (bi, i, g)),
        out_shape=jax.ShapeDtypeStruct((b, t, H * dh), BF16),
        scratch_shapes=[pltpu.VMEM((R, dh + 16, tq), F32)],
        compiler_params=_cparams("parallel", "parallel", "arbitrary"),
        name="nsa_attention",
    )(proj, proj, pgt, qg, kc, vct, ks, vst, kw, vwt, bias_c, bias_tiles, c2st, expand)


def nsa_layer(x, shift, scale, gate, norm_g, rel_bias, w_in, q_gain, k_gain,
              cmp_pe, cmp_w1, cmp_w2, w_out):
    d = x.shape[-1]
    n_in = w_in.shape[1]
    n_pad = -(-n_in // 1024) * 1024
    w_in_p = jnp.pad(w_in, ((0, 0), (0, n_pad - n_in))).astype(BF16)
    proj = norm_matmul(x, norm_g, shift, scale, w_in_p)
    kc, vct, ks, vst, kw, vwt = nsa_prep(proj, k_gain, cmp_pe, cmp_w1, cmp_w2)
    o = nsa_attention(proj, kc, vct, ks, vst, kw, vwt, rel_bias, q_gain)
    return out_proj_residual(o, w_out.astype(BF16), x, gate)


def _hgrn_kernel(q_ref, f_ref, v_ref, g_ref, lb_ref, og_ref, tri_ref, ones_ref,
                 o_ref, st_sc):
    C = HG_CHUNK
    dk = HGRN_DK

    @pl.when(pl.program_id(2) == 0)
    def _():
        st_sc[...] = jnp.zeros(st_sc.shape, F32)

    q = _silu(q_ref[0])
    fl = f_ref[0]
    v = v_ref[0]
    lb = lb_ref[0]
    log_sig = jnp.minimum(fl, 0.0) - jnp.log(1.0 + jnp.exp(-jnp.abs(fl)))
    ta = jnp.log(lb)
    tb = jnp.log1p(-lb) + log_sig
    lf = jnp.maximum(ta, tb) + jnp.log(1.0 + jnp.exp(-jnp.abs(ta - tb)))
    k = 1.0 - jnp.exp(lf)
    cum = _dot_exact(tri_ref[...], lf)

    row = lax.broadcasted_iota(jnp.int32, (C, C), 0)
    colm = lax.broadcasted_iota(jnp.int32, (C, C), 1)
    rloc = lax.broadcasted_iota(jnp.int32, (C, dk), 0)

    n_sub = C // HG_SUB
    k3 = k.reshape(n_sub, HG_SUB, dk)
    c3 = cum.reshape(n_sub, HG_SUB, dk)
    tloc = rloc % HG_SUB
    pieces = []
    for j in range(HG_SUB):
        kj = jnp.broadcast_to(k3[:, j:j + 1, :], (n_sub, HG_SUB, dk)).reshape(C, dk)
        cj = jnp.broadcast_to(c3[:, j:j + 1, :], (n_sub, HG_SUB, dk)).reshape(C, dk)
        arg = jnp.where(tloc >= j, cum - cj, NEG)
        pieces.append((q * kj * jnp.exp(arg)).astype(BF16))
    xcat = jnp.concatenate(pieces, axis=1)
    attn = jnp.where((row // HG_SUB) == (colm // HG_SUB), _dot(xcat, ones_ref[...]), 0.0)

    m = HG_SUB
    while m < C:
        nb = C // (2 * m)
        anchor = jnp.broadcast_to(cum.reshape(nb, 2 * m, dk)[:, m - 1:m, :],
                                  (nb, 2 * m, dk)).reshape(C, dk)
        e = jnp.exp(-jnp.abs(cum - anchor))
        upper = (rloc // m) % 2 == 1
        qm = jnp.where(upper, q * e, 0.0).astype(BF16)
        km = jnp.where(upper, 0.0, k * e).astype(BF16)
        attn = attn + jnp.where((row // (2 * m)) == (colm // (2 * m)), _dot_nt(qm, km), 0.0)
        m *= 2

    st = st_sc[...]
    o = _dot(attn.astype(BF16), v.astype(BF16))
    o = o + _dot_nt((q * jnp.exp(cum)).astype(BF16), st.astype(BF16))
    total = cum[C - 1:C, :]
    kd = (k * jnp.exp(total - cum)).astype(BF16)
    st_sc[...] = st * jnp.exp(total) + lax.dot_general(
        v.astype(BF16), kd, (((0,), (0,)), ((), ())), preferred_element_type=F32)

    ms = jnp.mean(o * o, axis=-1, keepdims=True)
    o = o * lax.rsqrt(ms + NORM_EPS) * og_ref[...]
    o_ref[0] = (o * _silu(g_ref[0])).astype(o_ref.dtype)


def hgrn_recurrence(proj, lb, out_gain):
    b, t, four_d = proj.shape
    d = four_d // 4
    dk = HGRN_DK
    nh = d // dk
    C = HG_CHUNK
    tri = jnp.asarray(np.tril(np.ones((C, C), np.float32)))
    ones = jnp.asarray(np.arange(HG_SUB * dk)[:, None] // dk == (np.arange(C)[None, :] % HG_SUB), BF16)
    return pl.pallas_call(
        _hgrn_kernel,
        grid=(b, nh, t // C),
        in_specs=[
            pl.BlockSpec((1, C, dk), lambda bi, h, c: (bi, c, h)),
            pl.BlockSpec((1, C, dk), lambda bi, h, c: (bi, c, nh + h)),
            pl.BlockSpec((1, C, dk), lambda bi, h, c: (bi, c, 2 * nh + h)),
            pl.BlockSpec((1, C, dk), lambda bi, h, c: (bi, c, 3 * nh + h)),
            pl.BlockSpec((1, 1, dk), lambda bi, h, c: (h, 0, 0)),
            pl.BlockSpec((1, dk), lambda bi, h, c: (0, 0)),
            pl.BlockSpec((C, C), lambda bi, h, c: (0, 0)),
            pl.BlockSpec((HG_SUB * dk, C), lambda bi, h, c: (0, 0)),
        ],
        out_specs=pl.BlockSpec((1, C, dk), lambda bi, h, c: (bi, c, h)),
        out_shape=jax.ShapeDtypeStruct((b, t, d), BF16),
        scratch_shapes=[pltpu.VMEM((dk, dk), F32)],
        compiler_params=_cparams("parallel", "parallel", "arbitrary"),
        name="hgrn_recurrence",
    )(proj, proj, proj, proj, lb.reshape(nh, 1, dk), out_gain.reshape(1, dk), tri, ones)


def hgrn_layer(x, shift, scale, gate, norm_g, lb, w_in, out_gain, w_out):
    proj = norm_matmul(x, norm_g, shift, scale, w_in.astype(BF16))
    o = hgrn_recurrence(proj, lb, out_gain)
    return out_proj_residual(o, w_out.astype(BF16), x, gate)


def _router_kernel(x_ref, g_ref, sh_ref, sc_ref, wr_ref, h_ref, eid_ref, ew_ref):
    h = _modulated_norm(x_ref[0], g_ref[...], sh_ref[0], sc_ref[0])
    h_ref[0] = h.astype(BF16)
    lt = _dot_exact(h, wr_ref[...]).T
    NG, EPG = MOE_GROUPS, MOE_EPG

    def softmax_rows(rows):
        mx = functools.reduce(jnp.maximum, rows)
        es = [jnp.exp(r - mx) for r in rows]
        tot = functools.reduce(lambda a, c: a + c, es)
        return [e / tot for e in es]

    def argmax_rows(rows):
        best, idx = rows[0], jnp.zeros(rows[0].shape, jnp.int32)
        for i in range(1, len(rows)):
            better = rows[i] > best
            best = jnp.where(better, rows[i], best)
            idx = jnp.where(better, i, idx)
        return best, idx

    pg = softmax_rows([lt[i:i + 1] for i in range(NG)])
    p_grp, grp = argmax_rows(pg)
    el = []
    for j in range(EPG):
        acc = lt[NG + j:NG + j + 1]
        for gi in range(1, NG):
            acc = jnp.where(grp == gi, lt[NG + gi * EPG + j:NG + gi * EPG + j + 1], acc)
        el.append(acc)
    pe = softmax_rows(el)
    p1, i1 = argmax_rows(pe)
    p2, i2 = argmax_rows([jnp.where(i1 == j, -1.0, pe[j]) for j in range(EPG)])
    den = p1 + p2
    eid_ref[0:1, :] = grp * EPG + i1
    eid_ref[1:2, :] = grp * EPG + i2
    ew_ref[0:1, :] = p_grp * p1 / den
    ew_ref[1:2, :] = p_grp * p2 / den


def moe_router(x, gain, shift, scale, w_group, w_expert, tm=512):
    b, t, d = x.shape
    tm = min(tm, t)
    nt = t // tm
    wr = jnp.concatenate([w_group, w_expert], axis=1)
    wr = jnp.pad(wr, ((0, 0), (0, LANES - wr.shape[1])))
    return pl.pallas_call(
        _router_kernel,
        grid=(b, nt),
        in_specs=[
            pl.BlockSpec((1, tm, d), lambda bi, i: (bi, i, 0)),
            pl.BlockSpec((1, d), lambda bi, i: (0, 0)),
            pl.BlockSpec((1, 1, d), lambda bi, i: (bi, 0, 0)),
            pl.BlockSpec((1, 1, d), lambda bi, i: (bi, 0, 0)),
            pl.BlockSpec((d, LANES), lambda bi, i: (0, 0)),
        ],
        out_specs=[
            pl.BlockSpec((1, tm, d), lambda bi, i: (bi, i, 0)),
            pl.BlockSpec((MOE_TOP_K, tm), lambda bi, i: (0, bi * nt + i)),
            pl.BlockSpec((MOE_TOP_K, tm), lambda bi, i: (0, bi * nt + i)),
        ],
        out_shape=[
            jax.ShapeDtypeStruct((b, t, d), BF16),
            jax.ShapeDtypeStruct((MOE_TOP_K, b * t), jnp.int32),
            jax.ShapeDtypeStruct((MOE_TOP_K, b * t), F32),
        ],
        compiler_params=_cparams("parallel", "parallel"),
        name="moe_router",
    )(x, gain.reshape(1, d), shift.reshape(b, 1, d), scale.reshape(b, 1, d), wr)


def _expert_ffn_kernel(be_ref, xb_ref, w1_ref, w3_ref, w2_ref, yb_ref):
    del be_ref
    xb = xb_ref[...]
    a = _dot(xb, w1_ref[0])
    g = _dot(xb, w3_ref[0])
    yb_ref[...] = _dot((_silu(a) * g).astype(BF16), w2_ref[0])


def expert_ffn(xb, blk_e, w1, w3, w2):
    p, d = xb.shape
    ff = w1.shape[2]
    rb = MOE_ROW_BLOCK
    grid_spec = pltpu.PrefetchScalarGridSpec(
        num_scalar_prefetch=1,
        grid=(p // rb,),
        in_specs=[
            pl.BlockSpec((rb, d), lambda i, be: (i, 0)),
            pl.BlockSpec((1, d, ff), lambda i, be: (be[i], 0, 0)),
            pl.BlockSpec((1, d, ff), lambda i, be: (be[i], 0, 0)),
            pl.BlockSpec((1, ff, d), lambda i, be: (be[i], 0, 0)),
        ],
        out_specs=pl.BlockSpec((rb, d), lambda i, be: (i, 0)),
    )
    return pl.pallas_call(
        _expert_ffn_kernel,
        grid_spec=grid_spec,
        out_shape=jax.ShapeDtypeStruct((p, d), F32),
        compiler_params=_cparams("arbitrary"),
        name="expert_ffn",
    )(blk_e, xb, w1, w3, w2)


def _combine_kernel(x_ref, gt_ref, y1_ref, y2_ref, w_ref, o_ref):
    w = w_ref[0]
    y = w[:, 0:1] * y1_ref[0] + w[:, 1:2] * y2_ref[0]
    o_ref[0] = x_ref[0] + gt_ref[0] * y


def moe_combine(x, gate, y1, y2, w, tm=512):
    b, t, d = x.shape
    tm = min(tm, t)
    spec = pl.BlockSpec((1, tm, d), lambda bi, i: (bi, i, 0))
    return pl.pallas_call(
        _combine_kernel,
        grid=(b, t // tm),
        in_specs=[spec, pl.BlockSpec((1, 1, d), lambda bi, i: (bi, 0, 0)), spec, spec,
                  pl.BlockSpec((1, tm, MOE_TOP_K), lambda bi, i: (bi, i, 0))],
        out_specs=spec,
        out_shape=jax.ShapeDtypeStruct((b, t, d), F32),
        compiler_params=_cparams("parallel", "parallel"),
        name="moe_combine",
    )(x, gate.reshape(b, 1, d), y1, y2, w)


def moe_layer(x, shift, scale, gate, norm_g, w_group, w_expert, w1, w3, w2):
    b, t, d = x.shape
    n = b * t
    a = n * MOE_TOP_K
    rb = MOE_ROW_BLOCK
    h, eid, ew = moe_router(x, norm_g, shift, scale, w_group, w_expert)
    flat_e = eid.T.reshape(a)
    onehot = (flat_e[:, None] == jnp.arange(MOE_EXPERTS, dtype=jnp.int32)).astype(jnp.int32)
    incl = jnp.cumsum(onehot, axis=0)
    counts = incl[-1]
    pos = jnp.sum((incl - onehot) * onehot, axis=1)
    padded = (counts + rb - 1) // rb * rb
    pad_end = jnp.cumsum(padded)
    pad_start = pad_end - padded
    dest = pad_start[flat_e] + pos
    n_blk = -(-(a + MOE_EXPERTS * (rb - 1)) // rb)
    p = n_blk * rb
    flat_t = jnp.arange(a, dtype=jnp.int32) // MOE_TOP_K
    buf_t = jnp.full((p,), n, jnp.int32).at[dest].set(flat_t)
    blk_e = jnp.minimum(jnp.searchsorted(pad_end, jnp.arange(n_blk) * rb, side='right'),
                        MOE_EXPERTS - 1).astype(jnp.int32)
    h_pad = jnp.concatenate([h.reshape(n, d), jnp.zeros((1, d), BF16)], axis=0)
    xb = h_pad[buf_t]
    yb = expert_ffn(xb, blk_e, w1.astype(BF16), w3.astype(BF16), w2.astype(BF16))
    ysel = yb[dest].reshape(b, t, MOE_TOP_K, d)
    return moe_combine(x, gate, ysel[:, :, 0], ysel[:, :, 1], ew.T.reshape(b, t, MOE_TOP_K))


def kernel(x, c, ada_w, ada_b, norm_g, rel_bias, nsa_w_in, nsa_q_gain, nsa_k_gain, nsa_cmp_pe,
           nsa_cmp_w1, nsa_cmp_w2, nsa_w_out, hgrn_w_in, hgrn_lower_bounds, hgrn_out_gain,
           hgrn_w_out, moe_router_group, moe_router_expert, moe_w1, moe_w3, moe_w2):
    depth = ada_w.shape[0]
    d = x.shape[-1]
    lb_soft = jax.nn.softmax(hgrn_lower_bounds.astype(F32), axis=0)
    lb_all = jnp.cumsum(lb_soft, axis=0) - lb_soft[0]
    mod = adaln_mod(c, ada_w, ada_b)
    for layer in range(depth):
        j = layer // 2
        shift, scale, gate = (mod[layer, 0, :, i * d:(i + 1) * d] for i in range(3))
        if layer % 2 == 0:
            x = nsa_layer(x, shift, scale, gate, norm_g[layer, 0], rel_bias, nsa_w_in[j],
                          nsa_q_gain[j], nsa_k_gain[j], nsa_cmp_pe[j], nsa_cmp_w1[j],
                          nsa_cmp_w2[j], nsa_w_out[j])
        else:
            x = hgrn_layer(x, shift, scale, gate, norm_g[layer, 0], lb_all[layer],
                           hgrn_w_in[j], hgrn_out_gain[j], hgrn_w_out[j])
        shift, scale, gate = (mod[layer, 1, :, i * d:(i + 1) * d] for i in range(3))
        x = moe_layer(x, shift, scale, gate, norm_g[layer, 1], moe_router_group[layer],
                      moe_router_expert[layer], moe_w1[layer], moe_w3[layer], moe_w2[layer])
    return x
```

```python
import functools
import math

import numpy as np
import jax
import jax.numpy as jnp
from jax import lax
from jax.experimental import pallas as pl
from jax.experimental.pallas import tpu as pltpu

F32 = jnp.float32
BF16 = jnp.bfloat16
HIGHEST = lax.Precision.HIGHEST

NSA_HEADS = 16
NSA_KV_GROUPS = 4
NSA_GROUP_SIZE = NSA_HEADS // NSA_KV_GROUPS
NSA_HEAD_DIM = 64
CMP_BLOCK = 32
CMP_STRIDE = 16
SEL_BLOCK = 64
N_SELECT = 8
WINDOW = 512
FORCED_SCORE = 1.0e4
REL_BUCKETS = 32
REL_MAX_DISTANCE = 1024
HGRN_DK = 128
MOE_GROUPS = 4
MOE_EPG = 4
MOE_EXPERTS = MOE_GROUPS * MOE_EPG
MOE_TOP_K = 2
MOE_ROW_BLOCK = 128
NORM_EPS = 1e-6

LANES = 128
NEG = -1.0e30
LOG2E = math.log2(math.e)
VMEM_LIMIT = 48 * 1024 * 1024

ATT_TQ = 256
ATT_TK = 256
HG_CHUNK = 128
HG_SUB = 16


def _cparams(*sem):
    return pltpu.CompilerParams(dimension_semantics=sem, vmem_limit_bytes=VMEM_LIMIT)


def _dot(a, b):
    return jnp.dot(a, b, preferred_element_type=F32)


def _dot_exact(a, b):
    return jnp.dot(a, b, preferred_element_type=F32, precision=HIGHEST)


def _dot_nt(a, b):
    return lax.dot_general(a, b, (((1,), (1,)), ((), ())), preferred_element_type=F32)


def _sigmoid(x):
    return 1.0 / (1.0 + jnp.exp(-x))


def _silu(x):
    return x * _sigmoid(x)


def _adaln_kernel(c_ref, w_ref, b_ref, o_ref):
    cond = _silu(c_ref[...])
    o_ref[0] = _dot_exact(cond, w_ref[0]) + b_ref[0]


def adaln_mod(c, ada_w, ada_b):
    depth, two, d, d3 = ada_w.shape
    b = c.shape[0]
    ls = depth * two
    tn = 1024
    out = pl.pallas_call(
        _adaln_kernel,
        grid=(ls, d3 // tn),
        in_specs=[
            pl.BlockSpec((b, d), lambda i, j: (0, 0)),
            pl.BlockSpec((1, d, tn), lambda i, j: (i, 0, j)),
            pl.BlockSpec((1, 1, tn), lambda i, j: (i, 0, j)),
        ],
        out_specs=pl.BlockSpec((1, b, tn), lambda i, j: (i, 0, j)),
        out_shape=jax.ShapeDtypeStruct((ls, b, d3), F32),
        compiler_params=_cparams("parallel", "parallel"),
        name="adaln_mod",
    )(c, ada_w.reshape(ls, d, d3), ada_b.reshape(ls, 1, d3))
    return out.reshape(depth, two, b, d3)


def _modulated_norm(x, gain, shift, scale):
    ms = jnp.mean(x * x, axis=-1, keepdims=True)
    y = x * lax.rsqrt(ms + NORM_EPS) * gain
    return y * (1.0 + scale) + shift


def _norm_matmul_kernel(x_ref, g_ref, sh_ref, sc_ref, w_ref, o_ref, h_sc):
    @pl.when(pl.program_id(2) == 0)
    def _():
        h = _modulated_norm(x_ref[0], g_ref[...], sh_ref[0], sc_ref[0])
        h_sc[...] = h.astype(BF16)

    o_ref[0] = _dot(h_sc[...], w_ref[...])


def norm_matmul(x, gain, shift, scale, w, tm=512, tn=1024):
    b, t, d = x.shape
    n = w.shape[1]
    tm = min(tm, t)
    return pl.pallas_call(
        _norm_matmul_kernel,
        grid=(b, t // tm, n // tn),
        in_specs=[
            pl.BlockSpec((1, tm, d), lambda bi, i, j: (bi, i, 0)),
            pl.BlockSpec((1, d), lambda bi, i, j: (0, 0)),
            pl.BlockSpec((1, 1, d), lambda bi, i, j: (bi, 0, 0)),
            pl.BlockSpec((1, 1, d), lambda bi, i, j: (bi, 0, 0)),
            pl.BlockSpec((d, tn), lambda bi, i, j: (0, j)),
        ],
        out_specs=pl.BlockSpec((1, tm, tn), lambda bi, i, j: (bi, i, j)),
        out_shape=jax.ShapeDtypeStruct((b, t, n), F32),
        scratch_shapes=[pltpu.VMEM((tm, d), BF16)],
        compiler_params=_cparams("parallel", "parallel", "arbitrary"),
        name="norm_matmul",
    )(x, gain.reshape(1, d), shift.reshape(b, 1, d), scale.reshape(b, 1, d), w)


def _out_proj_kernel(o_ref, w_ref, x_ref, gt_ref, y_ref):
    y = _dot(o_ref[0], w_ref[...])
    y_ref[0] = x_ref[0] + gt_ref[0] * y


def out_proj_residual(o, w, x, gate, tm=512):
    b, t, d = x.shape
    k = o.shape[-1]
    tm = min(tm, t)
    return pl.pallas_call(
        _out_proj_kernel,
        grid=(b, t // tm),
        in_specs=[
            pl.BlockSpec((1, tm, k), lambda bi, i: (bi, i, 0)),
            pl.BlockSpec((k, d), lambda bi, i: (0, 0)),
            pl.BlockSpec((1, tm, d), lambda bi, i: (bi, i, 0)),
            pl.BlockSpec((1, 1, d), lambda bi, i: (bi, 0, 0)),
        ],
        out_specs=pl.BlockSpec((1, tm, d), lambda bi, i: (bi, i, 0)),
        out_shape=jax.ShapeDtypeStruct((b, t, d), F32),
        compiler_params=_cparams("parallel", "parallel"),
        name="out_proj_residual",
    )(o, w, x, gate.reshape(b, 1, d))


def _pair_rms(x, gain2):
    lane = lax.broadcasted_iota(jnp.int32, x.shape, 1)
    lo = lane < NSA_HEAD_DIM
    x2 = x * x
    s_lo = jnp.sum(jnp.where(lo, x2, 0.0), axis=-1, keepdims=True)
    s_hi = jnp.sum(jnp.where(lo, 0.0, x2), axis=-1, keepdims=True)
    inv = jnp.where(lo, lax.rsqrt(s_lo / NSA_HEAD_DIM + NORM_EPS),
                    lax.rsqrt(s_hi / NSA_HEAD_DIM + NORM_EPS))
    return x * inv * gain2


def _nsa_prep_kernel(kc0_ref, kc1_ref, vc0_ref, vc1_ref, ks_ref, vs_ref, kw_ref, vw_ref,
                     kg_ref, pe_ref, w1_ref, w1big_ref, w2big_ref, ex_ref,
                     kco_ref, vct_ref, kso_ref, vst_ref, kwo_ref, vwt_ref):
    t = ks_ref.shape[1]
    nblk = t // CMP_STRIDE
    dh = NSA_HEAD_DIM
    G = NSA_KV_GROUPS

    for src, dst, gi, off in ((ks_ref, kso_ref, 1, LANES), (kw_ref, kwo_ref, 2, 0)):
        g2 = kg_ref[gi:gi + 1, :]
        for gp in range(G // 2):
            kn = _pair_rms(src[0, :, gp * LANES:(gp + 1) * LANES], g2).astype(BF16)
            dst[0, 2 * gp, :, off:off + dh] = kn[:, :dh]
            dst[0, 2 * gp + 1, :, off:off + dh] = kn[:, dh:]
    for g in range(G):
        kso_ref[0, g, :, :LANES] = ex_ref[...]
    for src, dst in ((vs_ref, vst_ref), (vw_ref, vwt_ref)):
        for gp in range(G // 2):
            vt = src[0, :, gp * LANES:(gp + 1) * LANES].T.astype(BF16)
            dst[0, 2 * gp] = vt[:dh]
            dst[0, 2 * gp + 1] = vt[dh:]

    for ci, (srcs, is_key) in enumerate((((kc0_ref, kc1_ref), True), ((vc0_ref, vc1_ref), False))):
        pe1 = _dot(pe_ref[ci].astype(BF16), w1_ref[ci])[0:1]
        pe2 = jnp.concatenate([pe1, pe1], axis=1)
        for gp, src in enumerate(srcs):
            parts = [src[0, pl.ds(l, nblk, stride=CMP_STRIDE), :] for l in range(CMP_STRIDE)]
            r = jnp.concatenate(parts, axis=1).astype(BF16)
            ab = _dot(r, w1big_ref[ci])
            second = ab[:, LANES:]
            shifted = jnp.concatenate([second[1:], jnp.zeros((1, LANES), F32)], axis=0)
            pre = ab[:, :LANES] + shifted + pe2
            hid = _dot(_silu(pre).astype(BF16), w2big_ref[ci])
            if is_key:
                kn = _pair_rms(hid, kg_ref[0:1, :]).astype(BF16)
                kco_ref[0, 2 * gp] = kn[:, :dh]
                kco_ref[0, 2 * gp + 1] = kn[:, dh:]
            else:
                vt = hid.T.astype(BF16)
                vct_ref[0, 2 * gp] = vt[:dh]
                vct_ref[0, 2 * gp + 1] = vt[dh:]


def nsa_prep(proj, k_gain, cmp_pe, cmp_w1, cmp_w2):
    b, t, _ = proj.shape
    G, dh = NSA_KV_GROUPS, NSA_HEAD_DIM
    kvw = G * dh
    q_blocks = (NSA_HEADS * dh) // kvw
    nblk = t // CMP_STRIDE
    kg2 = jnp.concatenate([k_gain, k_gain], axis=1)
    pe_flat = jnp.broadcast_to(cmp_pe.reshape(2, 1, CMP_BLOCK * dh), (2, 8, CMP_BLOCK * dh))

    w1r = cmp_w1.reshape(2, 2, CMP_STRIDE, dh, dh)
    eye2 = jnp.eye(2, dtype=cmp_w1.dtype)
    w1big = jnp.einsum('chlde,gk->clgdhke', w1r, eye2).reshape(2, CMP_STRIDE * 2 * dh, 4 * dh)
    w2big = jnp.einsum('cde,gk->cgdke', cmp_w2, eye2).reshape(2, 2 * dh, 2 * dh)
    expand = jnp.asarray((np.arange(t)[:, None] // SEL_BLOCK) == np.arange(LANES)[None, :], BF16)

    def col(i):
        return pl.BlockSpec((1, t, kvw), lambda bi, i=i: (bi, 0, q_blocks + i))

    def col_pair(i, gp):
        return pl.BlockSpec((1, t, LANES), lambda bi: (bi, 0, (q_blocks + i) * (kvw // LANES) + gp))

    def full(shape):
        return pl.BlockSpec(shape, lambda bi: (0,) * len(shape))

    def per_b(shape):
        return pl.BlockSpec((1,) + shape, lambda bi: (bi,) + (0,) * len(shape))

    return pl.pallas_call(
        _nsa_prep_kernel,
        grid=(b,),
        in_specs=[col_pair(0, 0), col_pair(0, 1), col_pair(1, 0), col_pair(1, 1),
                  col(2), col(3), col(4), col(5),
                  full((3, 2 * dh)), full((2, 8, CMP_BLOCK * dh)),
                  full((2, CMP_BLOCK * dh, dh)), full((2, 2 * CMP_STRIDE * dh, 4 * dh)),
                  full((2, 2 * dh, 2 * dh)), full((t, LANES))],
        out_specs=[per_b((G, nblk, dh)), per_b((G, dh, nblk)),
                   per_b((G, t, LANES + dh)), per_b((G, dh, t)),
                   per_b((G, t, dh)), per_b((G, dh, t))],
        out_shape=[jax.ShapeDtypeStruct((b, G, nblk, dh), BF16),
                   jax.ShapeDtypeStruct((b, G, dh, nblk), BF16),
                   jax.ShapeDtypeStruct((b, G, t, LANES + dh), BF16),
                   jax.ShapeDtypeStruct((b, G, dh, t), BF16),
                   jax.ShapeDtypeStruct((b, G, t, dh), BF16),
                   jax.ShapeDtypeStruct((b, G, dh, t), BF16)],
        compiler_params=_cparams("parallel"),
        name="nsa_prep",
    )(proj, proj, proj, proj, proj, proj, proj, proj, kg2, pe_flat,
      cmp_w1.astype(BF16), w1big.astype(BF16), w2big.astype(BF16), expand)


def _nsa_attn_kernel(q_ref, gl_ref, pgt_ref, qg_ref, kc_ref, vct_ref, ksa_ref, vst_ref,
                     kw_ref, vwt_ref, bc_ref, bs_ref, bw_ref, c2st_ref,
                     o_ref, acc_sc, qsel_sc, qwin_sc, *, e_sat, n_wtiles, n_sel_blocks, n_select):
    tq, tk = ATT_TQ, ATT_TK
    R, dh = NSA_GROUP_SIZE, NSA_HEAD_DIM
    qi = pl.program_id(2)
    scale = dh ** -0.5 * LOG2E

    qt = q_ref[0].T
    for r in range(R):
        qr = qt[r * dh:(r + 1) * dh]
        ms = jnp.mean(qr * qr, axis=0, keepdims=True)
        qn = (qr * lax.rsqrt(ms + NORM_EPS) * qg_ref[...] * scale).astype(BF16)
        qwin_sc[:, r * tq:(r + 1) * tq] = qn
        qsel_sc[LANES:, r * tq:(r + 1) * tq] = qn

    ncp = kc_ref.shape[2]
    n_idx = lax.broadcasted_iota(jnp.int32, (ncp, tq), 0)
    t_idx = qi * tq + lax.broadcasted_iota(jnp.int32, (ncp, tq), 1)
    vis = (n_idx * CMP_STRIDE + (CMP_BLOCK - 1)) <= t_idx
    psum = jnp.zeros((ncp, tq), F32)
    o_cmp = []
    for r in range(R):
        sc = _dot(kc_ref[0, 0], qwin_sc[:, r * tq:(r + 1) * tq])
        s = jnp.where(vis, sc + bc_ref[r], NEG)
        m = jnp.max(s, axis=0, keepdims=True)
        e = jnp.where(vis, jnp.exp2(s - m), 0.0)
        p = e * (1.0 / jnp.maximum(jnp.sum(e, axis=0, keepdims=True), 1e-30))
        psum = psum + p
        o_cmp.append(_dot(vct_ref[0, 0], p.astype(BF16)))

    imp_t = _dot_exact(c2st_ref[...], psum)
    blk = lax.broadcasted_iota(jnp.int32, (n_sel_blocks, tq), 0)
    tpos = qi * tq + lax.broadcasted_iota(jnp.int32, (n_sel_blocks, tq), 1)
    cur = tpos // SEL_BLOCK
    forced = (blk == 0) | (blk == cur) | (blk == cur - 1)
    score = jnp.where(forced, FORCED_SCORE, jnp.where(blk <= cur, imp_t, -1.0))
    rank = jnp.zeros((n_sel_blocks, tq), F32)
    for s2 in range(n_sel_blocks):
        row = score[s2:s2 + 1, :]
        beats = (row > score) | ((row == score) & (blk > s2))
        rank = rank + jnp.where(beats, 1.0, 0.0)
    negsel = jnp.where(rank < n_select, 0.0, NEG)
    if n_sel_blocks < LANES:
        negsel = jnp.concatenate([negsel, jnp.zeros((LANES - n_sel_blocks, tq), F32)], axis=0)
    negsel = negsel.astype(BF16)
    for r in range(R):
        qsel_sc[:LANES, r * tq:(r + 1) * tq] = negsel

    ones_rows = jnp.ones((16, tk), BF16)
    rr = tq // tk
    n_tiles = rr * qi + rr

    def sweep(k_ref, vt_ref, q_sc, b_ref, hi, e_last):
        acc_sc[...] = jnp.zeros(acc_sc.shape, F32)

        def body(d, ms):
            e = jnp.where(d == 0, rr - 1, jnp.where(d < rr, d - 1, d))
            col = pl.multiple_of((n_tiles - 1 - e) * tk, tk)
            kt = k_ref[0, 0, pl.ds(col, tk), :]
            bi = jnp.minimum(e, e_last)
            vaug = jnp.concatenate([vt_ref[0, 0, :, pl.ds(col, tk)], ones_rows], axis=0)
            new_ms = []
            s_all = _dot(kt, q_sc[...])
            for r in range(R):
                s = s_all[:, r * tq:(r + 1) * tq] + b_ref[r, bi]
                m_new = jnp.maximum(ms[r], jnp.max(s, axis=0, keepdims=True))
                alpha = jnp.exp2(ms[r] - m_new)
                p = jnp.exp2(s - m_new).astype(BF16)
                acc_sc[r] = acc_sc[r] * alpha + _dot(vaug, p)
                new_ms.append(m_new)
            return tuple(new_ms)

        lax.fori_loop(0, hi, body, tuple(jnp.full((1, tq), NEG, F32) for _ in range(R)))
        outs = []
        for r in range(R):
            acc = acc_sc[r]
            outs.append(acc[:dh] * (1.0 / acc[dh:dh + 1]))
        return outs

    o_sel = sweep(ksa_ref, vst_ref, qsel_sc, bs_ref, n_tiles, e_sat)
    o_win = sweep(kw_ref, vwt_ref, qwin_sc, bw_ref, jnp.minimum(n_tiles, n_wtiles), n_wtiles - 1)

    gates = _sigmoid(lax.dot_general(pgt_ref[0], gl_ref[0], (((1,), (1,)), ((), ())),
                                     preferred_element_type=F32, precision=HIGHEST))
    outs = []
    for r in range(R):
        outs.append(gates[3 * r:3 * r + 1] * o_cmp[r]
                    + gates[3 * r + 1:3 * r + 2] * o_sel[r]
                    + gates[3 * r + 2:3 * r + 3] * o_win[r])
    o_ref[0] = jnp.concatenate(outs, axis=0).T.astype(o_ref.dtype)


def _t5_bucket(dist):
    n = jnp.maximum(dist, 0)
    max_exact = REL_BUCKETS // 2
    nf = jnp.maximum(n, 1).astype(F32)
    large = max_exact + (jnp.log(nf / max_exact) / math.log(REL_MAX_DISTANCE / max_exact)
                         * (REL_BUCKETS - max_exact)).astype(jnp.int32)
    large = jnp.minimum(large, REL_BUCKETS - 1)
    return jnp.where(n < max_exact, n, large)


def _bias_of_dist(dist, rel_bias):
    onehot = (_t5_bucket(dist)[..., None] == jnp.arange(REL_BUCKETS)).astype(F32)
    out = jnp.einsum('...k,kh->...h', onehot, rel_bias.astype(F32), precision=HIGHEST)
    return jnp.moveaxis(out, -1, 0)


def _saturation_distance():
    max_exact = REL_BUCKETS // 2
    steps = REL_BUCKETS - max_exact
    n_sat = max_exact * (REL_MAX_DISTANCE / max_exact) ** ((steps - 1) / steps)
    return int(math.ceil(n_sat)) + 2


def nsa_attention(proj, kc, vct, ksa, vst, kw, vwt, rel_bias, q_gain):
    b, t, _ = proj.shape
    G, R, dh = NSA_KV_GROUPS, NSA_GROUP_SIZE, NSA_HEAD_DIM
    H = NSA_HEADS
    tq, tk = ATT_TQ, ATT_TK
    rr = tq // tk
    ncp = kc.shape[2]
    n_sel_blocks = t // SEL_BLOCK
    n_select = min(N_SELECT, n_sel_blocks)
    assert n_sel_blocks <= LANES and n_sel_blocks % 8 == 0 and tq % tk == 0 and t % tq == 0

    e_sat = -(-(_saturation_distance() + tk - 1) // tk) + rr - 1
    n_wtiles = -(-(WINDOW + tk - 1) // tk) + rr - 1
    jj = np.arange(tk)[:, None]
    ii = np.arange(tq)[None, :]

    def tile_dist(n_e):
        return (np.arange(n_e)[:, None, None] - (rr - 1)) * tk + (ii - jj)[None]

    dist = tile_dist(e_sat + 1)
    bias_sel = jnp.where(dist >= 0, _bias_of_dist(jnp.asarray(dist), rel_bias) * LOG2E, NEG)
    dwin = tile_dist(n_wtiles)
    bias_win = jnp.where((dwin >= 0) & (dwin < WINDOW),
                         _bias_of_dist(jnp.asarray(dwin), rel_bias) * LOG2E, NEG)
    dc = np.arange(t)[None, :] - (np.arange(ncp)[:, None] * CMP_STRIDE + CMP_BLOCK - 1)
    bias_c = _bias_of_dist(jnp.asarray(dc), rel_bias) * LOG2E

    cs = np.arange(ncp) * CMP_STRIDE
    ss = np.arange(n_sel_blocks) * SEL_BLOCK
    shared = (np.minimum(cs[None, :] + CMP_BLOCK, ss[:, None] + SEL_BLOCK)
              - np.maximum(cs[None, :], ss[:, None]))
    c2st = jnp.asarray(np.clip(shared, 0, None) / CMP_BLOCK, F32)
    pgt = np.zeros((G, LANES, LANES), np.float32)
    for g in range(G):
        for k in range(3 * R):
            pgt[g, k, 3 * R * g + k] = 1.0
    pgt = jnp.asarray(pgt)
    gate_blk = (H * dh + 6 * G * dh) // LANES
    qg = jnp.broadcast_to(q_gain.reshape(dh, 1), (dh, tq))

    kernel = functools.partial(_nsa_attn_kernel, e_sat=e_sat, n_wtiles=n_wtiles,
                               n_sel_blocks=n_sel_blocks, n_select=n_select)
    return pl.pallas_call(
        kernel,
        grid=(b, G, t // tq),
        in_specs=[
            pl.BlockSpec((1, tq, R * dh), lambda bi, g, i: (bi, i, g)),
            pl.BlockSpec((1, tq, LANES), lambda bi, g, i: (bi, i, gate_blk)),
            pl.BlockSpec((1, LANES, LANES), lambda bi, g, i: (g, 0, 0)),
            pl.BlockSpec((dh, tq), lambda bi, g, i: (0, 0)),
            pl.BlockSpec((1, 1, ncp, dh), lambda bi, g, i: (bi, g, 0, 0)),
            pl.BlockSpec((1, 1, dh, ncp), lambda bi, g, i: (bi, g, 0, 0)),
            pl.BlockSpec((1, 1, t, LANES + dh), lambda bi, g, i: (bi, g, 0, 0)),
            pl.BlockSpec((1, 1, dh, t), lambda bi, g, i: (bi, g, 0, 0)),
            pl.BlockSpec((1, 1, t, dh), lambda bi, g, i: (bi, g, 0, 0)),
            pl.BlockSpec((1, 1, dh, t), lambda bi, g, i: (bi, g, 0, 0)),
            pl.BlockSpec((R, ncp, tq), lambda bi, g, i: (g, 0, i)),
            pl.BlockSpec((R, e_sat + 1, tk, tq), lambda bi, g, i: (g, 0, 0, 0)),
            pl.BlockSpec((R, n_wtiles, tk, tq), lambda bi, g, i: (g, 0, 0, 0)),
            pl.BlockSpec((n_sel_blocks, ncp), lambda bi, g, i: (0, 0)),
        ],
        out_specs=pl.BlockSpec((1, tq, R * dh), lambda bi, g, i: (bi, i, g)),
        out_shape=jax.ShapeDtypeStruct((b, t, H * dh), BF16),
        scratch_shapes=[pltpu.VMEM((R, dh + 16, tq), F32),
                        pltpu.VMEM((LANES + dh, R * tq), BF16),
                        pltpu.VMEM((dh, R * tq), BF16)],
        compiler_params=_cparams("parallel", "parallel", "arbitrary"),
        name="nsa_attention",
    )(proj, proj, pgt, qg, kc, vct, ksa, vst, kw, vwt, bias_c, bias_sel, bias_win, c2st)


def nsa_layer(x, shift, scale, gate, norm_g, rel_bias, w_in, q_gain, k_gain,
              cmp_pe, cmp_w1, cmp_w2, w_out):
    d = x.shape[-1]
    n_in = w_in.shape[1]
    n_pad = -(-n_in // 1024) * 1024
    w_in_p = jnp.pad(w_in, ((0, 0), (0, n_pad - n_in))).astype(BF16)
    proj = norm_matmul(x, norm_g, shift, scale, w_in_p)
    kc, vct, ksa, vst, kw, vwt = nsa_prep(proj, k_gain, cmp_pe, cmp_w1, cmp_w2)
    o = nsa_attention(proj, kc, vct, ksa, vst, kw, vwt, rel_bias, q_gain)
    return out_proj_residual(o, w_out.astype(BF16), x, gate)


def _hgrn_kernel(q_ref, f_ref, v_ref, g_ref, lb_ref, og_ref, tri_ref, ones_ref,
                 o_ref, st_sc):
    C = HG_CHUNK
    dk = HGRN_DK

    @pl.when(pl.program_id(2) == 0)
    def _():
        st_sc[...] = jnp.zeros(st_sc.shape, F32)

    q = _silu(q_ref[0])
    fl = f_ref[0]
    v = v_ref[0]
    lb = lb_ref[0]
    log_sig = jnp.minimum(fl, 0.0) - jnp.log(1.0 + jnp.exp(-jnp.abs(fl)))
    ta = jnp.log(lb)
    tb = jnp.log1p(-lb) + log_sig
    lf = jnp.maximum(ta, tb) + jnp.log(1.0 + jnp.exp(-jnp.abs(ta - tb)))
    k = 1.0 - jnp.exp(lf)
    cum = _dot_exact(tri_ref[...], lf)

    row = lax.broadcasted_iota(jnp.int32, (C, C), 0)
    colm = lax.broadcasted_iota(jnp.int32, (C, C), 1)
    rloc = lax.broadcasted_iota(jnp.int32, (C, dk), 0)

    n_sub = C // HG_SUB
    k3 = k.reshape(n_sub, HG_SUB, dk)
    c3 = cum.reshape(n_sub, HG_SUB, dk)
    tloc = rloc % HG_SUB
    pieces = []
    for j in range(HG_SUB):
        kj = jnp.broadcast_to(k3[:, j:j + 1, :], (n_sub, HG_SUB, dk)).reshape(C, dk)
        cj = jnp.broadcast_to(c3[:, j:j + 1, :], (n_sub, HG_SUB, dk)).reshape(C, dk)
        arg = jnp.where(tloc >= j, cum - cj, NEG)
        pieces.append((q * kj * jnp.exp(arg)).astype(BF16))
    xcat = jnp.concatenate(pieces, axis=1)
    attn = jnp.where((row // HG_SUB) == (colm // HG_SUB), _dot(xcat, ones_ref[...]), 0.0)

    m = HG_SUB
    while m < C:
        nb = C // (2 * m)
        anchor = jnp.broadcast_to(cum.reshape(nb, 2 * m, dk)[:, m - 1:m, :],
                                  (nb, 2 * m, dk)).reshape(C, dk)
        e = jnp.exp(-jnp.abs(cum - anchor))
        upper = (rloc // m) % 2 == 1
        qm = jnp.where(upper, q * e, 0.0).astype(BF16)
        km = jnp.where(upper, 0.0, k * e).astype(BF16)
        attn = attn + jnp.where((row // (2 * m)) == (colm // (2 * m)), _dot_nt(qm, km), 0.0)
        m *= 2

    st = st_sc[...]
    o = _dot(attn.astype(BF16), v.astype(BF16))
    o = o + _dot_nt((q * jnp.exp(cum)).astype(BF16), st.astype(BF16))
    total = cum[C - 1:C, :]
    kd = (k * jnp.exp(total - cum)).astype(BF16)
    st_sc[...] = st * jnp.exp(total) + lax.dot_general(
        v.astype(BF16), kd, (((0,), (0,)), ((), ())), preferred_element_type=F32)

    ms = jnp.mean(o * o, axis=-1, keepdims=True)
    o = o * lax.rsqrt(ms + NORM_EPS) * og_ref[...]
    o_ref[0] = (o * _silu(g_ref[0])).astype(o_ref.dtype)


def hgrn_recurrence(proj, lb, out_gain):
    b, t, four_d = proj.shape
    d = four_d // 4
    dk = HGRN_DK
    nh = d // dk
    C = HG_CHUNK
    tri = jnp.asarray(np.tril(np.ones((C, C), np.float32)))
    ones = jnp.asarray(np.arange(HG_SUB * dk)[:, None] // dk == (np.arange(C)[None, :] % HG_SUB), BF16)
    return pl.pallas_call(
        _hgrn_kernel,
        grid=(b, nh, t // C),
        in_specs=[
            pl.BlockSpec((1, C, dk), lambda bi, h, c: (bi, c, h)),
            pl.BlockSpec((1, C, dk), lambda bi, h, c: (bi, c, nh + h)),
            pl.BlockSpec((1, C, dk), lambda bi, h, c: (bi, c, 2 * nh + h)),
            pl.BlockSpec((1, C, dk), lambda bi, h, c: (bi, c, 3 * nh + h)),
            pl.BlockSpec((1, 1, dk), lambda bi, h, c: (h, 0, 0)),
            pl.BlockSpec((1, dk), lambda bi, h, c: (0, 0)),
            pl.BlockSpec((C, C), lambda bi, h, c: (0, 0)),
            pl.BlockSpec((HG_SUB * dk, C), lambda bi, h, c: (0, 0)),
        ],
        out_specs=pl.BlockSpec((1, C, dk), lambda bi, h, c: (bi, c, h)),
        out_shape=jax.ShapeDtypeStruct((b, t, d), BF16),
        scratch_shapes=[pltpu.VMEM((dk, dk), F32)],
        compiler_params=_cparams("parallel", "parallel", "arbitrary"),
        name="hgrn_recurrence",
    )(proj, proj, proj, proj, lb.reshape(nh, 1, dk), out_gain.reshape(1, dk), tri, ones)


def hgrn_layer(x, shift, scale, gate, norm_g, lb, w_in, out_gain, w_out):
    proj = norm_matmul(x, norm_g, shift, scale, w_in.astype(BF16))
    o = hgrn_recurrence(proj, lb, out_gain)
    return out_proj_residual(o, w_out.astype(BF16), x, gate)


def _router_kernel(x_ref, g_ref, sh_ref, sc_ref, wr_ref, h_ref, eid_ref, ew_ref):
    h = _modulated_norm(x_ref[0], g_ref[...], sh_ref[0], sc_ref[0])
    h_ref[0] = h.astype(BF16)
    lt = _dot_exact(h, wr_ref[...]).T
    NG, EPG = MOE_GROUPS, MOE_EPG

    def softmax_rows(rows):
        mx = functools.reduce(jnp.maximum, rows)
        es = [jnp.exp(r - mx) for r in rows]
        tot = functools.reduce(lambda a, c: a + c, es)
        return [e / tot for e in es]

    def argmax_rows(rows):
        best, idx = rows[0], jnp.zeros(rows[0].shape, jnp.int32)
        for i in range(1, len(rows)):
            better = rows[i] > best
            best = jnp.where(better, rows[i], best)
            idx = jnp.where(better, i, idx)
        return best, idx

    pg = softmax_rows([lt[i:i + 1] for i in range(NG)])
    p_grp, grp = argmax_rows(pg)
    el = []
    for j in range(EPG):
        acc = lt[NG + j:NG + j + 1]
        for gi in range(1, NG):
            acc = jnp.where(grp == gi, lt[NG + gi * EPG + j:NG + gi * EPG + j + 1], acc)
        el.append(acc)
    pe = softmax_rows(el)
    p1, i1 = argmax_rows(pe)
    p2, i2 = argmax_rows([jnp.where(i1 == j, -1.0, pe[j]) for j in range(EPG)])
    den = p1 + p2
    eid_ref[0:1, :] = grp * EPG + i1
    eid_ref[1:2, :] = grp * EPG + i2
    ew_ref[0:1, :] = p_grp * p1 / den
    ew_ref[1:2, :] = p_grp * p2 / den


def moe_router(x, gain, shift, scale, w_group, w_expert, tm=512):
    b, t, d = x.shape
    tm = min(tm, t)
    nt = t // tm
    wr = jnp.concatenate([w_group, w_expert], axis=1)
    wr = jnp.pad(wr, ((0, 0), (0, LANES - wr.shape[1])))
    return pl.pallas_call(
        _router_kernel,
        grid=(b, nt),
        in_specs=[
            pl.BlockSpec((1, tm, d), lambda bi, i: (bi, i, 0)),
            pl.BlockSpec((1, d), lambda bi, i: (0, 0)),
            pl.BlockSpec((1, 1, d), lambda bi, i: (bi, 0, 0)),
            pl.BlockSpec((1, 1, d), lambda bi, i: (bi, 0, 0)),
            pl.BlockSpec((d, LANES), lambda bi, i: (0, 0)),
        ],
        out_specs=[
            pl.BlockSpec((1, tm, d), lambda bi, i: (bi, i, 0)),
            pl.BlockSpec((MOE_TOP_K, tm), lambda bi, i: (0, bi * nt + i)),
            pl.BlockSpec((MOE_TOP_K, tm), lambda bi, i: (0, bi * nt + i)),
        ],
        out_shape=[
            jax.ShapeDtypeStruct((b, t, d), BF16),
            jax.ShapeDtypeStruct((MOE_TOP_K, b * t), jnp.int32),
            jax.ShapeDtypeStruct((MOE_TOP_K, b * t), F32),
        ],
        compiler_params=_cparams("parallel", "parallel"),
        name="moe_router",
    )(x, gain.reshape(1, d), shift.reshape(b, 1, d), scale.reshape(b, 1, d), wr)


def _expert_ffn_kernel(be_ref, xb_ref, w1_ref, w3_ref, w2_ref, yb_ref):
    del be_ref
    xb = xb_ref[...]
    a = _dot(xb, w1_ref[0])
    g = _dot(xb, w3_ref[0])
    yb_ref[...] = _dot((_silu(a) * g).astype(BF16), w2_ref[0])


def expert_ffn(xb, blk_e, w1, w3, w2):
    p, d = xb.shape
    ff = w1.shape[2]
    rb = MOE_ROW_BLOCK
    grid_spec = pltpu.PrefetchScalarGridSpec(
        num_scalar_prefetch=1,
        grid=(p // rb,),
        in_specs=[
            pl.BlockSpec((rb, d), lambda i, be: (i, 0)),
            pl.BlockSpec((1, d, ff), lambda i, be: (be[i], 0, 0)),
            pl.BlockSpec((1, d, ff), lambda i, be: (be[i], 0, 0)),
            pl.BlockSpec((1, ff, d), lambda i, be: (be[i], 0, 0)),
        ],
        out_specs=pl.BlockSpec((rb, d), lambda i, be: (i, 0)),
    )
    return pl.pallas_call(
        _expert_ffn_kernel,
        grid_spec=grid_spec,
        out_shape=jax.ShapeDtypeStruct((p, d), F32),
        compiler_params=_cparams("arbitrary"),
        name="expert_ffn",
    )(blk_e, xb, w1, w3, w2)


def _combine_kernel(x_ref, gt_ref, y1_ref, y2_ref, w_ref, o_ref):
    w = w_ref[0]
    y = w[:, 0:1] * y1_ref[0] + w[:, 1:2] * y2_ref[0]
    o_ref[0] = x_ref[0] + gt_ref[0] * y


def moe_combine(x, gate, y1, y2, w, tm=512):
    b, t, d = x.shape
    tm = min(tm, t)
    spec = pl.BlockSpec((1, tm, d), lambda bi, i: (bi, i, 0))
    return pl.pallas_call(
        _combine_kernel,
        grid=(b, t // tm),
        in_specs=[spec, pl.BlockSpec((1, 1, d), lambda bi, i: (bi, 0, 0)), spec, spec,
                  pl.BlockSpec((1, tm, MOE_TOP_K), lambda bi, i: (bi, i, 0))],
        out_specs=spec,
        out_shape=jax.ShapeDtypeStruct((b, t, d), F32),
        compiler_params=_cparams("parallel", "parallel"),
        name="moe_combine",
    )(x, gate.reshape(b, 1, d), y1, y2, w)


def moe_layer(x, shift, scale, gate, norm_g, w_group, w_expert, w1, w3, w2):
    b, t, d = x.shape
    n = b * t
    a = n * MOE_TOP_K
    rb = MOE_ROW_BLOCK
    h, eid, ew = moe_router(x, norm_g, shift, scale, w_group, w_expert)
    flat_e = eid.T.reshape(a)
    onehot = (flat_e[:, None] == jnp.arange(MOE_EXPERTS, dtype=jnp.int32)).astype(jnp.int32)
    incl = jnp.cumsum(onehot, axis=0)
    counts = incl[-1]
    pos = jnp.sum((incl - onehot) * onehot, axis=1)
    padded = (counts + rb - 1) // rb * rb
    pad_end = jnp.cumsum(padded)
    pad_start = pad_end - padded
    dest = pad_start[flat_e] + pos
    n_blk = -(-(a + MOE_EXPERTS * (rb - 1)) // rb)
    p = n_blk * rb
    flat_t = jnp.arange(a, dtype=jnp.int32) // MOE_TOP_K
    buf_t = jnp.full((p,), n, jnp.int32).at[dest].set(flat_t)
    blk_e = jnp.minimum(jnp.searchsorted(pad_end, jnp.arange(n_blk) * rb, side='right'),
                        MOE_EXPERTS - 1).astype(jnp.int32)
    h_pad = jnp.concatenate([h.reshape(n, d), jnp.zeros((1, d), BF16)], axis=0)
    xb = h_pad[buf_t]
    yb = expert_ffn(xb, blk_e, w1.astype(BF16), w3.astype(BF16), w2.astype(BF16))
    ysel = yb[dest].reshape(b, t, MOE_TOP_K, d)
    return moe_combine(x, gate, ysel[:, :, 0], ysel[:, :, 1], ew.T.reshape(b, t, MOE_TOP_K))


def kernel(x, c, ada_w, ada_b, norm_g, rel_bias, nsa_w_in, nsa_q_gain, nsa_k_gain, nsa_cmp_pe,
           nsa_cmp_w1, nsa_cmp_w2, nsa_w_out, hgrn_w_in, hgrn_lower_bounds, hgrn_out_gain,
           hgrn_w_out, moe_router_group, moe_router_expert, moe_w1, moe_w3, moe_w2):
    depth = ada_w.shape[0]
    d = x.shape[-1]
    lb_soft = jax.nn.softmax(hgrn_lower_bounds.astype(F32), axis=0)
    lb_all = jnp.cumsum(lb_soft, axis=0) - lb_soft[0]
    mod = adaln_mod(c, ada_w, ada_b)
    for layer in range(depth):
        j = layer // 2
        shift, scale, gate = (mod[layer, 0, :, i * d:(i + 1) * d] for i in range(3))
        if layer % 2 == 0:
            x = nsa_layer(x, shift, scale, gate, norm_g[layer, 0], rel_bias, nsa_w_in[j],
                          nsa_q_gain[j], nsa_k_gain[j], nsa_cmp_pe[j], nsa_cmp_w1[j],
                          nsa_cmp_w2[j], nsa_w_out[j])
        else:
            x = hgrn_layer(x, shift, scale, gate, norm_g[layer, 0], lb_all[layer],
                           hgrn_w_in[j], hgrn_out_gain[j], hgrn_w_out[j])
        shift, scale, gate = (mod[layer, 1, :, i * d:(i + 1) * d] for i in range(3))
        x = moe_layer(x, shift, scale, gate, norm_g[layer, 1], moe_router_group[layer],
                      moe_router_expert[layer], moe_w1[layer], moe_w3[layer], moe_w2[layer])
    return x
```

```python
import functools
import math

import numpy as np
import jax
import jax.numpy as jnp
from jax import lax
from jax.experimental import pallas as pl
from jax.experimental.pallas import tpu as pltpu

F32 = jnp.float32
BF16 = jnp.bfloat16
HIGHEST = lax.Precision.HIGHEST

NSA_HEADS = 16
NSA_KV_GROUPS = 4
NSA_GROUP_SIZE = NSA_HEADS // NSA_KV_GROUPS
NSA_HEAD_DIM = 64
CMP_BLOCK = 32
CMP_STRIDE = 16
SEL_BLOCK = 64
N_SELECT = 8
WINDOW = 512
FORCED_SCORE = 1.0e4
REL_BUCKETS = 32
REL_MAX_DISTANCE = 1024
HGRN_DK = 128
MOE_GROUPS = 4
MOE_EPG = 4
MOE_EXPERTS = MOE_GROUPS * MOE_EPG
MOE_TOP_K = 2
MOE_ROW_BLOCK = 512
NORM_EPS = 1e-6

LANES = 128
NEG = -1.0e30
LOG2E = math.log2(math.e)
VMEM_LIMIT = 48 * 1024 * 1024

ATT_TQ = 256
ATT_TK = 256
HG_CHUNK = 128
HG_SUB = 8
HG_HEADS = 4


def _cparams(*sem):
    return pltpu.CompilerParams(dimension_semantics=sem, vmem_limit_bytes=VMEM_LIMIT)


def _dot(a, b):
    return jnp.dot(a, b, preferred_element_type=F32)


def _dot_exact(a, b):
    return jnp.dot(a, b, preferred_element_type=F32, precision=HIGHEST)


def _dot_lhs_exact(a, x):
    x1 = x.astype(BF16)
    r1 = x - x1.astype(F32)
    x2 = r1.astype(BF16)
    x3 = (r1 - x2.astype(F32)).astype(BF16)
    return _dot(a, x1) + _dot(a, x2) + _dot(a, x3)


def _dot_nt(a, b):
    return lax.dot_general(a, b, (((1,), (1,)), ((), ())), preferred_element_type=F32)


def _sigmoid(x):
    return 0.5 * jnp.tanh(0.5 * x) + 0.5


def _silu(x):
    return x * _sigmoid(x)


def _adaln_kernel(c_ref, w_ref, b_ref, o_ref):
    cond = _silu(c_ref[...])
    o_ref[0] = _dot_exact(cond, w_ref[0]) + b_ref[0]


def adaln_mod(c, ada_w, ada_b):
    depth, two, d, d3 = ada_w.shape
    b = c.shape[0]
    ls = depth * two
    tn = 1024
    out = pl.pallas_call(
        _adaln_kernel,
        grid=(ls, d3 // tn),
        in_specs=[
            pl.BlockSpec((b, d), lambda i, j: (0, 0)),
            pl.BlockSpec((1, d, tn), lambda i, j: (i, 0, j)),
            pl.BlockSpec((1, 1, tn), lambda i, j: (i, 0, j)),
        ],
        out_specs=pl.BlockSpec((1, b, tn), lambda i, j: (i, 0, j)),
        out_shape=jax.ShapeDtypeStruct((ls, b, d3), F32),
        compiler_params=_cparams("parallel", "parallel"),
        name="adaln_mod",
    )(c, ada_w.reshape(ls, d, d3), ada_b.reshape(ls, 1, d3))
    return out.reshape(depth, two, b, d3)


def _modulated_norm(x, gain, shift, scale):
    ms = jnp.mean(x * x, axis=-1, keepdims=True)
    y = x * lax.rsqrt(ms + NORM_EPS) * gain
    return y * (1.0 + scale) + shift


def _norm_matmul_kernel(x_ref, g_ref, sh_ref, sc_ref, w_ref, o_ref, h_sc):
    @pl.when(pl.program_id(2) == 0)
    def _():
        h = _modulated_norm(x_ref[0], g_ref[...], sh_ref[0], sc_ref[0])
        h_sc[...] = h.astype(BF16)

    o_ref[0] = _dot(h_sc[...], w_ref[...])


def norm_matmul(x, gain, shift, scale, w, tm=512, tn=1024):
    b, t, d = x.shape
    n = w.shape[1]
    tm = min(tm, t)
    return pl.pallas_call(
        _norm_matmul_kernel,
        grid=(b, t // tm, n // tn),
        in_specs=[
            pl.BlockSpec((1, tm, d), lambda bi, i, j: (bi, i, 0)),
            pl.BlockSpec((1, d), lambda bi, i, j: (0, 0)),
            pl.BlockSpec((1, 1, d), lambda bi, i, j: (bi, 0, 0)),
            pl.BlockSpec((1, 1, d), lambda bi, i, j: (bi, 0, 0)),
            pl.BlockSpec((d, tn), lambda bi, i, j: (0, j)),
        ],
        out_specs=pl.BlockSpec((1, tm, tn), lambda bi, i, j: (bi, i, j)),
        out_shape=jax.ShapeDtypeStruct((b, t, n), F32),
        scratch_shapes=[pltpu.VMEM((tm, d), BF16)],
        compiler_params=_cparams("parallel", "parallel", "arbitrary"),
        name="norm_matmul",
    )(x, gain.reshape(1, d), shift.reshape(b, 1, d), scale.reshape(b, 1, d), w)


def _out_proj_kernel(o_ref, w_ref, x_ref, gt_ref, y_ref):
    y = _dot(o_ref[0], w_ref[...])
    y_ref[0] = x_ref[0] + gt_ref[0] * y


def out_proj_residual(o, w, x, gate, tm=512):
    b, t, d = x.shape
    k = o.shape[-1]
    tm = min(tm, t)
    return pl.pallas_call(
        _out_proj_kernel,
        grid=(b, t // tm),
        in_specs=[
            pl.BlockSpec((1, tm, k), lambda bi, i: (bi, i, 0)),
            pl.BlockSpec((k, d), lambda bi, i: (0, 0)),
            pl.BlockSpec((1, tm, d), lambda bi, i: (bi, i, 0)),
            pl.BlockSpec((1, 1, d), lambda bi, i: (bi, 0, 0)),
        ],
        out_specs=pl.BlockSpec((1, tm, d), lambda bi, i: (bi, i, 0)),
        out_shape=jax.ShapeDtypeStruct((b, t, d), F32),
        compiler_params=_cparams("parallel", "parallel"),
        name="out_proj_residual",
    )(o, w, x, gate.reshape(b, 1, d))


def _pair_rms(x, gain2):
    lane = lax.broadcasted_iota(jnp.int32, x.shape, 1)
    lo = lane < NSA_HEAD_DIM
    x2 = x * x
    s_lo = jnp.sum(jnp.where(lo, x2, 0.0), axis=-1, keepdims=True)
    s_hi = jnp.sum(jnp.where(lo, 0.0, x2), axis=-1, keepdims=True)
    inv = jnp.where(lo, lax.rsqrt(s_lo / NSA_HEAD_DIM + NORM_EPS),
                    lax.rsqrt(s_hi / NSA_HEAD_DIM + NORM_EPS))
    return x * inv * gain2


def _nsa_prep_kernel(kc0_ref, kc1_ref, vc0_ref, vc1_ref, ks_ref, vs_ref, kw_ref, vw_ref,
                     kg_ref, pe_ref, w1_ref, w1big_ref, w2big_ref, ex_ref,
                     kco_ref, vct_ref, kso_ref, vst_ref, kwo_ref, vwt_ref):
    t = ks_ref.shape[1]
    nblk = t // CMP_STRIDE
    dh = NSA_HEAD_DIM
    G = NSA_KV_GROUPS

    for src, dst, gi, off in ((ks_ref, kso_ref, 1, LANES), (kw_ref, kwo_ref, 2, 0)):
        g2 = kg_ref[gi:gi + 1, :]
        for gp in range(G // 2):
            kn = _pair_rms(src[0, :, gp * LANES:(gp + 1) * LANES], g2).astype(BF16)
            dst[0, 2 * gp, :, off:off + dh] = kn[:, :dh]
            dst[0, 2 * gp + 1, :, off:off + dh] = kn[:, dh:]
    for g in range(G):
        kso_ref[0, g, :, :LANES] = ex_ref[...]
    for src, dst in ((vs_ref, vst_ref), (vw_ref, vwt_ref)):
        for gp in range(G // 2):
            vt = src[0, :, gp * LANES:(gp + 1) * LANES].T.astype(BF16)
            dst[0, 2 * gp] = vt[:dh]
            dst[0, 2 * gp + 1] = vt[dh:]

    for ci, (srcs, is_key) in enumerate((((kc0_ref, kc1_ref), True), ((vc0_ref, vc1_ref), False))):
        pe1 = _dot(pe_ref[ci].astype(BF16), w1_ref[ci])[0:1]
        pe2 = jnp.concatenate([pe1, pe1], axis=1)
        for gp, src in enumerate(srcs):
            parts = [src[0, pl.ds(l, nblk, stride=CMP_STRIDE), :] for l in range(CMP_STRIDE)]
            r = jnp.concatenate(parts, axis=1).astype(BF16)
            ab = _dot(r, w1big_ref[ci])
            second = ab[:, LANES:]
            shifted = jnp.concatenate([second[1:], jnp.zeros((1, LANES), F32)], axis=0)
            pre = ab[:, :LANES] + shifted + pe2
            hid = _dot(_silu(pre).astype(BF16), w2big_ref[ci])
            if is_key:
                kn = _pair_rms(hid, kg_ref[0:1, :]).astype(BF16)
                kco_ref[0, 2 * gp] = kn[:, :dh]
                kco_ref[0, 2 * gp + 1] = kn[:, dh:]
            else:
                vt = hid.T.astype(BF16)
                vct_ref[0, 2 * gp] = vt[:dh]
                vct_ref[0, 2 * gp + 1] = vt[dh:]


def nsa_prep(proj, k_gain, cmp_pe, cmp_w1, cmp_w2):
    b, t, _ = proj.shape
    G, dh = NSA_KV_GROUPS, NSA_HEAD_DIM
    kvw = G * dh
    q_blocks = (NSA_HEADS * dh) // kvw
    nblk = t // CMP_STRIDE
    kg2 = jnp.concatenate([k_gain, k_gain], axis=1)
    pe_flat = jnp.broadcast_to(cmp_pe.reshape(2, 1, CMP_BLOCK * dh), (2, 8, CMP_BLOCK * dh))

    w1r = cmp_w1.reshape(2, 2, CMP_STRIDE, dh, dh)
    eye2 = jnp.eye(2, dtype=cmp_w1.dtype)
    w1big = jnp.einsum('chlde,gk->clgdhke', w1r, eye2).reshape(2, CMP_STRIDE * 2 * dh, 4 * dh)
    w2big = jnp.einsum('cde,gk->cgdke', cmp_w2, eye2).reshape(2, 2 * dh, 2 * dh)
    expand = jnp.asarray((np.arange(t)[:, None] // SEL_BLOCK) == np.arange(LANES)[None, :], BF16)

    def col(i):
        return pl.BlockSpec((1, t, kvw), lambda bi, i=i: (bi, 0, q_blocks + i))

    def col_pair(i, gp):
        return pl.BlockSpec((1, t, LANES), lambda bi: (bi, 0, (q_blocks + i) * (kvw // LANES) + gp))

    def full(shape):
        return pl.BlockSpec(shape, lambda bi: (0,) * len(shape))

    def per_b(shape):
        return pl.BlockSpec((1,) + shape, lambda bi: (bi,) + (0,) * len(shape))

    return pl.pallas_call(
        _nsa_prep_kernel,
        grid=(b,),
        in_specs=[col_pair(0, 0), col_pair(0, 1), col_pair(1, 0), col_pair(1, 1),
                  col(2), col(3), col(4), col(5),
                  full((3, 2 * dh)), full((2, 8, CMP_BLOCK * dh)),
                  full((2, CMP_BLOCK * dh, dh)), full((2, 2 * CMP_STRIDE * dh, 4 * dh)),
                  full((2, 2 * dh, 2 * dh)), full((t, LANES))],
        out_specs=[per_b((G, nblk, dh)), per_b((G, dh, nblk)),
                   per_b((G, t, LANES + dh)), per_b((G, dh, t)),
                   per_b((G, t, dh)), per_b((G, dh, t))],
        out_shape=[jax.ShapeDtypeStruct((b, G, nblk, dh), BF16),
                   jax.ShapeDtypeStruct((b, G, dh, nblk), BF16),
                   jax.ShapeDtypeStruct((b, G, t, LANES + dh), BF16),
                   jax.ShapeDtypeStruct((b, G, dh, t), BF16),
                   jax.ShapeDtypeStruct((b, G, t, dh), BF16),
                   jax.ShapeDtypeStruct((b, G, dh, t), BF16)],
        compiler_params=_cparams("parallel"),
        name="nsa_prep",
    )(proj, proj, proj, proj, proj, proj, proj, proj, kg2, pe_flat,
      cmp_w1.astype(BF16), w1big.astype(BF16), w2big.astype(BF16), expand)


def _nsa_attn_kernel(q_ref, gl_ref, pgt_ref, qg_ref, kc_ref, vct_ref, ksa_ref, vst_ref,
                     kw_ref, vwt_ref, bc_ref, bs_ref, bw_ref, c2st_ref,
                     o_ref, acc_sc, qsel_sc, qwin_sc, *, e_sat, n_wtiles, n_sel_blocks, n_select):
    tq, tk = ATT_TQ, ATT_TK
    R, dh = NSA_GROUP_SIZE, NSA_HEAD_DIM
    qi = pl.program_id(2)
    scale = dh ** -0.5 * LOG2E

    qt = q_ref[0].T
    for r in range(R):
        qr = qt[r * dh:(r + 1) * dh]
        ms = jnp.mean(qr * qr, axis=0, keepdims=True)
        qn = (qr * lax.rsqrt(ms + NORM_EPS) * qg_ref[...] * scale).astype(BF16)
        qwin_sc[:, r * tq:(r + 1) * tq] = qn
        qsel_sc[LANES:, r * tq:(r + 1) * tq] = qn

    ncp = kc_ref.shape[2]
    n_idx = lax.broadcasted_iota(jnp.int32, (ncp, tq), 0)
    t_idx = qi * tq + lax.broadcasted_iota(jnp.int32, (ncp, tq), 1)
    vis = (n_idx * CMP_STRIDE + (CMP_BLOCK - 1)) <= t_idx
    psum = jnp.zeros((ncp, tq), F32)
    o_cmp = []
    for r in range(R):
        sc = _dot(kc_ref[0, 0], qwin_sc[:, r * tq:(r + 1) * tq])
        s = jnp.where(vis, sc + bc_ref[r], NEG)
        m = jnp.max(s, axis=0, keepdims=True)
        e = jnp.where(vis, jnp.exp2(s - m), 0.0)
        p = e * (1.0 / jnp.maximum(jnp.sum(e, axis=0, keepdims=True), 1e-30))
        psum = psum + p
        o_cmp.append(_dot(vct_ref[0, 0], p.astype(BF16)))

    imp_t = _dot_exact(c2st_ref[...], psum)
    blk = lax.broadcasted_iota(jnp.int32, (n_sel_blocks, tq), 0)
    tpos = qi * tq + lax.broadcasted_iota(jnp.int32, (n_sel_blocks, tq), 1)
    cur = tpos // SEL_BLOCK
    forced = (blk == 0) | (blk == cur) | (blk == cur - 1)
    score = jnp.where(forced, FORCED_SCORE, jnp.where(blk <= cur, imp_t, -1.0))
    rank = jnp.zeros((n_sel_blocks, tq), F32)
    for s2 in range(n_sel_blocks):
        row = score[s2:s2 + 1, :]
        beats = (row > score) | ((row == score) & (blk > s2))
        rank = rank + jnp.where(beats, 1.0, 0.0)
    negsel = jnp.where(rank < n_select, 0.0, NEG)
    if n_sel_blocks < LANES:
        negsel = jnp.concatenate([negsel, jnp.zeros((LANES - n_sel_blocks, tq), F32)], axis=0)
    negsel = negsel.astype(BF16)
    for r in range(R):
        qsel_sc[:LANES, r * tq:(r + 1) * tq] = negsel

    ones_rows = jnp.ones((16, tk), BF16)
    rr = tq // tk
    n_tiles = rr * qi + rr

    def sweep(k_ref, vt_ref, q_sc, b_ref, hi, e_last):
        acc_sc[...] = jnp.zeros(acc_sc.shape, F32)

        def body(d, ms):
            e = jnp.where(d == 0, rr - 1, jnp.where(d < rr, d - 1, d))
            col = pl.multiple_of((n_tiles - 1 - e) * tk, tk)
            kt = k_ref[0, 0, pl.ds(col, tk), :]
            bi = jnp.minimum(e, e_last)
            vaug = jnp.concatenate([vt_ref[0, 0, :, pl.ds(col, tk)], ones_rows], axis=0)
            new_ms = []
            s_all = _dot(kt, q_sc[...])
            for r in range(R):
                s = s_all[:, r * tq:(r + 1) * tq] + b_ref[r, bi]
                m_new = jnp.maximum(ms[r], jnp.max(s, axis=0, keepdims=True))
                alpha = jnp.exp2(ms[r] - m_new)
                p = jnp.exp2(s - m_new).astype(BF16)
                acc_sc[r] = acc_sc[r] * alpha + _dot(vaug, p)
                new_ms.append(m_new)
            return tuple(new_ms)

        lax.fori_loop(0, hi, body, tuple(jnp.full((1, tq), NEG, F32) for _ in range(R)))
        outs = []
        for r in range(R):
            acc = acc_sc[r]
            outs.append(acc[:dh] * (1.0 / acc[dh:dh + 1]))
        return outs

    o_sel = sweep(ksa_ref, vst_ref, qsel_sc, bs_ref, n_tiles, e_sat)
    o_win = sweep(kw_ref, vwt_ref, qwin_sc, bw_ref, jnp.minimum(n_tiles, n_wtiles), n_wtiles - 1)

    gates = _sigmoid(lax.dot_general(pgt_ref[0], gl_ref[0], (((1,), (1,)), ((), ())),
                                     preferred_element_type=F32, precision=HIGHEST))
    outs = []
    for r in range(R):
        outs.append(gates[3 * r:3 * r + 1] * o_cmp[r]
                    + gates[3 * r + 1:3 * r + 2] * o_sel[r]
                    + gates[3 * r + 2:3 * r + 3] * o_win[r])
    o_ref[0] = jnp.concatenate(outs, axis=0).T.astype(o_ref.dtype)


def _t5_bucket(dist):
    n = jnp.maximum(dist, 0)
    max_exact = REL_BUCKETS // 2
    nf = jnp.maximum(n, 1).astype(F32)
    large = max_exact + (jnp.log(nf / max_exact) / math.log(REL_MAX_DISTANCE / max_exact)
                         * (REL_BUCKETS - max_exact)).astype(jnp.int32)
    large = jnp.minimum(large, REL_BUCKETS - 1)
    return jnp.where(n < max_exact, n, large)


def _bias_of_dist(dist, rel_bias):
    onehot = (_t5_bucket(dist)[..., None] == jnp.arange(REL_BUCKETS)).astype(F32)
    out = jnp.einsum('...k,kh->...h', onehot, rel_bias.astype(F32), precision=HIGHEST)
    return jnp.moveaxis(out, -1, 0)


def _saturation_distance():
    max_exact = REL_BUCKETS // 2
    steps = REL_BUCKETS - max_exact
    n_sat = max_exact * (REL_MAX_DISTANCE / max_exact) ** ((steps - 1) / steps)
    return int(math.ceil(n_sat)) + 2


def nsa_attention(proj, kc, vct, ksa, vst, kw, vwt, rel_bias, q_gain):
    b, t, _ = proj.shape
    G, R, dh = NSA_KV_GROUPS, NSA_GROUP_SIZE, NSA_HEAD_DIM
    H = NSA_HEADS
    tq, tk = ATT_TQ, ATT_TK
    rr = tq // tk
    ncp = kc.shape[2]
    n_sel_blocks = t // SEL_BLOCK
    n_select = min(N_SELECT, n_sel_blocks)
    assert n_sel_blocks <= LANES and n_sel_blocks % 8 == 0 and tq % tk == 0 and t % tq == 0

    e_sat = -(-(_saturation_distance() + tk - 1) // tk) + rr - 1
    n_wtiles = -(-(WINDOW + tk - 1) // tk) + rr - 1
    jj = np.arange(tk)[:, None]
    ii = np.arange(tq)[None, :]

    def tile_dist(n_e):
        return (np.arange(n_e)[:, None, None] - (rr - 1)) * tk + (ii - jj)[None]

    dist = tile_dist(e_sat + 1)
    bias_sel = jnp.where(dist >= 0, _bias_of_dist(jnp.asarray(dist), rel_bias) * LOG2E, NEG)
    dwin = tile_dist(n_wtiles)
    bias_win = jnp.where((dwin >= 0) & (dwin < WINDOW),
                         _bias_of_dist(jnp.asarray(dwin), rel_bias) * LOG2E, NEG)
    dc = np.arange(t)[None, :] - (np.arange(ncp)[:, None] * CMP_STRIDE + CMP_BLOCK - 1)
    bias_c = _bias_of_dist(jnp.asarray(dc), rel_bias) * LOG2E

    cs = np.arange(ncp) * CMP_STRIDE
    ss = np.arange(n_sel_blocks) * SEL_BLOCK
    shared = (np.minimum(cs[None, :] + CMP_BLOCK, ss[:, None] + SEL_BLOCK)
              - np.maximum(cs[None, :], ss[:, None]))
    c2st = jnp.asarray(np.clip(shared, 0, None) / CMP_BLOCK, F32)
    pgt = np.zeros((G, LANES, LANES), np.float32)
    for g in range(G):
        for k in range(3 * R):
            pgt[g, k, 3 * R * g + k] = 1.0
    pgt = jnp.asarray(pgt)
    gate_blk = (H * dh + 6 * G * dh) // LANES
    qg = jnp.broadcast_to(q_gain.reshape(dh, 1), (dh, tq))

    kernel = functools.partial(_nsa_attn_kernel, e_sat=e_sat, n_wtiles=n_wtiles,
                               n_sel_blocks=n_sel_blocks, n_select=n_select)
    return pl.pallas_call(
        kernel,
        grid=(b, G, t // tq),
        in_specs=[
            pl.BlockSpec((1, tq, R * dh), lambda bi, g, i: (bi, i, g)),
            pl.BlockSpec((1, tq, LANES), lambda bi, g, i: (bi, i, gate_blk)),
            pl.BlockSpec((1, LANES, LANES), lambda bi, g, i: (g, 0, 0)),
            pl.BlockSpec((dh, tq), lambda bi, g, i: (0, 0)),
            pl.BlockSpec((1, 1, ncp, dh), lambda bi, g, i: (bi, g, 0, 0)),
            pl.BlockSpec((1, 1, dh, ncp), lambda bi, g, i: (bi, g, 0, 0)),
            pl.BlockSpec((1, 1, t, LANES + dh), lambda bi, g, i: (bi, g, 0, 0)),
            pl.BlockSpec((1, 1, dh, t), lambda bi, g, i: (bi, g, 0, 0)),
            pl.BlockSpec((1, 1, t, dh), lambda bi, g, i: (bi, g, 0, 0)),
            pl.BlockSpec((1, 1, dh, t), lambda bi, g, i: (bi, g, 0, 0)),
            pl.BlockSpec((R, ncp, tq), lambda bi, g, i: (g, 0, i)),
            pl.BlockSpec((R, e_sat + 1, tk, tq), lambda bi, g, i: (g, 0, 0, 0)),
            pl.BlockSpec((R, n_wtiles, tk, tq), lambda bi, g, i: (g, 0, 0, 0)),
            pl.BlockSpec((n_sel_blocks, ncp), lambda bi, g, i: (0, 0)),
        ],
        out_specs=pl.BlockSpec((1, tq, R * dh), lambda bi, g, i: (bi, i, g)),
        out_shape=jax.ShapeDtypeStruct((b, t, H * dh), BF16),
        scratch_shapes=[pltpu.VMEM((R, dh + 16, tq), F32),
                        pltpu.VMEM((LANES + dh, R * tq), BF16),
                        pltpu.VMEM((dh, R * tq), BF16)],
        compiler_params=_cparams("parallel", "parallel", "arbitrary"),
        name="nsa_attention",
    )(proj, proj, pgt, qg, kc, vct, ksa, vst, kw, vwt, bias_c, bias_sel, bias_win, c2st)


def nsa_layer(x, shift, scale, gate, norm_g, rel_bias, w_in, q_gain, k_gain,
              cmp_pe, cmp_w1, cmp_w2, w_out):
    d = x.shape[-1]
    n_in = w_in.shape[1]
    n_pad = -(-n_in // 1024) * 1024
    w_in_p = jnp.pad(w_in, ((0, 0), (0, n_pad - n_in))).astype(BF16)
    proj = norm_matmul(x, norm_g, shift, scale, w_in_p)
    kc, vct, ksa, vst, kw, vwt = nsa_prep(proj, k_gain, cmp_pe, cmp_w1, cmp_w2)
    o = nsa_attention(proj, kc, vct, ksa, vst, kw, vwt, rel_bias, q_gain)
    return out_proj_residual(o, w_out.astype(BF16), x, gate)


def _hgrn_kernel(q_ref, f_ref, v_ref, g_ref, lb_ref, og_ref, tri_ref, ones_ref,
                 o_ref, st_sc, k_sc, c_sc):
    C, dk, S = HG_CHUNK, HGRN_DK, HG_SUB
    n_sub = C // S

    @pl.when(pl.program_id(2) == 0)
    def _():
        st_sc[...] = jnp.zeros(st_sc.shape, F32)

    row = lax.broadcasted_iota(jnp.int32, (C, C), 0)
    colm = lax.broadcasted_iota(jnp.int32, (C, C), 1)
    rloc = lax.broadcasted_iota(jnp.int32, (C, dk), 0)
    diag_keep = ((row // S) == (colm // S)) & ((colm % S) <= (row % S))

    def rows_bcast(ref, hh, first, period):
        return jnp.concatenate(
            [jnp.broadcast_to(ref[hh, pl.ds(g * period + first, 1), :], (period, dk))
             for g in range(C // period)], axis=0)

    heads = range(HG_HEADS)
    sls = [slice(hh * dk, (hh + 1) * dk) for hh in heads]
    qs, ks, cums, vs = [], [], [], []
    for hh in heads:
        fl2 = f_ref[0, :, sls[hh]] * LOG2E
        lb = lb_ref[0, :, sls[hh]]
        log_sig = jnp.minimum(fl2, 0.0) - jnp.log2(1.0 + jnp.exp2(-jnp.abs(fl2)))
        ta = jnp.log2(lb)
        tb = jnp.log2(1.0 - lb) + log_sig
        lf = jnp.maximum(ta, tb) + jnp.log2(1.0 + jnp.exp2(-jnp.abs(ta - tb)))
        k = 1.0 - jnp.exp2(lf)
        cum = _dot_lhs_exact(tri_ref[...], lf)
        k_sc[hh] = k
        c_sc[hh] = cum
        ks.append(k)
        cums.append(cum)
        qs.append(_silu(q_ref[0, :, sls[hh]]))
        vs.append(v_ref[0, :, sls[hh]].astype(BF16))

    attns = []
    for hh in heads:
        pieces = []
        for j in range(S):
            kj = rows_bcast(k_sc, hh, j, S)
            cj = rows_bcast(c_sc, hh, j, S)
            pieces.append((qs[hh] * kj * jnp.exp2(jnp.minimum(cums[hh] - cj, 0.0))).astype(BF16))
        attns.append(jnp.where(diag_keep,
                               _dot(jnp.concatenate(pieces, axis=1), ones_ref[...]), 0.0))

    m = S
    while m < C:
        upper = (rloc // m) % 2 == 1
        same = (row // (2 * m)) == (colm // (2 * m))
        for hh in heads:
            e = jnp.exp2(-jnp.abs(cums[hh] - rows_bcast(c_sc, hh, m - 1, 2 * m)))
            qm = jnp.where(upper, qs[hh] * e, 0.0).astype(BF16)
            km = jnp.where(upper, 0.0, ks[hh] * e).astype(BF16)
            attns[hh] = attns[hh] + jnp.where(same, _dot_nt(qm, km), 0.0)
        m *= 2

    for hh in heads:
        st = st_sc[hh]
        cum, k, q, v = cums[hh], ks[hh], qs[hh], vs[hh]
        o = _dot(attns[hh].astype(BF16), v)
        o = o + _dot_nt((q * jnp.exp2(cum)).astype(BF16), st.astype(BF16))
        total = cum[C - 1:C, :]
        kd = (k * jnp.exp2(total - cum)).astype(BF16)
        st_sc[hh] = st * jnp.exp2(total) + lax.dot_general(
            v, kd, (((0,), (0,)), ((), ())), preferred_element_type=F32)
        ms = jnp.mean(o * o, axis=-1, keepdims=True)
        o = o * lax.rsqrt(ms + NORM_EPS) * og_ref[...]
        o_ref[0, :, sls[hh]] = (o * _silu(g_ref[0, :, sls[hh]])).astype(o_ref.dtype)


def hgrn_recurrence(proj, lb, out_gain):
    b, t, four_d = proj.shape
    d = four_d // 4
    dk = HGRN_DK
    C, S, hps = HG_CHUNK, HG_SUB, HG_HEADS
    w = hps * dk
    nhp = d // w
    tri = jnp.asarray(np.tril(np.ones((C, C), np.float32)), BF16)
    ones = jnp.asarray(np.arange(S * dk)[:, None] // dk == (np.arange(C)[None, :] % S), BF16)
    return pl.pallas_call(
        _hgrn_kernel,
        grid=(b, nhp, t // C),
        in_specs=[
            pl.BlockSpec((1, C, w), lambda bi, h, c: (bi, c, h)),
            pl.BlockSpec((1, C, w), lambda bi, h, c: (bi, c, nhp + h)),
            pl.BlockSpec((1, C, w), lambda bi, h, c: (bi, c, 2 * nhp + h)),
            pl.BlockSpec((1, C, w), lambda bi, h, c: (bi, c, 3 * nhp + h)),
            pl.BlockSpec((1, 1, w), lambda bi, h, c: (h, 0, 0)),
            pl.BlockSpec((1, dk), lambda bi, h, c: (0, 0)),
            pl.BlockSpec((C, C), lambda bi, h, c: (0, 0)),
            pl.BlockSpec((S * dk, C), lambda bi, h, c: (0, 0)),
        ],
        out_specs=pl.BlockSpec((1, C, w), lambda bi, h, c: (bi, c, h)),
        out_shape=jax.ShapeDtypeStruct((b, t, d), BF16),
        scratch_shapes=[pltpu.VMEM((hps, dk, dk), F32), pltpu.VMEM((hps, C, dk), F32),
                        pltpu.VMEM((hps, C, dk), F32)],
        compiler_params=_cparams("parallel", "parallel", "arbitrary"),
        name="hgrn_recurrence",
    )(proj, proj, proj, proj, lb.reshape(nhp, 1, w), out_gain.reshape(1, dk), tri, ones)


def hgrn_layer(x, shift, scale, gate, norm_g, lb, w_in, out_gain, w_out):
    proj = norm_matmul(x, norm_g, shift, scale, w_in.astype(BF16))
    o = hgrn_recurrence(proj, lb, out_gain)
    return out_proj_residual(o, w_out.astype(BF16), x, gate)


def _router_kernel(x_ref, g_ref, sh_ref, sc_ref, wr_ref, u_ref,
                   h_ref, eid_ref, ew_ref, pos_ref, cnt_ref, run_sc):
    @pl.when((pl.program_id(0) == 0) & (pl.program_id(1) == 0))
    def _():
        run_sc[...] = jnp.zeros(run_sc.shape, F32)

    h = _modulated_norm(x_ref[0], g_ref[...], sh_ref[0], sc_ref[0])
    h_ref[0] = h.astype(BF16)
    lt = _dot_exact(h, wr_ref[...]).T
    NG, EPG = MOE_GROUPS, MOE_EPG

    def softmax_rows(rows):
        mx = functools.reduce(jnp.maximum, rows)
        es = [jnp.exp(r - mx) for r in rows]
        tot = functools.reduce(lambda a, c: a + c, es)
        return [e / tot for e in es]

    def argmax_rows(rows):
        best, idx = rows[0], jnp.zeros(rows[0].shape, jnp.int32)
        for i in range(1, len(rows)):
            better = rows[i] > best
            best = jnp.where(better, rows[i], best)
            idx = jnp.where(better, i, idx)
        return best, idx

    pg = softmax_rows([lt[i:i + 1] for i in range(NG)])
    p_grp, grp = argmax_rows(pg)
    el = []
    for j in range(EPG):
        acc = lt[NG + j:NG + j + 1]
        for gi in range(1, NG):
            acc = jnp.where(grp == gi, lt[NG + gi * EPG + j:NG + gi * EPG + j + 1], acc)
        el.append(acc)
    pe = softmax_rows(el)
    p1, i1 = argmax_rows(pe)
    p2, i2 = argmax_rows([jnp.where(i1 == j, -1.0, pe[j]) for j in range(EPG)])
    den = p1 + p2
    e1 = grp * EPG + i1
    e2 = grp * EPG + i2
    eid_ref[0:1, :] = e1
    eid_ref[1:2, :] = e2
    ew_ref[0:1, :] = p_grp * p1 / den
    ew_ref[1:2, :] = p_grp * p2 / den

    tm = e1.shape[1]
    ex = lax.broadcasted_iota(jnp.int32, (MOE_EXPERTS, tm), 0)
    oh1 = jnp.where(ex == e1, 1.0, 0.0)
    oh2 = jnp.where(ex == e2, 1.0, 0.0)
    before1 = _dot(oh1.astype(BF16), u_ref[...])
    before2 = _dot(oh2.astype(BF16), u_ref[...])
    tot1 = jnp.sum(oh1, axis=1, keepdims=True)
    tot2 = jnp.sum(oh2, axis=1, keepdims=True)
    run = run_sc[...]
    pos1 = jnp.sum(oh1 * (before1 + run), axis=0, keepdims=True)
    pos2 = jnp.sum(oh2 * (before2 + (run + tot1)), axis=0, keepdims=True)
    pos_ref[0:1, :] = pos1.astype(jnp.int32)
    pos_ref[1:2, :] = pos2.astype(jnp.int32)
    run = run + tot1 + tot2
    run_sc[...] = run
    cnt_ref[...] = jnp.broadcast_to(run, cnt_ref.shape)


def moe_router(x, gain, shift, scale, w_group, w_expert, tm=512):
    b, t, d = x.shape
    tm = min(tm, t)
    nt = t // tm
    wr = jnp.concatenate([w_group, w_expert], axis=1)
    wr = jnp.pad(wr, ((0, 0), (0, LANES - wr.shape[1])))
    upper = jnp.asarray(np.triu(np.ones((tm, tm), np.float32), 1), BF16)
    return pl.pallas_call(
        _router_kernel,
        grid=(b, nt),
        in_specs=[
            pl.BlockSpec((1, tm, d), lambda bi, i: (bi, i, 0)),
            pl.BlockSpec((1, d), lambda bi, i: (0, 0)),
            pl.BlockSpec((1, 1, d), lambda bi, i: (bi, 0, 0)),
            pl.BlockSpec((1, 1, d), lambda bi, i: (bi, 0, 0)),
            pl.BlockSpec((d, LANES), lambda bi, i: (0, 0)),
            pl.BlockSpec((tm, tm), lambda bi, i: (0, 0)),
        ],
        out_specs=[
            pl.BlockSpec((1, tm, d), lambda bi, i: (bi, i, 0)),
            pl.BlockSpec((MOE_TOP_K, tm), lambda bi, i: (0, bi * nt + i)),
            pl.BlockSpec((MOE_TOP_K, tm), lambda bi, i: (0, bi * nt + i)),
            pl.BlockSpec((MOE_TOP_K, tm), lambda bi, i: (0, bi * nt + i)),
            pl.BlockSpec((MOE_EXPERTS, LANES), lambda bi, i: (0, 0)),
        ],
        out_shape=[
            jax.ShapeDtypeStruct((b, t, d), BF16),
            jax.ShapeDtypeStruct((MOE_TOP_K, b * t), jnp.int32),
            jax.ShapeDtypeStruct((MOE_TOP_K, b * t), F32),
            jax.ShapeDtypeStruct((MOE_TOP_K, b * t), jnp.int32),
            jax.ShapeDtypeStruct((MOE_EXPERTS, LANES), F32),
        ],
        scratch_shapes=[pltpu.VMEM((MOE_EXPERTS, 1), F32)],
        compiler_params=_cparams("arbitrary", "arbitrary"),
        name="moe_router",
    )(x, gain.reshape(1, d), shift.reshape(b, 1, d), scale.reshape(b, 1, d), wr, upper)


def _expert_ffn_kernel(be_ref, nu_ref, xb_ref, w1_ref, w3_ref, w2_ref, yb_ref,
                       w1_sc, w3_sc, w2_sc):
    i = pl.program_id(0)

    @pl.when((i == 0) | (be_ref[i] != be_ref[jnp.maximum(i - 1, 0)]))
    def _():
        w1_sc[...] = w1_ref[0].astype(BF16)
        w3_sc[...] = w3_ref[0].astype(BF16)
        w2_sc[...] = w2_ref[0].astype(BF16)

    @pl.when(i < nu_ref[0])
    def _():
        xb = xb_ref[...]
        a = _dot(xb, w1_sc[...])
        g = _dot(xb, w3_sc[...])
        yb_ref[...] = _dot((_silu(a) * g).astype(BF16), w2_sc[...]).astype(yb_ref.dtype)

    @pl.when(i >= nu_ref[0])
    def _():
        yb_ref[...] = jnp.zeros(yb_ref.shape, yb_ref.dtype)


def expert_ffn(xb, blk_e, n_used, w1, w3, w2):
    p, d = xb.shape
    ff = w1.shape[2]
    rb = MOE_ROW_BLOCK
    grid_spec = pltpu.PrefetchScalarGridSpec(
        num_scalar_prefetch=2,
        grid=(p // rb,),
        in_specs=[
            pl.BlockSpec((rb, d), lambda i, be, nu: (i, 0)),
            pl.BlockSpec((1, d, ff), lambda i, be, nu: (be[i], 0, 0)),
            pl.BlockSpec((1, d, ff), lambda i, be, nu: (be[i], 0, 0)),
            pl.BlockSpec((1, ff, d), lambda i, be, nu: (be[i], 0, 0)),
        ],
        out_specs=pl.BlockSpec((rb, d), lambda i, be, nu: (i, 0)),
        scratch_shapes=[pltpu.VMEM((d, ff), BF16), pltpu.VMEM((d, ff), BF16),
                        pltpu.VMEM((ff, d), BF16)],
    )
    return pl.pallas_call(
        _expert_ffn_kernel,
        grid_spec=grid_spec,
        out_shape=jax.ShapeDtypeStruct((p, d), BF16),
        compiler_params=_cparams("arbitrary"),
        name="expert_ffn",
    )(blk_e, n_used, xb, w1, w3, w2)


def _combine_kernel(x_ref, gt_ref, y1_ref, y2_ref, w_ref, o_ref):
    w = w_ref[0]
    y = w[:, 0:1] * y1_ref[0].astype(F32) + w[:, 1:2] * y2_ref[0].astype(F32)
    o_ref[0] = x_ref[0] + gt_ref[0] * y


def moe_combine(x, gate, y1, y2, w, tm=512):
    b, t, d = x.shape
    tm = min(tm, t)
    spec = pl.BlockSpec((1, tm, d), lambda bi, i: (bi, i, 0))
    return pl.pallas_call(
        _combine_kernel,
        grid=(b, t // tm),
        in_specs=[spec, pl.BlockSpec((1, 1, d), lambda bi, i: (bi, 0, 0)), spec, spec,
                  pl.BlockSpec((1, tm, MOE_TOP_K), lambda bi, i: (bi, i, 0))],
        out_specs=spec,
        out_shape=jax.ShapeDtypeStruct((b, t, d), F32),
        compiler_params=_cparams("parallel", "parallel"),
        name="moe_combine",
    )(x, gate.reshape(b, 1, d), y1, y2, w)


def moe_layer(x, shift, scale, gate, norm_g, w_group, w_expert, w1, w3, w2):
    b, t, d = x.shape
    n = b * t
    a = n * MOE_TOP_K
    rb = MOE_ROW_BLOCK
    h, eid, ew, pos, cnt = moe_router(x, norm_g, shift, scale, w_group, w_expert)
    counts = cnt[:, 0].astype(jnp.int32)
    padded = (counts + rb - 1) // rb * rb
    pad_end = jnp.cumsum(padded)
    pad_start = pad_end - padded
    is_e = eid[..., None] == jnp.arange(MOE_EXPERTS, dtype=jnp.int32)
    dest = jnp.sum(jnp.where(is_e, pad_start, 0), axis=-1) + pos
    n_blk = -(-(a + MOE_EXPERTS * (rb - 1)) // rb)
    p = n_blk * rb
    tok = jnp.tile(jnp.arange(n, dtype=jnp.int32), MOE_TOP_K)
    buf_t = jnp.zeros((p,), jnp.int32).at[dest.reshape(a)].set(tok)
    blk_e = jnp.minimum(jnp.searchsorted(pad_end, jnp.arange(n_blk) * rb, side='right'),
                        MOE_EXPERTS - 1).astype(jnp.int32)
    n_used = (pad_end[-1:] // rb).astype(jnp.int32)
    xb = h.reshape(n, d)[buf_t]
    yb = expert_ffn(xb, blk_e, n_used, w1, w3, w2)
    y1 = yb[dest[0]].reshape(b, t, d)
    y2 = yb[dest[1]].reshape(b, t, d)
    return moe_combine(x, gate, y1, y2, ew.T.reshape(b, t, MOE_TOP_K))


def kernel(x, c, ada_w, ada_b, norm_g, rel_bias, nsa_w_in, nsa_q_gain, nsa_k_gain, nsa_cmp_pe,
           nsa_cmp_w1, nsa_cmp_w2, nsa_w_out, hgrn_w_in, hgrn_lower_bounds, hgrn_out_gain,
           hgrn_w_out, moe_router_group, moe_router_expert, moe_w1, moe_w3, moe_w2):
    depth = ada_w.shape[0]
    d = x.shape[-1]
    lb_soft = jax.nn.softmax(hgrn_lower_bounds.astype(F32), axis=0)
    lb_all = jnp.cumsum(lb_soft, axis=0) - lb_soft[0]
    mod = adaln_mod(c, ada_w, ada_b)
    for layer in range(depth):
        j = layer // 2
        shift, scale, gate = (mod[layer, 0, :, i * d:(i + 1) * d] for i in range(3))
        if layer % 2 == 0:
            x = nsa_layer(x, shift, scale, gate, norm_g[layer, 0], rel_bias, nsa_w_in[j],
                          nsa_q_gain[j], nsa_k_gain[j], nsa_cmp_pe[j], nsa_cmp_w1[j],
                          nsa_cmp_w2[j], nsa_w_out[j])
        else:
            x = hgrn_layer(x, shift, scale, gate, norm_g[layer, 0], lb_all[layer],
                           hgrn_w_in[j], hgrn_out_gain[j], hgrn_w_out[j])
        shift, scale, gate = (mod[layer, 1, :, i * d:(i + 1) * d] for i in range(3))
        x = moe_layer(x, shift, scale, gate, norm_g[layer, 1], moe_router_group[layer],
                      moe_router_expert[layer], moe_w1[layer], moe_w3[layer], moe_w2[layer])
    return x
```

```python
import functools
import math

import numpy as np
import jax
import jax.numpy as jnp
from jax import lax
from jax.experimental import pallas as pl
from jax.experimental.pallas import tpu as pltpu

F32 = jnp.float32
BF16 = jnp.bfloat16
HIGHEST = lax.Precision.HIGHEST

NSA_HEADS = 16
NSA_KV_GROUPS = 4
NSA_GROUP_SIZE = NSA_HEADS // NSA_KV_GROUPS
NSA_HEAD_DIM = 64
CMP_BLOCK = 32
CMP_STRIDE = 16
SEL_BLOCK = 64
N_SELECT = 8
WINDOW = 512
FORCED_SCORE = 1.0e4
REL_BUCKETS = 32
REL_MAX_DISTANCE = 1024
HGRN_DK = 128
MOE_GROUPS = 4
MOE_EPG = 4
MOE_EXPERTS = MOE_GROUPS * MOE_EPG
MOE_TOP_K = 2
MOE_ROW_BLOCK = 512
NORM_EPS = 1e-6

LANES = 128
NEG = -1.0e30
LOG2E = math.log2(math.e)
VMEM_LIMIT = 48 * 1024 * 1024

ATT_TQ = 256
ATT_TK = 256
HG_CHUNK = 128
HG_SUB = 8
HG_HEADS = 4


def _cparams(*sem):
    return pltpu.CompilerParams(dimension_semantics=sem, vmem_limit_bytes=VMEM_LIMIT)


def _dot(a, b):
    return jnp.dot(a, b, preferred_element_type=F32)


def _dot_exact(a, b):
    return jnp.dot(a, b, preferred_element_type=F32, precision=HIGHEST)


def _dot_lhs_exact(a, x):
    x1 = x.astype(BF16)
    r1 = x - x1.astype(F32)
    x2 = r1.astype(BF16)
    x3 = (r1 - x2.astype(F32)).astype(BF16)
    return _dot(a, x1) + _dot(a, x2) + _dot(a, x3)


def _dot_nt(a, b):
    return lax.dot_general(a, b, (((1,), (1,)), ((), ())), preferred_element_type=F32)


def _sigmoid(x):
    return 0.5 * jnp.tanh(0.5 * x) + 0.5


def _silu(x):
    return x * _sigmoid(x)


def _adaln_kernel(c_ref, w_ref, b_ref, o_ref):
    cond = _silu(c_ref[...])
    o_ref[0] = _dot_exact(cond, w_ref[0]) + b_ref[0]


def adaln_mod(c, ada_w, ada_b):
    depth, two, d, d3 = ada_w.shape
    b = c.shape[0]
    ls = depth * two
    tn = 1024
    out = pl.pallas_call(
        _adaln_kernel,
        grid=(ls, d3 // tn),
        in_specs=[
            pl.BlockSpec((b, d), lambda i, j: (0, 0)),
            pl.BlockSpec((1, d, tn), lambda i, j: (i, 0, j)),
            pl.BlockSpec((1, 1, tn), lambda i, j: (i, 0, j)),
        ],
        out_specs=pl.BlockSpec((1, b, tn), lambda i, j: (i, 0, j)),
        out_shape=jax.ShapeDtypeStruct((ls, b, d3), F32),
        compiler_params=_cparams("parallel", "parallel"),
        name="adaln_mod",
    )(c, ada_w.reshape(ls, d, d3), ada_b.reshape(ls, 1, d3))
    return out.reshape(depth, two, b, d3)


def _modulated_norm(x, gain, shift, scale):
    ms = jnp.mean(x * x, axis=-1, keepdims=True)
    y = x * lax.rsqrt(ms + NORM_EPS) * gain
    return y * (1.0 + scale) + shift


def _norm_matmul_kernel(x_ref, g_ref, sh_ref, sc_ref, w_ref, o_ref, h_sc):
    @pl.when(pl.program_id(2) == 0)
    def _():
        h = _modulated_norm(x_ref[0], g_ref[...], sh_ref[0], sc_ref[0])
        h_sc[...] = h.astype(BF16)

    o_ref[0] = _dot(h_sc[...], w_ref[...])


def norm_matmul(x, gain, shift, scale, w, tm=512, tn=1024):
    b, t, d = x.shape
    n = w.shape[1]
    tm = min(tm, t)
    return pl.pallas_call(
        _norm_matmul_kernel,
        grid=(b, t // tm, n // tn),
        in_specs=[
            pl.BlockSpec((1, tm, d), lambda bi, i, j: (bi, i, 0)),
            pl.BlockSpec((1, d), lambda bi, i, j: (0, 0)),
            pl.BlockSpec((1, 1, d), lambda bi, i, j: (bi, 0, 0)),
            pl.BlockSpec((1, 1, d), lambda bi, i, j: (bi, 0, 0)),
            pl.BlockSpec((d, tn), lambda bi, i, j: (0, j)),
        ],
        out_specs=pl.BlockSpec((1, tm, tn), lambda bi, i, j: (bi, i, j)),
        out_shape=jax.ShapeDtypeStruct((b, t, n), F32),
        scratch_shapes=[pltpu.VMEM((tm, d), BF16)],
        compiler_params=_cparams("parallel", "parallel", "arbitrary"),
        name="norm_matmul",
    )(x, gain.reshape(1, d), shift.reshape(b, 1, d), scale.reshape(b, 1, d), w)


def _out_proj_kernel(o_ref, w_ref, x_ref, gt_ref, y_ref):
    y = _dot(o_ref[0], w_ref[...])
    y_ref[0] = x_ref[0] + gt_ref[0] * y


def out_proj_residual(o, w, x, gate, tm=512):
    b, t, d = x.shape
    k = o.shape[-1]
    tm = min(tm, t)
    return pl.pallas_call(
        _out_proj_kernel,
        grid=(b, t // tm),
        in_specs=[
            pl.BlockSpec((1, tm, k), lambda bi, i: (bi, i, 0)),
            pl.BlockSpec((k, d), lambda bi, i: (0, 0)),
            pl.BlockSpec((1, tm, d), lambda bi, i: (bi, i, 0)),
            pl.BlockSpec((1, 1, d), lambda bi, i: (bi, 0, 0)),
        ],
        out_specs=pl.BlockSpec((1, tm, d), lambda bi, i: (bi, i, 0)),
        out_shape=jax.ShapeDtypeStruct((b, t, d), F32),
        compiler_params=_cparams("parallel", "parallel"),
        name="out_proj_residual",
    )(o, w, x, gate.reshape(b, 1, d))


def _pair_rms(x, gain2):
    lane = lax.broadcasted_iota(jnp.int32, x.shape, 1)
    lo = lane < NSA_HEAD_DIM
    x2 = x * x
    s_lo = jnp.sum(jnp.where(lo, x2, 0.0), axis=-1, keepdims=True)
    s_hi = jnp.sum(jnp.where(lo, 0.0, x2), axis=-1, keepdims=True)
    inv = jnp.where(lo, lax.rsqrt(s_lo / NSA_HEAD_DIM + NORM_EPS),
                    lax.rsqrt(s_hi / NSA_HEAD_DIM + NORM_EPS))
    return x * inv * gain2


def _nsa_prep_kernel(kc0_ref, kc1_ref, vc0_ref, vc1_ref, ks_ref, vs_ref, kw_ref, vw_ref,
                     kg_ref, pe_ref, w1_ref, w1big_ref, w2big_ref, ex_ref,
                     kco_ref, vct_ref, kso_ref, vst_ref, kwo_ref, vwt_ref):
    t = ks_ref.shape[1]
    nblk = t // CMP_STRIDE
    dh = NSA_HEAD_DIM
    G = NSA_KV_GROUPS

    for src, dst, gi, off in ((ks_ref, kso_ref, 1, LANES), (kw_ref, kwo_ref, 2, 0)):
        g2 = kg_ref[gi:gi + 1, :]
        for gp in range(G // 2):
            kn = _pair_rms(src[0, :, gp * LANES:(gp + 1) * LANES], g2).astype(BF16)
            dst[0, 2 * gp, :, off:off + dh] = kn[:, :dh]
            dst[0, 2 * gp + 1, :, off:off + dh] = kn[:, dh:]
    for g in range(G):
        kso_ref[0, g, :, :LANES] = ex_ref[...]
    for src, dst in ((vs_ref, vst_ref), (vw_ref, vwt_ref)):
        for gp in range(G // 2):
            vt = src[0, :, gp * LANES:(gp + 1) * LANES].T.astype(BF16)
            dst[0, 2 * gp] = vt[:dh]
            dst[0, 2 * gp + 1] = vt[dh:]

    for ci, (srcs, is_key) in enumerate((((kc0_ref, kc1_ref), True), ((vc0_ref, vc1_ref), False))):
        pe1 = _dot(pe_ref[ci].astype(BF16), w1_ref[ci])[0:1]
        pe2 = jnp.concatenate([pe1, pe1], axis=1)
        for gp, src in enumerate(srcs):
            parts = [src[0, pl.ds(l, nblk, stride=CMP_STRIDE), :] for l in range(CMP_STRIDE)]
            r = jnp.concatenate(parts, axis=1).astype(BF16)
            ab = _dot(r, w1big_ref[ci])
            second = ab[:, LANES:]
            shifted = jnp.concatenate([second[1:], jnp.zeros((1, LANES), F32)], axis=0)
            pre = ab[:, :LANES] + shifted + pe2
            hid = _dot(_silu(pre).astype(BF16), w2big_ref[ci])
            if is_key:
                kn = _pair_rms(hid, kg_ref[0:1, :]).astype(BF16)
                kco_ref[0, 2 * gp] = kn[:, :dh]
                kco_ref[0, 2 * gp + 1] = kn[:, dh:]
            else:
                vt = hid.T.astype(BF16)
                vct_ref[0, 2 * gp] = vt[:dh]
                vct_ref[0, 2 * gp + 1] = vt[dh:]


def nsa_prep(proj, k_gain, cmp_pe, cmp_w1, cmp_w2):
    b, t, _ = proj.shape
    G, dh = NSA_KV_GROUPS, NSA_HEAD_DIM
    kvw = G * dh
    q_blocks = (NSA_HEADS * dh) // kvw
    nblk = t // CMP_STRIDE
    kg2 = jnp.concatenate([k_gain, k_gain], axis=1)
    pe_flat = jnp.broadcast_to(cmp_pe.reshape(2, 1, CMP_BLOCK * dh), (2, 8, CMP_BLOCK * dh))

    w1r = cmp_w1.reshape(2, 2, CMP_STRIDE, dh, dh)
    eye2 = jnp.eye(2, dtype=cmp_w1.dtype)
    w1big = jnp.einsum('chlde,gk->clgdhke', w1r, eye2).reshape(2, CMP_STRIDE * 2 * dh, 4 * dh)
    w2big = jnp.einsum('cde,gk->cgdke', cmp_w2, eye2).reshape(2, 2 * dh, 2 * dh)
    expand = jnp.asarray((np.arange(t)[:, None] // SEL_BLOCK) == np.arange(LANES)[None, :], BF16)

    def col(i):
        return pl.BlockSpec((1, t, kvw), lambda bi, i=i: (bi, 0, q_blocks + i))

    def col_pair(i, gp):
        return pl.BlockSpec((1, t, LANES), lambda bi: (bi, 0, (q_blocks + i) * (kvw // LANES) + gp))

    def full(shape):
        return pl.BlockSpec(shape, lambda bi: (0,) * len(shape))

    def per_b(shape):
        return pl.BlockSpec((1,) + shape, lambda bi: (bi,) + (0,) * len(shape))

    return pl.pallas_call(
        _nsa_prep_kernel,
        grid=(b,),
        in_specs=[col_pair(0, 0), col_pair(0, 1), col_pair(1, 0), col_pair(1, 1),
                  col(2), col(3), col(4), col(5),
                  full((3, 2 * dh)), full((2, 8, CMP_BLOCK * dh)),
                  full((2, CMP_BLOCK * dh, dh)), full((2, 2 * CMP_STRIDE * dh, 4 * dh)),
                  full((2, 2 * dh, 2 * dh)), full((t, LANES))],
        out_specs=[per_b((G, nblk, dh)), per_b((G, dh, nblk)),
                   per_b((G, t, LANES + dh)), per_b((G, dh, t)),
                   per_b((G, t, dh)), per_b((G, dh, t))],
        out_shape=[jax.ShapeDtypeStruct((b, G, nblk, dh), BF16),
                   jax.ShapeDtypeStruct((b, G, dh, nblk), BF16),
                   jax.ShapeDtypeStruct((b, G, t, LANES + dh), BF16),
                   jax.ShapeDtypeStruct((b, G, dh, t), BF16),
                   jax.ShapeDtypeStruct((b, G, t, dh), BF16),
                   jax.ShapeDtypeStruct((b, G, dh, t), BF16)],
        compiler_params=_cparams("parallel"),
        name="nsa_prep",
    )(proj, proj, proj, proj, proj, proj, proj, proj, kg2, pe_flat,
      cmp_w1.astype(BF16), w1big.astype(BF16), w2big.astype(BF16), expand)


def _nsa_attn_kernel(q_ref, gl_ref, pgt_ref, qg_ref, kc_ref, vct_ref, ksa_ref, vst_ref,
                     kw_ref, vwt_ref, bc_ref, bs_ref, bw_ref, c2st_ref,
                     o_ref, acc_sc, qsel_sc, qwin_sc, pc_sc, ssa_sc, ssb_sc, swa_sc, swb_sc, *,
                     e_sat, n_wtiles,
                     n_sel_blocks, n_select):
    tq, tk = ATT_TQ, ATT_TK
    R, dh = NSA_GROUP_SIZE, NSA_HEAD_DIM
    qi = pl.program_id(2)
    scale = dh ** -0.5 * LOG2E

    gl = gl_ref[0]
    g1 = gl.astype(BF16)
    gr = gl - g1.astype(F32)
    g2 = gr.astype(BF16)
    g3 = (gr - g2.astype(F32)).astype(BF16)
    gates = _sigmoid(_dot_nt(pgt_ref[0], g1) + _dot_nt(pgt_ref[0], g2) + _dot_nt(pgt_ref[0], g3))

    qt = q_ref[0].T
    for r in range(R):
        qr = qt[r * dh:(r + 1) * dh]
        ms = jnp.mean(qr * qr, axis=0, keepdims=True)
        qn = (qr * lax.rsqrt(ms + NORM_EPS) * qg_ref[...] * scale).astype(BF16)
        qwin_sc[:, r * tq:(r + 1) * tq] = qn
        qsel_sc[LANES:, r * tq:(r + 1) * tq] = qn

    ones_rows = jnp.ones((16, tk), BF16)
    rr = tq // tk
    n_tiles = rr * qi + rr

    def make_sweep(k_ref, vt_ref, q_sc, b_ref, hi, e_last, sa_sc, sb_sc):
        def tile_col(d):
            dc = jnp.minimum(d, hi - 1)
            e = jnp.where(dc == 0, rr - 1, jnp.where(dc < rr, dc - 1, dc))
            return pl.multiple_of((n_tiles - 1 - e) * tk, tk), jnp.minimum(e, e_last)

        def logits(d, dst_sc):
            col, _ = tile_col(d)
            dst_sc[...] = _dot(k_ref[0, 0, pl.ds(col, tk), :], q_sc[...])

        def softmax_pv(d, src_sc, ms):
            col, bi = tile_col(d)
            vaug = jnp.concatenate([vt_ref[0, 0, :, pl.ds(col, tk)], ones_rows], axis=0)
            new_ms = []
            for r in range(R):
                s = src_sc[:, r * tq:(r + 1) * tq] + b_ref[r, bi]
                m_new = jnp.maximum(ms[r], jnp.max(s, axis=0, keepdims=True))
                alpha = jnp.exp2(ms[r] - m_new)
                p = jnp.exp2(s - m_new).astype(BF16)
                acc_sc[r] = acc_sc[r] * alpha + _dot(vaug, p)
                new_ms.append(m_new)
            return tuple(new_ms)

        def body(dp, ms):
            d = 2 * dp
            logits(d + 1, sb_sc)
            ms = softmax_pv(d, sa_sc, ms)
            logits(d + 2, sa_sc)
            return softmax_pv(d + 1, sb_sc, ms)

        def start():
            logits(0, sa_sc)

        def run():
            acc_sc[...] = jnp.zeros(acc_sc.shape, F32)
            ms = lax.fori_loop(0, hi // 2, body,
                               tuple(jnp.full((1, tq), NEG, F32) for _ in range(R)))

            @pl.when(hi % 2 == 1)
            def _():
                softmax_pv(hi - 1, sa_sc, ms)

            outs = []
            for r in range(R):
                acc = acc_sc[r]
                outs.append(acc[:dh] * (1.0 / acc[dh:dh + 1]))
            return outs

        return start, run

    win_start, win_run = make_sweep(kw_ref, vwt_ref, qwin_sc, bw_ref,
                                    jnp.minimum(n_tiles, n_wtiles), n_wtiles - 1, swa_sc, swb_sc)
    win_start()

    ncp = kc_ref.shape[2]
    n_idx = lax.broadcasted_iota(jnp.int32, (ncp, tq), 0)
    t_idx = qi * tq + lax.broadcasted_iota(jnp.int32, (ncp, tq), 1)
    vis = (n_idx * CMP_STRIDE + (CMP_BLOCK - 1)) <= t_idx
    psum = jnp.zeros((ncp, tq), F32)
    sc_all = _dot(kc_ref[0, 0], qwin_sc[...])
    for r in range(R):
        s = jnp.where(vis, sc_all[:, r * tq:(r + 1) * tq] + bc_ref[r], NEG)
        m = jnp.max(s, axis=0, keepdims=True)
        e = jnp.where(vis, jnp.exp2(s - m), 0.0)
        p = e * (1.0 / jnp.maximum(jnp.sum(e, axis=0, keepdims=True), 1e-30))
        psum = psum + p
        pc_sc[:, r * tq:(r + 1) * tq] = p.astype(BF16)
    oc_all = _dot(vct_ref[0, 0], pc_sc[...])
    o_cmp = [oc_all[:, r * tq:(r + 1) * tq] for r in range(R)]

    imp_t = _dot_lhs_exact(c2st_ref[...], psum)
    blk = lax.broadcasted_iota(jnp.int32, (n_sel_blocks, tq), 0)
    tpos = qi * tq + lax.broadcasted_iota(jnp.int32, (n_sel_blocks, tq), 1)
    cur = tpos // SEL_BLOCK
    forced = (blk == 0) | (blk == cur) | (blk == cur - 1)
    score = jnp.where(forced, FORCED_SCORE, jnp.where(blk <= cur, imp_t, -1.0))
    rank = jnp.zeros((n_sel_blocks, tq), F32)
    for s2 in range(n_sel_blocks):
        row = score[s2:s2 + 1, :]
        beats = (row > score) | ((row == score) & (blk > s2))
        rank = rank + jnp.where(beats, 1.0, 0.0)
    negsel = jnp.where(rank < n_select, 0.0, NEG)
    if n_sel_blocks < LANES:
        negsel = jnp.concatenate([negsel, jnp.zeros((LANES - n_sel_blocks, tq), F32)], axis=0)
    negsel = negsel.astype(BF16)
    for r in range(R):
        qsel_sc[:LANES, r * tq:(r + 1) * tq] = negsel

    sel_start, sel_run = make_sweep(ksa_ref, vst_ref, qsel_sc, bs_ref, n_tiles, e_sat,
                                    ssa_sc, ssb_sc)
    sel_start()
    o_win = win_run()
    o_sel = sel_run()

    outs = []
    for r in range(R):
        outs.append(gates[3 * r:3 * r + 1] * o_cmp[r]
                    + gates[3 * r + 1:3 * r + 2] * o_sel[r]
                    + gates[3 * r + 2:3 * r + 3] * o_win[r])
    o_ref[0] = jnp.concatenate(outs, axis=0).T.astype(o_ref.dtype)


def _t5_bucket(dist):
    n = jnp.maximum(dist, 0)
    max_exact = REL_BUCKETS // 2
    nf = jnp.maximum(n, 1).astype(F32)
    large = max_exact + (jnp.log(nf / max_exact) / math.log(REL_MAX_DISTANCE / max_exact)
                         * (REL_BUCKETS - max_exact)).astype(jnp.int32)
    large = jnp.minimum(large, REL_BUCKETS - 1)
    return jnp.where(n < max_exact, n, large)


def _bias_of_dist(dist, rel_bias):
    onehot = (_t5_bucket(dist)[..., None] == jnp.arange(REL_BUCKETS)).astype(F32)
    out = jnp.einsum('...k,kh->...h', onehot, rel_bias.astype(F32), precision=HIGHEST)
    return jnp.moveaxis(out, -1, 0)


def _saturation_distance():
    max_exact = REL_BUCKETS // 2
    steps = REL_BUCKETS - max_exact
    n_sat = max_exact * (REL_MAX_DISTANCE / max_exact) ** ((steps - 1) / steps)
    return int(math.ceil(n_sat)) + 2


def nsa_attention(proj, kc, vct, ksa, vst, kw, vwt, rel_bias, q_gain):
    b, t, _ = proj.shape
    G, R, dh = NSA_KV_GROUPS, NSA_GROUP_SIZE, NSA_HEAD_DIM
    H = NSA_HEADS
    tq, tk = ATT_TQ, ATT_TK
    rr = tq // tk
    ncp = kc.shape[2]
    n_sel_blocks = t // SEL_BLOCK
    n_select = min(N_SELECT, n_sel_blocks)
    assert n_sel_blocks <= LANES and n_sel_blocks % 8 == 0 and tq % tk == 0 and t % tq == 0

    e_sat = -(-(_saturation_distance() + tk - 1) // tk) + rr - 1
    n_wtiles = -(-(WINDOW + tk - 1) // tk) + rr - 1
    jj = np.arange(tk)[:, None]
    ii = np.arange(tq)[None, :]

    def tile_dist(n_e):
        return (np.arange(n_e)[:, None, None] - (rr - 1)) * tk + (ii - jj)[None]

    dist = tile_dist(e_sat + 1)
    bias_sel = jnp.where(dist >= 0, _bias_of_dist(jnp.asarray(dist), rel_bias) * LOG2E, NEG)
    dwin = tile_dist(n_wtiles)
    bias_win = jnp.where((dwin >= 0) & (dwin < WINDOW),
                         _bias_of_dist(jnp.asarray(dwin), rel_bias) * LOG2E, NEG)
    dc = np.arange(t)[None, :] - (np.arange(ncp)[:, None] * CMP_STRIDE + CMP_BLOCK - 1)
    bias_c = _bias_of_dist(jnp.asarray(dc), rel_bias) * LOG2E

    cs = np.arange(ncp) * CMP_STRIDE
    ss = np.arange(n_sel_blocks) * SEL_BLOCK
    shared = (np.minimum(cs[None, :] + CMP_BLOCK, ss[:, None] + SEL_BLOCK)
              - np.maximum(cs[None, :], ss[:, None]))
    c2st = jnp.asarray(np.clip(shared, 0, None) / CMP_BLOCK, BF16)
    pgt = np.zeros((G, LANES, LANES), np.float32)
    for g in range(G):
        for k in range(3 * R):
            pgt[g, k, 3 * R * g + k] = 1.0
    pgt = jnp.asarray(pgt, BF16)
    gate_blk = (H * dh + 6 * G * dh) // LANES
    qg = jnp.broadcast_to(q_gain.reshape(dh, 1), (dh, tq))

    kernel = functools.partial(_nsa_attn_kernel, e_sat=e_sat, n_wtiles=n_wtiles,
                               n_sel_blocks=n_sel_blocks, n_select=n_select)
    return pl.pallas_call(
        kernel,
        grid=(b, G, t // tq),
        in_specs=[
            pl.BlockSpec((1, tq, R * dh), lambda bi, g, i: (bi, i, g)),
            pl.BlockSpec((1, tq, LANES), lambda bi, g, i: (bi, i, gate_blk)),
            pl.BlockSpec((1, LANES, LANES), lambda bi, g, i: (g, 0, 0)),
            pl.BlockSpec((dh, tq), lambda bi, g, i: (0, 0)),
            pl.BlockSpec((1, 1, ncp, dh), lambda bi, g, i: (bi, g, 0, 0)),
            pl.BlockSpec((1, 1, dh, ncp), lambda bi, g, i: (bi, g, 0, 0)),
            pl.BlockSpec((1, 1, t, LANES + dh), lambda bi, g, i: (bi, g, 0, 0)),
            pl.BlockSpec((1, 1, dh, t), lambda bi, g, i: (bi, g, 0, 0)),
            pl.BlockSpec((1, 1, t, dh), lambda bi, g, i: (bi, g, 0, 0)),
            pl.BlockSpec((1, 1, dh, t), lambda bi, g, i: (bi, g, 0, 0)),
            pl.BlockSpec((R, ncp, tq), lambda bi, g, i: (g, 0, i)),
            pl.BlockSpec((R, e_sat + 1, tk, tq), lambda bi, g, i: (g, 0, 0, 0)),
            pl.BlockSpec((R, n_wtiles, tk, tq), lambda bi, g, i: (g, 0, 0, 0)),
            pl.BlockSpec((n_sel_blocks, ncp), lambda bi, g, i: (0, 0)),
        ],
        out_specs=pl.BlockSpec((1, tq, R * dh), lambda bi, g, i: (bi, i, g)),
        out_shape=jax.ShapeDtypeStruct((b, t, H * dh), BF16),
        scratch_shapes=[pltpu.VMEM((R, dh + 16, tq), F32),
                        pltpu.VMEM((LANES + dh, R * tq), BF16),
                        pltpu.VMEM((dh, R * tq), BF16),
                        pltpu.VMEM((ncp, R * tq), BF16),
                        pltpu.VMEM((tk, R * tq), F32), pltpu.VMEM((tk, R * tq), F32),
                        pltpu.VMEM((tk, R * tq), F32), pltpu.VMEM((tk, R * tq), F32)],
        compiler_params=_cparams("parallel", "parallel", "arbitrary"),
        name="nsa_attention",
    )(proj, proj, pgt, qg, kc, vct, ksa, vst, kw, vwt, bias_c, bias_sel, bias_win, c2st)


def nsa_layer(x, shift, scale, gate, norm_g, rel_bias, w_in, q_gain, k_gain,
              cmp_pe, cmp_w1, cmp_w2, w_out):
    d = x.shape[-1]
    n_in = w_in.shape[1]
    n_pad = -(-n_in // 1024) * 1024
    w_in_p = jnp.pad(w_in, ((0, 0), (0, n_pad - n_in))).astype(BF16)
    proj = norm_matmul(x, norm_g, shift, scale, w_in_p)
    kc, vct, ksa, vst, kw, vwt = nsa_prep(proj, k_gain, cmp_pe, cmp_w1, cmp_w2)
    o = nsa_attention(proj, kc, vct, ksa, vst, kw, vwt, rel_bias, q_gain)
    return out_proj_residual(o, w_out.astype(BF16), x, gate)


def _hgrn_kernel(q_ref, f_ref, v_ref, g_ref, lb_ref, og_ref, tri_ref, ones_ref,
                 o_ref, st_sc, k_sc, c_sc):
    C, dk, S = HG_CHUNK, HGRN_DK, HG_SUB
    n_sub = C // S

    @pl.when(pl.program_id(2) == 0)
    def _():
        st_sc[...] = jnp.zeros(st_sc.shape, F32)

    row = lax.broadcasted_iota(jnp.int32, (C, C), 0)
    colm = lax.broadcasted_iota(jnp.int32, (C, C), 1)
    rloc = lax.broadcasted_iota(jnp.int32, (C, dk), 0)
    diag_keep = ((row // S) == (colm // S)) & ((colm % S) <= (row % S))

    def rows_bcast(ref, hh, first, period):
        return jnp.concatenate(
            [jnp.broadcast_to(ref[hh, pl.ds(g * period + first, 1), :], (period, dk))
             for g in range(C // period)], axis=0)

    heads = range(HG_HEADS)
    sls = [slice(hh * dk, (hh + 1) * dk) for hh in heads]
    qs, ks, cums, vs = [], [], [], []
    for hh in heads:
        fl2 = f_ref[0, :, sls[hh]] * LOG2E
        lb = lb_ref[0, :, sls[hh]]
        log_sig = jnp.minimum(fl2, 0.0) - jnp.log2(1.0 + jnp.exp2(-jnp.abs(fl2)))
        ta = jnp.log2(lb)
        tb = jnp.log2(1.0 - lb) + log_sig
        lf = jnp.maximum(ta, tb) + jnp.log2(1.0 + jnp.exp2(-jnp.abs(ta - tb)))
        k = 1.0 - jnp.exp2(lf)
        cum = _dot_lhs_exact(tri_ref[...], lf)
        k_sc[hh] = k
        c_sc[hh] = cum
        ks.append(k)
        cums.append(cum)
        qs.append(_silu(q_ref[0, :, sls[hh]]))
        vs.append(v_ref[0, :, sls[hh]].astype(BF16))

    attns = []
    for hh in heads:
        pieces = []
        for j in range(S):
            kj = rows_bcast(k_sc, hh, j, S)
            cj = rows_bcast(c_sc, hh, j, S)
            pieces.append((qs[hh] * kj * jnp.exp2(jnp.minimum(cums[hh] - cj, 0.0))).astype(BF16))
        attns.append(jnp.where(diag_keep,
                               _dot(jnp.concatenate(pieces, axis=1), ones_ref[...]), 0.0))

    m = S
    while m < C:
        upper = (rloc // m) % 2 == 1
        same = (row // (2 * m)) == (colm // (2 * m))
        for hh in heads:
            e = jnp.exp2(-jnp.abs(cums[hh] - rows_bcast(c_sc, hh, m - 1, 2 * m)))
            qm = jnp.where(upper, qs[hh] * e, 0.0).astype(BF16)
            km = jnp.where(upper, 0.0, ks[hh] * e).astype(BF16)
            attns[hh] = attns[hh] + jnp.where(same, _dot_nt(qm, km), 0.0)
        m *= 2

    for hh in heads:
        st = st_sc[hh]
        cum, k, q, v = cums[hh], ks[hh], qs[hh], vs[hh]
        o = _dot(attns[hh].astype(BF16), v)
        o = o + _dot_nt((q * jnp.exp2(cum)).astype(BF16), st.astype(BF16))
        total = cum[C - 1:C, :]
        kd = (k * jnp.exp2(total - cum)).astype(BF16)
        st_sc[hh] = st * jnp.exp2(total) + lax.dot_general(
            v, kd, (((0,), (0,)), ((), ())), preferred_element_type=F32)
        ms = jnp.mean(o * o, axis=-1, keepdims=True)
        o = o * lax.rsqrt(ms + NORM_EPS) * og_ref[...]
        o_ref[0, :, sls[hh]] = (o * _silu(g_ref[0, :, sls[hh]])).astype(o_ref.dtype)


def hgrn_recurrence(proj, lb, out_gain):
    b, t, four_d = proj.shape
    d = four_d // 4
    dk = HGRN_DK
    C, S, hps = HG_CHUNK, HG_SUB, HG_HEADS
    w = hps * dk
    nhp = d // w
    tri = jnp.asarray(np.tril(np.ones((C, C), np.float32)), BF16)
    ones = jnp.asarray(np.arange(S * dk)[:, None] // dk == (np.arange(C)[None, :] % S), BF16)
    return pl.pallas_call(
        _hgrn_kernel,
        grid=(b, nhp, t // C),
        in_specs=[
            pl.BlockSpec((1, C, w), lambda bi, h, c: (bi, c, h)),
            pl.BlockSpec((1, C, w), lambda bi, h, c: (bi, c, nhp + h)),
            pl.BlockSpec((1, C, w), lambda bi, h, c: (bi, c, 2 * nhp + h)),
            pl.BlockSpec((1, C, w), lambda bi, h, c: (bi, c, 3 * nhp + h)),
            pl.BlockSpec((1, 1, w), lambda bi, h, c: (h, 0, 0)),
            pl.BlockSpec((1, dk), lambda bi, h, c: (0, 0)),
            pl.BlockSpec((C, C), lambda bi, h, c: (0, 0)),
            pl.BlockSpec((S * dk, C), lambda bi, h, c: (0, 0)),
        ],
        out_specs=pl.BlockSpec((1, C, w), lambda bi, h, c: (bi, c, h)),
        out_shape=jax.ShapeDtypeStruct((b, t, d), BF16),
        scratch_shapes=[pltpu.VMEM((hps, dk, dk), F32), pltpu.VMEM((hps, C, dk), F32),
                        pltpu.VMEM((hps, C, dk), F32)],
        compiler_params=_cparams("parallel", "parallel", "arbitrary"),
        name="hgrn_recurrence",
    )(proj, proj, proj, proj, lb.reshape(nhp, 1, w), out_gain.reshape(1, dk), tri, ones)


def hgrn_layer(x, shift, scale, gate, norm_g, lb, w_in, out_gain, w_out):
    proj = norm_matmul(x, norm_g, shift, scale, w_in.astype(BF16))
    o = hgrn_recurrence(proj, lb, out_gain)
    return out_proj_residual(o, w_out.astype(BF16), x, gate)


def _router_kernel(x_ref, g_ref, sh_ref, sc_ref, wr_ref, u_ref,
                   h_ref, eid_ref, ew_ref, pos_ref, cnt_ref, run_sc):
    @pl.when((pl.program_id(0) == 0) & (pl.program_id(1) == 0))
    def _():
        run_sc[...] = jnp.zeros(run_sc.shape, F32)

    h = _modulated_norm(x_ref[0], g_ref[...], sh_ref[0], sc_ref[0])
    h_ref[0] = h.astype(BF16)
    lt = _dot_exact(h, wr_ref[...]).T
    NG, EPG = MOE_GROUPS, MOE_EPG

    def softmax_rows(rows):
        mx = functools.reduce(jnp.maximum, rows)
        es = [jnp.exp(r - mx) for r in rows]
        tot = functools.reduce(lambda a, c: a + c, es)
        return [e / tot for e in es]

    def argmax_rows(rows):
        best, idx = rows[0], jnp.zeros(rows[0].shape, jnp.int32)
        for i in range(1, len(rows)):
            better = rows[i] > best
            best = jnp.where(better, rows[i], best)
            idx = jnp.where(better, i, idx)
        return best, idx

    pg = softmax_rows([lt[i:i + 1] for i in range(NG)])
    p_grp, grp = argmax_rows(pg)
    el = []
    for j in range(EPG):
        acc = lt[NG + j:NG + j + 1]
        for gi in range(1, NG):
            acc = jnp.where(grp == gi, lt[NG + gi * EPG + j:NG + gi * EPG + j + 1], acc)
        el.append(acc)
    pe = softmax_rows(el)
    p1, i1 = argmax_rows(pe)
    p2, i2 = argmax_rows([jnp.where(i1 == j, -1.0, pe[j]) for j in range(EPG)])
    den = p1 + p2
    e1 = grp * EPG + i1
    e2 = grp * EPG + i2
    eid_ref[0:1, :] = e1
    eid_ref[1:2, :] = e2
    ew_ref[0:1, :] = p_grp * p1 / den
    ew_ref[1:2, :] = p_grp * p2 / den

    tm = e1.shape[1]
    ex = lax.broadcasted_iota(jnp.int32, (MOE_EXPERTS, tm), 0)
    oh1 = jnp.where(ex == e1, 1.0, 0.0)
    oh2 = jnp.where(ex == e2, 1.0, 0.0)
    before1 = _dot(oh1.astype(BF16), u_ref[...])
    before2 = _dot(oh2.astype(BF16), u_ref[...])
    tot1 = jnp.sum(oh1, axis=1, keepdims=True)
    tot2 = jnp.sum(oh2, axis=1, keepdims=True)
    run = run_sc[...]
    pos1 = jnp.sum(oh1 * (before1 + run), axis=0, keepdims=True)
    pos2 = jnp.sum(oh2 * (before2 + (run + tot1)), axis=0, keepdims=True)
    pos_ref[0:1, :] = pos1.astype(jnp.int32)
    pos_ref[1:2, :] = pos2.astype(jnp.int32)
    run = run + tot1 + tot2
    run_sc[...] = run
    cnt_ref[...] = jnp.broadcast_to(run, cnt_ref.shape)


def moe_router(x, gain, shift, scale, w_group, w_expert, tm=512):
    b, t, d = x.shape
    tm = min(tm, t)
    nt = t // tm
    wr = jnp.concatenate([w_group, w_expert], axis=1)
    wr = jnp.pad(wr, ((0, 0), (0, LANES - wr.shape[1])))
    upper = jnp.asarray(np.triu(np.ones((tm, tm), np.float32), 1), BF16)
    return pl.pallas_call(
        _router_kernel,
        grid=(b, nt),
        in_specs=[
            pl.BlockSpec((1, tm, d), lambda bi, i: (bi, i, 0)),
            pl.BlockSpec((1, d), lambda bi, i: (0, 0)),
            pl.BlockSpec((1, 1, d), lambda bi, i: (bi, 0, 0)),
            pl.BlockSpec((1, 1, d), lambda bi, i: (bi, 0, 0)),
            pl.BlockSpec((d, LANES), lambda bi, i: (0, 0)),
            pl.BlockSpec((tm, tm), lambda bi, i: (0, 0)),
        ],
        out_specs=[
            pl.BlockSpec((1, tm, d), lambda bi, i: (bi, i, 0)),
            pl.BlockSpec((MOE_TOP_K, tm), lambda bi, i: (0, bi * nt + i)),
            pl.BlockSpec((MOE_TOP_K, tm), lambda bi, i: (0, bi * nt + i)),
            pl.BlockSpec((MOE_TOP_K, tm), lambda bi, i: (0, bi * nt + i)),
            pl.BlockSpec((MOE_EXPERTS, LANES), lambda bi, i: (0, 0)),
        ],
        out_shape=[
            jax.ShapeDtypeStruct((b, t, d), BF16),
            jax.ShapeDtypeStruct((MOE_TOP_K, b * t), jnp.int32),
            jax.ShapeDtypeStruct((MOE_TOP_K, b * t), F32),
            jax.ShapeDtypeStruct((MOE_TOP_K, b * t), jnp.int32),
            jax.ShapeDtypeStruct((MOE_EXPERTS, LANES), F32),
        ],
        scratch_shapes=[pltpu.VMEM((MOE_EXPERTS, 1), F32)],
        compiler_params=_cparams("arbitrary", "arbitrary"),
        name="moe_router",
    )(x, gain.reshape(1, d), shift.reshape(b, 1, d), scale.reshape(b, 1, d), wr, upper)


def _expert_ffn_kernel(be_ref, nu_ref, xb_ref, w1_ref, w3_ref, w2_ref, yb_ref,
                       w1_sc, w3_sc, w2_sc):
    i = pl.program_id(0)

    @pl.when((i == 0) | (be_ref[i] != be_ref[jnp.maximum(i - 1, 0)]))
    def _():
        w1_sc[...] = w1_ref[0].astype(BF16)
        w3_sc[...] = w3_ref[0].astype(BF16)
        w2_sc[...] = w2_ref[0].astype(BF16)

    @pl.when(i < nu_ref[0])
    def _():
        xb = xb_ref[...]
        a = _dot(xb, w1_sc[...])
        g = _dot(xb, w3_sc[...])
        yb_ref[...] = _dot((_silu(a) * g).astype(BF16), w2_sc[...]).astype(yb_ref.dtype)

    @pl.when(i >= nu_ref[0])
    def _():
        yb_ref[...] = jnp.zeros(yb_ref.shape, yb_ref.dtype)


def expert_ffn(xb, blk_e, n_used, w1, w3, w2):
    p, d = xb.shape
    ff = w1.shape[2]
    rb = MOE_ROW_BLOCK
    grid_spec = pltpu.PrefetchScalarGridSpec(
        num_scalar_prefetch=2,
        grid=(p // rb,),
        in_specs=[
            pl.BlockSpec((rb, d), lambda i, be, nu: (i, 0)),
            pl.BlockSpec((1, d, ff), lambda i, be, nu: (be[i], 0, 0)),
            pl.BlockSpec((1, d, ff), lambda i, be, nu: (be[i], 0, 0)),
            pl.BlockSpec((1, ff, d), lambda i, be, nu: (be[i], 0, 0)),
        ],
        out_specs=pl.BlockSpec((rb, d), lambda i, be, nu: (i, 0)),
        scratch_shapes=[pltpu.VMEM((d, ff), BF16), pltpu.VMEM((d, ff), BF16),
                        pltpu.VMEM((ff, d), BF16)],
    )
    return pl.pallas_call(
        _expert_ffn_kernel,
        grid_spec=grid_spec,
        out_shape=jax.ShapeDtypeStruct((p, d), BF16),
        compiler_params=_cparams("arbitrary"),
        name="expert_ffn",
    )(blk_e, n_used, xb, w1, w3, w2)


def _combine_kernel(x_ref, gt_ref, y1_ref, y2_ref, w_ref, o_ref):
    w = w_ref[0]
    y = w[:, 0:1] * y1_ref[0].astype(F32) + w[:, 1:2] * y2_ref[0].astype(F32)
    o_ref[0] = x_ref[0] + gt_ref[0] * y


def moe_combine(x, gate, y1, y2, w, tm=512):
    b, t, d = x.shape
    tm = min(tm, t)
    spec = pl.BlockSpec((1, tm, d), lambda bi, i: (bi, i, 0))
    return pl.pallas_call(
        _combine_kernel,
        grid=(b, t // tm),
        in_specs=[spec, pl.BlockSpec((1, 1, d), lambda bi, i: (bi, 0, 0)), spec, spec,
                  pl.BlockSpec((1, tm, MOE_TOP_K), lambda bi, i: (bi, i, 0))],
        out_specs=spec,
        out_shape=jax.ShapeDtypeStruct((b, t, d), F32),
        compiler_params=_cparams("parallel", "parallel"),
        name="moe_combine",
    )(x, gate.reshape(b, 1, d), y1, y2, w)


def moe_layer(x, shift, scale, gate, norm_g, w_group, w_expert, w1, w3, w2):
    b, t, d = x.shape
    n = b * t
    a = n * MOE_TOP_K
    rb = MOE_ROW_BLOCK
    h, eid, ew, pos, cnt = moe_router(x, norm_g, shift, scale, w_group, w_expert)
    counts = cnt[:, 0].astype(jnp.int32)
    padded = (counts + rb - 1) // rb * rb
    pad_end = jnp.cumsum(padded)
    pad_start = pad_end - padded
    is_e = eid[..., None] == jnp.arange(MOE_EXPERTS, dtype=jnp.int32)
    dest = jnp.sum(jnp.where(is_e, pad_start, 0), axis=-1) + pos
    n_blk = -(-(a + MOE_EXPERTS * (rb - 1)) // rb)
    p = n_blk * rb
    tok = jnp.tile(jnp.arange(n, dtype=jnp.int32), MOE_TOP_K)
    buf_t = (jnp.arange(p, dtype=jnp.int32) % n).at[dest.reshape(a)].set(tok)
    blk_start = jnp.arange(n_blk, dtype=jnp.int32) * rb
    blk_e = jnp.minimum(jnp.sum((pad_end[None, :] <= blk_start[:, None]).astype(jnp.int32), axis=1),
                        MOE_EXPERTS - 1)
    n_used = (pad_end[-1:] // rb).astype(jnp.int32)
    xb = h.reshape(n, d)[buf_t]
    yb = expert_ffn(xb, blk_e, n_used, w1, w3, w2)
    y1 = yb[dest[0]].reshape(b, t, d)
    y2 = yb[dest[1]].reshape(b, t, d)
    return moe_combine(x, gate, y1, y2, ew.T.reshape(b, t, MOE_TOP_K))


def kernel(x, c, ada_w, ada_b, norm_g, rel_bias, nsa_w_in, nsa_q_gain, nsa_k_gain, nsa_cmp_pe,
           nsa_cmp_w1, nsa_cmp_w2, nsa_w_out, hgrn_w_in, hgrn_lower_bounds, hgrn_out_gain,
           hgrn_w_out, moe_router_group, moe_router_expert, moe_w1, moe_w3, moe_w2):
    depth = ada_w.shape[0]
    d = x.shape[-1]
    lb_soft = jax.nn.softmax(hgrn_lower_bounds.astype(F32), axis=0)
    lb_all = jnp.cumsum(lb_soft, axis=0) - lb_soft[0]
    mod = adaln_mod(c, ada_w, ada_b)
    for layer in range(depth):
        j = layer // 2
        shift, scale, gate = (mod[layer, 0, :, i * d:(i + 1) * d] for i in range(3))
        if layer % 2 == 0:
            x = nsa_layer(x, shift, scale, gate, norm_g[layer, 0], rel_bias, nsa_w_in[j],
                          nsa_q_gain[j], nsa_k_gain[j], nsa_cmp_pe[j], nsa_cmp_w1[j],
                          nsa_cmp_w2[j], nsa_w_out[j])
        else:
            x = hgrn_layer(x, shift, scale, gate, norm_g[layer, 0], lb_all[layer],
                           hgrn_w_in[j], hgrn_out_gain[j], hgrn_w_out[j])
        shift, scale, gate = (mod[layer, 1, :, i * d:(i + 1) * d] for i in range(3))
        x = moe_layer(x, shift, scale, gate, norm_g[layer, 1], moe_router_group[layer],
                      moe_router_expert[layer], moe_w1[layer], moe_w3[layer], moe_w2[layer])
    return x
```

```python
import functools
import math

import numpy as np
import jax
import jax.numpy as jnp
from jax import lax
from jax.experimental import pallas as pl
from jax.experimental.pallas import tpu as pltpu

F32 = jnp.float32
BF16 = jnp.bfloat16
HIGHEST = lax.Precision.HIGHEST

NSA_HEADS = 16
NSA_KV_GROUPS = 4
NSA_GROUP_SIZE = NSA_HEADS // NSA_KV_GROUPS
NSA_HEAD_DIM = 64
CMP_BLOCK = 32
CMP_STRIDE = 16
SEL_BLOCK = 64
N_SELECT = 8
WINDOW = 512
FORCED_SCORE = 1.0e4
REL_BUCKETS = 32
REL_MAX_DISTANCE = 1024
HGRN_DK = 128
MOE_GROUPS = 4
MOE_EPG = 4
MOE_EXPERTS = MOE_GROUPS * MOE_EPG
MOE_TOP_K = 2
MOE_ROW_BLOCK = 512
NORM_EPS = 1e-6

LANES = 128
NEG = -1.0e30
LOG2E = math.log2(math.e)
VMEM_LIMIT = 48 * 1024 * 1024

ATT_TQ = 256
ATT_TK = 256
HG_CHUNK = 128
HG_SUB = 8
HG_HEADS = 4


def _cparams(*sem):
    return pltpu.CompilerParams(dimension_semantics=sem, vmem_limit_bytes=VMEM_LIMIT)


def _dot(a, b):
    return jnp.dot(a, b, preferred_element_type=F32)


def _dot_exact(a, b):
    return jnp.dot(a, b, preferred_element_type=F32, precision=HIGHEST)


def _dot_lhs_exact(a, x):
    x1 = x.astype(BF16)
    r1 = x - x1.astype(F32)
    x2 = r1.astype(BF16)
    x3 = (r1 - x2.astype(F32)).astype(BF16)
    return _dot(a, x1) + _dot(a, x2) + _dot(a, x3)


def _dot_bf16x3(a, b):
    a1 = a.astype(BF16)
    a2 = (a - a1.astype(F32)).astype(BF16)
    b1 = b.astype(BF16)
    b2 = (b - b1.astype(F32)).astype(BF16)
    return _dot(a1, b1) + _dot(a1, b2) + _dot(a2, b1)


def _dot_nt(a, b):
    return lax.dot_general(a, b, (((1,), (1,)), ((), ())), preferred_element_type=F32)


def _sigmoid(x):
    return 0.5 * jnp.tanh(0.5 * x) + 0.5


def _silu(x):
    return x * _sigmoid(x)


def _adaln_kernel(c_ref, w_ref, b_ref, o_ref):
    cond = _silu(c_ref[...])
    o_ref[0] = _dot_exact(cond, w_ref[0]) + b_ref[0]


def adaln_mod(c, ada_w, ada_b):
    depth, two, d, d3 = ada_w.shape
    b = c.shape[0]
    ls = depth * two
    tn = 1024
    out = pl.pallas_call(
        _adaln_kernel,
        grid=(ls, d3 // tn),
        in_specs=[
            pl.BlockSpec((b, d), lambda i, j: (0, 0)),
            pl.BlockSpec((1, d, tn), lambda i, j: (i, 0, j)),
            pl.BlockSpec((1, 1, tn), lambda i, j: (i, 0, j)),
        ],
        out_specs=pl.BlockSpec((1, b, tn), lambda i, j: (i, 0, j)),
        out_shape=jax.ShapeDtypeStruct((ls, b, d3), F32),
        compiler_params=_cparams("parallel", "parallel"),
        name="adaln_mod",
    )(c, ada_w.reshape(ls, d, d3), ada_b.reshape(ls, 1, d3))
    return out.reshape(depth, two, b, d3)


def _modulated_norm(x, gain, shift, scale):
    ms = jnp.mean(x * x, axis=-1, keepdims=True)
    y = x * lax.rsqrt(ms + NORM_EPS) * gain
    return y * (1.0 + scale) + shift


def _norm_matmul_kernel(x_ref, g_ref, sh_ref, sc_ref, w_ref, o_ref, h_sc):
    @pl.when(pl.program_id(2) == 0)
    def _():
        h = _modulated_norm(x_ref[0], g_ref[...], sh_ref[0], sc_ref[0])
        h_sc[...] = h.astype(BF16)

    o_ref[0] = _dot(h_sc[...], w_ref[...])


def norm_matmul(x, gain, shift, scale, w, tm=1024, tn=1024):
    b, t, d = x.shape
    n = w.shape[1]
    tm = min(tm, t)
    return pl.pallas_call(
        _norm_matmul_kernel,
        grid=(b, t // tm, n // tn),
        in_specs=[
            pl.BlockSpec((1, tm, d), lambda bi, i, j: (bi, i, 0)),
            pl.BlockSpec((1, d), lambda bi, i, j: (0, 0)),
            pl.BlockSpec((1, 1, d), lambda bi, i, j: (bi, 0, 0)),
            pl.BlockSpec((1, 1, d), lambda bi, i, j: (bi, 0, 0)),
            pl.BlockSpec((d, tn), lambda bi, i, j: (0, j)),
        ],
        out_specs=pl.BlockSpec((1, tm, tn), lambda bi, i, j: (bi, i, j)),
        out_shape=jax.ShapeDtypeStruct((b, t, n), F32),
        scratch_shapes=[pltpu.VMEM((tm, d), BF16)],
        compiler_params=_cparams("parallel", "parallel", "arbitrary"),
        name="norm_matmul",
    )(x, gain.reshape(1, d), shift.reshape(b, 1, d), scale.reshape(b, 1, d), w)


def _out_proj_kernel(o_ref, w_ref, x_ref, gt_ref, y_ref):
    y = _dot(o_ref[0], w_ref[...])
    y_ref[0] = x_ref[0] + gt_ref[0] * y


def out_proj_residual(o, w, x, gate, tm=512):
    b, t, d = x.shape
    k = o.shape[-1]
    tm = min(tm, t)
    return pl.pallas_call(
        _out_proj_kernel,
        grid=(b, t // tm),
        in_specs=[
            pl.BlockSpec((1, tm, k), lambda bi, i: (bi, i, 0)),
            pl.BlockSpec((k, d), lambda bi, i: (0, 0)),
            pl.BlockSpec((1, tm, d), lambda bi, i: (bi, i, 0)),
            pl.BlockSpec((1, 1, d), lambda bi, i: (bi, 0, 0)),
        ],
        out_specs=pl.BlockSpec((1, tm, d), lambda bi, i: (bi, i, 0)),
        out_shape=jax.ShapeDtypeStruct((b, t, d), F32),
        compiler_params=_cparams("parallel", "parallel"),
        name="out_proj_residual",
    )(o, w, x, gate.reshape(b, 1, d))


def _pair_rms(x, gain2):
    lane = lax.broadcasted_iota(jnp.int32, x.shape, 1)
    lo = lane < NSA_HEAD_DIM
    x2 = x * x
    s_lo = jnp.sum(jnp.where(lo, x2, 0.0), axis=-1, keepdims=True)
    s_hi = jnp.sum(jnp.where(lo, 0.0, x2), axis=-1, keepdims=True)
    inv = jnp.where(lo, lax.rsqrt(s_lo / NSA_HEAD_DIM + NORM_EPS),
                    lax.rsqrt(s_hi / NSA_HEAD_DIM + NORM_EPS))
    return x * inv * gain2


def _nsa_prep_kernel(kc0_ref, kc1_ref, vc0_ref, vc1_ref, ks_ref, vs_ref, kw_ref, vw_ref,
                     kg_ref, pe_ref, w1_ref, w1big_ref, w2big_ref, ex_ref,
                     kco_ref, vct_ref, kso_ref, vst_ref, kwo_ref, vwt_ref):
    t = ks_ref.shape[1]
    nblk = t // CMP_STRIDE
    dh = NSA_HEAD_DIM
    G = NSA_KV_GROUPS

    for src, dst, gi, off in ((ks_ref, kso_ref, 1, LANES), (kw_ref, kwo_ref, 2, 0)):
        g2 = kg_ref[gi:gi + 1, :]
        for gp in range(G // 2):
            kn = _pair_rms(src[0, :, gp * LANES:(gp + 1) * LANES], g2).astype(BF16)
            dst[0, 2 * gp, :, off:off + dh] = kn[:, :dh]
            dst[0, 2 * gp + 1, :, off:off + dh] = kn[:, dh:]
    for g in range(G):
        kso_ref[0, g, :, :LANES] = ex_ref[...]
    for src, dst in ((vs_ref, vst_ref), (vw_ref, vwt_ref)):
        for gp in range(G // 2):
            vt = src[0, :, gp * LANES:(gp + 1) * LANES].T.astype(BF16)
            dst[0, 2 * gp] = vt[:dh]
            dst[0, 2 * gp + 1] = vt[dh:]

    for ci, (srcs, is_key) in enumerate((((kc0_ref, kc1_ref), True), ((vc0_ref, vc1_ref), False))):
        pe1 = _dot(pe_ref[ci].astype(BF16), w1_ref[ci])[0:1]
        pe2 = jnp.concatenate([pe1, pe1], axis=1)
        for gp, src in enumerate(srcs):
            parts = [src[0, pl.ds(l, nblk, stride=CMP_STRIDE), :] for l in range(CMP_STRIDE)]
            r = jnp.concatenate(parts, axis=1).astype(BF16)
            ab = _dot(r, w1big_ref[ci])
            second = ab[:, LANES:]
            shifted = jnp.concatenate([second[1:], jnp.zeros((1, LANES), F32)], axis=0)
            pre = ab[:, :LANES] + shifted + pe2
            hid = _dot(_silu(pre).astype(BF16), w2big_ref[ci])
            if is_key:
                kn = _pair_rms(hid, kg_ref[0:1, :]).astype(BF16)
                kco_ref[0, 2 * gp] = kn[:, :dh]
                kco_ref[0, 2 * gp + 1] = kn[:, dh:]
            else:
                vt = hid.T.astype(BF16)
                vct_ref[0, 2 * gp] = vt[:dh]
                vct_ref[0, 2 * gp + 1] = vt[dh:]


def nsa_prep(proj, k_gain, cmp_pe, cmp_w1, cmp_w2):
    b, t, _ = proj.shape
    G, dh = NSA_KV_GROUPS, NSA_HEAD_DIM
    kvw = G * dh
    q_blocks = (NSA_HEADS * dh) // kvw
    nblk = t // CMP_STRIDE
    kg2 = jnp.concatenate([k_gain, k_gain], axis=1)
    pe_flat = jnp.broadcast_to(cmp_pe.reshape(2, 1, CMP_BLOCK * dh), (2, 8, CMP_BLOCK * dh))

    w1r = cmp_w1.reshape(2, 2, CMP_STRIDE, dh, dh)
    eye2 = jnp.eye(2, dtype=cmp_w1.dtype)
    w1big = jnp.einsum('chlde,gk->clgdhke', w1r, eye2).reshape(2, CMP_STRIDE * 2 * dh, 4 * dh)
    w2big = jnp.einsum('cde,gk->cgdke', cmp_w2, eye2).reshape(2, 2 * dh, 2 * dh)
    expand = jnp.asarray((np.arange(t)[:, None] // SEL_BLOCK) == np.arange(LANES)[None, :], BF16)

    def col(i):
        return pl.BlockSpec((1, t, kvw), lambda bi, i=i: (bi, 0, q_blocks + i))

    def col_pair(i, gp):
        return pl.BlockSpec((1, t, LANES), lambda bi: (bi, 0, (q_blocks + i) * (kvw // LANES) + gp))

    def full(shape):
        return pl.BlockSpec(shape, lambda bi: (0,) * len(shape))

    def per_b(shape):
        return pl.BlockSpec((1,) + shape, lambda bi: (bi,) + (0,) * len(shape))

    return pl.pallas_call(
        _nsa_prep_kernel,
        grid=(b,),
        in_specs=[col_pair(0, 0), col_pair(0, 1), col_pair(1, 0), col_pair(1, 1),
                  col(2), col(3), col(4), col(5),
                  full((3, 2 * dh)), full((2, 8, CMP_BLOCK * dh)),
                  full((2, CMP_BLOCK * dh, dh)), full((2, 2 * CMP_STRIDE * dh, 4 * dh)),
                  full((2, 2 * dh, 2 * dh)), full((t, LANES))],
        out_specs=[per_b((G, nblk, dh)), per_b((G, dh, nblk)),
                   per_b((G, t, LANES + dh)), per_b((G, dh, t)),
                   per_b((G, t, dh)), per_b((G, dh, t))],
        out_shape=[jax.ShapeDtypeStruct((b, G, nblk, dh), BF16),
                   jax.ShapeDtypeStruct((b, G, dh, nblk), BF16),
                   jax.ShapeDtypeStruct((b, G, t, LANES + dh), BF16),
                   jax.ShapeDtypeStruct((b, G, dh, t), BF16),
                   jax.ShapeDtypeStruct((b, G, t, dh), BF16),
                   jax.ShapeDtypeStruct((b, G, dh, t), BF16)],
        compiler_params=_cparams("parallel"),
        name="nsa_prep",
    )(proj, proj, proj, proj, proj, proj, proj, proj, kg2, pe_flat,
      cmp_w1.astype(BF16), w1big.astype(BF16), w2big.astype(BF16), expand)


def _nsa_attn_kernel(q_ref, gl_ref, pgt_ref, qg_ref, kc_ref, vct_ref, ksa_ref, vst_ref,
                     kw_ref, vwt_ref, bc_ref, bs_ref, bw_ref, c2st_ref,
                     o_ref, acc_sc, qsel_sc, qwin_sc, pc_sc, ssa_sc, ssb_sc, swa_sc, swb_sc, *,
                     e_sat, n_wtiles,
                     n_sel_blocks, n_select):
    tq, tk = ATT_TQ, ATT_TK
    R, dh = NSA_GROUP_SIZE, NSA_HEAD_DIM
    qi = pl.program_id(2)
    scale = dh ** -0.5 * LOG2E

    gl = gl_ref[0]
    g1 = gl.astype(BF16)
    gr = gl - g1.astype(F32)
    g2 = gr.astype(BF16)
    g3 = (gr - g2.astype(F32)).astype(BF16)
    gates = _sigmoid(_dot_nt(pgt_ref[0], g1) + _dot_nt(pgt_ref[0], g2) + _dot_nt(pgt_ref[0], g3))

    qt = q_ref[0].T
    for r in range(R):
        qr = qt[r * dh:(r + 1) * dh]
        ms = jnp.mean(qr * qr, axis=0, keepdims=True)
        qn = (qr * lax.rsqrt(ms + NORM_EPS) * qg_ref[...] * scale).astype(BF16)
        qwin_sc[:, r * tq:(r + 1) * tq] = qn
        qsel_sc[LANES:, r * tq:(r + 1) * tq] = qn

    ones_rows = jnp.ones((16, tk), BF16)
    rr = tq // tk
    n_tiles = rr * qi + rr

    def make_sweep(k_ref, vt_ref, q_sc, b_ref, hi, e_last, sa_sc, sb_sc):
        def tile_col(d):
            dc = jnp.minimum(d, hi - 1)
            e = jnp.where(dc == 0, rr - 1, jnp.where(dc < rr, dc - 1, dc))
            return pl.multiple_of((n_tiles - 1 - e) * tk, tk), jnp.minimum(e, e_last)

        def logits(d, dst_sc):
            col, _ = tile_col(d)
            dst_sc[...] = _dot(k_ref[0, 0, pl.ds(col, tk), :], q_sc[...])

        def softmax_pv(d, src_sc, ms):
            col, bi = tile_col(d)
            vaug = jnp.concatenate([vt_ref[0, 0, :, pl.ds(col, tk)], ones_rows], axis=0)
            new_ms = []
            for r in range(R):
                s = src_sc[:, r * tq:(r + 1) * tq] + b_ref[r, bi]
                m_new = jnp.maximum(ms[r], jnp.max(s, axis=0, keepdims=True))
                alpha = jnp.exp2(ms[r] - m_new)
                p = jnp.exp2(s - m_new).astype(BF16)
                acc_sc[r] = acc_sc[r] * alpha + _dot(vaug, p)
                new_ms.append(m_new)
            return tuple(new_ms)

        def body(dp, ms):
            d = 2 * dp
            logits(d + 1, sb_sc)
            ms = softmax_pv(d, sa_sc, ms)
            logits(d + 2, sa_sc)
            return softmax_pv(d + 1, sb_sc, ms)

        def start():
            logits(0, sa_sc)

        def run():
            acc_sc[...] = jnp.zeros(acc_sc.shape, F32)
            ms = lax.fori_loop(0, hi // 2, body,
                               tuple(jnp.full((1, tq), NEG, F32) for _ in range(R)))

            @pl.when(hi % 2 == 1)
            def _():
                softmax_pv(hi - 1, sa_sc, ms)

            outs = []
            for r in range(R):
                acc = acc_sc[r]
                outs.append(acc[:dh] * (1.0 / acc[dh:dh + 1]))
            return outs

        return start, run

    win_start, win_run = make_sweep(kw_ref, vwt_ref, qwin_sc, bw_ref,
                                    jnp.minimum(n_tiles, n_wtiles), n_wtiles - 1, swa_sc, swb_sc)
    win_start()

    ncp = kc_ref.shape[2]
    n_idx = lax.broadcasted_iota(jnp.int32, (ncp, tq), 0)
    t_idx = qi * tq + lax.broadcasted_iota(jnp.int32, (ncp, tq), 1)
    vis = (n_idx * CMP_STRIDE + (CMP_BLOCK - 1)) <= t_idx
    psum = jnp.zeros((ncp, tq), F32)
    sc_all = _dot(kc_ref[0, 0], qwin_sc[...])
    for r in range(R):
        s = jnp.where(vis, sc_all[:, r * tq:(r + 1) * tq] + bc_ref[r], NEG)
        m = jnp.max(s, axis=0, keepdims=True)
        e = jnp.where(vis, jnp.exp2(s - m), 0.0)
        p = e * (1.0 / jnp.maximum(jnp.sum(e, axis=0, keepdims=True), 1e-30))
        psum = psum + p
        pc_sc[:, r * tq:(r + 1) * tq] = p.astype(BF16)
    oc_all = _dot(vct_ref[0, 0], pc_sc[...])
    o_cmp = [oc_all[:, r * tq:(r + 1) * tq] for r in range(R)]

    imp_t = _dot_lhs_exact(c2st_ref[...], psum)
    blk = lax.broadcasted_iota(jnp.int32, (n_sel_blocks, tq), 0)
    tpos = qi * tq + lax.broadcasted_iota(jnp.int32, (n_sel_blocks, tq), 1)
    cur = tpos // SEL_BLOCK
    forced = (blk == 0) | (blk == cur) | (blk == cur - 1)
    score = jnp.where(forced, FORCED_SCORE, jnp.where(blk <= cur, imp_t, -1.0))
    rank = jnp.zeros((n_sel_blocks, tq), F32)
    for s2 in range(n_sel_blocks):
        row = score[s2:s2 + 1, :]
        beats = (row > score) | ((row == score) & (blk > s2))
        rank = rank + jnp.where(beats, 1.0, 0.0)
    negsel = jnp.where(rank < n_select, 0.0, NEG)
    if n_sel_blocks < LANES:
        negsel = jnp.concatenate([negsel, jnp.zeros((LANES - n_sel_blocks, tq), F32)], axis=0)
    negsel = negsel.astype(BF16)
    for r in range(R):
        qsel_sc[:LANES, r * tq:(r + 1) * tq] = negsel

    sel_start, sel_run = make_sweep(ksa_ref, vst_ref, qsel_sc, bs_ref, n_tiles, e_sat,
                                    ssa_sc, ssb_sc)
    sel_start()
    o_win = win_run()
    o_sel = sel_run()

    outs = []
    for r in range(R):
        outs.append(gates[3 * r:3 * r + 1] * o_cmp[r]
                    + gates[3 * r + 1:3 * r + 2] * o_sel[r]
                    + gates[3 * r + 2:3 * r + 3] * o_win[r])
    o_ref[0] = jnp.concatenate(outs, axis=0).T.astype(o_ref.dtype)


def _t5_bucket(dist):
    n = jnp.maximum(dist, 0)
    max_exact = REL_BUCKETS // 2
    nf = jnp.maximum(n, 1).astype(F32)
    large = max_exact + (jnp.log(nf / max_exact) / math.log(REL_MAX_DISTANCE / max_exact)
                         * (REL_BUCKETS - max_exact)).astype(jnp.int32)
    large = jnp.minimum(large, REL_BUCKETS - 1)
    return jnp.where(n < max_exact, n, large)


def _bias_of_dist(dist, rel_bias):
    onehot = (_t5_bucket(dist)[..., None] == jnp.arange(REL_BUCKETS)).astype(F32)
    out = jnp.einsum('...k,kh->...h', onehot, rel_bias.astype(F32), precision=HIGHEST)
    return jnp.moveaxis(out, -1, 0)


def _saturation_distance():
    max_exact = REL_BUCKETS // 2
    steps = REL_BUCKETS - max_exact
    n_sat = max_exact * (REL_MAX_DISTANCE / max_exact) ** ((steps - 1) / steps)
    return int(math.ceil(n_sat)) + 2


def nsa_attention(proj, kc, vct, ksa, vst, kw, vwt, rel_bias, q_gain):
    b, t, _ = proj.shape
    G, R, dh = NSA_KV_GROUPS, NSA_GROUP_SIZE, NSA_HEAD_DIM
    H = NSA_HEADS
    tq, tk = ATT_TQ, ATT_TK
    rr = tq // tk
    ncp = kc.shape[2]
    n_sel_blocks = t // SEL_BLOCK
    n_select = min(N_SELECT, n_sel_blocks)
    assert n_sel_blocks <= LANES and n_sel_blocks % 8 == 0 and tq % tk == 0 and t % tq == 0

    e_sat = -(-(_saturation_distance() + tk - 1) // tk) + rr - 1
    n_wtiles = -(-(WINDOW + tk - 1) // tk) + rr - 1
    jj = np.arange(tk)[:, None]
    ii = np.arange(tq)[None, :]

    def tile_dist(n_e):
        return (np.arange(n_e)[:, None, None] - (rr - 1)) * tk + (ii - jj)[None]

    dist = tile_dist(e_sat + 1)
    bias_sel = jnp.where(dist >= 0, _bias_of_dist(jnp.asarray(dist), rel_bias) * LOG2E, NEG)
    dwin = tile_dist(n_wtiles)
    bias_win = jnp.where((dwin >= 0) & (dwin < WINDOW),
                         _bias_of_dist(jnp.asarray(dwin), rel_bias) * LOG2E, NEG)
    dc = np.arange(t)[None, :] - (np.arange(ncp)[:, None] * CMP_STRIDE + CMP_BLOCK - 1)
    bias_c = _bias_of_dist(jnp.asarray(dc), rel_bias) * LOG2E

    cs = np.arange(ncp) * CMP_STRIDE
    ss = np.arange(n_sel_blocks) * SEL_BLOCK
    shared = (np.minimum(cs[None, :] + CMP_BLOCK, ss[:, None] + SEL_BLOCK)
              - np.maximum(cs[None, :], ss[:, None]))
    c2st = jnp.asarray(np.clip(shared, 0, None) / CMP_BLOCK, BF16)
    pgt = np.zeros((G, LANES, LANES), np.float32)
    for g in range(G):
        for k in range(3 * R):
            pgt[g, k, 3 * R * g + k] = 1.0
    pgt = jnp.asarray(pgt, BF16)
    gate_blk = (H * dh + 6 * G * dh) // LANES
    qg = jnp.broadcast_to(q_gain.reshape(dh, 1), (dh, tq))

    kernel = functools.partial(_nsa_attn_kernel, e_sat=e_sat, n_wtiles=n_wtiles,
                               n_sel_blocks=n_sel_blocks, n_select=n_select)
    return pl.pallas_call(
        kernel,
        grid=(b, G, t // tq),
        in_specs=[
            pl.BlockSpec((1, tq, R * dh), lambda bi, g, i: (bi, i, g)),
            pl.BlockSpec((1, tq, LANES), lambda bi, g, i: (bi, i, gate_blk)),
            pl.BlockSpec((1, LANES, LANES), lambda bi, g, i: (g, 0, 0)),
            pl.BlockSpec((dh, tq), lambda bi, g, i: (0, 0)),
            pl.BlockSpec((1, 1, ncp, dh), lambda bi, g, i: (bi, g, 0, 0)),
            pl.BlockSpec((1, 1, dh, ncp), lambda bi, g, i: (bi, g, 0, 0)),
            pl.BlockSpec((1, 1, t, LANES + dh), lambda bi, g, i: (bi, g, 0, 0)),
            pl.BlockSpec((1, 1, dh, t), lambda bi, g, i: (bi, g, 0, 0)),
            pl.BlockSpec((1, 1, t, dh), lambda bi, g, i: (bi, g, 0, 0)),
            pl.BlockSpec((1, 1, dh, t), lambda bi, g, i: (bi, g, 0, 0)),
            pl.BlockSpec((R, ncp, tq), lambda bi, g, i: (g, 0, i)),
            pl.BlockSpec((R, e_sat + 1, tk, tq), lambda bi, g, i: (g, 0, 0, 0)),
            pl.BlockSpec((R, n_wtiles, tk, tq), lambda bi, g, i: (g, 0, 0, 0)),
            pl.BlockSpec((n_sel_blocks, ncp), lambda bi, g, i: (0, 0)),
        ],
        out_specs=pl.BlockSpec((1, tq, R * dh), lambda bi, g, i: (bi, i, g)),
        out_shape=jax.ShapeDtypeStruct((b, t, H * dh), BF16),
        scratch_shapes=[pltpu.VMEM((R, dh + 16, tq), F32),
                        pltpu.VMEM((LANES + dh, R * tq), BF16),
                        pltpu.VMEM((dh, R * tq), BF16),
                        pltpu.VMEM((ncp, R * tq), BF16),
                        pltpu.VMEM((tk, R * tq), F32), pltpu.VMEM((tk, R * tq), F32),
                        pltpu.VMEM((tk, R * tq), F32), pltpu.VMEM((tk, R * tq), F32)],
        compiler_params=_cparams("parallel", "parallel", "arbitrary"),
        name="nsa_attention",
    )(proj, proj, pgt, qg, kc, vct, ksa, vst, kw, vwt, bias_c, bias_sel, bias_win, c2st)


def nsa_layer(x, shift, scale, gate, norm_g, rel_bias, w_in, q_gain, k_gain,
              cmp_pe, cmp_w1, cmp_w2, w_out):
    d = x.shape[-1]
    n_in = w_in.shape[1]
    n_pad = -(-n_in // 1024) * 1024
    w_in_p = jnp.pad(w_in, ((0, 0), (0, n_pad - n_in))).astype(BF16)
    proj = norm_matmul(x, norm_g, shift, scale, w_in_p)
    kc, vct, ksa, vst, kw, vwt = nsa_prep(proj, k_gain, cmp_pe, cmp_w1, cmp_w2)
    o = nsa_attention(proj, kc, vct, ksa, vst, kw, vwt, rel_bias, q_gain)
    return out_proj_residual(o, w_out.astype(BF16), x, gate)


def _hgrn_kernel(q_ref, f_ref, v_ref, g_ref, lb_ref, og_ref, tri_ref, ones_ref,
                 o_ref, st_sc, k_sc, c_sc):
    C, dk, S = HG_CHUNK, HGRN_DK, HG_SUB
    n_sub = C // S

    @pl.when(pl.program_id(2) == 0)
    def _():
        st_sc[...] = jnp.zeros(st_sc.shape, F32)

    row = lax.broadcasted_iota(jnp.int32, (C, C), 0)
    colm = lax.broadcasted_iota(jnp.int32, (C, C), 1)
    rloc = lax.broadcasted_iota(jnp.int32, (C, dk), 0)
    diag_keep = ((row // S) == (colm // S)) & ((colm % S) <= (row % S))

    def rows_bcast(ref, hh, first, period):
        return jnp.concatenate(
            [jnp.broadcast_to(ref[hh, pl.ds(g * period + first, 1), :], (period, dk))
             for g in range(C // period)], axis=0)

    heads = range(HG_HEADS)
    sls = [slice(hh * dk, (hh + 1) * dk) for hh in heads]
    qs, ks, cums, vs = [], [], [], []
    for hh in heads:
        fl2 = f_ref[0, :, sls[hh]] * LOG2E
        lb = lb_ref[0, :, sls[hh]]
        log_sig = jnp.minimum(fl2, 0.0) - jnp.log2(1.0 + jnp.exp2(-jnp.abs(fl2)))
        ta = jnp.log2(lb)
        tb = jnp.log2(1.0 - lb) + log_sig
        lf = jnp.maximum(ta, tb) + jnp.log2(1.0 + jnp.exp2(-jnp.abs(ta - tb)))
        k = 1.0 - jnp.exp2(lf)
        cum = _dot_lhs_exact(tri_ref[...], lf)
        k_sc[hh] = k
        c_sc[hh] = cum
        ks.append(k)
        cums.append(cum)
        qs.append(_silu(q_ref[0, :, sls[hh]]))
        vs.append(v_ref[0, :, sls[hh]].astype(BF16))

    attns = []
    for hh in heads:
        pieces = []
        for j in range(S):
            kj = rows_bcast(k_sc, hh, j, S)
            cj = rows_bcast(c_sc, hh, j, S)
            pieces.append((qs[hh] * kj * jnp.exp2(jnp.minimum(cums[hh] - cj, 0.0))).astype(BF16))
        attns.append(jnp.where(diag_keep,
                               _dot(jnp.concatenate(pieces, axis=1), ones_ref[...]), 0.0))

    m = S
    while m < C:
        upper = (rloc // m) % 2 == 1
        same = (row // (2 * m)) == (colm // (2 * m))
        for hh in heads:
            e = jnp.exp2(-jnp.abs(cums[hh] - rows_bcast(c_sc, hh, m - 1, 2 * m)))
            qm = jnp.where(upper, qs[hh] * e, 0.0).astype(BF16)
            km = jnp.where(upper, 0.0, ks[hh] * e).astype(BF16)
            attns[hh] = attns[hh] + jnp.where(same, _dot_nt(qm, km), 0.0)
        m *= 2

    for hh in heads:
        st = st_sc[hh]
        cum, k, q, v = cums[hh], ks[hh], qs[hh], vs[hh]
        o = _dot(attns[hh].astype(BF16), v)
        o = o + _dot_nt((q * jnp.exp2(cum)).astype(BF16), st.astype(BF16))
        total = cum[C - 1:C, :]
        kd = (k * jnp.exp2(total - cum)).astype(BF16)
        st_sc[hh] = st * jnp.exp2(total) + lax.dot_general(
            v, kd, (((0,), (0,)), ((), ())), preferred_element_type=F32)
        ms = jnp.mean(o * o, axis=-1, keepdims=True)
        o = o * lax.rsqrt(ms + NORM_EPS) * og_ref[...]
        o_ref[0, :, sls[hh]] = (o * _silu(g_ref[0, :, sls[hh]])).astype(o_ref.dtype)


def hgrn_recurrence(proj, lb, out_gain):
    b, t, four_d = proj.shape
    d = four_d // 4
    dk = HGRN_DK
    C, S, hps = HG_CHUNK, HG_SUB, HG_HEADS
    w = hps * dk
    nhp = d // w
    tri = jnp.asarray(np.tril(np.ones((C, C), np.float32)), BF16)
    ones = jnp.asarray(np.arange(S * dk)[:, None] // dk == (np.arange(C)[None, :] % S), BF16)
    return pl.pallas_call(
        _hgrn_kernel,
        grid=(b, nhp, t // C),
        in_specs=[
            pl.BlockSpec((1, C, w), lambda bi, h, c: (bi, c, h)),
            pl.BlockSpec((1, C, w), lambda bi, h, c: (bi, c, nhp + h)),
            pl.BlockSpec((1, C, w), lambda bi, h, c: (bi, c, 2 * nhp + h)),
            pl.BlockSpec((1, C, w), lambda bi, h, c: (bi, c, 3 * nhp + h)),
            pl.BlockSpec((1, 1, w), lambda bi, h, c: (h, 0, 0)),
            pl.BlockSpec((1, dk), lambda bi, h, c: (0, 0)),
            pl.BlockSpec((C, C), lambda bi, h, c: (0, 0)),
            pl.BlockSpec((S * dk, C), lambda bi, h, c: (0, 0)),
        ],
        out_specs=pl.BlockSpec((1, C, w), lambda bi, h, c: (bi, c, h)),
        out_shape=jax.ShapeDtypeStruct((b, t, d), BF16),
        scratch_shapes=[pltpu.VMEM((hps, dk, dk), F32), pltpu.VMEM((hps, C, dk), F32),
                        pltpu.VMEM((hps, C, dk), F32)],
        compiler_params=_cparams("parallel", "parallel", "arbitrary"),
        name="hgrn_recurrence",
    )(proj, proj, proj, proj, lb.reshape(nhp, 1, w), out_gain.reshape(1, dk), tri, ones)


def hgrn_layer(x, shift, scale, gate, norm_g, lb, w_in, out_gain, w_out):
    proj = norm_matmul(x, norm_g, shift, scale, w_in.astype(BF16))
    o = hgrn_recurrence(proj, lb, out_gain)
    return out_proj_residual(o, w_out.astype(BF16), x, gate)


def _router_kernel(x_ref, g_ref, sh_ref, sc_ref, wr_ref, u_ref,
                   h_ref, eid_ref, ew_ref, pos_ref, cnt_ref, run_sc):
    @pl.when((pl.program_id(0) == 0) & (pl.program_id(1) == 0))
    def _():
        run_sc[...] = jnp.zeros(run_sc.shape, F32)

    h = _modulated_norm(x_ref[0], g_ref[...], sh_ref[0], sc_ref[0])
    h_ref[0] = h.astype(BF16)
    lt = _dot_bf16x3(h, wr_ref[...]).T
    NG, EPG = MOE_GROUPS, MOE_EPG

    def softmax_rows(rows):
        mx = functools.reduce(jnp.maximum, rows)
        es = [jnp.exp(r - mx) for r in rows]
        tot = functools.reduce(lambda a, c: a + c, es)
        return [e / tot for e in es]

    def argmax_rows(rows):
        best, idx = rows[0], jnp.zeros(rows[0].shape, jnp.int32)
        for i in range(1, len(rows)):
            better = rows[i] > best
            best = jnp.where(better, rows[i], best)
            idx = jnp.where(better, i, idx)
        return best, idx

    pg = softmax_rows([lt[i:i + 1] for i in range(NG)])
    p_grp, grp = argmax_rows(pg)
    el = []
    for j in range(EPG):
        acc = lt[NG + j:NG + j + 1]
        for gi in range(1, NG):
            acc = jnp.where(grp == gi, lt[NG + gi * EPG + j:NG + gi * EPG + j + 1], acc)
        el.append(acc)
    pe = softmax_rows(el)
    p1, i1 = argmax_rows(pe)
    p2, i2 = argmax_rows([jnp.where(i1 == j, -1.0, pe[j]) for j in range(EPG)])
    den = p1 + p2
    e1 = grp * EPG + i1
    e2 = grp * EPG + i2
    eid_ref[0:1, :] = e1
    eid_ref[1:2, :] = e2
    ew_ref[0:1, :] = p_grp * p1 / den
    ew_ref[1:2, :] = p_grp * p2 / den

    tm = e1.shape[1]
    ex = lax.broadcasted_iota(jnp.int32, (MOE_EXPERTS, tm), 0)
    oh1 = jnp.where(ex == e1, 1.0, 0.0)
    oh2 = jnp.where(ex == e2, 1.0, 0.0)
    before1 = _dot(oh1.astype(BF16), u_ref[...])
    before2 = _dot(oh2.astype(BF16), u_ref[...])
    tot1 = jnp.sum(oh1, axis=1, keepdims=True)
    tot2 = jnp.sum(oh2, axis=1, keepdims=True)
    run = run_sc[...]
    pos1 = jnp.sum(oh1 * (before1 + run), axis=0, keepdims=True)
    pos2 = jnp.sum(oh2 * (before2 + (run + tot1)), axis=0, keepdims=True)
    pos_ref[0:1, :] = pos1.astype(jnp.int32)
    pos_ref[1:2, :] = pos2.astype(jnp.int32)
    run = run + tot1 + tot2
    run_sc[...] = run
    cnt_ref[...] = jnp.broadcast_to(run, cnt_ref.shape)


def moe_router(x, gain, shift, scale, w_group, w_expert, tm=512):
    b, t, d = x.shape
    tm = min(tm, t)
    nt = t // tm
    wr = jnp.concatenate([w_group, w_expert], axis=1)
    wr = jnp.pad(wr, ((0, 0), (0, LANES - wr.shape[1])))
    upper = jnp.asarray(np.triu(np.ones((tm, tm), np.float32), 1), BF16)
    return pl.pallas_call(
        _router_kernel,
        grid=(b, nt),
        in_specs=[
            pl.BlockSpec((1, tm, d), lambda bi, i: (bi, i, 0)),
            pl.BlockSpec((1, d), lambda bi, i: (0, 0)),
            pl.BlockSpec((1, 1, d), lambda bi, i: (bi, 0, 0)),
            pl.BlockSpec((1, 1, d), lambda bi, i: (bi, 0, 0)),
            pl.BlockSpec((d, LANES), lambda bi, i: (0, 0)),
            pl.BlockSpec((tm, tm), lambda bi, i: (0, 0)),
        ],
        out_specs=[
            pl.BlockSpec((1, tm, d), lambda bi, i: (bi, i, 0)),
            pl.BlockSpec((MOE_TOP_K, tm), lambda bi, i: (0, bi * nt + i)),
            pl.BlockSpec((MOE_TOP_K, tm), lambda bi, i: (0, bi * nt + i)),
            pl.BlockSpec((MOE_TOP_K, tm), lambda bi, i: (0, bi * nt + i)),
            pl.BlockSpec((MOE_EXPERTS, LANES), lambda bi, i: (0, 0)),
        ],
        out_shape=[
            jax.ShapeDtypeStruct((b, t, d), BF16),
            jax.ShapeDtypeStruct((MOE_TOP_K, b * t), jnp.int32),
            jax.ShapeDtypeStruct((MOE_TOP_K, b * t), F32),
            jax.ShapeDtypeStruct((MOE_TOP_K, b * t), jnp.int32),
            jax.ShapeDtypeStruct((MOE_EXPERTS, LANES), F32),
        ],
        scratch_shapes=[pltpu.VMEM((MOE_EXPERTS, 1), F32)],
        compiler_params=_cparams("arbitrary", "arbitrary"),
        name="moe_router",
    )(x, gain.reshape(1, d), shift.reshape(b, 1, d), scale.reshape(b, 1, d), wr, upper)


def _expert_ffn_kernel(be_ref, nu_ref, xb_ref, w1_ref, w3_ref, w2_ref, yb_ref,
                       w1_sc, w3_sc, w2_sc):
    i = pl.program_id(0)

    @pl.when((i == 0) | (be_ref[i] != be_ref[jnp.maximum(i - 1, 0)]))
    def _():
        w1_sc[...] = w1_ref[0].astype(BF16)
        w3_sc[...] = w3_ref[0].astype(BF16)
        w2_sc[...] = w2_ref[0].astype(BF16)

    @pl.when(i < nu_ref[0])
    def _():
        xb = xb_ref[...]
        a = _dot(xb, w1_sc[...])
        g = _dot(xb, w3_sc[...])
        yb_ref[...] = _dot((_silu(a) * g).astype(BF16), w2_sc[...]).astype(yb_ref.dtype)

    @pl.when(i >= nu_ref[0])
    def _():
        yb_ref[...] = jnp.zeros(yb_ref.shape, yb_ref.dtype)


def expert_ffn(xb, blk_e, n_used, w1, w3, w2):
    p, d = xb.shape
    ff = w1.shape[2]
    rb = MOE_ROW_BLOCK
    grid_spec = pltpu.PrefetchScalarGridSpec(
        num_scalar_prefetch=2,
        grid=(p // rb,),
        in_specs=[
            pl.BlockSpec((rb, d), lambda i, be, nu: (i, 0)),
            pl.BlockSpec((1, d, ff), lambda i, be, nu: (be[i], 0, 0)),
            pl.BlockSpec((1, d, ff), lambda i, be, nu: (be[i], 0, 0)),
            pl.BlockSpec((1, ff, d), lambda i, be, nu: (be[i], 0, 0)),
        ],
        out_specs=pl.BlockSpec((rb, d), lambda i, be, nu: (i, 0)),
        scratch_shapes=[pltpu.VMEM((d, ff), BF16), pltpu.VMEM((d, ff), BF16),
                        pltpu.VMEM((ff, d), BF16)],
    )
    return pl.pallas_call(
        _expert_ffn_kernel,
        grid_spec=grid_spec,
        out_shape=jax.ShapeDtypeStruct((p, d), BF16),
        compiler_params=_cparams("arbitrary"),
        name="expert_ffn",
    )(blk_e, n_used, xb, w1, w3, w2)


def _combine_kernel(x_ref, gt_ref, y1_ref, y2_ref, w_ref, o_ref):
    w = w_ref[0]
    y = w[:, 0:1] * y1_ref[0].astype(F32) + w[:, 1:2] * y2_ref[0].astype(F32)
    o_ref[0] = x_ref[0] + gt_ref[0] * y


def moe_combine(x, gate, y1, y2, w, tm=512):
    b, t, d = x.shape
    tm = min(tm, t)
    spec = pl.BlockSpec((1, tm, d), lambda bi, i: (bi, i, 0))
    return pl.pallas_call(
        _combine_kernel,
        grid=(b, t // tm),
        in_specs=[spec, pl.BlockSpec((1, 1, d), lambda bi, i: (bi, 0, 0)), spec, spec,
                  pl.BlockSpec((1, tm, MOE_TOP_K), lambda bi, i: (bi, i, 0))],
        out_specs=spec,
        out_shape=jax.ShapeDtypeStruct((b, t, d), F32),
        compiler_params=_cparams("parallel", "parallel"),
        name="moe_combine",
    )(x, gate.reshape(b, 1, d), y1, y2, w)


def moe_layer(x, shift, scale, gate, norm_g, w_group, w_expert, w1, w3, w2):
    b, t, d = x.shape
    n = b * t
    a = n * MOE_TOP_K
    rb = MOE_ROW_BLOCK
    h, eid, ew, pos, cnt = moe_router(x, norm_g, shift, scale, w_group, w_expert)
    counts = cnt[:, 0].astype(jnp.int32)
    padded = (counts + rb - 1) // rb * rb
    pad_end = jnp.cumsum(padded)
    pad_start = pad_end - padded
    is_e = eid[..., None] == jnp.arange(MOE_EXPERTS, dtype=jnp.int32)
    dest = jnp.sum(jnp.where(is_e, pad_start, 0), axis=-1) + pos
    n_blk = -(-(a + MOE_EXPERTS * (rb - 1)) // rb)
    p = n_blk * rb
    blk_start = jnp.arange(n_blk, dtype=jnp.int32) * rb
    blk_e = jnp.minimum(jnp.sum((pad_end[None, :] <= blk_start[:, None]).astype(jnp.int32), axis=1),
                        MOE_EXPERTS - 1)
    n_used = (pad_end[-1:] // rb).astype(jnp.int32)
    tok = jnp.tile(jnp.arange(n, dtype=jnp.int32), MOE_TOP_K)
    _, tok_by_slot = lax.sort_key_val(dest.reshape(a), tok)
    seg_shift = pad_start - (jnp.cumsum(counts) - counts)
    slot = jnp.arange(p, dtype=jnp.int32)
    e_of_slot = jnp.repeat(blk_e, rb)
    shift = jnp.sum(jnp.where(e_of_slot[:, None] == jnp.arange(MOE_EXPERTS, dtype=jnp.int32),
                              seg_shift, 0), axis=-1)
    idx = slot - shift
    buf_t = tok_by_slot[jnp.where(idx < a, idx, slot % a)]
    xb = h.reshape(n, d)[buf_t]
    yb = expert_ffn(xb, blk_e, n_used, w1, w3, w2)
    y1 = yb[dest[0]].reshape(b, t, d)
    y2 = yb[dest[1]].reshape(b, t, d)
    return moe_combine(x, gate, y1, y2, ew.T.reshape(b, t, MOE_TOP_K))


def kernel(x, c, ada_w, ada_b, norm_g, rel_bias, nsa_w_in, nsa_q_gain, nsa_k_gain, nsa_cmp_pe,
           nsa_cmp_w1, nsa_cmp_w2, nsa_w_out, hgrn_w_in, hgrn_lower_bounds, hgrn_out_gain,
           hgrn_w_out, moe_router_group, moe_router_expert, moe_w1, moe_w3, moe_w2):
    depth = ada_w.shape[0]
    d = x.shape[-1]
    lb_soft = jax.nn.softmax(hgrn_lower_bounds.astype(F32), axis=0)
    lb_all = jnp.cumsum(lb_soft, axis=0) - lb_soft[0]
    mod = adaln_mod(c, ada_w, ada_b)
    for layer in range(depth):
        j = layer // 2
        shift, scale, gate = (mod[layer, 0, :, i * d:(i + 1) * d] for i in range(3))
        if layer % 2 == 0:
            x = nsa_layer(x, shift, scale, gate, norm_g[layer, 0], rel_bias, nsa_w_in[j],
                          nsa_q_gain[j], nsa_k_gain[j], nsa_cmp_pe[j], nsa_cmp_w1[j],
                          nsa_cmp_w2[j], nsa_w_out[j])
        else:
            x = hgrn_layer(x, shift, scale, gate, norm_g[layer, 0], lb_all[layer],
                           hgrn_w_in[j], hgrn_out_gain[j], hgrn_w_out[j])
        shift, scale, gate = (mod[layer, 1, :, i * d:(i + 1) * d] for i in range(3))
        x = moe_layer(x, shift, scale, gate, norm_g[layer, 1], moe_router_group[layer],
                      moe_router_expert[layer], moe_w1[layer], moe_w3[layer], moe_w2[layer])
    return x
```

```python
import functools
import math

import numpy as np
import jax
import jax.numpy as jnp
from jax import lax
from jax.experimental import pallas as pl
from jax.experimental.pallas import tpu as pltpu

F32 = jnp.float32
BF16 = jnp.bfloat16
HIGHEST = lax.Precision.HIGHEST

NSA_HEADS = 16
NSA_KV_GROUPS = 4
NSA_GROUP_SIZE = NSA_HEADS // NSA_KV_GROUPS
NSA_HEAD_DIM = 64
CMP_BLOCK = 32
CMP_STRIDE = 16
SEL_BLOCK = 64
N_SELECT = 8
WINDOW = 512
FORCED_SCORE = 1.0e4
REL_BUCKETS = 32
REL_MAX_DISTANCE = 1024
HGRN_DK = 128
MOE_GROUPS = 4
MOE_EPG = 4
MOE_EXPERTS = MOE_GROUPS * MOE_EPG
MOE_TOP_K = 2
MOE_ROW_BLOCK = 512
NORM_EPS = 1e-6

LANES = 128
NEG = -1.0e30
LOG2E = math.log2(math.e)
VMEM_LIMIT = 48 * 1024 * 1024

ATT_TQ = 256
ATT_TK = 256
HG_CHUNK = 128
HG_SUB = 8
HG_HEADS = 4


def _cparams(*sem):
    return pltpu.CompilerParams(dimension_semantics=sem, vmem_limit_bytes=VMEM_LIMIT)


def _dot(a, b):
    return jnp.dot(a, b, preferred_element_type=F32)


def _dot_exact(a, b):
    return jnp.dot(a, b, preferred_element_type=F32, precision=HIGHEST)


def _dot_lhs_exact(a, x):
    x1 = x.astype(BF16)
    r1 = x - x1.astype(F32)
    x2 = r1.astype(BF16)
    x3 = (r1 - x2.astype(F32)).astype(BF16)
    return _dot(a, x1) + _dot(a, x2) + _dot(a, x3)


def _dot_bf16x3(a, b):
    a1 = a.astype(BF16)
    a2 = (a - a1.astype(F32)).astype(BF16)
    b1 = b.astype(BF16)
    b2 = (b - b1.astype(F32)).astype(BF16)
    return _dot(a1, b1) + _dot(a1, b2) + _dot(a2, b1)


def _dot_nt(a, b):
    return lax.dot_general(a, b, (((1,), (1,)), ((), ())), preferred_element_type=F32)


def _sigmoid(x):
    return 0.5 * jnp.tanh(0.5 * x) + 0.5


def _silu(x):
    return x * _sigmoid(x)


def _adaln_kernel(c_ref, w_ref, b_ref, o_ref):
    cond = _silu(c_ref[...])
    o_ref[0] = _dot_exact(cond, w_ref[0]) + b_ref[0]


def adaln_mod(c, ada_w, ada_b):
    depth, two, d, d3 = ada_w.shape
    b = c.shape[0]
    ls = depth * two
    tn = 1024
    out = pl.pallas_call(
        _adaln_kernel,
        grid=(ls, d3 // tn),
        in_specs=[
            pl.BlockSpec((b, d), lambda i, j: (0, 0)),
            pl.BlockSpec((1, d, tn), lambda i, j: (i, 0, j)),
            pl.BlockSpec((1, 1, tn), lambda i, j: (i, 0, j)),
        ],
        out_specs=pl.BlockSpec((1, b, tn), lambda i, j: (i, 0, j)),
        out_shape=jax.ShapeDtypeStruct((ls, b, d3), F32),
        compiler_params=_cparams("parallel", "parallel"),
        name="adaln_mod",
    )(c, ada_w.reshape(ls, d, d3), ada_b.reshape(ls, 1, d3))
    return out.reshape(depth, two, b, d3)


def _modulated_norm(x, gain, shift, scale):
    ms = jnp.mean(x * x, axis=-1, keepdims=True)
    y = x * lax.rsqrt(ms + NORM_EPS) * gain
    return y * (1.0 + scale) + shift


def _norm_matmul_kernel(x_ref, g_ref, sh_ref, sc_ref, w_ref, o_ref, h_sc):
    @pl.when(pl.program_id(2) == 0)
    def _():
        h = _modulated_norm(x_ref[0], g_ref[...], sh_ref[0], sc_ref[0])
        h_sc[...] = h.astype(BF16)

    o_ref[0] = _dot(h_sc[...], w_ref[...])


def norm_matmul(x, gain, shift, scale, w, tm=1024, tn=1024):
    b, t, d = x.shape
    n = w.shape[1]
    tm = min(tm, t)
    return pl.pallas_call(
        _norm_matmul_kernel,
        grid=(b, t // tm, n // tn),
        in_specs=[
            pl.BlockSpec((1, tm, d), lambda bi, i, j: (bi, i, 0)),
            pl.BlockSpec((1, d), lambda bi, i, j: (0, 0)),
            pl.BlockSpec((1, 1, d), lambda bi, i, j: (bi, 0, 0)),
            pl.BlockSpec((1, 1, d), lambda bi, i, j: (bi, 0, 0)),
            pl.BlockSpec((d, tn), lambda bi, i, j: (0, j)),
        ],
        out_specs=pl.BlockSpec((1, tm, tn), lambda bi, i, j: (bi, i, j)),
        out_shape=jax.ShapeDtypeStruct((b, t, n), F32),
        scratch_shapes=[pltpu.VMEM((tm, d), BF16)],
        compiler_params=_cparams("parallel", "parallel", "arbitrary"),
        name="norm_matmul",
    )(x, gain.reshape(1, d), shift.reshape(b, 1, d), scale.reshape(b, 1, d), w)


def _out_proj_kernel(o_ref, w_ref, x_ref, gt_ref, y_ref):
    y = _dot(o_ref[0], w_ref[...])
    y_ref[0] = x_ref[0] + gt_ref[0] * y


def out_proj_residual(o, w, x, gate, tm=512):
    b, t, d = x.shape
    k = o.shape[-1]
    tm = min(tm, t)
    return pl.pallas_call(
        _out_proj_kernel,
        grid=(b, t // tm),
        in_specs=[
            pl.BlockSpec((1, tm, k), lambda bi, i: (bi, i, 0)),
            pl.BlockSpec((k, d), lambda bi, i: (0, 0)),
            pl.BlockSpec((1, tm, d), lambda bi, i: (bi, i, 0)),
            pl.BlockSpec((1, 1, d), lambda bi, i: (bi, 0, 0)),
        ],
        out_specs=pl.BlockSpec((1, tm, d), lambda bi, i: (bi, i, 0)),
        out_shape=jax.ShapeDtypeStruct((b, t, d), F32),
        compiler_params=_cparams("parallel", "parallel"),
        name="out_proj_residual",
    )(o, w, x, gate.reshape(b, 1, d))


def _pair_rms(x, gain2):
    lane = lax.broadcasted_iota(jnp.int32, x.shape, 1)
    lo = lane < NSA_HEAD_DIM
    x2 = x * x
    s_lo = jnp.sum(jnp.where(lo, x2, 0.0), axis=-1, keepdims=True)
    s_hi = jnp.sum(jnp.where(lo, 0.0, x2), axis=-1, keepdims=True)
    inv = jnp.where(lo, lax.rsqrt(s_lo / NSA_HEAD_DIM + NORM_EPS),
                    lax.rsqrt(s_hi / NSA_HEAD_DIM + NORM_EPS))
    return x * inv * gain2


def _nsa_prep_kernel(kc0_ref, kc1_ref, vc0_ref, vc1_ref, ks_ref, vs_ref, kw_ref, vw_ref,
                     kg_ref, pe_ref, w1_ref, w1big_ref, w2big_ref, ex_ref,
                     kco_ref, vct_ref, kso_ref, vst_ref, kwo_ref, vwt_ref):
    t = ks_ref.shape[1]
    nblk = t // CMP_STRIDE
    dh = NSA_HEAD_DIM
    G = NSA_KV_GROUPS

    for src, dst, gi, off in ((ks_ref, kso_ref, 1, LANES), (kw_ref, kwo_ref, 2, 0)):
        g2 = kg_ref[gi:gi + 1, :]
        for gp in range(G // 2):
            kn = _pair_rms(src[0, :, gp * LANES:(gp + 1) * LANES], g2).astype(BF16)
            dst[0, 2 * gp, :, off:off + dh] = kn[:, :dh]
            dst[0, 2 * gp + 1, :, off:off + dh] = kn[:, dh:]
    for g in range(G):
        kso_ref[0, g, :, :LANES] = ex_ref[...]
    for src, dst in ((vs_ref, vst_ref), (vw_ref, vwt_ref)):
        for gp in range(G // 2):
            vt = src[0, :, gp * LANES:(gp + 1) * LANES].T.astype(BF16)
            dst[0, 2 * gp] = vt[:dh]
            dst[0, 2 * gp + 1] = vt[dh:]

    for ci, (srcs, is_key) in enumerate((((kc0_ref, kc1_ref), True), ((vc0_ref, vc1_ref), False))):
        pe1 = _dot(pe_ref[ci].astype(BF16), w1_ref[ci])[0:1]
        pe2 = jnp.concatenate([pe1, pe1], axis=1)
        for gp, src in enumerate(srcs):
            parts = [src[0, pl.ds(l, nblk, stride=CMP_STRIDE), :] for l in range(CMP_STRIDE)]
            r = jnp.concatenate(parts, axis=1).astype(BF16)
            ab = _dot(r, w1big_ref[ci])
            second = ab[:, LANES:]
            shifted = jnp.concatenate([second[1:], jnp.zeros((1, LANES), F32)], axis=0)
            pre = ab[:, :LANES] + shifted + pe2
            hid = _dot(_silu(pre).astype(BF16), w2big_ref[ci])
            if is_key:
                kn = _pair_rms(hid, kg_ref[0:1, :]).astype(BF16)
                kco_ref[0, 2 * gp] = kn[:, :dh]
                kco_ref[0, 2 * gp + 1] = kn[:, dh:]
            else:
                vt = hid.T.astype(BF16)
                vct_ref[0, 2 * gp] = vt[:dh]
                vct_ref[0, 2 * gp + 1] = vt[dh:]


def nsa_prep(proj, k_gain, cmp_pe, cmp_w1, cmp_w2):
    b, t, _ = proj.shape
    G, dh = NSA_KV_GROUPS, NSA_HEAD_DIM
    kvw = G * dh
    q_blocks = (NSA_HEADS * dh) // kvw
    nblk = t // CMP_STRIDE
    kg2 = jnp.concatenate([k_gain, k_gain], axis=1)
    pe_flat = jnp.broadcast_to(cmp_pe.reshape(2, 1, CMP_BLOCK * dh), (2, 8, CMP_BLOCK * dh))

    w1r = cmp_w1.reshape(2, 2, CMP_STRIDE, dh, dh)
    eye2 = jnp.eye(2, dtype=cmp_w1.dtype)
    w1big = jnp.einsum('chlde,gk->clgdhke', w1r, eye2).reshape(2, CMP_STRIDE * 2 * dh, 4 * dh)
    w2big = jnp.einsum('cde,gk->cgdke', cmp_w2, eye2).reshape(2, 2 * dh, 2 * dh)
    expand = jnp.asarray((np.arange(t)[:, None] // SEL_BLOCK) == np.arange(LANES)[None, :], BF16)

    def col(i):
        return pl.BlockSpec((1, t, kvw), lambda bi, i=i: (bi, 0, q_blocks + i))

    def col_pair(i, gp):
        return pl.BlockSpec((1, t, LANES), lambda bi: (bi, 0, (q_blocks + i) * (kvw // LANES) + gp))

    def full(shape):
        return pl.BlockSpec(shape, lambda bi: (0,) * len(shape))

    def per_b(shape):
        return pl.BlockSpec((1,) + shape, lambda bi: (bi,) + (0,) * len(shape))

    return pl.pallas_call(
        _nsa_prep_kernel,
        grid=(b,),
        in_specs=[col_pair(0, 0), col_pair(0, 1), col_pair(1, 0), col_pair(1, 1),
                  col(2), col(3), col(4), col(5),
                  full((3, 2 * dh)), full((2, 8, CMP_BLOCK * dh)),
                  full((2, CMP_BLOCK * dh, dh)), full((2, 2 * CMP_STRIDE * dh, 4 * dh)),
                  full((2, 2 * dh, 2 * dh)), full((t, LANES))],
        out_specs=[per_b((G, nblk, dh)), per_b((G, dh, nblk)),
                   per_b((G, t, LANES + dh)), per_b((G, dh, t)),
                   per_b((G, t, dh)), per_b((G, dh, t))],
        out_shape=[jax.ShapeDtypeStruct((b, G, nblk, dh), BF16),
                   jax.ShapeDtypeStruct((b, G, dh, nblk), BF16),
                   jax.ShapeDtypeStruct((b, G, t, LANES + dh), BF16),
                   jax.ShapeDtypeStruct((b, G, dh, t), BF16),
                   jax.ShapeDtypeStruct((b, G, t, dh), BF16),
                   jax.ShapeDtypeStruct((b, G, dh, t), BF16)],
        compiler_params=_cparams("parallel"),
        name="nsa_prep",
    )(proj, proj, proj, proj, proj, proj, proj, proj, kg2, pe_flat,
      cmp_w1.astype(BF16), w1big.astype(BF16), w2big.astype(BF16), expand)


def _nsa_attn_kernel(q_ref, gl_ref, pgt_ref, qg_ref, kc_ref, vct_ref, ksa_ref, vst_ref,
                     kw_ref, vwt_ref, bc_ref, bs_ref, bw_ref, c2st_ref,
                     o_ref, acc_sc, qsel_sc, qwin_sc, pc_sc, ssa_sc, ssb_sc, swa_sc, swb_sc,
                     score_sc, *,
                     e_sat, n_wtiles,
                     n_sel_blocks, n_select):
    tq, tk = ATT_TQ, ATT_TK
    R, dh = NSA_GROUP_SIZE, NSA_HEAD_DIM
    qi = pl.program_id(2)
    scale = dh ** -0.5 * LOG2E

    gl = gl_ref[0]
    g1 = gl.astype(BF16)
    gr = gl - g1.astype(F32)
    g2 = gr.astype(BF16)
    g3 = (gr - g2.astype(F32)).astype(BF16)
    gates = _sigmoid(_dot_nt(pgt_ref[0], g1) + _dot_nt(pgt_ref[0], g2) + _dot_nt(pgt_ref[0], g3))

    qt = q_ref[0].T
    for r in range(R):
        qr = qt[r * dh:(r + 1) * dh]
        ms = jnp.mean(qr * qr, axis=0, keepdims=True)
        qn = (qr * lax.rsqrt(ms + NORM_EPS) * qg_ref[...] * scale).astype(BF16)
        qwin_sc[:, r * tq:(r + 1) * tq] = qn
        qsel_sc[LANES:, r * tq:(r + 1) * tq] = qn

    ones_rows = jnp.ones((16, tk), BF16)
    rr = tq // tk
    n_tiles = rr * qi + rr

    def make_sweep(k_ref, vt_ref, q_sc, b_ref, hi, e_last, sa_sc, sb_sc):
        def tile_col(d):
            dc = jnp.minimum(d, hi - 1)
            e = jnp.where(dc == 0, rr - 1, jnp.where(dc < rr, dc - 1, dc))
            return pl.multiple_of((n_tiles - 1 - e) * tk, tk), jnp.minimum(e, e_last)

        def logits(d, dst_sc):
            col, _ = tile_col(d)
            dst_sc[...] = _dot(k_ref[0, 0, pl.ds(col, tk), :], q_sc[...])

        def softmax_pv(d, src_sc, ms):
            col, bi = tile_col(d)
            vaug = jnp.concatenate([vt_ref[0, 0, :, pl.ds(col, tk)], ones_rows], axis=0)
            new_ms = []
            for r in range(R):
                s = src_sc[:, r * tq:(r + 1) * tq] + b_ref[r, bi]
                m_new = jnp.maximum(ms[r], jnp.max(s, axis=0, keepdims=True))
                alpha = jnp.exp2(ms[r] - m_new)
                p = jnp.exp2(s - m_new).astype(BF16)
                acc_sc[r] = acc_sc[r] * alpha + _dot(vaug, p)
                new_ms.append(m_new)
            return tuple(new_ms)

        def body(dp, ms):
            d = 2 * dp
            logits(d + 1, sb_sc)
            ms = softmax_pv(d, sa_sc, ms)
            logits(d + 2, sa_sc)
            return softmax_pv(d + 1, sb_sc, ms)

        def start():
            logits(0, sa_sc)

        def run():
            acc_sc[...] = jnp.zeros(acc_sc.shape, F32)
            ms = lax.fori_loop(0, hi // 2, body,
                               tuple(jnp.full((1, tq), NEG, F32) for _ in range(R)))

            @pl.when(hi % 2 == 1)
            def _():
                softmax_pv(hi - 1, sa_sc, ms)

            outs = []
            for r in range(R):
                acc = acc_sc[r]
                outs.append(acc[:dh] * (1.0 / acc[dh:dh + 1]))
            return outs

        return start, run

    win_start, win_run = make_sweep(kw_ref, vwt_ref, qwin_sc, bw_ref,
                                    jnp.minimum(n_tiles, n_wtiles), n_wtiles - 1, swa_sc, swb_sc)
    win_start()

    ncp = kc_ref.shape[2]
    n_idx = lax.broadcasted_iota(jnp.int32, (ncp, tq), 0)
    t_idx = qi * tq + lax.broadcasted_iota(jnp.int32, (ncp, tq), 1)
    vis = (n_idx * CMP_STRIDE + (CMP_BLOCK - 1)) <= t_idx
    psum = jnp.zeros((ncp, tq), F32)
    sc_all = _dot(kc_ref[0, 0], qwin_sc[...])
    for r in range(R):
        s = jnp.where(vis, sc_all[:, r * tq:(r + 1) * tq] + bc_ref[r], NEG)
        m = jnp.max(s, axis=0, keepdims=True)
        e = jnp.where(vis, jnp.exp2(s - m), 0.0)
        p = e * (1.0 / jnp.maximum(jnp.sum(e, axis=0, keepdims=True), 1e-30))
        psum = psum + p
        pc_sc[:, r * tq:(r + 1) * tq] = p.astype(BF16)
    oc_all = _dot(vct_ref[0, 0], pc_sc[...])
    o_cmp = [oc_all[:, r * tq:(r + 1) * tq] for r in range(R)]

    imp_t = _dot_lhs_exact(c2st_ref[...], psum)
    blk = lax.broadcasted_iota(jnp.int32, (n_sel_blocks, tq), 0)
    tpos = qi * tq + lax.broadcasted_iota(jnp.int32, (n_sel_blocks, tq), 1)
    cur = tpos // SEL_BLOCK
    forced = (blk == 0) | (blk == cur) | (blk == cur - 1)
    score = jnp.where(forced, FORCED_SCORE, jnp.where(blk <= cur, imp_t, -1.0))
    score_sc[...] = score
    per_tile = tq // SEL_BLOCK

    def rank_body(g, rank):
        for u in range(per_tile):
            s2 = g * per_tile + u
            row = score_sc[pl.ds(s2, 1), :]
            beats = (row > score) | ((row == score) & (blk > s2))
            rank = rank + jnp.where(beats, 1.0, 0.0)
        return rank

    rank = lax.fori_loop(0, qi + 1, rank_body, jnp.zeros((n_sel_blocks, tq), F32))
    negsel = jnp.where(rank < n_select, 0.0, NEG)
    if n_sel_blocks < LANES:
        negsel = jnp.concatenate([negsel, jnp.zeros((LANES - n_sel_blocks, tq), F32)], axis=0)
    negsel = negsel.astype(BF16)
    for r in range(R):
        qsel_sc[:LANES, r * tq:(r + 1) * tq] = negsel

    sel_start, sel_run = make_sweep(ksa_ref, vst_ref, qsel_sc, bs_ref, n_tiles, e_sat,
                                    ssa_sc, ssb_sc)
    sel_start()
    o_win = win_run()
    o_sel = sel_run()

    outs = []
    for r in range(R):
        outs.append(gates[3 * r:3 * r + 1] * o_cmp[r]
                    + gates[3 * r + 1:3 * r + 2] * o_sel[r]
                    + gates[3 * r + 2:3 * r + 3] * o_win[r])
    o_ref[0] = jnp.concatenate(outs, axis=0).T.astype(o_ref.dtype)


def _t5_bucket(dist):
    n = jnp.maximum(dist, 0)
    max_exact = REL_BUCKETS // 2
    nf = jnp.maximum(n, 1).astype(F32)
    large = max_exact + (jnp.log(nf / max_exact) / math.log(REL_MAX_DISTANCE / max_exact)
                         * (REL_BUCKETS - max_exact)).astype(jnp.int32)
    large = jnp.minimum(large, REL_BUCKETS - 1)
    return jnp.where(n < max_exact, n, large)


def _bias_of_dist(dist, rel_bias):
    onehot = (_t5_bucket(dist)[..., None] == jnp.arange(REL_BUCKETS)).astype(F32)
    out = jnp.einsum('...k,kh->...h', onehot, rel_bias.astype(F32), precision=HIGHEST)
    return jnp.moveaxis(out, -1, 0)


def _saturation_distance():
    max_exact = REL_BUCKETS // 2
    steps = REL_BUCKETS - max_exact
    n_sat = max_exact * (REL_MAX_DISTANCE / max_exact) ** ((steps - 1) / steps)
    return int(math.ceil(n_sat)) + 2


def nsa_attention(proj, kc, vct, ksa, vst, kw, vwt, rel_bias, q_gain):
    b, t, _ = proj.shape
    G, R, dh = NSA_KV_GROUPS, NSA_GROUP_SIZE, NSA_HEAD_DIM
    H = NSA_HEADS
    tq, tk = ATT_TQ, ATT_TK
    rr = tq // tk
    ncp = kc.shape[2]
    n_sel_blocks = t // SEL_BLOCK
    n_select = min(N_SELECT, n_sel_blocks)
    assert n_sel_blocks <= LANES and n_sel_blocks % 8 == 0 and tq % tk == 0 and t % tq == 0

    e_sat = -(-(_saturation_distance() + tk - 1) // tk) + rr - 1
    n_wtiles = -(-(WINDOW + tk - 1) // tk) + rr - 1
    jj = np.arange(tk)[:, None]
    ii = np.arange(tq)[None, :]

    def tile_dist(n_e):
        return (np.arange(n_e)[:, None, None] - (rr - 1)) * tk + (ii - jj)[None]

    dist = tile_dist(e_sat + 1)
    bias_sel = jnp.where(dist >= 0, _bias_of_dist(jnp.asarray(dist), rel_bias) * LOG2E, NEG)
    dwin = tile_dist(n_wtiles)
    bias_win = jnp.where((dwin >= 0) & (dwin < WINDOW),
                         _bias_of_dist(jnp.asarray(dwin), rel_bias) * LOG2E, NEG)
    dc = np.arange(t)[None, :] - (np.arange(ncp)[:, None] * CMP_STRIDE + CMP_BLOCK - 1)
    bias_c = _bias_of_dist(jnp.asarray(dc), rel_bias) * LOG2E

    cs = np.arange(ncp) * CMP_STRIDE
    ss = np.arange(n_sel_blocks) * SEL_BLOCK
    shared = (np.minimum(cs[None, :] + CMP_BLOCK, ss[:, None] + SEL_BLOCK)
              - np.maximum(cs[None, :], ss[:, None]))
    c2st = jnp.asarray(np.clip(shared, 0, None) / CMP_BLOCK, BF16)
    pgt = np.zeros((G, LANES, LANES), np.float32)
    for g in range(G):
        for k in range(3 * R):
            pgt[g, k, 3 * R * g + k] = 1.0
    pgt = jnp.asarray(pgt, BF16)
    gate_blk = (H * dh + 6 * G * dh) // LANES
    qg = jnp.broadcast_to(q_gain.reshape(dh, 1), (dh, tq))

    kernel = functools.partial(_nsa_attn_kernel, e_sat=e_sat, n_wtiles=n_wtiles,
                               n_sel_blocks=n_sel_blocks, n_select=n_select)
    return pl.pallas_call(
        kernel,
        grid=(G, b, t // tq),
        in_specs=[
            pl.BlockSpec((1, tq, R * dh), lambda g, bi, i: (bi, i, g)),
            pl.BlockSpec((1, tq, LANES), lambda g, bi, i: (bi, i, gate_blk)),
            pl.BlockSpec((1, LANES, LANES), lambda g, bi, i: (g, 0, 0)),
            pl.BlockSpec((dh, tq), lambda g, bi, i: (0, 0)),
            pl.BlockSpec((1, 1, ncp, dh), lambda g, bi, i: (bi, g, 0, 0)),
            pl.BlockSpec((1, 1, dh, ncp), lambda g, bi, i: (bi, g, 0, 0)),
            pl.BlockSpec((1, 1, t, LANES + dh), lambda g, bi, i: (bi, g, 0, 0)),
            pl.BlockSpec((1, 1, dh, t), lambda g, bi, i: (bi, g, 0, 0)),
            pl.BlockSpec((1, 1, t, dh), lambda g, bi, i: (bi, g, 0, 0)),
            pl.BlockSpec((1, 1, dh, t), lambda g, bi, i: (bi, g, 0, 0)),
            pl.BlockSpec((R, ncp, tq), lambda g, bi, i: (g, 0, i)),
            pl.BlockSpec((R, e_sat + 1, tk, tq), lambda g, bi, i: (g, 0, 0, 0)),
            pl.BlockSpec((R, n_wtiles, tk, tq), lambda g, bi, i: (g, 0, 0, 0)),
            pl.BlockSpec((n_sel_blocks, ncp), lambda g, bi, i: (0, 0)),
        ],
        out_specs=pl.BlockSpec((1, tq, R * dh), lambda g, bi, i: (bi, i, g)),
        out_shape=jax.ShapeDtypeStruct((b, t, H * dh), BF16),
        scratch_shapes=[pltpu.VMEM((R, dh + 16, tq), F32),
                        pltpu.VMEM((LANES + dh, R * tq), BF16),
                        pltpu.VMEM((dh, R * tq), BF16),
                        pltpu.VMEM((ncp, R * tq), BF16),
                        pltpu.VMEM((tk, R * tq), F32), pltpu.VMEM((tk, R * tq), F32),
                        pltpu.VMEM((tk, R * tq), F32), pltpu.VMEM((tk, R * tq), F32),
                        pltpu.VMEM((n_sel_blocks, tq), F32)],
        compiler_params=_cparams("parallel", "parallel", "arbitrary"),
        name="nsa_attention",
    )(proj, proj, pgt, qg, kc, vct, ksa, vst, kw, vwt, bias_c, bias_sel, bias_win, c2st)


def nsa_layer(x, shift, scale, gate, norm_g, rel_bias, w_in, q_gain, k_gain,
              cmp_pe, cmp_w1, cmp_w2, w_out):
    d = x.shape[-1]
    n_in = w_in.shape[1]
    n_pad = -(-n_in // 1024) * 1024
    w_in_p = jnp.pad(w_in, ((0, 0), (0, n_pad - n_in))).astype(BF16)
    proj = norm_matmul(x, norm_g, shift, scale, w_in_p)
    kc, vct, ksa, vst, kw, vwt = nsa_prep(proj, k_gain, cmp_pe, cmp_w1, cmp_w2)
    o = nsa_attention(proj, kc, vct, ksa, vst, kw, vwt, rel_bias, q_gain)
    return out_proj_residual(o, w_out.astype(BF16), x, gate)


def _hgrn_kernel(q_ref, f_ref, v_ref, g_ref, lb_ref, og_ref, tri_ref, ones_ref,
                 o_ref, st_sc, k_sc, c_sc):
    C, dk, S = HG_CHUNK, HGRN_DK, HG_SUB
    n_sub = C // S

    @pl.when(pl.program_id(2) == 0)
    def _():
        st_sc[...] = jnp.zeros(st_sc.shape, F32)

    row = lax.broadcasted_iota(jnp.int32, (C, C), 0)
    colm = lax.broadcasted_iota(jnp.int32, (C, C), 1)
    rloc = lax.broadcasted_iota(jnp.int32, (C, dk), 0)
    diag_keep = ((row // S) == (colm // S)) & ((colm % S) <= (row % S))

    def rows_bcast(ref, hh, first, period):
        return jnp.concatenate(
            [jnp.broadcast_to(ref[hh, pl.ds(g * period + first, 1), :], (period, dk))
             for g in range(C // period)], axis=0)

    heads = range(HG_HEADS)
    sls = [slice(hh * dk, (hh + 1) * dk) for hh in heads]
    qs, ks, cums, vs = [], [], [], []
    for hh in heads:
        fl2 = f_ref[0, :, sls[hh]] * LOG2E
        lb = lb_ref[0, :, sls[hh]]
        log_sig = jnp.minimum(fl2, 0.0) - jnp.log2(1.0 + jnp.exp2(-jnp.abs(fl2)))
        ta = jnp.log2(lb)
        tb = jnp.log2(1.0 - lb) + log_sig
        lf = jnp.maximum(ta, tb) + jnp.log2(1.0 + jnp.exp2(-jnp.abs(ta - tb)))
        k = 1.0 - jnp.exp2(lf)
        cum = _dot_lhs_exact(tri_ref[...], lf)
        k_sc[hh] = k
        c_sc[hh] = cum
        ks.append(k)
        cums.append(cum)
        qs.append(_silu(q_ref[0, :, sls[hh]]))
        vs.append(v_ref[0, :, sls[hh]].astype(BF16))

    attns = []
    for hh in heads:
        pieces = []
        for j in range(S):
            kj = rows_bcast(k_sc, hh, j, S)
            cj = rows_bcast(c_sc, hh, j, S)
            pieces.append((qs[hh] * kj * jnp.exp2(jnp.minimum(cums[hh] - cj, 0.0))).astype(BF16))
        attns.append(jnp.where(diag_keep,
                               _dot(jnp.concatenate(pieces, axis=1), ones_ref[...]), 0.0))

    m = S
    while m < C:
        upper = (rloc // m) % 2 == 1
        same = (row // (2 * m)) == (colm // (2 * m))
        for hh in heads:
            e = jnp.exp2(-jnp.abs(cums[hh] - rows_bcast(c_sc, hh, m - 1, 2 * m)))
            qm = jnp.where(upper, qs[hh] * e, 0.0).astype(BF16)
            km = jnp.where(upper, 0.0, ks[hh] * e).astype(BF16)
            attns[hh] = attns[hh] + jnp.where(same, _dot_nt(qm, km), 0.0)
        m *= 2

    for hh in heads:
        st = st_sc[hh]
        cum, k, q, v = cums[hh], ks[hh], qs[hh], vs[hh]
        o = _dot(attns[hh].astype(BF16), v)
        o = o + _dot_nt((q * jnp.exp2(cum)).astype(BF16), st.astype(BF16))
        total = cum[C - 1:C, :]
        kd = (k * jnp.exp2(total - cum)).astype(BF16)
        st_sc[hh] = st * jnp.exp2(total) + lax.dot_general(
            v, kd, (((0,), (0,)), ((), ())), preferred_element_type=F32)
        ms = jnp.mean(o * o, axis=-1, keepdims=True)
        o = o * lax.rsqrt(ms + NORM_EPS) * og_ref[...]
        o_ref[0, :, sls[hh]] = (o * _silu(g_ref[0, :, sls[hh]])).astype(o_ref.dtype)


def hgrn_recurrence(proj, lb, out_gain):
    b, t, four_d = proj.shape
    d = four_d // 4
    dk = HGRN_DK
    C, S, hps = HG_CHUNK, HG_SUB, HG_HEADS
    w = hps * dk
    nhp = d // w
    tri = jnp.asarray(np.tril(np.ones((C, C), np.float32)), BF16)
    ones = jnp.asarray(np.arange(S * dk)[:, None] // dk == (np.arange(C)[None, :] % S), BF16)
    return pl.pallas_call(
        _hgrn_kernel,
        grid=(b, nhp, t // C),
        in_specs=[
            pl.BlockSpec((1, C, w), lambda bi, h, c: (bi, c, h)),
            pl.BlockSpec((1, C, w), lambda bi, h, c: (bi, c, nhp + h)),
            pl.BlockSpec((1, C, w), lambda bi, h, c: (bi, c, 2 * nhp + h)),
            pl.BlockSpec((1, C, w), lambda bi, h, c: (bi, c, 3 * nhp + h)),
            pl.BlockSpec((1, 1, w), lambda bi, h, c: (h, 0, 0)),
            pl.BlockSpec((1, dk), lambda bi, h, c: (0, 0)),
            pl.BlockSpec((C, C), lambda bi, h, c: (0, 0)),
            pl.BlockSpec((S * dk, C), lambda bi, h, c: (0, 0)),
        ],
        out_specs=pl.BlockSpec((1, C, w), lambda bi, h, c: (bi, c, h)),
        out_shape=jax.ShapeDtypeStruct((b, t, d), BF16),
        scratch_shapes=[pltpu.VMEM((hps, dk, dk), F32), pltpu.VMEM((hps, C, dk), F32),
                        pltpu.VMEM((hps, C, dk), F32)],
        compiler_params=_cparams("parallel", "parallel", "arbitrary"),
        name="hgrn_recurrence",
    )(proj, proj, proj, proj, lb.reshape(nhp, 1, w), out_gain.reshape(1, dk), tri, ones)


def hgrn_layer(x, shift, scale, gate, norm_g, lb, w_in, out_gain, w_out):
    proj = norm_matmul(x, norm_g, shift, scale, w_in.astype(BF16))
    o = hgrn_recurrence(proj, lb, out_gain)
    return out_proj_residual(o, w_out.astype(BF16), x, gate)


def _router_kernel(x_ref, g_ref, sh_ref, sc_ref, wr_ref, u_ref,
                   h_ref, eid_ref, ew_ref, pos_ref, cnt_ref, run_sc):
    @pl.when((pl.program_id(0) == 0) & (pl.program_id(1) == 0))
    def _():
        run_sc[...] = jnp.zeros(run_sc.shape, F32)

    h = _modulated_norm(x_ref[0], g_ref[...], sh_ref[0], sc_ref[0])
    h_ref[0] = h.astype(BF16)
    lt = _dot_bf16x3(h, wr_ref[...]).T
    NG, EPG = MOE_GROUPS, MOE_EPG

    def softmax_rows(rows):
        mx = functools.reduce(jnp.maximum, rows)
        es = [jnp.exp(r - mx) for r in rows]
        tot = functools.reduce(lambda a, c: a + c, es)
        return [e / tot for e in es]

    def argmax_rows(rows):
        best, idx = rows[0], jnp.zeros(rows[0].shape, jnp.int32)
        for i in range(1, len(rows)):
            better = rows[i] > best
            best = jnp.where(better, rows[i], best)
            idx = jnp.where(better, i, idx)
        return best, idx

    pg = softmax_rows([lt[i:i + 1] for i in range(NG)])
    p_grp, grp = argmax_rows(pg)
    el = []
    for j in range(EPG):
        acc = lt[NG + j:NG + j + 1]
        for gi in range(1, NG):
            acc = jnp.where(grp == gi, lt[NG + gi * EPG + j:NG + gi * EPG + j + 1], acc)
        el.append(acc)
    pe = softmax_rows(el)
    p1, i1 = argmax_rows(pe)
    p2, i2 = argmax_rows([jnp.where(i1 == j, -1.0, pe[j]) for j in range(EPG)])
    den = p1 + p2
    e1 = grp * EPG + i1
    e2 = grp * EPG + i2
    eid_ref[0:1, :] = e1
    eid_ref[1:2, :] = e2
    ew_ref[0:1, :] = p_grp * p1 / den
    ew_ref[1:2, :] = p_grp * p2 / den

    tm = e1.shape[1]
    ex = lax.broadcasted_iota(jnp.int32, (MOE_EXPERTS, tm), 0)
    oh1 = jnp.where(ex == e1, 1.0, 0.0)
    oh2 = jnp.where(ex == e2, 1.0, 0.0)
    before1 = _dot(oh1.astype(BF16), u_ref[...])
    before2 = _dot(oh2.astype(BF16), u_ref[...])
    tot1 = jnp.sum(oh1, axis=1, keepdims=True)
    tot2 = jnp.sum(oh2, axis=1, keepdims=True)
    run = run_sc[...]
    pos1 = jnp.sum(oh1 * (before1 + run), axis=0, keepdims=True)
    pos2 = jnp.sum(oh2 * (before2 + (run + tot1)), axis=0, keepdims=True)
    pos_ref[0:1, :] = pos1.astype(jnp.int32)
    pos_ref[1:2, :] = pos2.astype(jnp.int32)
    run = run + tot1 + tot2
    run_sc[...] = run
    cnt_ref[...] = jnp.broadcast_to(run, cnt_ref.shape)


def moe_router(x, gain, shift, scale, w_group, w_expert, tm=512):
    b, t, d = x.shape
    tm = min(tm, t)
    nt = t // tm
    wr = jnp.concatenate([w_group, w_expert], axis=1)
    wr = jnp.pad(wr, ((0, 0), (0, LANES - wr.shape[1])))
    upper = jnp.asarray(np.triu(np.ones((tm, tm), np.float32), 1), BF16)
    return pl.pallas_call(
        _router_kernel,
        grid=(b, nt),
        in_specs=[
            pl.BlockSpec((1, tm, d), lambda bi, i: (bi, i, 0)),
            pl.BlockSpec((1, d), lambda bi, i: (0, 0)),
            pl.BlockSpec((1, 1, d), lambda bi, i: (bi, 0, 0)),
            pl.BlockSpec((1, 1, d), lambda bi, i: (bi, 0, 0)),
            pl.BlockSpec((d, LANES), lambda bi, i: (0, 0)),
            pl.BlockSpec((tm, tm), lambda bi, i: (0, 0)),
        ],
        out_specs=[
            pl.BlockSpec((1, tm, d), lambda bi, i: (bi, i, 0)),
            pl.BlockSpec((MOE_TOP_K, tm), lambda bi, i: (0, bi * nt + i)),
            pl.BlockSpec((MOE_TOP_K, tm), lambda bi, i: (0, bi * nt + i)),
            pl.BlockSpec((MOE_TOP_K, tm), lambda bi, i: (0, bi * nt + i)),
            pl.BlockSpec((MOE_EXPERTS, LANES), lambda bi, i: (0, 0)),
        ],
        out_shape=[
            jax.ShapeDtypeStruct((b, t, d), BF16),
            jax.ShapeDtypeStruct((MOE_TOP_K, b * t), jnp.int32),
            jax.ShapeDtypeStruct((MOE_TOP_K, b * t), F32),
            jax.ShapeDtypeStruct((MOE_TOP_K, b * t), jnp.int32),
            jax.ShapeDtypeStruct((MOE_EXPERTS, LANES), F32),
        ],
        scratch_shapes=[pltpu.VMEM((MOE_EXPERTS, 1), F32)],
        compiler_params=_cparams("arbitrary", "arbitrary"),
        name="moe_router",
    )(x, gain.reshape(1, d), shift.reshape(b, 1, d), scale.reshape(b, 1, d), wr, upper)


def _expert_ffn_kernel(be_ref, nu_ref, xb_ref, w1_ref, w3_ref, w2_ref, yb_ref,
                       w1_sc, w3_sc, w2_sc):
    i = pl.program_id(0)

    @pl.when((i == 0) | (be_ref[i] != be_ref[jnp.maximum(i - 1, 0)]))
    def _():
        w1_sc[...] = w1_ref[0].astype(BF16)
        w3_sc[...] = w3_ref[0].astype(BF16)
        w2_sc[...] = w2_ref[0].astype(BF16)

    @pl.when(i < nu_ref[0])
    def _():
        xb = xb_ref[...]
        a = _dot(xb, w1_sc[...])
        g = _dot(xb, w3_sc[...])
        yb_ref[...] = _dot((_silu(a) * g).astype(BF16), w2_sc[...]).astype(yb_ref.dtype)

    @pl.when(i >= nu_ref[0])
    def _():
        yb_ref[...] = jnp.zeros(yb_ref.shape, yb_ref.dtype)


def expert_ffn(xb, blk_e, n_used, w1, w3, w2):
    p, d = xb.shape
    ff = w1.shape[2]
    rb = MOE_ROW_BLOCK
    grid_spec = pltpu.PrefetchScalarGridSpec(
        num_scalar_prefetch=2,
        grid=(p // rb,),
        in_specs=[
            pl.BlockSpec((rb, d), lambda i, be, nu: (i, 0)),
            pl.BlockSpec((1, d, ff), lambda i, be, nu: (be[i], 0, 0)),
            pl.BlockSpec((1, d, ff), lambda i, be, nu: (be[i], 0, 0)),
            pl.BlockSpec((1, ff, d), lambda i, be, nu: (be[i], 0, 0)),
        ],
        out_specs=pl.BlockSpec((rb, d), lambda i, be, nu: (i, 0)),
        scratch_shapes=[pltpu.VMEM((d, ff), BF16), pltpu.VMEM((d, ff), BF16),
                        pltpu.VMEM((ff, d), BF16)],
    )
    return pl.pallas_call(
        _expert_ffn_kernel,
        grid_spec=grid_spec,
        out_shape=jax.ShapeDtypeStruct((p, d), BF16),
        compiler_params=_cparams("arbitrary"),
        name="expert_ffn",
    )(blk_e, n_used, xb, w1, w3, w2)


def _combine_kernel(x_ref, gt_ref, y1_ref, y2_ref, w_ref, o_ref):
    w = w_ref[0]
    y = w[:, 0:1] * y1_ref[0].astype(F32) + w[:, 1:2] * y2_ref[0].astype(F32)
    o_ref[0] = x_ref[0] + gt_ref[0] * y


def moe_combine(x, gate, y1, y2, w, tm=512):
    b, t, d = x.shape
    tm = min(tm, t)
    spec = pl.BlockSpec((1, tm, d), lambda bi, i: (bi, i, 0))
    return pl.pallas_call(
        _combine_kernel,
        grid=(b, t // tm),
        in_specs=[spec, pl.BlockSpec((1, 1, d), lambda bi, i: (bi, 0, 0)), spec, spec,
                  pl.BlockSpec((1, tm, MOE_TOP_K), lambda bi, i: (bi, i, 0))],
        out_specs=spec,
        out_shape=jax.ShapeDtypeStruct((b, t, d), F32),
        compiler_params=_cparams("parallel", "parallel"),
        name="moe_combine",
    )(x, gate.reshape(b, 1, d), y1, y2, w)


def moe_layer(x, shift, scale, gate, norm_g, w_group, w_expert, w1, w3, w2):
    b, t, d = x.shape
    n = b * t
    a = n * MOE_TOP_K
    rb = MOE_ROW_BLOCK
    h, eid, ew, pos, cnt = moe_router(x, norm_g, shift, scale, w_group, w_expert)
    counts = cnt[:, 0].astype(jnp.int32)
    padded = (counts + rb - 1) // rb * rb
    pad_end = jnp.cumsum(padded)
    pad_start = pad_end - padded
    is_e = eid[..., None] == jnp.arange(MOE_EXPERTS, dtype=jnp.int32)
    dest = jnp.sum(jnp.where(is_e, pad_start, 0), axis=-1) + pos
    n_blk = -(-(a + MOE_EXPERTS * (rb - 1)) // rb)
    p = n_blk * rb
    blk_start = jnp.arange(n_blk, dtype=jnp.int32) * rb
    blk_e = jnp.minimum(jnp.sum((pad_end[None, :] <= blk_start[:, None]).astype(jnp.int32), axis=1),
                        MOE_EXPERTS - 1)
    n_used = (pad_end[-1:] // rb).astype(jnp.int32)
    tok = jnp.tile(jnp.arange(n, dtype=jnp.int32), MOE_TOP_K)
    _, tok_by_slot = lax.sort_key_val(dest.reshape(a), tok)
    seg_shift = pad_start - (jnp.cumsum(counts) - counts)
    slot = jnp.arange(p, dtype=jnp.int32)
    e_of_slot = jnp.repeat(blk_e, rb)
    shift = jnp.sum(jnp.where(e_of_slot[:, None] == jnp.arange(MOE_EXPERTS, dtype=jnp.int32),
                              seg_shift, 0), axis=-1)
    idx = slot - shift
    buf_t = tok_by_slot[jnp.where(idx < a, idx, slot % a)]
    xb = h.reshape(n, d)[buf_t]
    yb = expert_ffn(xb, blk_e, n_used, w1, w3, w2)
    y1 = yb[dest[0]].reshape(b, t, d)
    y2 = yb[dest[1]].reshape(b, t, d)
    return moe_combine(x, gate, y1, y2, ew.T.reshape(b, t, MOE_TOP_K))


def kernel(x, c, ada_w, ada_b, norm_g, rel_bias, nsa_w_in, nsa_q_gain, nsa_k_gain, nsa_cmp_pe,
           nsa_cmp_w1, nsa_cmp_w2, nsa_w_out, hgrn_w_in, hgrn_lower_bounds, hgrn_out_gain,
           hgrn_w_out, moe_router_group, moe_router_expert, moe_w1, moe_w3, moe_w2):
    depth = ada_w.shape[0]
    d = x.shape[-1]
    lb_soft = jax.nn.softmax(hgrn_lower_bounds.astype(F32), axis=0)
    lb_all = jnp.cumsum(lb_soft, axis=0) - lb_soft[0]
    mod = adaln_mod(c, ada_w, ada_b)
    for layer in range(depth):
        j = layer // 2
        shift, scale, gate = (mod[layer, 0, :, i * d:(i + 1) * d] for i in range(3))
        if layer % 2 == 0:
            x = nsa_layer(x, shift, scale, gate, norm_g[layer, 0], rel_bias, nsa_w_in[j],
                          nsa_q_gain[j], nsa_k_gain[j], nsa_cmp_pe[j], nsa_cmp_w1[j],
                          nsa_cmp_w2[j], nsa_w_out[j])
        else:
            x = hgrn_layer(x, shift, scale, gate, norm_g[layer, 0], lb_all[layer],
                           hgrn_w_in[j], hgrn_out_gain[j], hgrn_w_out[j])
        shift, scale, gate = (mod[layer, 1, :, i * d:(i + 1) * d] for i in range(3))
        x = moe_layer(x, shift, scale, gate, norm_g[layer, 1], moe_router_group[layer],
                      moe_router_expert[layer], moe_w1[layer], moe_w3[layer], moe_w2[layer])
    return x
```

```python
import functools
import math

import numpy as np
import jax
import jax.numpy as jnp
from jax import lax
from jax.experimental import pallas as pl
from jax.experimental.pallas import tpu as pltpu

F32 = jnp.float32
BF16 = jnp.bfloat16
HIGHEST = lax.Precision.HIGHEST

NSA_HEADS = 16
NSA_KV_GROUPS = 4
NSA_GROUP_SIZE = NSA_HEADS // NSA_KV_GROUPS
NSA_HEAD_DIM = 64
CMP_BLOCK = 32
CMP_STRIDE = 16
SEL_BLOCK = 64
N_SELECT = 8
WINDOW = 512
FORCED_SCORE = 1.0e4
REL_BUCKETS = 32
REL_MAX_DISTANCE = 1024
HGRN_DK = 128
MOE_GROUPS = 4
MOE_EPG = 4
MOE_EXPERTS = MOE_GROUPS * MOE_EPG
MOE_TOP_K = 2
MOE_ROW_BLOCK = 512
NORM_EPS = 1e-6

LANES = 128
NEG = -1.0e30
LOG2E = math.log2(math.e)
VMEM_LIMIT = 48 * 1024 * 1024

ATT_TQ = 256
ATT_TK = 256
SHIFT_ROWS = 16
SAFE_GAP = 60.0
HG_CHUNK = 128
HG_SUB = 8
HG_HEADS = 4


def _cparams(*sem):
    return pltpu.CompilerParams(dimension_semantics=sem, vmem_limit_bytes=VMEM_LIMIT)


def _dot(a, b):
    return jnp.dot(a, b, preferred_element_type=F32)


def _dot_exact(a, b):
    return jnp.dot(a, b, preferred_element_type=F32, precision=HIGHEST)


def _dot_lhs_exact(a, x):
    x1 = x.astype(BF16)
    r1 = x - x1.astype(F32)
    x2 = r1.astype(BF16)
    x3 = (r1 - x2.astype(F32)).astype(BF16)
    return _dot(a, x1) + _dot(a, x2) + _dot(a, x3)


def _dot_bf16x3(a, b):
    a1 = a.astype(BF16)
    a2 = (a - a1.astype(F32)).astype(BF16)
    b1 = b.astype(BF16)
    b2 = (b - b1.astype(F32)).astype(BF16)
    return _dot(a1, b1) + _dot(a1, b2) + _dot(a2, b1)


def _dot_nt(a, b):
    return lax.dot_general(a, b, (((1,), (1,)), ((), ())), preferred_element_type=F32)


def _sigmoid(x):
    return 0.5 * jnp.tanh(0.5 * x) + 0.5


def _silu(x):
    return x * _sigmoid(x)


def _adaln_kernel(c_ref, w_ref, b_ref, o_ref):
    cond = _silu(c_ref[...])
    o_ref[0] = _dot_exact(cond, w_ref[0]) + b_ref[0]


def adaln_mod(c, ada_w, ada_b):
    depth, two, d, d3 = ada_w.shape
    b = c.shape[0]
    ls = depth * two
    tn = 1024
    out = pl.pallas_call(
        _adaln_kernel,
        grid=(ls, d3 // tn),
        in_specs=[
            pl.BlockSpec((b, d), lambda i, j: (0, 0)),
            pl.BlockSpec((1, d, tn), lambda i, j: (i, 0, j)),
            pl.BlockSpec((1, 1, tn), lambda i, j: (i, 0, j)),
        ],
        out_specs=pl.BlockSpec((1, b, tn), lambda i, j: (i, 0, j)),
        out_shape=jax.ShapeDtypeStruct((ls, b, d3), F32),
        compiler_params=_cparams("parallel", "parallel"),
        name="adaln_mod",
    )(c, ada_w.reshape(ls, d, d3), ada_b.reshape(ls, 1, d3))
    return out.reshape(depth, two, b, d3)


def _modulated_norm(x, gain, shift, scale):
    ms = jnp.mean(x * x, axis=-1, keepdims=True)
    y = x * lax.rsqrt(ms + NORM_EPS) * gain
    return y * (1.0 + scale) + shift


def _norm_matmul_kernel(x_ref, g_ref, sh_ref, sc_ref, w_ref, o_ref, h_sc):
    @pl.when(pl.program_id(2) == 0)
    def _():
        h = _modulated_norm(x_ref[0], g_ref[...], sh_ref[0], sc_ref[0])
        h_sc[...] = h.astype(BF16)

    o_ref[0] = _dot(h_sc[...], w_ref[...])


def norm_matmul(x, gain, shift, scale, w, tm=1024, tn=1024):
    b, t, d = x.shape
    n = w.shape[1]
    tm = min(tm, t)
    return pl.pallas_call(
        _norm_matmul_kernel,
        grid=(b, t // tm, n // tn),
        in_specs=[
            pl.BlockSpec((1, tm, d), lambda bi, i, j: (bi, i, 0)),
            pl.BlockSpec((1, d), lambda bi, i, j: (0, 0)),
            pl.BlockSpec((1, 1, d), lambda bi, i, j: (bi, 0, 0)),
            pl.BlockSpec((1, 1, d), lambda bi, i, j: (bi, 0, 0)),
            pl.BlockSpec((d, tn), lambda bi, i, j: (0, j)),
        ],
        out_specs=pl.BlockSpec((1, tm, tn), lambda bi, i, j: (bi, i, j)),
        out_shape=jax.ShapeDtypeStruct((b, t, n), F32),
        scratch_shapes=[pltpu.VMEM((tm, d), BF16)],
        compiler_params=_cparams("parallel", "parallel", "arbitrary"),
        name="norm_matmul",
    )(x, gain.reshape(1, d), shift.reshape(b, 1, d), scale.reshape(b, 1, d), w)


def _out_proj_kernel(o_ref, w_ref, x_ref, gt_ref, y_ref):
    y = _dot(o_ref[0], w_ref[...])
    y_ref[0] = x_ref[0] + gt_ref[0] * y


def out_proj_residual(o, w, x, gate, tm=512):
    b, t, d = x.shape
    k = o.shape[-1]
    tm = min(tm, t)
    return pl.pallas_call(
        _out_proj_kernel,
        grid=(b, t // tm),
        in_specs=[
            pl.BlockSpec((1, tm, k), lambda bi, i: (bi, i, 0)),
            pl.BlockSpec((k, d), lambda bi, i: (0, 0)),
            pl.BlockSpec((1, tm, d), lambda bi, i: (bi, i, 0)),
            pl.BlockSpec((1, 1, d), lambda bi, i: (bi, 0, 0)),
        ],
        out_specs=pl.BlockSpec((1, tm, d), lambda bi, i: (bi, i, 0)),
        out_shape=jax.ShapeDtypeStruct((b, t, d), F32),
        compiler_params=_cparams("parallel", "parallel"),
        name="out_proj_residual",
    )(o, w, x, gate.reshape(b, 1, d))


def _pair_rms(x, gain2):
    lane = lax.broadcasted_iota(jnp.int32, x.shape, 1)
    lo = lane < NSA_HEAD_DIM
    x2 = x * x
    s_lo = jnp.sum(jnp.where(lo, x2, 0.0), axis=-1, keepdims=True)
    s_hi = jnp.sum(jnp.where(lo, 0.0, x2), axis=-1, keepdims=True)
    inv = jnp.where(lo, lax.rsqrt(s_lo / NSA_HEAD_DIM + NORM_EPS),
                    lax.rsqrt(s_hi / NSA_HEAD_DIM + NORM_EPS))
    return x * inv * gain2


def _nsa_prep_kernel(kc0_ref, kc1_ref, vc0_ref, vc1_ref, ks_ref, vs_ref, kw_ref, vw_ref,
                     kg_ref, pe_ref, w1_ref, w1big_ref, w2big_ref, ex_ref,
                     kco_ref, vct_ref, kso_ref, vst_ref, kwo_ref, vwt_ref, k2m_ref):
    t = ks_ref.shape[1]
    nblk = t // CMP_STRIDE
    dh = NSA_HEAD_DIM
    G = NSA_KV_GROUPS

    lane = lax.broadcasted_iota(jnp.int32, (t, LANES), 1)
    ones_cols = jnp.ones((t, SHIFT_ROWS), BF16)
    for br, (src, dst, gi, off) in enumerate(((ks_ref, kso_ref, 1, LANES), (kw_ref, kwo_ref, 2, 0))):
        g2 = kg_ref[gi:gi + 1, :]
        for gp in range(G // 2):
            kn = _pair_rms(src[0, :, gp * LANES:(gp + 1) * LANES], g2).astype(BF16)
            dst[0, 2 * gp, :, off:off + dh] = kn[:, :dh]
            dst[0, 2 * gp + 1, :, off:off + dh] = kn[:, dh:]
            k2 = kn.astype(F32) ** 2
            for half, keep in enumerate((lane < dh, lane >= dh)):
                n2 = jnp.sum(jnp.where(keep, k2, 0.0), axis=-1, keepdims=True)
                row = 2 * (2 * gp + half) + br
                k2m_ref[0, row:row + 1, :] = jnp.broadcast_to(
                    jnp.max(n2, axis=0, keepdims=True), (1, LANES))
        for g in range(G):
            dst[0, g, :, off + dh:off + dh + SHIFT_ROWS] = ones_cols
    for g in range(G):
        kso_ref[0, g, :, :LANES] = ex_ref[...]
    for src, dst in ((vs_ref, vst_ref), (vw_ref, vwt_ref)):
        for gp in range(G // 2):
            vt = src[0, :, gp * LANES:(gp + 1) * LANES].T.astype(BF16)
            dst[0, 2 * gp] = vt[:dh]
            dst[0, 2 * gp + 1] = vt[dh:]

    for ci, (srcs, is_key) in enumerate((((kc0_ref, kc1_ref), True), ((vc0_ref, vc1_ref), False))):
        pe1 = _dot(pe_ref[ci].astype(BF16), w1_ref[ci])[0:1]
        pe2 = jnp.concatenate([pe1, pe1], axis=1)
        for gp, src in enumerate(srcs):
            parts = [src[0, pl.ds(l, nblk, stride=CMP_STRIDE), :] for l in range(CMP_STRIDE)]
            r = jnp.concatenate(parts, axis=1).astype(BF16)
            ab = _dot(r, w1big_ref[ci])
            second = ab[:, LANES:]
            shifted = jnp.concatenate([second[1:], jnp.zeros((1, LANES), F32)], axis=0)
            pre = ab[:, :LANES] + shifted + pe2
            hid = _dot(_silu(pre).astype(BF16), w2big_ref[ci])
            if is_key:
                kn = _pair_rms(hid, kg_ref[0:1, :]).astype(BF16)
                kco_ref[0, 2 * gp] = kn[:, :dh]
                kco_ref[0, 2 * gp + 1] = kn[:, dh:]
            else:
                vt = hid.T.astype(BF16)
                vct_ref[0, 2 * gp] = vt[:dh]
                vct_ref[0, 2 * gp + 1] = vt[dh:]


def nsa_prep(proj, k_gain, cmp_pe, cmp_w1, cmp_w2):
    b, t, _ = proj.shape
    G, dh = NSA_KV_GROUPS, NSA_HEAD_DIM
    kvw = G * dh
    q_blocks = (NSA_HEADS * dh) // kvw
    nblk = t // CMP_STRIDE
    kg2 = jnp.concatenate([k_gain, k_gain], axis=1)
    pe_flat = jnp.broadcast_to(cmp_pe.reshape(2, 1, CMP_BLOCK * dh), (2, 8, CMP_BLOCK * dh))

    w1r = cmp_w1.reshape(2, 2, CMP_STRIDE, dh, dh)
    eye2 = jnp.eye(2, dtype=cmp_w1.dtype)
    w1big = jnp.einsum('chlde,gk->clgdhke', w1r, eye2).reshape(2, CMP_STRIDE * 2 * dh, 4 * dh)
    w2big = jnp.einsum('cde,gk->cgdke', cmp_w2, eye2).reshape(2, 2 * dh, 2 * dh)
    expand = jnp.asarray((np.arange(t)[:, None] // SEL_BLOCK) == np.arange(LANES)[None, :], BF16)

    def col(i):
        return pl.BlockSpec((1, t, kvw), lambda bi, i=i: (bi, 0, q_blocks + i))

    def col_pair(i, gp):
        return pl.BlockSpec((1, t, LANES), lambda bi: (bi, 0, (q_blocks + i) * (kvw // LANES) + gp))

    def full(shape):
        return pl.BlockSpec(shape, lambda bi: (0,) * len(shape))

    def per_b(shape):
        return pl.BlockSpec((1,) + shape, lambda bi: (bi,) + (0,) * len(shape))

    return pl.pallas_call(
        _nsa_prep_kernel,
        grid=(b,),
        in_specs=[col_pair(0, 0), col_pair(0, 1), col_pair(1, 0), col_pair(1, 1),
                  col(2), col(3), col(4), col(5),
                  full((3, 2 * dh)), full((2, 8, CMP_BLOCK * dh)),
                  full((2, CMP_BLOCK * dh, dh)), full((2, 2 * CMP_STRIDE * dh, 4 * dh)),
                  full((2, 2 * dh, 2 * dh)), full((t, LANES))],
        out_specs=[per_b((G, nblk, dh)), per_b((G, dh, nblk)),
                   per_b((G, t, LANES + dh + SHIFT_ROWS)), per_b((G, dh, t)),
                   per_b((G, t, dh + SHIFT_ROWS)), per_b((G, dh, t)),
                   per_b((2 * G, LANES))],
        out_shape=[jax.ShapeDtypeStruct((b, G, nblk, dh), BF16),
                   jax.ShapeDtypeStruct((b, G, dh, nblk), BF16),
                   jax.ShapeDtypeStruct((b, G, t, LANES + dh + SHIFT_ROWS), BF16),
                   jax.ShapeDtypeStruct((b, G, dh, t), BF16),
                   jax.ShapeDtypeStruct((b, G, t, dh + SHIFT_ROWS), BF16),
                   jax.ShapeDtypeStruct((b, G, dh, t), BF16),
                   jax.ShapeDtypeStruct((b, 2 * G, LANES), F32)],
        compiler_params=_cparams("parallel"),
        name="nsa_prep",
    )(proj, proj, proj, proj, proj, proj, proj, proj, kg2, pe_flat,
      cmp_w1.astype(BF16), w1big.astype(BF16), w2big.astype(BF16), expand)


def _nsa_attn_kernel(k2m_ref, hb_ref, q_ref, gl_ref, pgt_ref, qg_ref, kc_ref, vct_ref,
                     ksa_ref, vst_ref, kw_ref, vwt_ref, bc_ref, bs_ref, bw_ref, c2st_ref,
                     o_ref, acc_sc, qsel_sc, qwin_sc, pc_sc, ssa_sc, ssb_sc, swa_sc, swb_sc,
                     score_sc, *,
                     e_sat, n_wtiles,
                     n_sel_blocks, n_select):
    tq, tk = ATT_TQ, ATT_TK
    R, dh = NSA_GROUP_SIZE, NSA_HEAD_DIM
    g_id, b_id, qi = pl.program_id(0), pl.program_id(1), pl.program_id(2)
    scale = dh ** -0.5 * LOG2E
    qoff = LANES

    gl = gl_ref[0]
    g1 = gl.astype(BF16)
    gr = gl - g1.astype(F32)
    g2 = gr.astype(BF16)
    g3 = (gr - g2.astype(F32)).astype(BF16)
    gates = _sigmoid(_dot_nt(pgt_ref[0], g1) + _dot_nt(pgt_ref[0], g2) + _dot_nt(pgt_ref[0], g3))

    qt = q_ref[0].T
    q2 = []
    for r in range(R):
        qr = qt[r * dh:(r + 1) * dh]
        ms = jnp.mean(qr * qr, axis=0, keepdims=True)
        qn = (qr * lax.rsqrt(ms + NORM_EPS) * qg_ref[...] * scale).astype(BF16)
        qwin_sc[:dh, r * tq:(r + 1) * tq] = qn
        qsel_sc[qoff:qoff + dh, r * tq:(r + 1) * tq] = qn
        q2.append(jnp.sum(qn.astype(F32) ** 2, axis=0, keepdims=True))

    k2_sel = k2m_ref[b_id, 2 * g_id]
    k2_win = k2m_ref[b_id, 2 * g_id + 1]
    q2_max = jnp.max(functools.reduce(jnp.maximum, q2))
    thr = hb_ref[NSA_HEADS]
    fast = 4.08 * q2_max * jnp.maximum(k2_sel, k2_win) <= thr * thr
    shift_pad = jnp.zeros((SHIFT_ROWS - 1, tq), F32)
    for r in range(R):
        bmax = hb_ref[g_id * R + r]
        for k2, q_sc, row0 in ((k2_sel, qsel_sc, qoff + dh), (k2_win, qwin_sc, dh)):
            bound = jnp.sqrt(q2[r] * k2) * 1.01 + bmax
            rows = jnp.concatenate([jnp.where(fast, -bound, 0.0), shift_pad], axis=0)
            q_sc[row0:row0 + SHIFT_ROWS, r * tq:(r + 1) * tq] = rows.astype(BF16)

    ones_rows = jnp.ones((16, tk), BF16)
    rr = tq // tk
    n_tiles = rr * qi + rr

    def make_sweep(k_ref, vt_ref, q_sc, b_ref, hi, e_last, sa_sc, sb_sc):
        def tile_col(d):
            dc = jnp.minimum(d, hi - 1)
            e = jnp.where(dc == 0, rr - 1, jnp.where(dc < rr, dc - 1, dc))
            return pl.multiple_of((n_tiles - 1 - e) * tk, tk), jnp.minimum(e, e_last)

        def logits(d, dst_sc):
            col, _ = tile_col(d)
            dst_sc[...] = _dot(k_ref[0, 0, pl.ds(col, tk), :], q_sc[...])

        def softmax_pv(d, src_sc, ms):
            col, bi = tile_col(d)
            vaug = jnp.concatenate([vt_ref[0, 0, :, pl.ds(col, tk)], ones_rows], axis=0)
            new_ms = []
            for r in range(R):
                s = src_sc[:, r * tq:(r + 1) * tq] + b_ref[r, bi]
                m_new = jnp.maximum(ms[r], jnp.max(s, axis=0, keepdims=True))
                alpha = jnp.exp2(ms[r] - m_new)
                p = jnp.exp2(s - m_new).astype(BF16)
                acc_sc[r] = acc_sc[r] * alpha + _dot(vaug, p)
                new_ms.append(m_new)
            return tuple(new_ms)

        def shifted_pv(d, src_sc):
            col, bi = tile_col(d)
            vaug = jnp.concatenate([vt_ref[0, 0, :, pl.ds(col, tk)], ones_rows], axis=0)
            for r in range(R):
                p = jnp.exp2(src_sc[:, r * tq:(r + 1) * tq] + b_ref[r, bi]).astype(BF16)
                acc_sc[r] += _dot(vaug, p)

        def body(dp, ms):
            d = 2 * dp
            logits(d + 1, sb_sc)
            ms = softmax_pv(d, sa_sc, ms)
            logits(d + 2, sa_sc)
            return softmax_pv(d + 1, sb_sc, ms)

        def shifted_body(dp, carry):
            d = 2 * dp
            logits(d + 1, sb_sc)
            shifted_pv(d, sa_sc)
            logits(d + 2, sa_sc)
            shifted_pv(d + 1, sb_sc)
            return carry

        def start():
            logits(0, sa_sc)

        def run():
            acc_sc[...] = jnp.zeros(acc_sc.shape, F32)

            @pl.when(fast)
            def _():
                lax.fori_loop(0, hi // 2, shifted_body, 0)

                @pl.when(hi % 2 == 1)
                def _():
                    shifted_pv(hi - 1, sa_sc)

            @pl.when(jnp.logical_not(fast))
            def _():
                ms = lax.fori_loop(0, hi // 2, body,
                                   tuple(jnp.full((1, tq), NEG, F32) for _ in range(R)))

                @pl.when(hi % 2 == 1)
                def _():
                    softmax_pv(hi - 1, sa_sc, ms)

            outs = []
            for r in range(R):
                acc = acc_sc[r]
                outs.append(acc[:dh] * (1.0 / acc[dh:dh + 1]))
            return outs

        return start, run

    win_start, win_run = make_sweep(kw_ref, vwt_ref, qwin_sc, bw_ref,
                                    jnp.minimum(n_tiles, n_wtiles), n_wtiles - 1, swa_sc, swb_sc)
    win_start()

    ncp = kc_ref.shape[2]
    n_idx = lax.broadcasted_iota(jnp.int32, (ncp, tq), 0)
    t_idx = qi * tq + lax.broadcasted_iota(jnp.int32, (ncp, tq), 1)
    vis = (n_idx * CMP_STRIDE + (CMP_BLOCK - 1)) <= t_idx
    psum = jnp.zeros((ncp, tq), F32)
    sc_all = _dot(kc_ref[0, 0], qwin_sc[:dh])
    for r in range(R):
        s = jnp.where(vis, sc_all[:, r * tq:(r + 1) * tq] + bc_ref[r], NEG)
        m = jnp.max(s, axis=0, keepdims=True)
        e = jnp.where(vis, jnp.exp2(s - m), 0.0)
        p = e * (1.0 / jnp.maximum(jnp.sum(e, axis=0, keepdims=True), 1e-30))
        psum = psum + p
        pc_sc[:, r * tq:(r + 1) * tq] = p.astype(BF16)
    oc_all = _dot(vct_ref[0, 0], pc_sc[...])
    o_cmp = [oc_all[:, r * tq:(r + 1) * tq] for r in range(R)]

    imp_t = _dot_lhs_exact(c2st_ref[...], psum)
    blk = lax.broadcasted_iota(jnp.int32, (n_sel_blocks, tq), 0)
    tpos = qi * tq + lax.broadcasted_iota(jnp.int32, (n_sel_blocks, tq), 1)
    cur = tpos // SEL_BLOCK
    forced = (blk == 0) | (blk == cur) | (blk == cur - 1)
    score = jnp.where(forced, FORCED_SCORE, jnp.where(blk <= cur, imp_t, -1.0))
    score_sc[...] = score
    per_tile = tq // SEL_BLOCK

    def rank_body(g, rank):
        for u in range(per_tile):
            s2 = g * per_tile + u
            row = score_sc[pl.ds(s2, 1), :]
            beats = (row > score) | ((row == score) & (blk > s2))
            rank = rank + jnp.where(beats, 1.0, 0.0)
        return rank

    rank = lax.fori_loop(0, qi + 1, rank_body, jnp.zeros((n_sel_blocks, tq), F32))
    negsel = jnp.where(rank < n_select, 0.0, NEG)
    if n_sel_blocks < LANES:
        negsel = jnp.concatenate([negsel, jnp.zeros((LANES - n_sel_blocks, tq), F32)], axis=0)
    negsel = negsel.astype(BF16)
    for r in range(R):
        qsel_sc[:qoff, r * tq:(r + 1) * tq] = negsel

    sel_start, sel_run = make_sweep(ksa_ref, vst_ref, qsel_sc, bs_ref, n_tiles, e_sat,
                                    ssa_sc, ssb_sc)
    sel_start()
    o_win = win_run()
    o_sel = sel_run()

    outs = []
    for r in range(R):
        outs.append(gates[3 * r:3 * r + 1] * o_cmp[r]
                    + gates[3 * r + 1:3 * r + 2] * o_sel[r]
                    + gates[3 * r + 2:3 * r + 3] * o_win[r])
    o_ref[0] = jnp.concatenate(outs, axis=0).T.astype(o_ref.dtype)


def _t5_bucket(dist):
    n = jnp.maximum(dist, 0)
    max_exact = REL_BUCKETS // 2
    nf = jnp.maximum(n, 1).astype(F32)
    large = max_exact + (jnp.log(nf / max_exact) / math.log(REL_MAX_DISTANCE / max_exact)
                         * (REL_BUCKETS - max_exact)).astype(jnp.int32)
    large = jnp.minimum(large, REL_BUCKETS - 1)
    return jnp.where(n < max_exact, n, large)


def _bias_of_dist(dist, rel_bias):
    onehot = (_t5_bucket(dist)[..., None] == jnp.arange(REL_BUCKETS)).astype(F32)
    out = jnp.einsum('...k,kh->...h', onehot, rel_bias.astype(F32), precision=HIGHEST)
    return jnp.moveaxis(out, -1, 0)


def _saturation_distance():
    max_exact = REL_BUCKETS // 2
    steps = REL_BUCKETS - max_exact
    n_sat = max_exact * (REL_MAX_DISTANCE / max_exact) ** ((steps - 1) / steps)
    return int(math.ceil(n_sat)) + 2


def nsa_attention(proj, kc, vct, ksa, vst, kw, vwt, k2m_rows, rel_bias, q_gain):
    b, t, _ = proj.shape
    G, R, dh = NSA_KV_GROUPS, NSA_GROUP_SIZE, NSA_HEAD_DIM
    H = NSA_HEADS
    tq, tk = ATT_TQ, ATT_TK
    rr = tq // tk
    ncp = kc.shape[2]
    n_sel_blocks = t // SEL_BLOCK
    n_select = min(N_SELECT, n_sel_blocks)
    assert n_sel_blocks <= LANES and n_sel_blocks % 8 == 0 and tq % tk == 0 and t % tq == 0

    e_sat = -(-(_saturation_distance() + tk - 1) // tk) + rr - 1
    n_wtiles = -(-(WINDOW + tk - 1) // tk) + rr - 1
    jj = np.arange(tk)[:, None]
    ii = np.arange(tq)[None, :]

    def tile_dist(n_e):
        return (np.arange(n_e)[:, None, None] - (rr - 1)) * tk + (ii - jj)[None]

    dist = tile_dist(e_sat + 1)
    bias_sel = jnp.where(dist >= 0, _bias_of_dist(jnp.asarray(dist), rel_bias) * LOG2E, NEG)
    dwin = tile_dist(n_wtiles)
    bias_win = jnp.where((dwin >= 0) & (dwin < WINDOW),
                         _bias_of_dist(jnp.asarray(dwin), rel_bias) * LOG2E, NEG)
    dc = np.arange(t)[None, :] - (np.arange(ncp)[:, None] * CMP_STRIDE + CMP_BLOCK - 1)
    bias_c = _bias_of_dist(jnp.asarray(dc), rel_bias) * LOG2E

    cs = np.arange(ncp) * CMP_STRIDE
    ss = np.arange(n_sel_blocks) * SEL_BLOCK
    shared = (np.minimum(cs[None, :] + CMP_BLOCK, ss[:, None] + SEL_BLOCK)
              - np.maximum(cs[None, :], ss[:, None]))
    c2st = jnp.asarray(np.clip(shared, 0, None) / CMP_BLOCK, BF16)
    pgt = np.zeros((G, LANES, LANES), np.float32)
    for g in range(G):
        for k in range(3 * R):
            pgt[g, k, 3 * R * g + k] = 1.0
    pgt = jnp.asarray(pgt, BF16)
    gate_blk = (H * dh + 6 * G * dh) // LANES
    qg = jnp.broadcast_to(q_gain.reshape(dh, 1), (dh, tq))

    rb2 = rel_bias.astype(F32) * LOG2E
    spread = jnp.max(jnp.max(rb2, axis=0) - jnp.min(rb2, axis=0))
    head_bias = jnp.concatenate([jnp.max(rb2, axis=0), jnp.maximum(SAFE_GAP - spread, 0.0)[None]])
    k2m = k2m_rows[:, :, 0]

    kernel = functools.partial(_nsa_attn_kernel, e_sat=e_sat, n_wtiles=n_wtiles,
                               n_sel_blocks=n_sel_blocks, n_select=n_select)
    return pl.pallas_call(
        kernel,
        grid=(G, b, t // tq),
        in_specs=[
            pl.BlockSpec(memory_space=pltpu.SMEM),
            pl.BlockSpec(memory_space=pltpu.SMEM),
            pl.BlockSpec((1, tq, R * dh), lambda g, bi, i: (bi, i, g)),
            pl.BlockSpec((1, tq, LANES), lambda g, bi, i: (bi, i, gate_blk)),
            pl.BlockSpec((1, LANES, LANES), lambda g, bi, i: (g, 0, 0)),
            pl.BlockSpec((dh, tq), lambda g, bi, i: (0, 0)),
            pl.BlockSpec((1, 1, ncp, dh), lambda g, bi, i: (bi, g, 0, 0)),
            pl.BlockSpec((1, 1, dh, ncp), lambda g, bi, i: (bi, g, 0, 0)),
            pl.BlockSpec((1, 1, t, LANES + dh + SHIFT_ROWS), lambda g, bi, i: (bi, g, 0, 0)),
            pl.BlockSpec((1, 1, dh, t), lambda g, bi, i: (bi, g, 0, 0)),
            pl.BlockSpec((1, 1, t, dh + SHIFT_ROWS), lambda g, bi, i: (bi, g, 0, 0)),
            pl.BlockSpec((1, 1, dh, t), lambda g, bi, i: (bi, g, 0, 0)),
            pl.BlockSpec((R, ncp, tq), lambda g, bi, i: (g, 0, i)),
            pl.BlockSpec((R, e_sat + 1, tk, tq), lambda g, bi, i: (g, 0, 0, 0)),
            pl.BlockSpec((R, n_wtiles, tk, tq), lambda g, bi, i: (g, 0, 0, 0)),
            pl.BlockSpec((n_sel_blocks, ncp), lambda g, bi, i: (0, 0)),
        ],
        out_specs=pl.BlockSpec((1, tq, R * dh), lambda g, bi, i: (bi, i, g)),
        out_shape=jax.ShapeDtypeStruct((b, t, H * dh), BF16),
        scratch_shapes=[pltpu.VMEM((R, dh + 16, tq), F32),
                        pltpu.VMEM((LANES + dh + SHIFT_ROWS, R * tq), BF16),
                        pltpu.VMEM((dh + SHIFT_ROWS, R * tq), BF16),
                        pltpu.VMEM((ncp, R * tq), BF16),
                        pltpu.VMEM((tk, R * tq), F32), pltpu.VMEM((tk, R * tq), F32),
                        pltpu.VMEM((tk, R * tq), F32), pltpu.VMEM((tk, R * tq), F32),
                        pltpu.VMEM((n_sel_blocks, tq), F32)],
        compiler_params=_cparams("parallel", "parallel", "arbitrary"),
        name="nsa_attention",
    )(k2m, head_bias, proj, proj, pgt, qg, kc, vct, ksa, vst, kw, vwt, bias_c, bias_sel, bias_win,
      c2st)


def nsa_layer(x, shift, scale, gate, norm_g, rel_bias, w_in, q_gain, k_gain,
              cmp_pe, cmp_w1, cmp_w2, w_out):
    d = x.shape[-1]
    n_in = w_in.shape[1]
    n_pad = -(-n_in // 1024) * 1024
    w_in_p = jnp.pad(w_in, ((0, 0), (0, n_pad - n_in))).astype(BF16)
    proj = norm_matmul(x, norm_g, shift, scale, w_in_p)
    kc, vct, ksa, vst, kw, vwt, k2m_rows = nsa_prep(proj, k_gain, cmp_pe, cmp_w1, cmp_w2)
    o = nsa_attention(proj, kc, vct, ksa, vst, kw, vwt, k2m_rows, rel_bias, q_gain)
    return out_proj_residual(o, w_out.astype(BF16), x, gate)


def _hgrn_kernel(q_ref, f_ref, v_ref, g_ref, lb_ref, og_ref, tri_ref, ones_ref,
                 o_ref, st_sc, k_sc, c_sc):
    C, dk, S = HG_CHUNK, HGRN_DK, HG_SUB
    n_sub = C // S

    @pl.when(pl.program_id(2) == 0)
    def _():
        st_sc[...] = jnp.zeros(st_sc.shape, F32)

    row = lax.broadcasted_iota(jnp.int32, (C, C), 0)
    colm = lax.broadcasted_iota(jnp.int32, (C, C), 1)
    rloc = lax.broadcasted_iota(jnp.int32, (C, dk), 0)
    diag_keep = ((row // S) == (colm // S)) & ((colm % S) <= (row % S))

    def rows_bcast(ref, hh, first, period):
        return jnp.concatenate(
            [jnp.broadcast_to(ref[hh, pl.ds(g * period + first, 1), :], (period, dk))
             for g in range(C // period)], axis=0)

    heads = range(HG_HEADS)
    sls = [slice(hh * dk, (hh + 1) * dk) for hh in heads]
    qs, ks, cums, vs = [], [], [], []
    for hh in heads:
        fl2 = f_ref[0, :, sls[hh]] * LOG2E
        lb = lb_ref[0, :, sls[hh]]
        log_sig = jnp.minimum(fl2, 0.0) - jnp.log2(1.0 + jnp.exp2(-jnp.abs(fl2)))
        ta = jnp.log2(lb)
        tb = jnp.log2(1.0 - lb) + log_sig
        lf = jnp.maximum(ta, tb) + jnp.log2(1.0 + jnp.exp2(-jnp.abs(ta - tb)))
        k = 1.0 - jnp.exp2(lf)
        cum = _dot_lhs_exact(tri_ref[...], lf)
        k_sc[hh] = k
        c_sc[hh] = cum
        ks.append(k)
        cums.append(cum)
        qs.append(_silu(q_ref[0, :, sls[hh]]))
        vs.append(v_ref[0, :, sls[hh]].astype(BF16))

    attns = []
    for hh in heads:
        pieces = []
        for j in range(S):
            kj = rows_bcast(k_sc, hh, j, S)
            cj = rows_bcast(c_sc, hh, j, S)
            pieces.append((qs[hh] * kj * jnp.exp2(jnp.minimum(cums[hh] - cj, 0.0))).astype(BF16))
        attns.append(jnp.where(diag_keep,
                               _dot(jnp.concatenate(pieces, axis=1), ones_ref[...]), 0.0))

    m = S
    while m < C:
        upper = (rloc // m) % 2 == 1
        same = (row // (2 * m)) == (colm // (2 * m))
        for hh in heads:
            e = jnp.exp2(-jnp.abs(cums[hh] - rows_bcast(c_sc, hh, m - 1, 2 * m)))
            qm = jnp.where(upper, qs[hh] * e, 0.0).astype(BF16)
            km = jnp.where(upper, 0.0, ks[hh] * e).astype(BF16)
            attns[hh] = attns[hh] + jnp.where(same, _dot_nt(qm, km), 0.0)
        m *= 2

    for hh in heads:
        st = st_sc[hh]
        cum, k, q, v = cums[hh], ks[hh], qs[hh], vs[hh]
        o = _dot(attns[hh].astype(BF16), v)
        o = o + _dot_nt((q * jnp.exp2(cum)).astype(BF16), st.astype(BF16))
        total = cum[C - 1:C, :]
        kd = (k * jnp.exp2(total - cum)).astype(BF16)
        st_sc[hh] = st * jnp.exp2(total) + lax.dot_general(
            v, kd, (((0,), (0,)), ((), ())), preferred_element_type=F32)
        ms = jnp.mean(o * o, axis=-1, keepdims=True)
        o = o * lax.rsqrt(ms + NORM_EPS) * og_ref[...]
        o_ref[0, :, sls[hh]] = (o * _silu(g_ref[0, :, sls[hh]])).astype(o_ref.dtype)


def hgrn_recurrence(proj, lb, out_gain):
    b, t, four_d = proj.shape
    d = four_d // 4
    dk = HGRN_DK
    C, S, hps = HG_CHUNK, HG_SUB, HG_HEADS
    w = hps * dk
    nhp = d // w
    tri = jnp.asarray(np.tril(np.ones((C, C), np.float32)), BF16)
    ones = jnp.asarray(np.arange(S * dk)[:, None] // dk == (np.arange(C)[None, :] % S), BF16)
    return pl.pallas_call(
        _hgrn_kernel,
        grid=(b, nhp, t // C),
        in_specs=[
            pl.BlockSpec((1, C, w), lambda bi, h, c: (bi, c, h)),
            pl.BlockSpec((1, C, w), lambda bi, h, c: (bi, c, nhp + h)),
            pl.BlockSpec((1, C, w), lambda bi, h, c: (bi, c, 2 * nhp + h)),
            pl.BlockSpec((1, C, w), lambda bi, h, c: (bi, c, 3 * nhp + h)),
            pl.BlockSpec((1, 1, w), lambda bi, h, c: (h, 0, 0)),
            pl.BlockSpec((1, dk), lambda bi, h, c: (0, 0)),
            pl.BlockSpec((C, C), lambda bi, h, c: (0, 0)),
            pl.BlockSpec((S * dk, C), lambda bi, h, c: (0, 0)),
        ],
        out_specs=pl.BlockSpec((1, C, w), lambda bi, h, c: (bi, c, h)),
        out_shape=jax.ShapeDtypeStruct((b, t, d), BF16),
        scratch_shapes=[pltpu.VMEM((hps, dk, dk), F32), pltpu.VMEM((hps, C, dk), F32),
                        pltpu.VMEM((hps, C, dk), F32)],
        compiler_params=_cparams("parallel", "parallel", "arbitrary"),
        name="hgrn_recurrence",
    )(proj, proj, proj, proj, lb.reshape(nhp, 1, w), out_gain.reshape(1, dk), tri, ones)


def hgrn_layer(x, shift, scale, gate, norm_g, lb, w_in, out_gain, w_out):
    proj = norm_matmul(x, norm_g, shift, scale, w_in.astype(BF16))
    o = hgrn_recurrence(proj, lb, out_gain)
    return out_proj_residual(o, w_out.astype(BF16), x, gate)


def _router_kernel(x_ref, g_ref, sh_ref, sc_ref, wr_ref, u_ref,
                   h_ref, eid_ref, ew_ref, pos_ref, cnt_ref, run_sc):
    @pl.when((pl.program_id(0) == 0) & (pl.program_id(1) == 0))
    def _():
        run_sc[...] = jnp.zeros(run_sc.shape, F32)

    h = _modulated_norm(x_ref[0], g_ref[...], sh_ref[0], sc_ref[0])
    h_ref[0] = h.astype(BF16)
    lt = _dot_bf16x3(h, wr_ref[...]).T
    NG, EPG = MOE_GROUPS, MOE_EPG

    def softmax_rows(rows):
        mx = functools.reduce(jnp.maximum, rows)
        es = [jnp.exp(r - mx) for r in rows]
        tot = functools.reduce(lambda a, c: a + c, es)
        return [e / tot for e in es]

    def argmax_rows(rows):
        best, idx = rows[0], jnp.zeros(rows[0].shape, jnp.int32)
        for i in range(1, len(rows)):
            better = rows[i] > best
            best = jnp.where(better, rows[i], best)
            idx = jnp.where(better, i, idx)
        return best, idx

    pg = softmax_rows([lt[i:i + 1] for i in range(NG)])
    p_grp, grp = argmax_rows(pg)
    el = []
    for j in range(EPG):
        acc = lt[NG + j:NG + j + 1]
        for gi in range(1, NG):
            acc = jnp.where(grp == gi, lt[NG + gi * EPG + j:NG + gi * EPG + j + 1], acc)
        el.append(acc)
    pe = softmax_rows(el)
    p1, i1 = argmax_rows(pe)
    p2, i2 = argmax_rows([jnp.where(i1 == j, -1.0, pe[j]) for j in range(EPG)])
    den = p1 + p2
    e1 = grp * EPG + i1
    e2 = grp * EPG + i2
    eid_ref[0:1, :] = e1
    eid_ref[1:2, :] = e2
    ew_ref[0:1, :] = p_grp * p1 / den
    ew_ref[1:2, :] = p_grp * p2 / den

    tm = e1.shape[1]
    ex = lax.broadcasted_iota(jnp.int32, (MOE_EXPERTS, tm), 0)
    oh1 = jnp.where(ex == e1, 1.0, 0.0)
    oh2 = jnp.where(ex == e2, 1.0, 0.0)
    before1 = _dot(oh1.astype(BF16), u_ref[...])
    before2 = _dot(oh2.astype(BF16), u_ref[...])
    tot1 = jnp.sum(oh1, axis=1, keepdims=True)
    tot2 = jnp.sum(oh2, axis=1, keepdims=True)
    run = run_sc[...]
    pos1 = jnp.sum(oh1 * (before1 + run), axis=0, keepdims=True)
    pos2 = jnp.sum(oh2 * (before2 + (run + tot1)), axis=0, keepdims=True)
    pos_ref[0:1, :] = pos1.astype(jnp.int32)
    pos_ref[1:2, :] = pos2.astype(jnp.int32)
    run = run + tot1 + tot2
    run_sc[...] = run
    cnt_ref[...] = jnp.broadcast_to(run, cnt_ref.shape)


def moe_router(x, gain, shift, scale, w_group, w_expert, tm=512):
    b, t, d = x.shape
    tm = min(tm, t)
    nt = t // tm
    wr = jnp.concatenate([w_group, w_expert], axis=1)
    wr = jnp.pad(wr, ((0, 0), (0, LANES - wr.shape[1])))
    upper = jnp.asarray(np.triu(np.ones((tm, tm), np.float32), 1), BF16)
    return pl.pallas_call(
        _router_kernel,
        grid=(b, nt),
        in_specs=[
            pl.BlockSpec((1, tm, d), lambda bi, i: (bi, i, 0)),
            pl.BlockSpec((1, d), lambda bi, i: (0, 0)),
            pl.BlockSpec((1, 1, d), lambda bi, i: (bi, 0, 0)),
            pl.BlockSpec((1, 1, d), lambda bi, i: (bi, 0, 0)),
            pl.BlockSpec((d, LANES), lambda bi, i: (0, 0)),
            pl.BlockSpec((tm, tm), lambda bi, i: (0, 0)),
        ],
        out_specs=[
            pl.BlockSpec((1, tm, d), lambda bi, i: (bi, i, 0)),
            pl.BlockSpec((MOE_TOP_K, tm), lambda bi, i: (0, bi * nt + i)),
            pl.BlockSpec((MOE_TOP_K, tm), lambda bi, i: (0, bi * nt + i)),
            pl.BlockSpec((MOE_TOP_K, tm), lambda bi, i: (0, bi * nt + i)),
            pl.BlockSpec((MOE_EXPERTS, LANES), lambda bi, i: (0, 0)),
        ],
        out_shape=[
            jax.ShapeDtypeStruct((b, t, d), BF16),
            jax.ShapeDtypeStruct((MOE_TOP_K, b * t), jnp.int32),
            jax.ShapeDtypeStruct((MOE_TOP_K, b * t), F32),
            jax.ShapeDtypeStruct((MOE_TOP_K, b * t), jnp.int32),
            jax.ShapeDtypeStruct((MOE_EXPERTS, LANES), F32),
        ],
        scratch_shapes=[pltpu.VMEM((MOE_EXPERTS, 1), F32)],
        compiler_params=_cparams("arbitrary", "arbitrary"),
        name="moe_router",
    )(x, gain.reshape(1, d), shift.reshape(b, 1, d), scale.reshape(b, 1, d), wr, upper)


def _expert_ffn_kernel(be_ref, nu_ref, xb_ref, w1_ref, w3_ref, w2_ref, yb_ref,
                       w1_sc, w3_sc, w2_sc):
    i = pl.program_id(0)

    @pl.when((i == 0) | (be_ref[i] != be_ref[jnp.maximum(i - 1, 0)]))
    def _():
        w1_sc[...] = w1_ref[0].astype(BF16)
        w3_sc[...] = w3_ref[0].astype(BF16)
        w2_sc[...] = w2_ref[0].astype(BF16)

    @pl.when(i < nu_ref[0])
    def _():
        xb = xb_ref[...]
        a = _dot(xb, w1_sc[...])
        g = _dot(xb, w3_sc[...])
        yb_ref[...] = _dot((_silu(a) * g).astype(BF16), w2_sc[...]).astype(yb_ref.dtype)

    @pl.when(i >= nu_ref[0])
    def _():
        yb_ref[...] = jnp.zeros(yb_ref.shape, yb_ref.dtype)


def expert_ffn(xb, blk_e, n_used, w1, w3, w2):
    p, d = xb.shape
    ff = w1.shape[2]
    rb = MOE_ROW_BLOCK
    grid_spec = pltpu.PrefetchScalarGridSpec(
        num_scalar_prefetch=2,
        grid=(p // rb,),
        in_specs=[
            pl.BlockSpec((rb, d), lambda i, be, nu: (i, 0)),
            pl.BlockSpec((1, d, ff), lambda i, be, nu: (be[i], 0, 0)),
            pl.BlockSpec((1, d, ff), lambda i, be, nu: (be[i], 0, 0)),
            pl.BlockSpec((1, ff, d), lambda i, be, nu: (be[i], 0, 0)),
        ],
        out_specs=pl.BlockSpec((rb, d), lambda i, be, nu: (i, 0)),
        scratch_shapes=[pltpu.VMEM((d, ff), BF16), pltpu.VMEM((d, ff), BF16),
                        pltpu.VMEM((ff, d), BF16)],
    )
    return pl.pallas_call(
        _expert_ffn_kernel,
        grid_spec=grid_spec,
        out_shape=jax.ShapeDtypeStruct((p, d), BF16),
        compiler_params=_cparams("arbitrary"),
        name="expert_ffn",
    )(blk_e, n_used, xb, w1, w3, w2)


def _combine_kernel(x_ref, gt_ref, y1_ref, y2_ref, w_ref, o_ref):
    w = w_ref[0]
    y = w[:, 0:1] * y1_ref[0].astype(F32) + w[:, 1:2] * y2_ref[0].astype(F32)
    o_ref[0] = x_ref[0] + gt_ref[0] * y


def moe_combine(x, gate, y1, y2, w, tm=512):
    b, t, d = x.shape
    tm = min(tm, t)
    spec = pl.BlockSpec((1, tm, d), lambda bi, i: (bi, i, 0))
    return pl.pallas_call(
        _combine_kernel,
        grid=(b, t // tm),
        in_specs=[spec, pl.BlockSpec((1, 1, d), lambda bi, i: (bi, 0, 0)), spec, spec,
                  pl.BlockSpec((1, tm, MOE_TOP_K), lambda bi, i: (bi, i, 0))],
        out_specs=spec,
        out_shape=jax.ShapeDtypeStruct((b, t, d), F32),
        compiler_params=_cparams("parallel", "parallel"),
        name="moe_combine",
    )(x, gate.reshape(b, 1, d), y1, y2, w)


def moe_layer(x, shift, scale, gate, norm_g, w_group, w_expert, w1, w3, w2):
    b, t, d = x.shape
    n = b * t
    a = n * MOE_TOP_K
    rb = MOE_ROW_BLOCK
    h, eid, ew, pos, cnt = moe_router(x, norm_g, shift, scale, w_group, w_expert)
    counts = cnt[:, 0].astype(jnp.int32)
    padded = (counts + rb - 1) // rb * rb
    pad_end = jnp.cumsum(padded)
    pad_start = pad_end - padded
    is_e = eid[..., None] == jnp.arange(MOE_EXPERTS, dtype=jnp.int32)
    dest = jnp.sum(jnp.where(is_e, pad_start, 0), axis=-1) + pos
    n_blk = -(-(a + MOE_EXPERTS * (rb - 1)) // rb)
    p = n_blk * rb
    blk_start = jnp.arange(n_blk, dtype=jnp.int32) * rb
    blk_e = jnp.minimum(jnp.sum((pad_end[None, :] <= blk_start[:, None]).astype(jnp.int32), axis=1),
                        MOE_EXPERTS - 1)
    n_used = (pad_end[-1:] // rb).astype(jnp.int32)
    tok = jnp.tile(jnp.arange(n, dtype=jnp.int32), MOE_TOP_K)
    _, tok_by_slot = lax.sort_key_val(dest.reshape(a), tok)
    seg_shift = pad_start - (jnp.cumsum(counts) - counts)
    slot = jnp.arange(p, dtype=jnp.int32)
    e_of_slot = jnp.repeat(blk_e, rb)
    shift = jnp.sum(jnp.where(e_of_slot[:, None] == jnp.arange(MOE_EXPERTS, dtype=jnp.int32),
                              seg_shift, 0), axis=-1)
    idx = slot - shift
    buf_t = tok_by_slot[jnp.where(idx < a, idx, slot % a)]
    xb = h.reshape(n, d)[buf_t]
    yb = expert_ffn(xb, blk_e, n_used, w1, w3, w2)
    y1 = yb[dest[0]].reshape(b, t, d)
    y2 = yb[dest[1]].reshape(b, t, d)
    return moe_combine(x, gate, y1, y2, ew.T.reshape(b, t, MOE_TOP_K))


def kernel(x, c, ada_w, ada_b, norm_g, rel_bias, nsa_w_in, nsa_q_gain, nsa_k_gain, nsa_cmp_pe,
           nsa_cmp_w1, nsa_cmp_w2, nsa_w_out, hgrn_w_in, hgrn_lower_bounds, hgrn_out_gain,
           hgrn_w_out, moe_router_group, moe_router_expert, moe_w1, moe_w3, moe_w2):
    depth = ada_w.shape[0]
    d = x.shape[-1]
    lb_soft = jax.nn.softmax(hgrn_lower_bounds.astype(F32), axis=0)
    lb_all = jnp.cumsum(lb_soft, axis=0) - lb_soft[0]
    mod = adaln_mod(c, ada_w, ada_b)
    for layer in range(depth):
        j = layer // 2
        shift, scale, gate = (mod[layer, 0, :, i * d:(i + 1) * d] for i in range(3))
        if layer % 2 == 0:
            x = nsa_layer(x, shift, scale, gate, norm_g[layer, 0], rel_bias, nsa_w_in[j],
                          nsa_q_gain[j], nsa_k_gain[j], nsa_cmp_pe[j], nsa_cmp_w1[j],
                          nsa_cmp_w2[j], nsa_w_out[j])
        else:
            x = hgrn_layer(x, shift, scale, gate, norm_g[layer, 0], lb_all[layer],
                           hgrn_w_in[j], hgrn_out_gain[j], hgrn_w_out[j])
        shift, scale, gate = (mod[layer, 1, :, i * d:(i + 1) * d] for i in range(3))
        x = moe_layer(x, shift, scale, gate, norm_g[layer, 1], moe_router_group[layer],
                      moe_router_expert[layer], moe_w1[layer], moe_w3[layer], moe_w2[layer])
    return x
```

```python
import functools
import math

import numpy as np
import jax
import jax.numpy as jnp
from jax import lax
from jax.experimental import pallas as pl
from jax.experimental.pallas import tpu as pltpu

F32 = jnp.float32
BF16 = jnp.bfloat16
HIGHEST = lax.Precision.HIGHEST

NSA_HEADS = 16
NSA_KV_GROUPS = 4
NSA_GROUP_SIZE = NSA_HEADS // NSA_KV_GROUPS
NSA_HEAD_DIM = 64
CMP_BLOCK = 32
CMP_STRIDE = 16
SEL_BLOCK = 64
N_SELECT = 8
WINDOW = 512
FORCED_SCORE = 1.0e4
REL_BUCKETS = 32
REL_MAX_DISTANCE = 1024
HGRN_DK = 128
MOE_GROUPS = 4
MOE_EPG = 4
MOE_EXPERTS = MOE_GROUPS * MOE_EPG
MOE_TOP_K = 2
MOE_ROW_BLOCK = 512
NORM_EPS = 1e-6

LANES = 128
NEG = -1.0e30
LOG2E = math.log2(math.e)
VMEM_LIMIT = 48 * 1024 * 1024

ATT_TQ = 256
ATT_TK = 256
SHIFT_ROWS = 16
SAFE_GAP = 60.0
BOUND_SLACK = 1.01
DENOM_ROWS = 16
HG_CHUNK = 128
HG_SUB = 8
HG_HEADS = 4


def _cparams(*sem):
    return pltpu.CompilerParams(dimension_semantics=sem, vmem_limit_bytes=VMEM_LIMIT)


def _dot(a, b):
    return jnp.dot(a, b, preferred_element_type=F32)


def _dot_exact(a, b):
    return jnp.dot(a, b, preferred_element_type=F32, precision=HIGHEST)


def _dot_lhs_exact(a, x):
    x1 = x.astype(BF16)
    r1 = x - x1.astype(F32)
    x2 = r1.astype(BF16)
    x3 = (r1 - x2.astype(F32)).astype(BF16)
    return _dot(a, x1) + _dot(a, x2) + _dot(a, x3)


def _dot_bf16x3(a, b):
    a1 = a.astype(BF16)
    a2 = (a - a1.astype(F32)).astype(BF16)
    b1 = b.astype(BF16)
    b2 = (b - b1.astype(F32)).astype(BF16)
    return _dot(a1, b1) + _dot(a1, b2) + _dot(a2, b1)


def _dot_nt(a, b):
    return lax.dot_general(a, b, (((1,), (1,)), ((), ())), preferred_element_type=F32)


def _sigmoid(x):
    return 0.5 * jnp.tanh(0.5 * x) + 0.5


def _silu(x):
    return x * _sigmoid(x)


def _adaln_kernel(c_ref, w_ref, b_ref, o_ref):
    cond = _silu(c_ref[...])
    o_ref[0] = _dot_exact(cond, w_ref[0]) + b_ref[0]


def adaln_mod(c, ada_w, ada_b):
    depth, two, d, d3 = ada_w.shape
    b = c.shape[0]
    ls = depth * two
    tn = 1024
    out = pl.pallas_call(
        _adaln_kernel,
        grid=(ls, d3 // tn),
        in_specs=[
            pl.BlockSpec((b, d), lambda i, j: (0, 0)),
            pl.BlockSpec((1, d, tn), lambda i, j: (i, 0, j)),
            pl.BlockSpec((1, 1, tn), lambda i, j: (i, 0, j)),
        ],
        out_specs=pl.BlockSpec((1, b, tn), lambda i, j: (i, 0, j)),
        out_shape=jax.ShapeDtypeStruct((ls, b, d3), F32),
        compiler_params=_cparams("parallel", "parallel"),
        name="adaln_mod",
    )(c, ada_w.reshape(ls, d, d3), ada_b.reshape(ls, 1, d3))
    return out.reshape(depth, two, b, d3)


def _modulated_norm(x, gain, shift, scale):
    ms = jnp.mean(x * x, axis=-1, keepdims=True)
    y = x * lax.rsqrt(ms + NORM_EPS) * gain
    return y * (1.0 + scale) + shift


def _norm_matmul_kernel(x_ref, g_ref, sh_ref, sc_ref, w_ref, o_ref):
    h = _modulated_norm(x_ref[0], g_ref[...], sh_ref[0], sc_ref[0])
    o_ref[0] = _dot(h.astype(BF16), w_ref[...])


def norm_matmul(x, gain, shift, scale, w, tm=512):
    b, t, d = x.shape
    n = w.shape[1]
    tm = min(tm, t)
    return pl.pallas_call(
        _norm_matmul_kernel,
        grid=(b, t // tm),
        in_specs=[
            pl.BlockSpec((1, tm, d), lambda bi, i: (bi, i, 0)),
            pl.BlockSpec((1, d), lambda bi, i: (0, 0)),
            pl.BlockSpec((1, 1, d), lambda bi, i: (bi, 0, 0)),
            pl.BlockSpec((1, 1, d), lambda bi, i: (bi, 0, 0)),
            pl.BlockSpec((d, n), lambda bi, i: (0, 0)),
        ],
        out_specs=pl.BlockSpec((1, tm, n), lambda bi, i: (bi, i, 0)),
        out_shape=jax.ShapeDtypeStruct((b, t, n), F32),
        compiler_params=_cparams("parallel", "parallel"),
        name="norm_matmul",
    )(x, gain.reshape(1, d), shift.reshape(b, 1, d), scale.reshape(b, 1, d), w)


def _out_proj_kernel(o_ref, w_ref, x_ref, gt_ref, y_ref):
    y = _dot(o_ref[0], w_ref[...])
    y_ref[0] = x_ref[0] + gt_ref[0] * y


def out_proj_residual(o, w, x, gate, tm=512):
    b, t, d = x.shape
    k = o.shape[-1]
    tm = min(tm, t)
    return pl.pallas_call(
        _out_proj_kernel,
        grid=(b, t // tm),
        in_specs=[
            pl.BlockSpec((1, tm, k), lambda bi, i: (bi, i, 0)),
            pl.BlockSpec((k, d), lambda bi, i: (0, 0)),
            pl.BlockSpec((1, tm, d), lambda bi, i: (bi, i, 0)),
            pl.BlockSpec((1, 1, d), lambda bi, i: (bi, 0, 0)),
        ],
        out_specs=pl.BlockSpec((1, tm, d), lambda bi, i: (bi, i, 0)),
        out_shape=jax.ShapeDtypeStruct((b, t, d), F32),
        compiler_params=_cparams("parallel", "parallel"),
        name="out_proj_residual",
    )(o, w, x, gate.reshape(b, 1, d))


def _pair_rms(x, gain2):
    lane = lax.broadcasted_iota(jnp.int32, x.shape, 1)
    lo = lane < NSA_HEAD_DIM
    x2 = x * x
    s_lo = jnp.sum(jnp.where(lo, x2, 0.0), axis=-1, keepdims=True)
    s_hi = jnp.sum(jnp.where(lo, 0.0, x2), axis=-1, keepdims=True)
    inv = jnp.where(lo, lax.rsqrt(s_lo / NSA_HEAD_DIM + NORM_EPS),
                    lax.rsqrt(s_hi / NSA_HEAD_DIM + NORM_EPS))
    return x * inv * gain2


def _nsa_prep_kernel(kc0_ref, kc1_ref, vc0_ref, vc1_ref, ks_ref, vs_ref, kw_ref, vw_ref,
                     kg_ref, pe_ref, w1_ref, w1big_ref, w2big_ref, ex_ref,
                     kco_ref, vct_ref, kso_ref, vst_ref, kwo_ref, vwt_ref, k2m_ref):
    t = ks_ref.shape[1]
    nblk = t // CMP_STRIDE
    dh = NSA_HEAD_DIM
    G = NSA_KV_GROUPS

    lane = lax.broadcasted_iota(jnp.int32, (t, LANES), 1)
    ones_cols = jnp.ones((t, SHIFT_ROWS), BF16)
    for br, (src, dst, gi, off) in enumerate(((ks_ref, kso_ref, 1, LANES), (kw_ref, kwo_ref, 2, 0))):
        g2 = kg_ref[gi:gi + 1, :]
        for gp in range(G // 2):
            kn = _pair_rms(src[0, :, gp * LANES:(gp + 1) * LANES], g2).astype(BF16)
            dst[0, 2 * gp, :, off:off + dh] = kn[:, :dh]
            dst[0, 2 * gp + 1, :, off:off + dh] = kn[:, dh:]
            k2 = kn.astype(F32) ** 2
            for half, keep in enumerate((lane < dh, lane >= dh)):
                n2 = jnp.sum(jnp.where(keep, k2, 0.0), axis=-1, keepdims=True)
                row = 2 * (2 * gp + half) + br
                k2m_ref[0, row:row + 1, :] = jnp.broadcast_to(
                    jnp.max(n2, axis=0, keepdims=True), (1, LANES))
        for g in range(G):
            dst[0, g, :, off + dh:off + dh + SHIFT_ROWS] = ones_cols
    for g in range(G):
        kso_ref[0, g, :, :LANES] = ex_ref[...]
    for src, dst in ((vs_ref, vst_ref), (vw_ref, vwt_ref)):
        for gp in range(G // 2):
            vt = src[0, :, gp * LANES:(gp + 1) * LANES].T.astype(BF16)
            dst[0, 2 * gp] = vt[:dh]
            dst[0, 2 * gp + 1] = vt[dh:]

    for ci, (srcs, is_key) in enumerate((((kc0_ref, kc1_ref), True), ((vc0_ref, vc1_ref), False))):
        pe1 = _dot(pe_ref[ci].astype(BF16), w1_ref[ci])[0:1]
        pe2 = jnp.concatenate([pe1, pe1], axis=1)
        for gp, src in enumerate(srcs):
            parts = [src[0, pl.ds(l, nblk, stride=CMP_STRIDE), :] for l in range(CMP_STRIDE)]
            r = jnp.concatenate(parts, axis=1).astype(BF16)
            ab = _dot(r, w1big_ref[ci])
            second = ab[:, LANES:]
            shifted = jnp.concatenate([second[1:], jnp.zeros((1, LANES), F32)], axis=0)
            pre = ab[:, :LANES] + shifted + pe2
            hid = _dot(_silu(pre).astype(BF16), w2big_ref[ci])
            if is_key:
                kn = _pair_rms(hid, kg_ref[0:1, :]).astype(BF16)
                kco_ref[0, 2 * gp] = kn[:, :dh]
                kco_ref[0, 2 * gp + 1] = kn[:, dh:]
            else:
                vt = hid.T.astype(BF16)
                vct_ref[0, 2 * gp] = vt[:dh]
                vct_ref[0, 2 * gp + 1] = vt[dh:]


def nsa_prep(proj, k_gain, cmp_pe, cmp_w1, cmp_w2):
    b, t, _ = proj.shape
    G, dh = NSA_KV_GROUPS, NSA_HEAD_DIM
    kvw = G * dh
    q_blocks = (NSA_HEADS * dh) // kvw
    nblk = t // CMP_STRIDE
    kg2 = jnp.concatenate([k_gain, k_gain], axis=1)
    pe_flat = jnp.broadcast_to(cmp_pe.reshape(2, 1, CMP_BLOCK * dh), (2, 8, CMP_BLOCK * dh))

    w1r = cmp_w1.reshape(2, 2, CMP_STRIDE, dh, dh)
    eye2 = jnp.eye(2, dtype=cmp_w1.dtype)
    w1big = jnp.einsum('chlde,gk->clgdhke', w1r, eye2).reshape(2, CMP_STRIDE * 2 * dh, 4 * dh)
    w2big = jnp.einsum('cde,gk->cgdke', cmp_w2, eye2).reshape(2, 2 * dh, 2 * dh)
    expand = jnp.asarray((np.arange(t)[:, None] // SEL_BLOCK) == np.arange(LANES)[None, :], BF16)

    def col(i):
        return pl.BlockSpec((1, t, kvw), lambda bi, i=i: (bi, 0, q_blocks + i))

    def col_pair(i, gp):
        return pl.BlockSpec((1, t, LANES), lambda bi: (bi, 0, (q_blocks + i) * (kvw // LANES) + gp))

    def full(shape):
        return pl.BlockSpec(shape, lambda bi: (0,) * len(shape))

    def per_b(shape):
        return pl.BlockSpec((1,) + shape, lambda bi: (bi,) + (0,) * len(shape))

    return pl.pallas_call(
        _nsa_prep_kernel,
        grid=(b,),
        in_specs=[col_pair(0, 0), col_pair(0, 1), col_pair(1, 0), col_pair(1, 1),
                  col(2), col(3), col(4), col(5),
                  full((3, 2 * dh)), full((2, 8, CMP_BLOCK * dh)),
                  full((2, CMP_BLOCK * dh, dh)), full((2, 2 * CMP_STRIDE * dh, 4 * dh)),
                  full((2, 2 * dh, 2 * dh)), full((t, LANES))],
        out_specs=[per_b((G, nblk, dh)), per_b((G, dh, nblk)),
                   per_b((G, t, LANES + dh + SHIFT_ROWS)), per_b((G, dh, t)),
                   per_b((G, t, dh + SHIFT_ROWS)), per_b((G, dh, t)),
                   per_b((2 * G, LANES))],
        out_shape=[jax.ShapeDtypeStruct((b, G, nblk, dh), BF16),
                   jax.ShapeDtypeStruct((b, G, dh, nblk), BF16),
                   jax.ShapeDtypeStruct((b, G, t, LANES + dh + SHIFT_ROWS), BF16),
                   jax.ShapeDtypeStruct((b, G, dh, t), BF16),
                   jax.ShapeDtypeStruct((b, G, t, dh + SHIFT_ROWS), BF16),
                   jax.ShapeDtypeStruct((b, G, dh, t), BF16),
                   jax.ShapeDtypeStruct((b, 2 * G, LANES), F32)],
        compiler_params=_cparams("parallel"),
        name="nsa_prep",
    )(proj, proj, proj, proj, proj, proj, proj, proj, kg2, pe_flat,
      cmp_w1.astype(BF16), w1big.astype(BF16), w2big.astype(BF16), expand)


def _nsa_attn_kernel(k2m_ref, hb_ref, q_ref, gl_ref, pgt_ref, qg_ref, kc_ref, vct_ref,
                     ksa_ref, vst_ref, kw_ref, vwt_ref, bc_ref, bs_ref, bw_ref, c2st_ref,
                     o_ref, acc_sc, qsel_sc, qwin_sc, pc_sc, ssa_sc, ssb_sc, swa_sc, swb_sc,
                     score_sc, *,
                     e_sat, n_wtiles,
                     n_sel_blocks, n_select):
    tq, tk = ATT_TQ, ATT_TK
    R, dh = NSA_GROUP_SIZE, NSA_HEAD_DIM
    g_id, b_id, qi = pl.program_id(0), pl.program_id(1), pl.program_id(2)
    scale = dh ** -0.5 * LOG2E
    qoff = LANES

    gl = gl_ref[0]
    g1 = gl.astype(BF16)
    gr = gl - g1.astype(F32)
    g2 = gr.astype(BF16)
    g3 = (gr - g2.astype(F32)).astype(BF16)
    gates = _sigmoid(_dot_nt(pgt_ref[0], g1) + _dot_nt(pgt_ref[0], g2) + _dot_nt(pgt_ref[0], g3))

    qt = q_ref[0].T
    q2 = []
    for r in range(R):
        qr = qt[r * dh:(r + 1) * dh]
        ms = jnp.mean(qr * qr, axis=0, keepdims=True)
        qn = (qr * lax.rsqrt(ms + NORM_EPS) * qg_ref[...] * scale).astype(BF16)
        qwin_sc[:dh, r * tq:(r + 1) * tq] = qn
        qsel_sc[qoff:qoff + dh, r * tq:(r + 1) * tq] = qn
        q2.append(jnp.sum(qn.astype(F32) ** 2, axis=0, keepdims=True))

    k2_sel = k2m_ref[b_id, 2 * g_id]
    k2_win = k2m_ref[b_id, 2 * g_id + 1]
    q2_max = jnp.max(functools.reduce(jnp.maximum, q2))
    thr = hb_ref[NSA_HEADS]
    fast = (2 * BOUND_SLACK) ** 2 * q2_max * jnp.maximum(k2_sel, k2_win) <= thr * thr
    shift_pad = jnp.zeros((SHIFT_ROWS - 1, tq), F32)
    for r in range(R):
        bmax = hb_ref[g_id * R + r]
        for k2, q_sc, row0 in ((k2_sel, qsel_sc, qoff + dh), (k2_win, qwin_sc, dh)):
            bound = jnp.sqrt(q2[r] * k2) * BOUND_SLACK + bmax
            rows = jnp.concatenate([jnp.where(fast, -bound, 0.0), shift_pad], axis=0)
            q_sc[row0:row0 + SHIFT_ROWS, r * tq:(r + 1) * tq] = rows.astype(BF16)

    ones_rows = jnp.ones((DENOM_ROWS, tk), BF16)
    rr = tq // tk
    n_tiles = rr * qi + rr

    def make_sweep(k_ref, vt_ref, q_sc, b_ref, hi, e_last, sa_sc, sb_sc):
        def tile_col(d):
            dc = jnp.minimum(d, hi - 1)
            e = jnp.where(dc == 0, rr - 1, jnp.where(dc < rr, dc - 1, dc))
            return pl.multiple_of((n_tiles - 1 - e) * tk, tk), jnp.minimum(e, e_last)

        def logits(d, dst_sc):
            col, _ = tile_col(d)
            dst_sc[...] = _dot(k_ref[0, 0, pl.ds(col, tk), :], q_sc[...])

        def softmax_pv(d, src_sc, ms):
            col, bi = tile_col(d)
            vaug = jnp.concatenate([vt_ref[0, 0, :, pl.ds(col, tk)], ones_rows], axis=0)
            new_ms = []
            for r in range(R):
                s = src_sc[:, r * tq:(r + 1) * tq] + b_ref[r, bi]
                m_new = jnp.maximum(ms[r], jnp.max(s, axis=0, keepdims=True))
                alpha = jnp.exp2(ms[r] - m_new)
                p = jnp.exp2(s - m_new).astype(BF16)
                acc_sc[r] = acc_sc[r] * alpha + _dot(vaug, p)
                new_ms.append(m_new)
            return tuple(new_ms)

        def shifted_pv(d, src_sc):
            col, bi = tile_col(d)
            vaug = jnp.concatenate([vt_ref[0, 0, :, pl.ds(col, tk)], ones_rows], axis=0)
            for r in range(R):
                p = jnp.exp2(src_sc[:, r * tq:(r + 1) * tq] + b_ref[r, bi]).astype(BF16)
                acc_sc[r] += _dot(vaug, p)

        def body(dp, ms):
            d = 2 * dp
            logits(d + 1, sb_sc)
            ms = softmax_pv(d, sa_sc, ms)
            logits(d + 2, sa_sc)
            return softmax_pv(d + 1, sb_sc, ms)

        def shifted_body(dp, carry):
            d = 2 * dp
            logits(d + 1, sb_sc)
            shifted_pv(d, sa_sc)
            logits(d + 2, sa_sc)
            shifted_pv(d + 1, sb_sc)
            return carry

        def start():
            logits(0, sa_sc)

        def run():
            acc_sc[...] = jnp.zeros(acc_sc.shape, F32)

            @pl.when(fast)
            def _():
                lax.fori_loop(0, hi // 2, shifted_body, 0)

                @pl.when(hi % 2 == 1)
                def _():
                    shifted_pv(hi - 1, sa_sc)

            @pl.when(jnp.logical_not(fast))
            def _():
                ms = lax.fori_loop(0, hi // 2, body,
                                   tuple(jnp.full((1, tq), NEG, F32) for _ in range(R)))

                @pl.when(hi % 2 == 1)
                def _():
                    softmax_pv(hi - 1, sa_sc, ms)

            outs = []
            for r in range(R):
                acc = acc_sc[r]
                outs.append(acc[:dh] * (1.0 / acc[dh:dh + 1]))
            return outs

        return start, run

    win_start, win_run = make_sweep(kw_ref, vwt_ref, qwin_sc, bw_ref,
                                    jnp.minimum(n_tiles, n_wtiles), n_wtiles - 1, swa_sc, swb_sc)
    win_start()

    ncp = kc_ref.shape[2]
    n_idx = lax.broadcasted_iota(jnp.int32, (ncp, tq), 0)
    t_idx = qi * tq + lax.broadcasted_iota(jnp.int32, (ncp, tq), 1)
    vis = (n_idx * CMP_STRIDE + (CMP_BLOCK - 1)) <= t_idx
    psum = jnp.zeros((ncp, tq), F32)
    sc_all = _dot(kc_ref[0, 0], qwin_sc[:dh])
    for r in range(R):
        s = jnp.where(vis, sc_all[:, r * tq:(r + 1) * tq] + bc_ref[r], NEG)
        m = jnp.max(s, axis=0, keepdims=True)
        e = jnp.where(vis, jnp.exp2(s - m), 0.0)
        p = e * (1.0 / jnp.maximum(jnp.sum(e, axis=0, keepdims=True), 1e-30))
        psum = psum + p
        pc_sc[:, r * tq:(r + 1) * tq] = p.astype(BF16)
    oc_all = _dot(vct_ref[0, 0], pc_sc[...])
    o_cmp = [oc_all[:, r * tq:(r + 1) * tq] for r in range(R)]

    imp_t = _dot_lhs_exact(c2st_ref[...], psum)
    blk = lax.broadcasted_iota(jnp.int32, (n_sel_blocks, tq), 0)
    tpos = qi * tq + lax.broadcasted_iota(jnp.int32, (n_sel_blocks, tq), 1)
    cur = tpos // SEL_BLOCK
    forced = (blk == 0) | (blk == cur) | (blk == cur - 1)
    score = jnp.where(forced, FORCED_SCORE, jnp.where(blk <= cur, imp_t, -1.0))
    score_sc[...] = score
    per_tile = tq // SEL_BLOCK

    def rank_body(g, rank):
        for u in range(per_tile):
            s2 = g * per_tile + u
            row = score_sc[pl.ds(s2, 1), :]
            beats = (row > score) | ((row == score) & (blk > s2))
            rank = rank + jnp.where(beats, 1.0, 0.0)
        return rank

    rank = lax.fori_loop(0, qi + 1, rank_body, jnp.zeros((n_sel_blocks, tq), F32))
    negsel = jnp.where(rank < n_select, 0.0, NEG)
    if n_sel_blocks < LANES:
        negsel = jnp.concatenate([negsel, jnp.zeros((LANES - n_sel_blocks, tq), F32)], axis=0)
    negsel = negsel.astype(BF16)
    for r in range(R):
        qsel_sc[:qoff, r * tq:(r + 1) * tq] = negsel

    sel_start, sel_run = make_sweep(ksa_ref, vst_ref, qsel_sc, bs_ref, n_tiles, e_sat,
                                    ssa_sc, ssb_sc)
    sel_start()
    o_win = win_run()
    o_sel = sel_run()

    outs = []
    for r in range(R):
        outs.append(gates[3 * r:3 * r + 1] * o_cmp[r]
                    + gates[3 * r + 1:3 * r + 2] * o_sel[r]
                    + gates[3 * r + 2:3 * r + 3] * o_win[r])
    o_ref[0] = jnp.concatenate(outs, axis=0).T.astype(o_ref.dtype)


def _t5_bucket(dist):
    n = jnp.maximum(dist, 0)
    max_exact = REL_BUCKETS // 2
    nf = jnp.maximum(n, 1).astype(F32)
    large = max_exact + (jnp.log(nf / max_exact) / math.log(REL_MAX_DISTANCE / max_exact)
                         * (REL_BUCKETS - max_exact)).astype(jnp.int32)
    large = jnp.minimum(large, REL_BUCKETS - 1)
    return jnp.where(n < max_exact, n, large)


def _bias_of_dist(dist, rel_bias):
    onehot = (_t5_bucket(dist)[..., None] == jnp.arange(REL_BUCKETS)).astype(F32)
    out = jnp.einsum('...k,kh->...h', onehot, rel_bias.astype(F32), precision=HIGHEST)
    return jnp.moveaxis(out, -1, 0)


def _saturation_distance():
    max_exact = REL_BUCKETS // 2
    steps = REL_BUCKETS - max_exact
    n_sat = max_exact * (REL_MAX_DISTANCE / max_exact) ** ((steps - 1) / steps)
    return int(math.ceil(n_sat)) + 2


def nsa_attention(proj, kc, vct, ksa, vst, kw, vwt, k2m_rows, rel_bias, q_gain):
    b, t, _ = proj.shape
    G, R, dh = NSA_KV_GROUPS, NSA_GROUP_SIZE, NSA_HEAD_DIM
    H = NSA_HEADS
    tq, tk = ATT_TQ, ATT_TK
    rr = tq // tk
    ncp = kc.shape[2]
    n_sel_blocks = t // SEL_BLOCK
    n_select = min(N_SELECT, n_sel_blocks)
    assert n_sel_blocks <= LANES and n_sel_blocks % 8 == 0 and tq % tk == 0 and t % tq == 0

    e_sat = -(-(_saturation_distance() + tk - 1) // tk) + rr - 1
    n_wtiles = -(-(WINDOW + tk - 1) // tk) + rr - 1
    jj = np.arange(tk)[:, None]
    ii = np.arange(tq)[None, :]

    def tile_dist(n_e):
        return (np.arange(n_e)[:, None, None] - (rr - 1)) * tk + (ii - jj)[None]

    dist = tile_dist(e_sat + 1)
    bias_sel = jnp.where(dist >= 0, _bias_of_dist(jnp.asarray(dist), rel_bias) * LOG2E, NEG)
    dwin = tile_dist(n_wtiles)
    bias_win = jnp.where((dwin >= 0) & (dwin < WINDOW),
                         _bias_of_dist(jnp.asarray(dwin), rel_bias) * LOG2E, NEG)
    dc = np.arange(t)[None, :] - (np.arange(ncp)[:, None] * CMP_STRIDE + CMP_BLOCK - 1)
    bias_c = _bias_of_dist(jnp.asarray(dc), rel_bias) * LOG2E

    cs = np.arange(ncp) * CMP_STRIDE
    ss = np.arange(n_sel_blocks) * SEL_BLOCK
    shared = (np.minimum(cs[None, :] + CMP_BLOCK, ss[:, None] + SEL_BLOCK)
              - np.maximum(cs[None, :], ss[:, None]))
    c2st = jnp.asarray(np.clip(shared, 0, None) / CMP_BLOCK, BF16)
    pgt = np.zeros((G, LANES, LANES), np.float32)
    for g in range(G):
        for k in range(3 * R):
            pgt[g, k, 3 * R * g + k] = 1.0
    pgt = jnp.asarray(pgt, BF16)
    gate_blk = (H * dh + 6 * G * dh) // LANES
    qg = jnp.broadcast_to(q_gain.reshape(dh, 1), (dh, tq))

    rb2 = rel_bias.astype(F32) * LOG2E
    spread = jnp.max(jnp.max(rb2, axis=0) - jnp.min(rb2, axis=0))
    head_bias = jnp.concatenate([jnp.max(rb2, axis=0), jnp.maximum(SAFE_GAP - spread, 0.0)[None]])
    k2m = k2m_rows[:, :, 0]

    kernel = functools.partial(_nsa_attn_kernel, e_sat=e_sat, n_wtiles=n_wtiles,
                               n_sel_blocks=n_sel_blocks, n_select=n_select)
    return pl.pallas_call(
        kernel,
        grid=(G, b, t // tq),
        in_specs=[
            pl.BlockSpec(memory_space=pltpu.SMEM),
            pl.BlockSpec(memory_space=pltpu.SMEM),
            pl.BlockSpec((1, tq, R * dh), lambda g, bi, i: (bi, i, g)),
            pl.BlockSpec((1, tq, LANES), lambda g, bi, i: (bi, i, gate_blk)),
            pl.BlockSpec((1, LANES, LANES), lambda g, bi, i: (g, 0, 0)),
            pl.BlockSpec((dh, tq), lambda g, bi, i: (0, 0)),
            pl.BlockSpec((1, 1, ncp, dh), lambda g, bi, i: (bi, g, 0, 0)),
            pl.BlockSpec((1, 1, dh, ncp), lambda g, bi, i: (bi, g, 0, 0)),
            pl.BlockSpec((1, 1, t, LANES + dh + SHIFT_ROWS), lambda g, bi, i: (bi, g, 0, 0)),
            pl.BlockSpec((1, 1, dh, t), lambda g, bi, i: (bi, g, 0, 0)),
            pl.BlockSpec((1, 1, t, dh + SHIFT_ROWS), lambda g, bi, i: (bi, g, 0, 0)),
            pl.BlockSpec((1, 1, dh, t), lambda g, bi, i: (bi, g, 0, 0)),
            pl.BlockSpec((R, ncp, tq), lambda g, bi, i: (g, 0, i)),
            pl.BlockSpec((R, e_sat + 1, tk, tq), lambda g, bi, i: (g, 0, 0, 0)),
            pl.BlockSpec((R, n_wtiles, tk, tq), lambda g, bi, i: (g, 0, 0, 0)),
            pl.BlockSpec((n_sel_blocks, ncp), lambda g, bi, i: (0, 0)),
        ],
        out_specs=pl.BlockSpec((1, tq, R * dh), lambda g, bi, i: (bi, i, g)),
        out_shape=jax.ShapeDtypeStruct((b, t, H * dh), BF16),
        scratch_shapes=[pltpu.VMEM((R, dh + DENOM_ROWS, tq), F32),
                        pltpu.VMEM((LANES + dh + SHIFT_ROWS, R * tq), BF16),
                        pltpu.VMEM((dh + SHIFT_ROWS, R * tq), BF16),
                        pltpu.VMEM((ncp, R * tq), BF16),
                        pltpu.VMEM((tk, R * tq), F32), pltpu.VMEM((tk, R * tq), F32),
                        pltpu.VMEM((tk, R * tq), F32), pltpu.VMEM((tk, R * tq), F32),
                        pltpu.VMEM((n_sel_blocks, tq), F32)],
        compiler_params=_cparams("parallel", "parallel", "arbitrary"),
        name="nsa_attention",
    )(k2m, head_bias, proj, proj, pgt, qg, kc, vct, ksa, vst, kw, vwt, bias_c, bias_sel, bias_win,
      c2st)


def nsa_layer(x, shift, scale, gate, norm_g, rel_bias, w_in, q_gain, k_gain,
              cmp_pe, cmp_w1, cmp_w2, w_out):
    d = x.shape[-1]
    n_in = w_in.shape[1]
    n_pad = -(-n_in // LANES) * LANES
    w_in_p = jnp.pad(w_in.astype(BF16), ((0, 0), (0, n_pad - n_in)))
    proj = norm_matmul(x, norm_g, shift, scale, w_in_p)
    kc, vct, ksa, vst, kw, vwt, k2m_rows = nsa_prep(proj, k_gain, cmp_pe, cmp_w1, cmp_w2)
    o = nsa_attention(proj, kc, vct, ksa, vst, kw, vwt, k2m_rows, rel_bias, q_gain)
    return out_proj_residual(o, w_out.astype(BF16), x, gate)


def _hgrn_kernel(q_ref, f_ref, v_ref, g_ref, lb_ref, og_ref, tri_ref, ones_ref,
                 o_ref, st_sc, k_sc, c_sc):
    C, dk, S = HG_CHUNK, HGRN_DK, HG_SUB
    n_sub = C // S

    @pl.when(pl.program_id(2) == 0)
    def _():
        st_sc[...] = jnp.zeros(st_sc.shape, F32)

    row = lax.broadcasted_iota(jnp.int32, (C, C), 0)
    colm = lax.broadcasted_iota(jnp.int32, (C, C), 1)
    diag_keep = ((row // S) == (colm // S)) & ((colm % S) <= (row % S))

    def rows_bcast(ref, hh, first, period):
        return jnp.concatenate(
            [jnp.broadcast_to(ref[hh, pl.ds(g * period + first, 1), :], (period, dk))
             for g in range(C // period)], axis=0)

    heads = range(HG_HEADS)
    sls = [slice(hh * dk, (hh + 1) * dk) for hh in heads]
    qs, ks, cums, vs = [], [], [], []
    for hh in heads:
        fl2 = f_ref[0, :, sls[hh]] * LOG2E
        lb = lb_ref[0, :, sls[hh]]
        log_sig = jnp.minimum(fl2, 0.0) - jnp.log2(1.0 + jnp.exp2(-jnp.abs(fl2)))
        ta = jnp.log2(lb)
        tb = jnp.log2(1.0 - lb) + log_sig
        lf = jnp.maximum(ta, tb) + jnp.log2(1.0 + jnp.exp2(-jnp.abs(ta - tb)))
        k = 1.0 - jnp.exp2(lf)
        cum = _dot_lhs_exact(tri_ref[...], lf)
        k_sc[hh] = k
        c_sc[hh] = cum
        ks.append(k)
        cums.append(cum)
        qs.append(_silu(q_ref[0, :, sls[hh]]))
        vs.append(v_ref[0, :, sls[hh]].astype(BF16))

    attns = []
    for hh in heads:
        pieces = []
        for j in range(S):
            kj = rows_bcast(k_sc, hh, j, S)
            cj = rows_bcast(c_sc, hh, j, S)
            pieces.append((qs[hh] * kj * jnp.exp2(jnp.minimum(cums[hh] - cj, 0.0))).astype(BF16))
        attns.append(jnp.where(diag_keep,
                               _dot(jnp.concatenate(pieces, axis=1), ones_ref[...]), 0.0))

    rloc = lax.broadcasted_iota(jnp.int32, (C, dk), 0)
    m = S
    while m < C:
        upper = (rloc // m) % 2 == 1
        same = (row // (2 * m)) == (colm // (2 * m))
        for hh in heads:
            e = jnp.exp2(-jnp.abs(cums[hh] - rows_bcast(c_sc, hh, m - 1, 2 * m)))
            qm = jnp.where(upper, qs[hh] * e, 0.0).astype(BF16)
            km = jnp.where(upper, 0.0, ks[hh] * e).astype(BF16)
            attns[hh] = attns[hh] + jnp.where(same, _dot_nt(qm, km), 0.0)
        m *= 2

    for hh in heads:
        st = st_sc[hh]
        cum, k, q, v = cums[hh], ks[hh], qs[hh], vs[hh]
        o = _dot(attns[hh].astype(BF16), v)
        o = o + _dot_nt((q * jnp.exp2(cum)).astype(BF16), st.astype(BF16))
        total = cum[C - 1:C, :]
        kd = (k * jnp.exp2(total - cum)).astype(BF16)
        st_sc[hh] = st * jnp.exp2(total) + lax.dot_general(
            v, kd, (((0,), (0,)), ((), ())), preferred_element_type=F32)
        ms = jnp.mean(o * o, axis=-1, keepdims=True)
        o = o * lax.rsqrt(ms + NORM_EPS) * og_ref[...]
        o_ref[0, :, sls[hh]] = (o * _silu(g_ref[0, :, sls[hh]])).astype(o_ref.dtype)


def hgrn_recurrence(proj, lb, out_gain):
    b, t, four_d = proj.shape
    d = four_d // 4
    dk = HGRN_DK
    C, S, hps = HG_CHUNK, HG_SUB, HG_HEADS
    w = hps * dk
    nhp = d // w
    tri = jnp.asarray(np.tril(np.ones((C, C), np.float32)), BF16)
    ones = jnp.asarray(np.arange(S * dk)[:, None] // dk == (np.arange(C)[None, :] % S), BF16)
    return pl.pallas_call(
        _hgrn_kernel,
        grid=(b, nhp, t // C),
        in_specs=[
            pl.BlockSpec((1, C, w), lambda bi, h, c: (bi, c, h)),
            pl.BlockSpec((1, C, w), lambda bi, h, c: (bi, c, nhp + h)),
            pl.BlockSpec((1, C, w), lambda bi, h, c: (bi, c, 2 * nhp + h)),
            pl.BlockSpec((1, C, w), lambda bi, h, c: (bi, c, 3 * nhp + h)),
            pl.BlockSpec((1, 1, w), lambda bi, h, c: (h, 0, 0)),
            pl.BlockSpec((1, dk), lambda bi, h, c: (0, 0)),
            pl.BlockSpec((C, C), lambda bi, h, c: (0, 0)),
            pl.BlockSpec((S * dk, C), lambda bi, h, c: (0, 0)),
        ],
        out_specs=pl.BlockSpec((1, C, w), lambda bi, h, c: (bi, c, h)),
        out_shape=jax.ShapeDtypeStruct((b, t, d), BF16),
        scratch_shapes=[pltpu.VMEM((hps, dk, dk), F32), pltpu.VMEM((hps, C, dk), F32),
                        pltpu.VMEM((hps, C, dk), F32)],
        compiler_params=_cparams("parallel", "parallel", "arbitrary"),
        name="hgrn_recurrence",
    )(proj, proj, proj, proj, lb.reshape(nhp, 1, w), out_gain.reshape(1, dk), tri, ones)


def hgrn_layer(x, shift, scale, gate, norm_g, lb, w_in, out_gain, w_out):
    proj = norm_matmul(x, norm_g, shift, scale, w_in.astype(BF16))
    o = hgrn_recurrence(proj, lb, out_gain)
    return out_proj_residual(o, w_out.astype(BF16), x, gate)


def _router_kernel(x_ref, g_ref, sh_ref, sc_ref, wr_ref, u_ref,
                   h_ref, eid_ref, ew_ref, pos_ref, cnt_ref, run_sc):
    @pl.when((pl.program_id(0) == 0) & (pl.program_id(1) == 0))
    def _():
        run_sc[...] = jnp.zeros(run_sc.shape, F32)

    h = _modulated_norm(x_ref[0], g_ref[...], sh_ref[0], sc_ref[0])
    h_ref[0] = h.astype(BF16)
    lt = _dot_bf16x3(h, wr_ref[...]).T
    NG, EPG = MOE_GROUPS, MOE_EPG

    def softmax_rows(rows):
        mx = functools.reduce(jnp.maximum, rows)
        es = [jnp.exp(r - mx) for r in rows]
        tot = functools.reduce(lambda a, c: a + c, es)
        return [e / tot for e in es]

    def argmax_rows(rows):
        best, idx = rows[0], jnp.zeros(rows[0].shape, jnp.int32)
        for i in range(1, len(rows)):
            better = rows[i] > best
            best = jnp.where(better, rows[i], best)
            idx = jnp.where(better, i, idx)
        return best, idx

    pg = softmax_rows([lt[i:i + 1] for i in range(NG)])
    p_grp, grp = argmax_rows(pg)
    el = []
    for j in range(EPG):
        acc = lt[NG + j:NG + j + 1]
        for gi in range(1, NG):
            acc = jnp.where(grp == gi, lt[NG + gi * EPG + j:NG + gi * EPG + j + 1], acc)
        el.append(acc)
    pe = softmax_rows(el)
    p1, i1 = argmax_rows(pe)
    p2, i2 = argmax_rows([jnp.where(i1 == j, -1.0, pe[j]) for j in range(EPG)])
    den = p1 + p2
    e1 = grp * EPG + i1
    e2 = grp * EPG + i2
    eid_ref[0:1, :] = e1
    eid_ref[1:2, :] = e2
    ew_ref[0:1, :] = p_grp * p1 / den
    ew_ref[1:2, :] = p_grp * p2 / den

    tm = e1.shape[1]
    ex = lax.broadcasted_iota(jnp.int32, (MOE_EXPERTS, tm), 0)
    oh1 = jnp.where(ex == e1, 1.0, 0.0)
    oh2 = jnp.where(ex == e2, 1.0, 0.0)
    before1 = _dot(oh1.astype(BF16), u_ref[...])
    before2 = _dot(oh2.astype(BF16), u_ref[...])
    tot1 = jnp.sum(oh1, axis=1, keepdims=True)
    tot2 = jnp.sum(oh2, axis=1, keepdims=True)
    run = run_sc[...]
    pos1 = jnp.sum(oh1 * (before1 + run), axis=0, keepdims=True)
    pos2 = jnp.sum(oh2 * (before2 + (run + tot1)), axis=0, keepdims=True)
    pos_ref[0:1, :] = pos1.astype(jnp.int32)
    pos_ref[1:2, :] = pos2.astype(jnp.int32)
    run = run + tot1 + tot2
    run_sc[...] = run
    cnt_ref[...] = jnp.broadcast_to(run, cnt_ref.shape)


def moe_router(x, gain, shift, scale, w_group, w_expert, tm=512):
    b, t, d = x.shape
    tm = min(tm, t)
    nt = t // tm
    wr = jnp.concatenate([w_group, w_expert], axis=1)
    wr = jnp.pad(wr, ((0, 0), (0, LANES - wr.shape[1])))
    upper = jnp.asarray(np.triu(np.ones((tm, tm), np.float32), 1), BF16)
    return pl.pallas_call(
        _router_kernel,
        grid=(b, nt),
        in_specs=[
            pl.BlockSpec((1, tm, d), lambda bi, i: (bi, i, 0)),
            pl.BlockSpec((1, d), lambda bi, i: (0, 0)),
            pl.BlockSpec((1, 1, d), lambda bi, i: (bi, 0, 0)),
            pl.BlockSpec((1, 1, d), lambda bi, i: (bi, 0, 0)),
            pl.BlockSpec((d, LANES), lambda bi, i: (0, 0)),
            pl.BlockSpec((tm, tm), lambda bi, i: (0, 0)),
        ],
        out_specs=[
            pl.BlockSpec((1, tm, d), lambda bi, i: (bi, i, 0)),
            pl.BlockSpec((MOE_TOP_K, tm), lambda bi, i: (0, bi * nt + i)),
            pl.BlockSpec((MOE_TOP_K, tm), lambda bi, i: (0, bi * nt + i)),
            pl.BlockSpec((MOE_TOP_K, tm), lambda bi, i: (0, bi * nt + i)),
            pl.BlockSpec((MOE_EXPERTS, LANES), lambda bi, i: (0, 0)),
        ],
        out_shape=[
            jax.ShapeDtypeStruct((b, t, d), BF16),
            jax.ShapeDtypeStruct((MOE_TOP_K, b * t), jnp.int32),
            jax.ShapeDtypeStruct((MOE_TOP_K, b * t), F32),
            jax.ShapeDtypeStruct((MOE_TOP_K, b * t), jnp.int32),
            jax.ShapeDtypeStruct((MOE_EXPERTS, LANES), F32),
        ],
        scratch_shapes=[pltpu.VMEM((MOE_EXPERTS, 1), F32)],
        compiler_params=_cparams("arbitrary", "arbitrary"),
        name="moe_router",
    )(x, gain.reshape(1, d), shift.reshape(b, 1, d), scale.reshape(b, 1, d), wr, upper)


def _expert_ffn_kernel(be_ref, nu_ref, xb_ref, w1_ref, w3_ref, w2_ref, yb_ref,
                       w1_sc, w3_sc, w2_sc):
    i = pl.program_id(0)

    @pl.when((i == 0) | (be_ref[i] != be_ref[jnp.maximum(i - 1, 0)]))
    def _():
        w1_sc[...] = w1_ref[0].astype(BF16)
        w3_sc[...] = w3_ref[0].astype(BF16)
        w2_sc[...] = w2_ref[0].astype(BF16)

    @pl.when(i < nu_ref[0])
    def _():
        xb = xb_ref[...]
        a = _dot(xb, w1_sc[...])
        g = _dot(xb, w3_sc[...])
        yb_ref[...] = _dot((_silu(a) * g).astype(BF16), w2_sc[...]).astype(yb_ref.dtype)

    @pl.when(i >= nu_ref[0])
    def _():
        yb_ref[...] = jnp.zeros(yb_ref.shape, yb_ref.dtype)


def expert_ffn(xb, blk_e, n_used, w1, w3, w2):
    p, d = xb.shape
    ff = w1.shape[2]
    rb = MOE_ROW_BLOCK
    grid_spec = pltpu.PrefetchScalarGridSpec(
        num_scalar_prefetch=2,
        grid=(p // rb,),
        in_specs=[
            pl.BlockSpec((rb, d), lambda i, be, nu: (i, 0)),
            pl.BlockSpec((1, d, ff), lambda i, be, nu: (be[i], 0, 0)),
            pl.BlockSpec((1, d, ff), lambda i, be, nu: (be[i], 0, 0)),
            pl.BlockSpec((1, ff, d), lambda i, be, nu: (be[i], 0, 0)),
        ],
        out_specs=pl.BlockSpec((rb, d), lambda i, be, nu: (i, 0)),
        scratch_shapes=[pltpu.VMEM((d, ff), BF16), pltpu.VMEM((d, ff), BF16),
                        pltpu.VMEM((ff, d), BF16)],
    )
    return pl.pallas_call(
        _expert_ffn_kernel,
        grid_spec=grid_spec,
        out_shape=jax.ShapeDtypeStruct((p, d), BF16),
        compiler_params=_cparams("arbitrary"),
        name="expert_ffn",
    )(blk_e, n_used, xb, w1, w3, w2)


def _combine_kernel(x_ref, gt_ref, y1_ref, y2_ref, w_ref, o_ref):
    w = w_ref[0]
    y = w[:, 0:1] * y1_ref[0].astype(F32) + w[:, 1:2] * y2_ref[0].astype(F32)
    o_ref[0] = x_ref[0] + gt_ref[0] * y


def moe_combine(x, gate, y1, y2, w, tm=512):
    b, t, d = x.shape
    tm = min(tm, t)
    spec = pl.BlockSpec((1, tm, d), lambda bi, i: (bi, i, 0))
    return pl.pallas_call(
        _combine_kernel,
        grid=(b, t // tm),
        in_specs=[spec, pl.BlockSpec((1, 1, d), lambda bi, i: (bi, 0, 0)), spec, spec,
                  pl.BlockSpec((1, tm, MOE_TOP_K), lambda bi, i: (bi, i, 0))],
        out_specs=spec,
        out_shape=jax.ShapeDtypeStruct((b, t, d), F32),
        compiler_params=_cparams("parallel", "parallel"),
        name="moe_combine",
    )(x, gate.reshape(b, 1, d), y1, y2, w)


def moe_layer(x, shift, scale, gate, norm_g, w_group, w_expert, w1, w3, w2):
    b, t, d = x.shape
    n = b * t
    a = n * MOE_TOP_K
    rb = MOE_ROW_BLOCK
    h, eid, ew, pos, cnt = moe_router(x, norm_g, shift, scale, w_group, w_expert)
    counts = cnt[:, 0].astype(jnp.int32)
    padded = (counts + rb - 1) // rb * rb
    pad_end = jnp.cumsum(padded)
    pad_start = pad_end - padded
    is_e = eid[..., None] == jnp.arange(MOE_EXPERTS, dtype=jnp.int32)
    dest = jnp.sum(jnp.where(is_e, pad_start, 0), axis=-1) + pos
    n_blk = -(-(a + MOE_EXPERTS * (rb - 1)) // rb)
    p = n_blk * rb
    blk_start = jnp.arange(n_blk, dtype=jnp.int32) * rb
    blk_e = jnp.minimum(jnp.sum((pad_end[None, :] <= blk_start[:, None]).astype(jnp.int32), axis=1),
                        MOE_EXPERTS - 1)
    n_used = (pad_end[-1:] // rb).astype(jnp.int32)
    tok = jnp.tile(jnp.arange(n, dtype=jnp.int32), MOE_TOP_K)
    _, tok_by_slot = lax.sort_key_val(dest.reshape(a), tok)
    seg_shift = pad_start - (jnp.cumsum(counts) - counts)
    slot = jnp.arange(p, dtype=jnp.int32)
    e_of_slot = jnp.repeat(blk_e, rb)
    shift = jnp.sum(jnp.where(e_of_slot[:, None] == jnp.arange(MOE_EXPERTS, dtype=jnp.int32),
                              seg_shift, 0), axis=-1)
    idx = slot - shift
    buf_t = tok_by_slot[jnp.where(idx < a, idx, slot % a)]
    xb = h.reshape(n, d)[buf_t]
    yb = expert_ffn(xb, blk_e, n_used, w1, w3, w2)
    y1 = yb[dest[0]].reshape(b, t, d)
    y2 = yb[dest[1]].reshape(b, t, d)
    return moe_combine(x, gate, y1, y2, ew.T.reshape(b, t, MOE_TOP_K))


def kernel(x, c, ada_w, ada_b, norm_g, rel_bias, nsa_w_in, nsa_q_gain, nsa_k_gain, nsa_cmp_pe,
           nsa_cmp_w1, nsa_cmp_w2, nsa_w_out, hgrn_w_in, hgrn_lower_bounds, hgrn_out_gain,
           hgrn_w_out, moe_router_group, moe_router_expert, moe_w1, moe_w3, moe_w2):
    depth = ada_w.shape[0]
    d = x.shape[-1]
    lb_soft = jax.nn.softmax(hgrn_lower_bounds.astype(F32), axis=0)
    lb_all = jnp.cumsum(lb_soft, axis=0) - lb_soft[0]
    mod = adaln_mod(c, ada_w, ada_b)
    for layer in range(depth):
        j = layer // 2
        shift, scale, gate = (mod[layer, 0, :, i * d:(i + 1) * d] for i in range(3))
        if layer % 2 == 0:
            x = nsa_layer(x, shift, scale, gate, norm_g[layer, 0], rel_bias, nsa_w_in[j],
                          nsa_q_gain[j], nsa_k_gain[j], nsa_cmp_pe[j], nsa_cmp_w1[j],
                          nsa_cmp_w2[j], nsa_w_out[j])
        else:
            x = hgrn_layer(x, shift, scale, gate, norm_g[layer, 0], lb_all[layer],
                           hgrn_w_in[j], hgrn_out_gain[j], hgrn_w_out[j])
        shift, scale, gate = (mod[layer, 1, :, i * d:(i + 1) * d] for i in range(3))
        x = moe_layer(x, shift, scale, gate, norm_g[layer, 1], moe_router_group[layer],
                      moe_router_expert[layer], moe_w1[layer], moe_w3[layer], moe_w2[layer])
    return x
```

```python
import functools
import math

import numpy as np
import jax
import jax.numpy as jnp
from jax import lax
from jax.experimental import pallas as pl
from jax.experimental.pallas import tpu as pltpu

F32 = jnp.float32
BF16 = jnp.bfloat16
HIGHEST = lax.Precision.HIGHEST

NSA_HEADS = 16
NSA_KV_GROUPS = 4
NSA_GROUP_SIZE = NSA_HEADS // NSA_KV_GROUPS
NSA_HEAD_DIM = 64
CMP_BLOCK = 32
CMP_STRIDE = 16
SEL_BLOCK = 64
N_SELECT = 8
WINDOW = 512
FORCED_SCORE = 1.0e4
REL_BUCKETS = 32
REL_MAX_DISTANCE = 1024
HGRN_DK = 128
MOE_GROUPS = 4
MOE_EPG = 4
MOE_EXPERTS = MOE_GROUPS * MOE_EPG
MOE_TOP_K = 2
MOE_ROW_BLOCK = 512
NORM_EPS = 1e-6

LANES = 128
NEG = -1.0e30
LOG2E = math.log2(math.e)
VMEM_LIMIT = 48 * 1024 * 1024

ATT_TQ = 256
ATT_TK = 256
SHIFT_ROWS = 16
SAFE_GAP = 60.0
BOUND_SLACK = 1.01
DENOM_ROWS = 16
HG_CHUNK = 128
HG_SUB = 8
HG_HEADS = 4


def _cparams(*sem):
    return pltpu.CompilerParams(dimension_semantics=sem, vmem_limit_bytes=VMEM_LIMIT)


def _dot(a, b):
    return jnp.dot(a, b, preferred_element_type=F32)


def _dot_exact(a, b):
    return jnp.dot(a, b, preferred_element_type=F32, precision=HIGHEST)


def _dot_lhs_exact(a, x):
    x1 = x.astype(BF16)
    r1 = x - x1.astype(F32)
    x2 = r1.astype(BF16)
    x3 = (r1 - x2.astype(F32)).astype(BF16)
    return _dot(a, x1) + _dot(a, x2) + _dot(a, x3)


def _dot_bf16x3(a, b):
    a1 = a.astype(BF16)
    a2 = (a - a1.astype(F32)).astype(BF16)
    b1 = b.astype(BF16)
    b2 = (b - b1.astype(F32)).astype(BF16)
    return _dot(a1, b1) + _dot(a1, b2) + _dot(a2, b1)


def _dot_nt(a, b):
    return lax.dot_general(a, b, (((1,), (1,)), ((), ())), preferred_element_type=F32)


def _sigmoid(x):
    return 0.5 * jnp.tanh(0.5 * x) + 0.5


def _silu(x):
    return x * _sigmoid(x)


def _adaln_kernel(c_ref, w_ref, b_ref, o_ref):
    cond = _silu(c_ref[...])
    o_ref[0] = _dot_exact(cond, w_ref[0]) + b_ref[0]


def adaln_mod(c, ada_w, ada_b):
    depth, two, d, d3 = ada_w.shape
    b = c.shape[0]
    ls = depth * two
    tn = 1024
    out = pl.pallas_call(
        _adaln_kernel,
        grid=(ls, d3 // tn),
        in_specs=[
            pl.BlockSpec((b, d), lambda i, j: (0, 0)),
            pl.BlockSpec((1, d, tn), lambda i, j: (i, 0, j)),
            pl.BlockSpec((1, 1, tn), lambda i, j: (i, 0, j)),
        ],
        out_specs=pl.BlockSpec((1, b, tn), lambda i, j: (i, 0, j)),
        out_shape=jax.ShapeDtypeStruct((ls, b, d3), F32),
        compiler_params=_cparams("parallel", "parallel"),
        name="adaln_mod",
    )(c, ada_w.reshape(ls, d, d3), ada_b.reshape(ls, 1, d3))
    return out.reshape(depth, two, b, d3)


def _modulated_norm(x, gain, shift, scale):
    ms = jnp.mean(x * x, axis=-1, keepdims=True)
    y = x * lax.rsqrt(ms + NORM_EPS) * gain
    return y * (1.0 + scale) + shift


def _norm_matmul_kernel(x_ref, g_ref, sh_ref, sc_ref, w_ref, o_ref):
    h = _modulated_norm(x_ref[0], g_ref[...], sh_ref[0], sc_ref[0])
    o_ref[0] = _dot(h.astype(BF16), w_ref[...])


def norm_matmul(x, gain, shift, scale, w, tm=512):
    b, t, d = x.shape
    n = w.shape[1]
    tm = min(tm, t)
    return pl.pallas_call(
        _norm_matmul_kernel,
        grid=(b, t // tm),
        in_specs=[
            pl.BlockSpec((1, tm, d), lambda bi, i: (bi, i, 0)),
            pl.BlockSpec((1, d), lambda bi, i: (0, 0)),
            pl.BlockSpec((1, 1, d), lambda bi, i: (bi, 0, 0)),
            pl.BlockSpec((1, 1, d), lambda bi, i: (bi, 0, 0)),
            pl.BlockSpec((d, n), lambda bi, i: (0, 0)),
        ],
        out_specs=pl.BlockSpec((1, tm, n), lambda bi, i: (bi, i, 0)),
        out_shape=jax.ShapeDtypeStruct((b, t, n), F32),
        compiler_params=_cparams("parallel", "parallel"),
        name="norm_matmul",
    )(x, gain.reshape(1, d), shift.reshape(b, 1, d), scale.reshape(b, 1, d), w)


def _pair_rms(x, gain2):
    lane = lax.broadcasted_iota(jnp.int32, x.shape, 1)
    lo = lane < NSA_HEAD_DIM
    x2 = x * x
    s_lo = jnp.sum(jnp.where(lo, x2, 0.0), axis=-1, keepdims=True)
    s_hi = jnp.sum(jnp.where(lo, 0.0, x2), axis=-1, keepdims=True)
    inv = jnp.where(lo, lax.rsqrt(s_lo / NSA_HEAD_DIM + NORM_EPS),
                    lax.rsqrt(s_hi / NSA_HEAD_DIM + NORM_EPS))
    return x * inv * gain2


def _nsa_prep_kernel(kc0_ref, kc1_ref, vc0_ref, vc1_ref, ks_ref, vs_ref, kw_ref, vw_ref,
                     kg_ref, pe_ref, w1_ref, w1big_ref, w2big_ref, ex_ref,
                     kco_ref, vct_ref, kso_ref, vst_ref, kwo_ref, vwt_ref, k2m_ref):
    t = ks_ref.shape[1]
    nblk = t // CMP_STRIDE
    dh = NSA_HEAD_DIM
    G = NSA_KV_GROUPS

    lane = lax.broadcasted_iota(jnp.int32, (t, LANES), 1)
    ones_cols = jnp.ones((t, SHIFT_ROWS), BF16)
    for br, (src, dst, gi, off) in enumerate(((ks_ref, kso_ref, 1, LANES), (kw_ref, kwo_ref, 2, 0))):
        g2 = kg_ref[gi:gi + 1, :]
        for gp in range(G // 2):
            kn = _pair_rms(src[0, :, gp * LANES:(gp + 1) * LANES], g2).astype(BF16)
            dst[0, 2 * gp, :, off:off + dh] = kn[:, :dh]
            dst[0, 2 * gp + 1, :, off:off + dh] = kn[:, dh:]
            k2 = kn.astype(F32) ** 2
            for half, keep in enumerate((lane < dh, lane >= dh)):
                n2 = jnp.sum(jnp.where(keep, k2, 0.0), axis=-1, keepdims=True)
                row = 2 * (2 * gp + half) + br
                k2m_ref[0, row:row + 1, :] = jnp.broadcast_to(
                    jnp.max(n2, axis=0, keepdims=True), (1, LANES))
        for g in range(G):
            dst[0, g, :, off + dh:off + dh + SHIFT_ROWS] = ones_cols
    for g in range(G):
        kso_ref[0, g, :, :LANES] = ex_ref[...]
    for src, dst in ((vs_ref, vst_ref), (vw_ref, vwt_ref)):
        for gp in range(G // 2):
            vt = src[0, :, gp * LANES:(gp + 1) * LANES].T.astype(BF16)
            dst[0, 2 * gp] = vt[:dh]
            dst[0, 2 * gp + 1] = vt[dh:]

    for ci, (srcs, is_key) in enumerate((((kc0_ref, kc1_ref), True), ((vc0_ref, vc1_ref), False))):
        pe1 = _dot(pe_ref[ci].astype(BF16), w1_ref[ci])[0:1]
        pe2 = jnp.concatenate([pe1, pe1], axis=1)
        for gp, src in enumerate(srcs):
            parts = [src[0, pl.ds(l, nblk, stride=CMP_STRIDE), :] for l in range(CMP_STRIDE)]
            r = jnp.concatenate(parts, axis=1).astype(BF16)
            ab = _dot(r, w1big_ref[ci])
            second = ab[:, LANES:]
            shifted = jnp.concatenate([second[1:], jnp.zeros((1, LANES), F32)], axis=0)
            pre = ab[:, :LANES] + shifted + pe2
            hid = _dot(_silu(pre).astype(BF16), w2big_ref[ci])
            if is_key:
                kn = _pair_rms(hid, kg_ref[0:1, :]).astype(BF16)
                kco_ref[0, 2 * gp] = kn[:, :dh]
                kco_ref[0, 2 * gp + 1] = kn[:, dh:]
            else:
                vt = hid.T.astype(BF16)
                vct_ref[0, 2 * gp] = vt[:dh]
                vct_ref[0, 2 * gp + 1] = vt[dh:]


def nsa_prep(proj, k_gain, cmp_pe, cmp_w1, cmp_w2):
    b, t, _ = proj.shape
    G, dh = NSA_KV_GROUPS, NSA_HEAD_DIM
    kvw = G * dh
    q_blocks = (NSA_HEADS * dh) // kvw
    nblk = t // CMP_STRIDE
    kg2 = jnp.concatenate([k_gain, k_gain], axis=1)
    pe_flat = jnp.broadcast_to(cmp_pe.reshape(2, 1, CMP_BLOCK * dh), (2, 8, CMP_BLOCK * dh))

    w1r = cmp_w1.reshape(2, 2, CMP_STRIDE, dh, dh)
    eye2 = jnp.eye(2, dtype=cmp_w1.dtype)
    w1big = jnp.einsum('chlde,gk->clgdhke', w1r, eye2).reshape(2, CMP_STRIDE * 2 * dh, 4 * dh)
    w2big = jnp.einsum('cde,gk->cgdke', cmp_w2, eye2).reshape(2, 2 * dh, 2 * dh)
    expand = jnp.asarray((np.arange(t)[:, None] // SEL_BLOCK) == np.arange(LANES)[None, :], BF16)

    def col(i):
        return pl.BlockSpec((1, t, kvw), lambda bi, i=i: (bi, 0, q_blocks + i))

    def col_pair(i, gp):
        return pl.BlockSpec((1, t, LANES), lambda bi: (bi, 0, (q_blocks + i) * (kvw // LANES) + gp))

    def full(shape):
        return pl.BlockSpec(shape, lambda bi: (0,) * len(shape))

    def per_b(shape):
        return pl.BlockSpec((1,) + shape, lambda bi: (bi,) + (0,) * len(shape))

    return pl.pallas_call(
        _nsa_prep_kernel,
        grid=(b,),
        in_specs=[col_pair(0, 0), col_pair(0, 1), col_pair(1, 0), col_pair(1, 1),
                  col(2), col(3), col(4), col(5),
                  full((3, 2 * dh)), full((2, 8, CMP_BLOCK * dh)),
                  full((2, CMP_BLOCK * dh, dh)), full((2, 2 * CMP_STRIDE * dh, 4 * dh)),
                  full((2, 2 * dh, 2 * dh)), full((t, LANES))],
        out_specs=[per_b((G, nblk, dh)), per_b((G, dh, nblk)),
                   per_b((G, t, LANES + dh + SHIFT_ROWS)), per_b((G, dh, t)),
                   per_b((G, t, dh + SHIFT_ROWS)), per_b((G, dh, t)),
                   per_b((2 * G, LANES))],
        out_shape=[jax.ShapeDtypeStruct((b, G, nblk, dh), BF16),
                   jax.ShapeDtypeStruct((b, G, dh, nblk), BF16),
                   jax.ShapeDtypeStruct((b, G, t, LANES + dh + SHIFT_ROWS), BF16),
                   jax.ShapeDtypeStruct((b, G, dh, t), BF16),
                   jax.ShapeDtypeStruct((b, G, t, dh + SHIFT_ROWS), BF16),
                   jax.ShapeDtypeStruct((b, G, dh, t), BF16),
                   jax.ShapeDtypeStruct((b, 2 * G, LANES), F32)],
        compiler_params=_cparams("parallel"),
        name="nsa_prep",
    )(proj, proj, proj, proj, proj, proj, proj, proj, kg2, pe_flat,
      cmp_w1.astype(BF16), w1big.astype(BF16), w2big.astype(BF16), expand)


def _nsa_attn_kernel(k2m_ref, hb_ref, q_ref, gl_ref, pgt_ref, qg_ref, kc_ref, vct_ref,
                     ksa_ref, vst_ref, kw_ref, vwt_ref, bc_ref, bs_ref, bw_ref, c2st_ref,
                     o_ref, acc_sc, qsel_sc, qwin_sc, pc_sc, ssa_sc, ssb_sc, swa_sc, swb_sc,
                     score_sc, *,
                     e_sat, n_wtiles,
                     n_sel_blocks, n_select):
    tq, tk = ATT_TQ, ATT_TK
    R, dh = NSA_GROUP_SIZE, NSA_HEAD_DIM
    g_id, b_id, qi = pl.program_id(0), pl.program_id(1), pl.program_id(2)
    scale = dh ** -0.5 * LOG2E
    qoff = LANES

    gl = gl_ref[0]
    g1 = gl.astype(BF16)
    gr = gl - g1.astype(F32)
    g2 = gr.astype(BF16)
    g3 = (gr - g2.astype(F32)).astype(BF16)
    gates = _sigmoid(_dot_nt(pgt_ref[0], g1) + _dot_nt(pgt_ref[0], g2) + _dot_nt(pgt_ref[0], g3))

    qt = q_ref[0].T
    q2 = []
    for r in range(R):
        qr = qt[r * dh:(r + 1) * dh]
        ms = jnp.mean(qr * qr, axis=0, keepdims=True)
        qn = (qr * lax.rsqrt(ms + NORM_EPS) * qg_ref[...] * scale).astype(BF16)
        qwin_sc[:dh, r * tq:(r + 1) * tq] = qn
        qsel_sc[qoff:qoff + dh, r * tq:(r + 1) * tq] = qn
        q2.append(jnp.sum(qn.astype(F32) ** 2, axis=0, keepdims=True))

    k2_sel = k2m_ref[b_id, 2 * g_id]
    k2_win = k2m_ref[b_id, 2 * g_id + 1]
    q2_max = jnp.max(functools.reduce(jnp.maximum, q2))
    thr = hb_ref[NSA_HEADS]
    fast = (2 * BOUND_SLACK) ** 2 * q2_max * jnp.maximum(k2_sel, k2_win) <= thr * thr
    shift_pad = jnp.zeros((SHIFT_ROWS - 1, tq), F32)
    for r in range(R):
        bmax = hb_ref[g_id * R + r]
        for k2, q_sc, row0 in ((k2_sel, qsel_sc, qoff + dh), (k2_win, qwin_sc, dh)):
            bound = jnp.sqrt(q2[r] * k2) * BOUND_SLACK + bmax
            rows = jnp.concatenate([jnp.where(fast, -bound, 0.0), shift_pad], axis=0)
            q_sc[row0:row0 + SHIFT_ROWS, r * tq:(r + 1) * tq] = rows.astype(BF16)

    ones_rows = jnp.ones((DENOM_ROWS, tk), BF16)
    rr = tq // tk
    n_tiles = rr * qi + rr

    def make_sweep(k_ref, vt_ref, q_sc, b_ref, hi, e_last, sa_sc, sb_sc):
        def tile_col(d):
            dc = jnp.minimum(d, hi - 1)
            e = jnp.where(dc == 0, rr - 1, jnp.where(dc < rr, dc - 1, dc))
            return pl.multiple_of((n_tiles - 1 - e) * tk, tk), jnp.minimum(e, e_last)

        def logits(d, dst_sc):
            col, _ = tile_col(d)
            dst_sc[...] = _dot(k_ref[0, 0, pl.ds(col, tk), :], q_sc[...])

        def softmax_pv(d, src_sc, ms):
            col, bi = tile_col(d)
            vaug = jnp.concatenate([vt_ref[0, 0, :, pl.ds(col, tk)], ones_rows], axis=0)
            new_ms = []
            for r in range(R):
                s = src_sc[:, r * tq:(r + 1) * tq] + b_ref[r, bi]
                m_new = jnp.maximum(ms[r], jnp.max(s, axis=0, keepdims=True))
                alpha = jnp.exp2(ms[r] - m_new)
                p = jnp.exp2(s - m_new).astype(BF16)
                acc_sc[r] = acc_sc[r] * alpha + _dot(vaug, p)
                new_ms.append(m_new)
            return tuple(new_ms)

        def shifted_pv(d, src_sc):
            col, bi = tile_col(d)
            vaug = jnp.concatenate([vt_ref[0, 0, :, pl.ds(col, tk)], ones_rows], axis=0)
            for r in range(R):
                p = jnp.exp2(src_sc[:, r * tq:(r + 1) * tq] + b_ref[r, bi]).astype(BF16)
                acc_sc[r] += _dot(vaug, p)

        def body(dp, ms):
            d = 2 * dp
            logits(d + 1, sb_sc)
            ms = softmax_pv(d, sa_sc, ms)
            logits(d + 2, sa_sc)
            return softmax_pv(d + 1, sb_sc, ms)

        def shifted_body(dp, carry):
            d = 2 * dp
            logits(d + 1, sb_sc)
            shifted_pv(d, sa_sc)
            logits(d + 2, sa_sc)
            shifted_pv(d + 1, sb_sc)
            return carry

        def start():
            logits(0, sa_sc)

        def run():
            acc_sc[...] = jnp.zeros(acc_sc.shape, F32)

            @pl.when(fast)
            def _():
                lax.fori_loop(0, hi // 2, shifted_body, 0)

                @pl.when(hi % 2 == 1)
                def _():
                    shifted_pv(hi - 1, sa_sc)

            @pl.when(jnp.logical_not(fast))
            def _():
                ms = lax.fori_loop(0, hi // 2, body,
                                   tuple(jnp.full((1, tq), NEG, F32) for _ in range(R)))

                @pl.when(hi % 2 == 1)
                def _():
                    softmax_pv(hi - 1, sa_sc, ms)

            outs = []
            for r in range(R):
                acc = acc_sc[r]
                outs.append(acc[:dh] * (1.0 / acc[dh:dh + 1]))
            return outs

        return start, run

    win_start, win_run = make_sweep(kw_ref, vwt_ref, qwin_sc, bw_ref,
                                    jnp.minimum(n_tiles, n_wtiles), n_wtiles - 1, swa_sc, swb_sc)
    win_start()

    ncp = kc_ref.shape[2]
    n_idx = lax.broadcasted_iota(jnp.int32, (ncp, tq), 0)
    t_idx = qi * tq + lax.broadcasted_iota(jnp.int32, (ncp, tq), 1)
    vis = (n_idx * CMP_STRIDE + (CMP_BLOCK - 1)) <= t_idx
    psum = jnp.zeros((ncp, tq), F32)
    sc_all = _dot(kc_ref[0, 0], qwin_sc[:dh])
    for r in range(R):
        s = jnp.where(vis, sc_all[:, r * tq:(r + 1) * tq] + bc_ref[r], NEG)
        m = jnp.max(s, axis=0, keepdims=True)
        e = jnp.where(vis, jnp.exp2(s - m), 0.0)
        p = e * (1.0 / jnp.maximum(jnp.sum(e, axis=0, keepdims=True), 1e-30))
        psum = psum + p
        pc_sc[:, r * tq:(r + 1) * tq] = p.astype(BF16)
    oc_all = _dot(vct_ref[0, 0], pc_sc[...])
    o_cmp = [oc_all[:, r * tq:(r + 1) * tq] for r in range(R)]

    imp_t = _dot_lhs_exact(c2st_ref[...], psum)
    blk = lax.broadcasted_iota(jnp.int32, (n_sel_blocks, tq), 0)
    tpos = qi * tq + lax.broadcasted_iota(jnp.int32, (n_sel_blocks, tq), 1)
    cur = tpos // SEL_BLOCK
    forced = (blk == 0) | (blk == cur) | (blk == cur - 1)
    score = jnp.where(forced, FORCED_SCORE, jnp.where(blk <= cur, imp_t, -1.0))
    score_sc[...] = score
    per_tile = tq // SEL_BLOCK

    def rank_body(g, rank):
        for u in range(per_tile):
            s2 = g * per_tile + u
            row = score_sc[pl.ds(s2, 1), :]
            beats = (row > score) | ((row == score) & (blk > s2))
            rank = rank + jnp.where(beats, 1.0, 0.0)
        return rank

    rank = lax.fori_loop(0, qi + 1, rank_body, jnp.zeros((n_sel_blocks, tq), F32))
    negsel = jnp.where(rank < n_select, 0.0, NEG)
    if n_sel_blocks < LANES:
        negsel = jnp.concatenate([negsel, jnp.zeros((LANES - n_sel_blocks, tq), F32)], axis=0)
    negsel = negsel.astype(BF16)
    for r in range(R):
        qsel_sc[:qoff, r * tq:(r + 1) * tq] = negsel

    sel_start, sel_run = make_sweep(ksa_ref, vst_ref, qsel_sc, bs_ref, n_tiles, e_sat,
                                    ssa_sc, ssb_sc)
    sel_start()
    o_win = win_run()
    o_sel = sel_run()

    outs = []
    for r in range(R):
        outs.append(gates[3 * r:3 * r + 1] * o_cmp[r]
                    + gates[3 * r + 1:3 * r + 2] * o_sel[r]
                    + gates[3 * r + 2:3 * r + 3] * o_win[r])
    o_ref[0] = jnp.concatenate(outs, axis=0).T.astype(o_ref.dtype)


def _t5_bucket(dist):
    n = jnp.maximum(dist, 0)
    max_exact = REL_BUCKETS // 2
    nf = jnp.maximum(n, 1).astype(F32)
    large = max_exact + (jnp.log(nf / max_exact) / math.log(REL_MAX_DISTANCE / max_exact)
                         * (REL_BUCKETS - max_exact)).astype(jnp.int32)
    large = jnp.minimum(large, REL_BUCKETS - 1)
    return jnp.where(n < max_exact, n, large)


def _bias_of_dist(dist, rel_bias):
    onehot = (_t5_bucket(dist)[..., None] == jnp.arange(REL_BUCKETS)).astype(F32)
    out = jnp.einsum('...k,kh->...h', onehot, rel_bias.astype(F32), precision=HIGHEST)
    return jnp.moveaxis(out, -1, 0)


def _saturation_distance():
    max_exact = REL_BUCKETS // 2
    steps = REL_BUCKETS - max_exact
    n_sat = max_exact * (REL_MAX_DISTANCE / max_exact) ** ((steps - 1) / steps)
    return int(math.ceil(n_sat)) + 2


def nsa_attention(proj, kc, vct, ksa, vst, kw, vwt, k2m_rows, rel_bias, q_gain):
    b, t, _ = proj.shape
    G, R, dh = NSA_KV_GROUPS, NSA_GROUP_SIZE, NSA_HEAD_DIM
    H = NSA_HEADS
    tq, tk = ATT_TQ, ATT_TK
    rr = tq // tk
    ncp = kc.shape[2]
    n_sel_blocks = t // SEL_BLOCK
    n_select = min(N_SELECT, n_sel_blocks)
    assert n_sel_blocks <= LANES and n_sel_blocks % 8 == 0 and tq % tk == 0 and t % tq == 0

    e_sat = -(-(_saturation_distance() + tk - 1) // tk) + rr - 1
    n_wtiles = -(-(WINDOW + tk - 1) // tk) + rr - 1
    jj = np.arange(tk)[:, None]
    ii = np.arange(tq)[None, :]

    def tile_dist(n_e):
        return (np.arange(n_e)[:, None, None] - (rr - 1)) * tk + (ii - jj)[None]

    dist = tile_dist(e_sat + 1)
    bias_sel = jnp.where(dist >= 0, _bias_of_dist(jnp.asarray(dist), rel_bias) * LOG2E, NEG)
    dwin = tile_dist(n_wtiles)
    bias_win = jnp.where((dwin >= 0) & (dwin < WINDOW),
                         _bias_of_dist(jnp.asarray(dwin), rel_bias) * LOG2E, NEG)
    dc = np.arange(t)[None, :] - (np.arange(ncp)[:, None] * CMP_STRIDE + CMP_BLOCK - 1)
    bias_c = _bias_of_dist(jnp.asarray(dc), rel_bias) * LOG2E

    cs = np.arange(ncp) * CMP_STRIDE
    ss = np.arange(n_sel_blocks) * SEL_BLOCK
    shared = (np.minimum(cs[None, :] + CMP_BLOCK, ss[:, None] + SEL_BLOCK)
              - np.maximum(cs[None, :], ss[:, None]))
    c2st = jnp.asarray(np.clip(shared, 0, None) / CMP_BLOCK, BF16)
    pgt = np.zeros((G, LANES, LANES), np.float32)
    for g in range(G):
        for k in range(3 * R):
            pgt[g, k, 3 * R * g + k] = 1.0
    pgt = jnp.asarray(pgt, BF16)
    gate_blk = (H * dh + 6 * G * dh) // LANES
    qg = jnp.broadcast_to(q_gain.reshape(dh, 1), (dh, tq))

    rb2 = rel_bias.astype(F32) * LOG2E
    spread = jnp.max(jnp.max(rb2, axis=0) - jnp.min(rb2, axis=0))
    head_bias = jnp.concatenate([jnp.max(rb2, axis=0), jnp.maximum(SAFE_GAP - spread, 0.0)[None]])
    k2m = k2m_rows[:, :, 0]

    kernel = functools.partial(_nsa_attn_kernel, e_sat=e_sat, n_wtiles=n_wtiles,
                               n_sel_blocks=n_sel_blocks, n_select=n_select)
    return pl.pallas_call(
        kernel,
        grid=(G, b, t // tq),
        in_specs=[
            pl.BlockSpec(memory_space=pltpu.SMEM),
            pl.BlockSpec(memory_space=pltpu.SMEM),
            pl.BlockSpec((1, tq, R * dh), lambda g, bi, i: (bi, i, g)),
            pl.BlockSpec((1, tq, LANES), lambda g, bi, i: (bi, i, gate_blk)),
            pl.BlockSpec((1, LANES, LANES), lambda g, bi, i: (g, 0, 0)),
            pl.BlockSpec((dh, tq), lambda g, bi, i: (0, 0)),
            pl.BlockSpec((1, 1, ncp, dh), lambda g, bi, i: (bi, g, 0, 0)),
            pl.BlockSpec((1, 1, dh, ncp), lambda g, bi, i: (bi, g, 0, 0)),
            pl.BlockSpec((1, 1, t, LANES + dh + SHIFT_ROWS), lambda g, bi, i: (bi, g, 0, 0)),
            pl.BlockSpec((1, 1, dh, t), lambda g, bi, i: (bi, g, 0, 0)),
            pl.BlockSpec((1, 1, t, dh + SHIFT_ROWS), lambda g, bi, i: (bi, g, 0, 0)),
            pl.BlockSpec((1, 1, dh, t), lambda g, bi, i: (bi, g, 0, 0)),
            pl.BlockSpec((R, ncp, tq), lambda g, bi, i: (g, 0, i)),
            pl.BlockSpec((R, e_sat + 1, tk, tq), lambda g, bi, i: (g, 0, 0, 0)),
            pl.BlockSpec((R, n_wtiles, tk, tq), lambda g, bi, i: (g, 0, 0, 0)),
            pl.BlockSpec((n_sel_blocks, ncp), lambda g, bi, i: (0, 0)),
        ],
        out_specs=pl.BlockSpec((1, tq, R * dh), lambda g, bi, i: (bi, i, g)),
        out_shape=jax.ShapeDtypeStruct((b, t, H * dh), BF16),
        scratch_shapes=[pltpu.VMEM((R, dh + DENOM_ROWS, tq), F32),
                        pltpu.VMEM((LANES + dh + SHIFT_ROWS, R * tq), BF16),
                        pltpu.VMEM((dh + SHIFT_ROWS, R * tq), BF16),
                        pltpu.VMEM((ncp, R * tq), BF16),
                        pltpu.VMEM((tk, R * tq), F32), pltpu.VMEM((tk, R * tq), F32),
                        pltpu.VMEM((tk, R * tq), F32), pltpu.VMEM((tk, R * tq), F32),
                        pltpu.VMEM((n_sel_blocks, tq), F32)],
        compiler_params=_cparams("parallel", "parallel", "arbitrary"),
        name="nsa_attention",
    )(k2m, head_bias, proj, proj, pgt, qg, kc, vct, ksa, vst, kw, vwt, bias_c, bias_sel, bias_win,
      c2st)


def nsa_mixer(x, shift, scale, norm_g, rel_bias, w_in, q_gain, k_gain, cmp_pe, cmp_w1, cmp_w2):
    d = x.shape[-1]
    n_in = w_in.shape[1]
    n_pad = -(-n_in // LANES) * LANES
    w_in_p = jnp.pad(w_in.astype(BF16), ((0, 0), (0, n_pad - n_in)))
    proj = norm_matmul(x, norm_g, shift, scale, w_in_p)
    kc, vct, ksa, vst, kw, vwt, k2m_rows = nsa_prep(proj, k_gain, cmp_pe, cmp_w1, cmp_w2)
    return nsa_attention(proj, kc, vct, ksa, vst, kw, vwt, k2m_rows, rel_bias, q_gain)


def _hgrn_kernel(q_ref, f_ref, v_ref, g_ref, lb_ref, og_ref, tri_ref, ones_ref,
                 o_ref, st_sc, k_sc, c_sc):
    C, dk, S = HG_CHUNK, HGRN_DK, HG_SUB
    n_sub = C // S

    @pl.when(pl.program_id(2) == 0)
    def _():
        st_sc[...] = jnp.zeros(st_sc.shape, F32)

    row = lax.broadcasted_iota(jnp.int32, (C, C), 0)
    colm = lax.broadcasted_iota(jnp.int32, (C, C), 1)
    diag_keep = ((row // S) == (colm // S)) & ((colm % S) <= (row % S))

    def rows_bcast(ref, hh, first, period):
        return jnp.concatenate(
            [jnp.broadcast_to(ref[hh, pl.ds(g * period + first, 1), :], (period, dk))
             for g in range(C // period)], axis=0)

    heads = range(HG_HEADS)
    sls = [slice(hh * dk, (hh + 1) * dk) for hh in heads]
    qs, ks, cums, vs = [], [], [], []
    for hh in heads:
        fl2 = f_ref[0, :, sls[hh]] * LOG2E
        lb = lb_ref[0, :, sls[hh]]
        log_sig = jnp.minimum(fl2, 0.0) - jnp.log2(1.0 + jnp.exp2(-jnp.abs(fl2)))
        ta = jnp.log2(lb)
        tb = jnp.log2(1.0 - lb) + log_sig
        lf = jnp.maximum(ta, tb) + jnp.log2(1.0 + jnp.exp2(-jnp.abs(ta - tb)))
        k = 1.0 - jnp.exp2(lf)
        cum = _dot_lhs_exact(tri_ref[...], lf)
        k_sc[hh] = k
        c_sc[hh] = cum
        ks.append(k)
        cums.append(cum)
        qs.append(_silu(q_ref[0, :, sls[hh]]))
        vs.append(v_ref[0, :, sls[hh]].astype(BF16))

    attns = []
    for hh in heads:
        pieces = []
        for j in range(S):
            kj = rows_bcast(k_sc, hh, j, S)
            cj = rows_bcast(c_sc, hh, j, S)
            pieces.append((qs[hh] * kj * jnp.exp2(jnp.minimum(cums[hh] - cj, 0.0))).astype(BF16))
        attns.append(jnp.where(diag_keep,
                               _dot(jnp.concatenate(pieces, axis=1), ones_ref[...]), 0.0))

    rloc = lax.broadcasted_iota(jnp.int32, (C, dk), 0)
    m = S
    while m < C:
        upper = (rloc // m) % 2 == 1
        same = (row // (2 * m)) == (colm // (2 * m))
        for hh in heads:
            e = jnp.exp2(-jnp.abs(cums[hh] - rows_bcast(c_sc, hh, m - 1, 2 * m)))
            qm = jnp.where(upper, qs[hh] * e, 0.0).astype(BF16)
            km = jnp.where(upper, 0.0, ks[hh] * e).astype(BF16)
            attns[hh] = attns[hh] + jnp.where(same, _dot_nt(qm, km), 0.0)
        m *= 2

    for hh in heads:
        st = st_sc[hh]
        cum, k, q, v = cums[hh], ks[hh], qs[hh], vs[hh]
        o = _dot(attns[hh].astype(BF16), v)
        o = o + _dot_nt((q * jnp.exp2(cum)).astype(BF16), st.astype(BF16))
        total = cum[C - 1:C, :]
        kd = (k * jnp.exp2(total - cum)).astype(BF16)
        st_sc[hh] = st * jnp.exp2(total) + lax.dot_general(
            v, kd, (((0,), (0,)), ((), ())), preferred_element_type=F32)
        ms = jnp.mean(o * o, axis=-1, keepdims=True)
        o = o * lax.rsqrt(ms + NORM_EPS) * og_ref[...]
        o_ref[0, :, sls[hh]] = (o * _silu(g_ref[0, :, sls[hh]])).astype(o_ref.dtype)


def hgrn_recurrence(proj, lb, out_gain):
    b, t, four_d = proj.shape
    d = four_d // 4
    dk = HGRN_DK
    C, S, hps = HG_CHUNK, HG_SUB, HG_HEADS
    w = hps * dk
    nhp = d // w
    tri = jnp.asarray(np.tril(np.ones((C, C), np.float32)), BF16)
    ones = jnp.asarray(np.arange(S * dk)[:, None] // dk == (np.arange(C)[None, :] % S), BF16)
    return pl.pallas_call(
        _hgrn_kernel,
        grid=(b, nhp, t // C),
        in_specs=[
            pl.BlockSpec((1, C, w), lambda bi, h, c: (bi, c, h)),
            pl.BlockSpec((1, C, w), lambda bi, h, c: (bi, c, nhp + h)),
            pl.BlockSpec((1, C, w), lambda bi, h, c: (bi, c, 2 * nhp + h)),
            pl.BlockSpec((1, C, w), lambda bi, h, c: (bi, c, 3 * nhp + h)),
            pl.BlockSpec((1, 1, w), lambda bi, h, c: (h, 0, 0)),
            pl.BlockSpec((1, dk), lambda bi, h, c: (0, 0)),
            pl.BlockSpec((C, C), lambda bi, h, c: (0, 0)),
            pl.BlockSpec((S * dk, C), lambda bi, h, c: (0, 0)),
        ],
        out_specs=pl.BlockSpec((1, C, w), lambda bi, h, c: (bi, c, h)),
        out_shape=jax.ShapeDtypeStruct((b, t, d), BF16),
        scratch_shapes=[pltpu.VMEM((hps, dk, dk), F32), pltpu.VMEM((hps, C, dk), F32),
                        pltpu.VMEM((hps, C, dk), F32)],
        compiler_params=_cparams("parallel", "parallel", "arbitrary"),
        name="hgrn_recurrence",
    )(proj, proj, proj, proj, lb.reshape(nhp, 1, w), out_gain.reshape(1, dk), tri, ones)


def hgrn_mixer(x, shift, scale, norm_g, lb, w_in, out_gain):
    proj = norm_matmul(x, norm_g, shift, scale, w_in.astype(BF16))
    return hgrn_recurrence(proj, lb, out_gain)


def _out_proj_router_kernel(o_ref, wo_ref, x_ref, gt_ref, g_ref, sh_ref, sc_ref, wr_ref, u_ref,
                            xn_ref, h_ref, eid_ref, ew_ref, pos_ref, cnt_ref, run_sc):
    @pl.when((pl.program_id(0) == 0) & (pl.program_id(1) == 0))
    def _():
        run_sc[...] = jnp.zeros(run_sc.shape, F32)

    xn = x_ref[0] + gt_ref[0] * _dot(o_ref[0], wo_ref[...])
    xn_ref[0] = xn
    h = _modulated_norm(xn, g_ref[...], sh_ref[0], sc_ref[0])
    h_ref[0] = h.astype(BF16)
    lt = _dot_bf16x3(h, wr_ref[...]).T
    NG, EPG = MOE_GROUPS, MOE_EPG

    def softmax_rows(rows):
        mx = functools.reduce(jnp.maximum, rows)
        es = [jnp.exp(r - mx) for r in rows]
        tot = functools.reduce(lambda a, c: a + c, es)
        return [e / tot for e in es]

    def argmax_rows(rows):
        best, idx = rows[0], jnp.zeros(rows[0].shape, jnp.int32)
        for i in range(1, len(rows)):
            better = rows[i] > best
            best = jnp.where(better, rows[i], best)
            idx = jnp.where(better, i, idx)
        return best, idx

    pg = softmax_rows([lt[i:i + 1] for i in range(NG)])
    p_grp, grp = argmax_rows(pg)
    el = []
    for j in range(EPG):
        acc = lt[NG + j:NG + j + 1]
        for gi in range(1, NG):
            acc = jnp.where(grp == gi, lt[NG + gi * EPG + j:NG + gi * EPG + j + 1], acc)
        el.append(acc)
    pe = softmax_rows(el)
    p1, i1 = argmax_rows(pe)
    p2, i2 = argmax_rows([jnp.where(i1 == j, -1.0, pe[j]) for j in range(EPG)])
    den = p1 + p2
    e1 = grp * EPG + i1
    e2 = grp * EPG + i2
    eid_ref[0:1, :] = e1
    eid_ref[1:2, :] = e2
    ew_ref[0:1, :] = p_grp * p1 / den
    ew_ref[1:2, :] = p_grp * p2 / den

    tm = e1.shape[1]
    ex = lax.broadcasted_iota(jnp.int32, (MOE_EXPERTS, tm), 0)
    oh1 = jnp.where(ex == e1, 1.0, 0.0)
    oh2 = jnp.where(ex == e2, 1.0, 0.0)
    before1 = _dot(oh1.astype(BF16), u_ref[...])
    before2 = _dot(oh2.astype(BF16), u_ref[...])
    tot1 = jnp.sum(oh1, axis=1, keepdims=True)
    tot2 = jnp.sum(oh2, axis=1, keepdims=True)
    run = run_sc[...]
    pos1 = jnp.sum(oh1 * (before1 + run), axis=0, keepdims=True)
    pos2 = jnp.sum(oh2 * (before2 + (run + tot1)), axis=0, keepdims=True)
    pos_ref[0:1, :] = pos1.astype(jnp.int32)
    pos_ref[1:2, :] = pos2.astype(jnp.int32)
    run = run + tot1 + tot2
    run_sc[...] = run
    cnt_ref[...] = jnp.broadcast_to(run, cnt_ref.shape)


def out_proj_router(o, w_out, x, gate, gain, shift, scale, w_group, w_expert, tm=512):
    b, t, d = x.shape
    k = o.shape[-1]
    tm = min(tm, t)
    nt = t // tm
    wr = jnp.concatenate([w_group, w_expert], axis=1)
    wr = jnp.pad(wr, ((0, 0), (0, LANES - wr.shape[1])))
    upper = jnp.asarray(np.triu(np.ones((tm, tm), np.float32), 1), BF16)
    return pl.pallas_call(
        _out_proj_router_kernel,
        grid=(b, nt),
        in_specs=[
            pl.BlockSpec((1, tm, k), lambda bi, i: (bi, i, 0)),
            pl.BlockSpec((k, d), lambda bi, i: (0, 0)),
            pl.BlockSpec((1, tm, d), lambda bi, i: (bi, i, 0)),
            pl.BlockSpec((1, 1, d), lambda bi, i: (bi, 0, 0)),
            pl.BlockSpec((1, d), lambda bi, i: (0, 0)),
            pl.BlockSpec((1, 1, d), lambda bi, i: (bi, 0, 0)),
            pl.BlockSpec((1, 1, d), lambda bi, i: (bi, 0, 0)),
            pl.BlockSpec((d, LANES), lambda bi, i: (0, 0)),
            pl.BlockSpec((tm, tm), lambda bi, i: (0, 0)),
        ],
        out_specs=[
            pl.BlockSpec((1, tm, d), lambda bi, i: (bi, i, 0)),
            pl.BlockSpec((1, tm, d), lambda bi, i: (bi, i, 0)),
            pl.BlockSpec((MOE_TOP_K, tm), lambda bi, i: (0, bi * nt + i)),
            pl.BlockSpec((MOE_TOP_K, tm), lambda bi, i: (0, bi * nt + i)),
            pl.BlockSpec((MOE_TOP_K, tm), lambda bi, i: (0, bi * nt + i)),
            pl.BlockSpec((MOE_EXPERTS, LANES), lambda bi, i: (0, 0)),
        ],
        out_shape=[
            jax.ShapeDtypeStruct((b, t, d), F32),
            jax.ShapeDtypeStruct((b, t, d), BF16),
            jax.ShapeDtypeStruct((MOE_TOP_K, b * t), jnp.int32),
            jax.ShapeDtypeStruct((MOE_TOP_K, b * t), F32),
            jax.ShapeDtypeStruct((MOE_TOP_K, b * t), jnp.int32),
            jax.ShapeDtypeStruct((MOE_EXPERTS, LANES), F32),
        ],
        scratch_shapes=[pltpu.VMEM((MOE_EXPERTS, 1), F32)],
        compiler_params=_cparams("arbitrary", "arbitrary"),
        name="out_proj_router",
    )(o, w_out, x, gate.reshape(b, 1, d), gain.reshape(1, d), shift.reshape(b, 1, d),
      scale.reshape(b, 1, d), wr, upper)


def _expert_ffn_kernel(be_ref, nu_ref, xb_ref, w1_ref, w3_ref, w2_ref, yb_ref,
                       w1_sc, w3_sc, w2_sc):
    i = pl.program_id(0)

    @pl.when((i == 0) | (be_ref[i] != be_ref[jnp.maximum(i - 1, 0)]))
    def _():
        w1_sc[...] = w1_ref[0].astype(BF16)
        w3_sc[...] = w3_ref[0].astype(BF16)
        w2_sc[...] = w2_ref[0].astype(BF16)

    @pl.when(i < nu_ref[0])
    def _():
        xb = xb_ref[...]
        a = _dot(xb, w1_sc[...])
        g = _dot(xb, w3_sc[...])
        yb_ref[...] = _dot((_silu(a) * g).astype(BF16), w2_sc[...]).astype(yb_ref.dtype)

    @pl.when(i >= nu_ref[0])
    def _():
        yb_ref[...] = jnp.zeros(yb_ref.shape, yb_ref.dtype)


def expert_ffn(xb, blk_e, n_used, w1, w3, w2):
    p, d = xb.shape
    ff = w1.shape[2]
    rb = MOE_ROW_BLOCK
    grid_spec = pltpu.PrefetchScalarGridSpec(
        num_scalar_prefetch=2,
        grid=(p // rb,),
        in_specs=[
            pl.BlockSpec((rb, d), lambda i, be, nu: (i, 0)),
            pl.BlockSpec((1, d, ff), lambda i, be, nu: (be[i], 0, 0)),
            pl.BlockSpec((1, d, ff), lambda i, be, nu: (be[i], 0, 0)),
            pl.BlockSpec((1, ff, d), lambda i, be, nu: (be[i], 0, 0)),
        ],
        out_specs=pl.BlockSpec((rb, d), lambda i, be, nu: (i, 0)),
        scratch_shapes=[pltpu.VMEM((d, ff), BF16), pltpu.VMEM((d, ff), BF16),
                        pltpu.VMEM((ff, d), BF16)],
    )
    return pl.pallas_call(
        _expert_ffn_kernel,
        grid_spec=grid_spec,
        out_shape=jax.ShapeDtypeStruct((p, d), BF16),
        compiler_params=_cparams("arbitrary"),
        name="expert_ffn",
    )(blk_e, n_used, xb, w1, w3, w2)


def _combine_kernel(x_ref, gt_ref, y1_ref, y2_ref, w_ref, o_ref):
    w = w_ref[0]
    y = w[:, 0:1] * y1_ref[0].astype(F32) + w[:, 1:2] * y2_ref[0].astype(F32)
    o_ref[0] = x_ref[0] + gt_ref[0] * y


def moe_combine(x, gate, y1, y2, w, tm=512):
    b, t, d = x.shape
    tm = min(tm, t)
    spec = pl.BlockSpec((1, tm, d), lambda bi, i: (bi, i, 0))
    return pl.pallas_call(
        _combine_kernel,
        grid=(b, t // tm),
        in_specs=[spec, pl.BlockSpec((1, 1, d), lambda bi, i: (bi, 0, 0)), spec, spec,
                  pl.BlockSpec((1, tm, MOE_TOP_K), lambda bi, i: (bi, i, 0))],
        out_specs=spec,
        out_shape=jax.ShapeDtypeStruct((b, t, d), F32),
        compiler_params=_cparams("parallel", "parallel"),
        name="moe_combine",
    )(x, gate.reshape(b, 1, d), y1, y2, w)


def moe_layer(x, gate, routed, w1, w3, w2):
    b, t, d = x.shape
    n = b * t
    a = n * MOE_TOP_K
    rb = MOE_ROW_BLOCK
    h, eid, ew, pos, cnt = routed
    counts = cnt[:, 0].astype(jnp.int32)
    padded = (counts + rb - 1) // rb * rb
    pad_end = jnp.cumsum(padded)
    pad_start = pad_end - padded
    is_e = eid[..., None] == jnp.arange(MOE_EXPERTS, dtype=jnp.int32)
    dest = jnp.sum(jnp.where(is_e, pad_start, 0), axis=-1) + pos
    n_blk = -(-(a + MOE_EXPERTS * (rb - 1)) // rb)
    p = n_blk * rb
    blk_start = jnp.arange(n_blk, dtype=jnp.int32) * rb
    blk_e = jnp.minimum(jnp.sum((pad_end[None, :] <= blk_start[:, None]).astype(jnp.int32), axis=1),
                        MOE_EXPERTS - 1)
    n_used = (pad_end[-1:] // rb).astype(jnp.int32)
    tok = jnp.tile(jnp.arange(n, dtype=jnp.int32), MOE_TOP_K)
    _, tok_by_slot = lax.sort_key_val(dest.reshape(a), tok)
    seg_shift = pad_start - (jnp.cumsum(counts) - counts)
    slot = jnp.arange(p, dtype=jnp.int32)
    e_of_slot = jnp.repeat(blk_e, rb)
    shift = jnp.sum(jnp.where(e_of_slot[:, None] == jnp.arange(MOE_EXPERTS, dtype=jnp.int32),
                              seg_shift, 0), axis=-1)
    idx = slot - shift
    buf_t = tok_by_slot[jnp.where(idx < a, idx, slot % a)]
    xb = h.reshape(n, d)[buf_t]
    yb = expert_ffn(xb, blk_e, n_used, w1, w3, w2)
    y1 = yb[dest[0]].reshape(b, t, d)
    y2 = yb[dest[1]].reshape(b, t, d)
    return moe_combine(x, gate, y1, y2, ew.T.reshape(b, t, MOE_TOP_K))


def kernel(x, c, ada_w, ada_b, norm_g, rel_bias, nsa_w_in, nsa_q_gain, nsa_k_gain, nsa_cmp_pe,
           nsa_cmp_w1, nsa_cmp_w2, nsa_w_out, hgrn_w_in, hgrn_lower_bounds, hgrn_out_gain,
           hgrn_w_out, moe_router_group, moe_router_expert, moe_w1, moe_w3, moe_w2):
    depth = ada_w.shape[0]
    d = x.shape[-1]
    lb_soft = jax.nn.softmax(hgrn_lower_bounds.astype(F32), axis=0)
    lb_all = jnp.cumsum(lb_soft, axis=0) - lb_soft[0]
    mod = adaln_mod(c, ada_w, ada_b)
    for layer in range(depth):
        j = layer // 2
        shift, scale, gate = (mod[layer, 0, :, i * d:(i + 1) * d] for i in range(3))
        if layer % 2 == 0:
            o = nsa_mixer(x, shift, scale, norm_g[layer, 0], rel_bias, nsa_w_in[j], nsa_q_gain[j],
                          nsa_k_gain[j], nsa_cmp_pe[j], nsa_cmp_w1[j], nsa_cmp_w2[j])
            w_out = nsa_w_out[j]
        else:
            o = hgrn_mixer(x, shift, scale, norm_g[layer, 0], lb_all[layer], hgrn_w_in[j],
                           hgrn_out_gain[j])
            w_out = hgrn_w_out[j]
        shift2, scale2, gate2 = (mod[layer, 1, :, i * d:(i + 1) * d] for i in range(3))
        x, *routed = out_proj_router(o, w_out.astype(BF16), x, gate, norm_g[layer, 1], shift2,
                                     scale2, moe_router_group[layer], moe_router_expert[layer])
        x = moe_layer(x, gate2, routed, moe_w1[layer], moe_w3[layer], moe_w2[layer])
    return x
```

```python
import functools
import math

import numpy as np
import jax
import jax.numpy as jnp
from jax import lax
from jax.experimental import pallas as pl
from jax.experimental.pallas import tpu as pltpu

F32 = jnp.float32
BF16 = jnp.bfloat16
HIGHEST = lax.Precision.HIGHEST

NSA_HEADS = 16
NSA_KV_GROUPS = 4
NSA_GROUP_SIZE = NSA_HEADS // NSA_KV_GROUPS
NSA_HEAD_DIM = 64
CMP_BLOCK = 32
CMP_STRIDE = 16
SEL_BLOCK = 64
N_SELECT = 8
WINDOW = 512
FORCED_SCORE = 1.0e4
REL_BUCKETS = 32
REL_MAX_DISTANCE = 1024
HGRN_DK = 128
MOE_GROUPS = 4
MOE_EPG = 4
MOE_EXPERTS = MOE_GROUPS * MOE_EPG
MOE_TOP_K = 2
MOE_ROW_BLOCK = 512
NORM_EPS = 1e-6

LANES = 128
NEG = -1.0e30
LOG2E = math.log2(math.e)
VMEM_LIMIT = 48 * 1024 * 1024

ATT_TQ = 256
ATT_TK = 256
SHIFT_ROWS = 16
SAFE_GAP = 60.0
BOUND_SLACK = 1.01
DENOM_ROWS = 16
HG_CHUNK = 128
HG_SUB = 8
HG_HEADS = 4


def _cparams(*sem):
    return pltpu.CompilerParams(dimension_semantics=sem, vmem_limit_bytes=VMEM_LIMIT)


def _dot(a, b):
    return jnp.dot(a, b, preferred_element_type=F32)


def _dot_exact(a, b):
    return jnp.dot(a, b, preferred_element_type=F32, precision=HIGHEST)


def _dot_lhs_exact(a, x):
    x1 = x.astype(BF16)
    r1 = x - x1.astype(F32)
    x2 = r1.astype(BF16)
    x3 = (r1 - x2.astype(F32)).astype(BF16)
    return _dot(a, x1) + _dot(a, x2) + _dot(a, x3)


def _dot_bf16x3(a, b):
    a1 = a.astype(BF16)
    a2 = (a - a1.astype(F32)).astype(BF16)
    b1 = b.astype(BF16)
    b2 = (b - b1.astype(F32)).astype(BF16)
    return _dot(a1, b1) + _dot(a1, b2) + _dot(a2, b1)


def _dot_nt(a, b):
    return lax.dot_general(a, b, (((1,), (1,)), ((), ())), preferred_element_type=F32)


def _sigmoid(x):
    return 0.5 * jnp.tanh(0.5 * x) + 0.5


def _silu(x):
    return x * _sigmoid(x)


def _adaln_kernel(c_ref, w_ref, b_ref, o_ref):
    cond = _silu(c_ref[...])
    o_ref[0] = _dot_exact(cond, w_ref[0]) + b_ref[0]


def adaln_mod(c, ada_w, ada_b):
    depth, two, d, d3 = ada_w.shape
    b = c.shape[0]
    ls = depth * two
    tn = 1024
    out = pl.pallas_call(
        _adaln_kernel,
        grid=(ls, d3 // tn),
        in_specs=[
            pl.BlockSpec((b, d), lambda i, j: (0, 0)),
            pl.BlockSpec((1, d, tn), lambda i, j: (i, 0, j)),
            pl.BlockSpec((1, 1, tn), lambda i, j: (i, 0, j)),
        ],
        out_specs=pl.BlockSpec((1, b, tn), lambda i, j: (i, 0, j)),
        out_shape=jax.ShapeDtypeStruct((ls, b, d3), F32),
        compiler_params=_cparams("parallel", "parallel"),
        name="adaln_mod",
    )(c, ada_w.reshape(ls, d, d3), ada_b.reshape(ls, 1, d3))
    return out.reshape(depth, two, b, d3)


def _modulated_norm(x, gain, shift, scale):
    ms = jnp.mean(x * x, axis=-1, keepdims=True)
    y = x * lax.rsqrt(ms + NORM_EPS) * gain
    return y * (1.0 + scale) + shift


def _norm_matmul_kernel(x_ref, g_ref, sh_ref, sc_ref, w_ref, o_ref):
    h = _modulated_norm(x_ref[0], g_ref[...], sh_ref[0], sc_ref[0])
    o_ref[0] = _dot(h.astype(BF16), w_ref[...])


def norm_matmul(x, gain, shift, scale, w, tm=512):
    b, t, d = x.shape
    n = w.shape[1]
    tm = min(tm, t)
    return pl.pallas_call(
        _norm_matmul_kernel,
        grid=(b, t // tm),
        in_specs=[
            pl.BlockSpec((1, tm, d), lambda bi, i: (bi, i, 0)),
            pl.BlockSpec((1, d), lambda bi, i: (0, 0)),
            pl.BlockSpec((1, 1, d), lambda bi, i: (bi, 0, 0)),
            pl.BlockSpec((1, 1, d), lambda bi, i: (bi, 0, 0)),
            pl.BlockSpec((d, n), lambda bi, i: (0, 0)),
        ],
        out_specs=pl.BlockSpec((1, tm, n), lambda bi, i: (bi, i, 0)),
        out_shape=jax.ShapeDtypeStruct((b, t, n), F32),
        compiler_params=_cparams("parallel", "parallel"),
        name="norm_matmul",
    )(x, gain.reshape(1, d), shift.reshape(b, 1, d), scale.reshape(b, 1, d), w)


def _out_proj_kernel(o_ref, w_ref, x_ref, gt_ref, y_ref):
    y = _dot(o_ref[0], w_ref[...])
    y_ref[0] = x_ref[0] + gt_ref[0] * y


def out_proj_residual(o, w, x, gate, tm=512):
    b, t, d = x.shape
    k = o.shape[-1]
    tm = min(tm, t)
    return pl.pallas_call(
        _out_proj_kernel,
        grid=(b, t // tm),
        in_specs=[
            pl.BlockSpec((1, tm, k), lambda bi, i: (bi, i, 0)),
            pl.BlockSpec((k, d), lambda bi, i: (0, 0)),
            pl.BlockSpec((1, tm, d), lambda bi, i: (bi, i, 0)),
            pl.BlockSpec((1, 1, d), lambda bi, i: (bi, 0, 0)),
        ],
        out_specs=pl.BlockSpec((1, tm, d), lambda bi, i: (bi, i, 0)),
        out_shape=jax.ShapeDtypeStruct((b, t, d), F32),
        compiler_params=_cparams("parallel", "parallel"),
        name="out_proj_residual",
    )(o, w, x, gate.reshape(b, 1, d))


def _pair_rms(x, gain2):
    lane = lax.broadcasted_iota(jnp.int32, x.shape, 1)
    lo = lane < NSA_HEAD_DIM
    x2 = x * x
    s_lo = jnp.sum(jnp.where(lo, x2, 0.0), axis=-1, keepdims=True)
    s_hi = jnp.sum(jnp.where(lo, 0.0, x2), axis=-1, keepdims=True)
    inv = jnp.where(lo, lax.rsqrt(s_lo / NSA_HEAD_DIM + NORM_EPS),
                    lax.rsqrt(s_hi / NSA_HEAD_DIM + NORM_EPS))
    return x * inv * gain2


def _nsa_prep_kernel(kc0_ref, kc1_ref, vc0_ref, vc1_ref, ks_ref, vs_ref, kw_ref, vw_ref,
                     kg_ref, pe_ref, w1_ref, w1big_ref, w2big_ref, ex_ref,
                     kco_ref, vct_ref, kso_ref, vst_ref, kwo_ref, vwt_ref, k2m_ref):
    t = ks_ref.shape[1]
    nblk = t // CMP_STRIDE
    dh = NSA_HEAD_DIM
    G = NSA_KV_GROUPS

    lane = lax.broadcasted_iota(jnp.int32, (t, LANES), 1)
    ones_cols = jnp.ones((t, SHIFT_ROWS), BF16)
    for br, (src, dst, gi, off) in enumerate(((ks_ref, kso_ref, 1, LANES), (kw_ref, kwo_ref, 2, 0))):
        g2 = kg_ref[gi:gi + 1, :]
        for gp in range(G // 2):
            kn = _pair_rms(src[0, :, gp * LANES:(gp + 1) * LANES], g2).astype(BF16)
            dst[0, 2 * gp, :, off:off + dh] = kn[:, :dh]
            dst[0, 2 * gp + 1, :, off:off + dh] = kn[:, dh:]
            k2 = kn.astype(F32) ** 2
            for half, keep in enumerate((lane < dh, lane >= dh)):
                n2 = jnp.sum(jnp.where(keep, k2, 0.0), axis=-1, keepdims=True)
                row = 2 * (2 * gp + half) + br
                k2m_ref[0, row:row + 1, :] = jnp.broadcast_to(
                    jnp.max(n2, axis=0, keepdims=True), (1, LANES))
        for g in range(G):
            dst[0, g, :, off + dh:off + dh + SHIFT_ROWS] = ones_cols
    for g in range(G):
        kso_ref[0, g, :, :LANES] = ex_ref[...]
    for src, dst in ((vs_ref, vst_ref), (vw_ref, vwt_ref)):
        for gp in range(G // 2):
            vt = src[0, :, gp * LANES:(gp + 1) * LANES].T.astype(BF16)
            dst[0, 2 * gp] = vt[:dh]
            dst[0, 2 * gp + 1] = vt[dh:]

    for ci, (srcs, is_key) in enumerate((((kc0_ref, kc1_ref), True), ((vc0_ref, vc1_ref), False))):
        pe1 = _dot(pe_ref[ci].astype(BF16), w1_ref[ci])[0:1]
        pe2 = jnp.concatenate([pe1, pe1], axis=1)
        for gp, src in enumerate(srcs):
            parts = [src[0, pl.ds(l, nblk, stride=CMP_STRIDE), :] for l in range(CMP_STRIDE)]
            r = jnp.concatenate(parts, axis=1).astype(BF16)
            ab = _dot(r, w1big_ref[ci])
            second = ab[:, LANES:]
            shifted = jnp.concatenate([second[1:], jnp.zeros((1, LANES), F32)], axis=0)
            pre = ab[:, :LANES] + shifted + pe2
            hid = _dot(_silu(pre).astype(BF16), w2big_ref[ci])
            if is_key:
                kn = _pair_rms(hid, kg_ref[0:1, :]).astype(BF16)
                kco_ref[0, 2 * gp] = kn[:, :dh]
                kco_ref[0, 2 * gp + 1] = kn[:, dh:]
            else:
                vt = hid.T.astype(BF16)
                vct_ref[0, 2 * gp] = vt[:dh]
                vct_ref[0, 2 * gp + 1] = vt[dh:]


def nsa_prep(proj, k_gain, cmp_pe, cmp_w1, cmp_w2):
    b, t, _ = proj.shape
    G, dh = NSA_KV_GROUPS, NSA_HEAD_DIM
    kvw = G * dh
    q_blocks = (NSA_HEADS * dh) // kvw
    nblk = t // CMP_STRIDE
    kg2 = jnp.concatenate([k_gain, k_gain], axis=1)
    pe_flat = jnp.broadcast_to(cmp_pe.reshape(2, 1, CMP_BLOCK * dh), (2, 8, CMP_BLOCK * dh))

    w1r = cmp_w1.reshape(2, 2, CMP_STRIDE, dh, dh)
    eye2 = jnp.eye(2, dtype=cmp_w1.dtype)
    w1big = jnp.einsum('chlde,gk->clgdhke', w1r, eye2).reshape(2, CMP_STRIDE * 2 * dh, 4 * dh)
    w2big = jnp.einsum('cde,gk->cgdke', cmp_w2, eye2).reshape(2, 2 * dh, 2 * dh)
    expand = jnp.asarray((np.arange(t)[:, None] // SEL_BLOCK) == np.arange(LANES)[None, :], BF16)

    def col(i):
        return pl.BlockSpec((1, t, kvw), lambda bi, i=i: (bi, 0, q_blocks + i))

    def col_pair(i, gp):
        return pl.BlockSpec((1, t, LANES), lambda bi: (bi, 0, (q_blocks + i) * (kvw // LANES) + gp))

    def full(shape):
        return pl.BlockSpec(shape, lambda bi: (0,) * len(shape))

    def per_b(shape):
        return pl.BlockSpec((1,) + shape, lambda bi: (bi,) + (0,) * len(shape))

    return pl.pallas_call(
        _nsa_prep_kernel,
        grid=(b,),
        in_specs=[col_pair(0, 0), col_pair(0, 1), col_pair(1, 0), col_pair(1, 1),
                  col(2), col(3), col(4), col(5),
                  full((3, 2 * dh)), full((2, 8, CMP_BLOCK * dh)),
                  full((2, CMP_BLOCK * dh, dh)), full((2, 2 * CMP_STRIDE * dh, 4 * dh)),
                  full((2, 2 * dh, 2 * dh)), full((t, LANES))],
        out_specs=[per_b((G, nblk, dh)), per_b((G, dh, nblk)),
                   per_b((G, t, LANES + dh + SHIFT_ROWS)), per_b((G, dh, t)),
                   per_b((G, t, dh + SHIFT_ROWS)), per_b((G, dh, t)),
                   per_b((2 * G, LANES))],
        out_shape=[jax.ShapeDtypeStruct((b, G, nblk, dh), BF16),
                   jax.ShapeDtypeStruct((b, G, dh, nblk), BF16),
                   jax.ShapeDtypeStruct((b, G, t, LANES + dh + SHIFT_ROWS), BF16),
                   jax.ShapeDtypeStruct((b, G, dh, t), BF16),
                   jax.ShapeDtypeStruct((b, G, t, dh + SHIFT_ROWS), BF16),
                   jax.ShapeDtypeStruct((b, G, dh, t), BF16),
                   jax.ShapeDtypeStruct((b, 2 * G, LANES), F32)],
        compiler_params=_cparams("parallel"),
        name="nsa_prep",
    )(proj, proj, proj, proj, proj, proj, proj, proj, kg2, pe_flat,
      cmp_w1.astype(BF16), w1big.astype(BF16), w2big.astype(BF16), expand)


def _nsa_attn_kernel(k2m_ref, hb_ref, q_ref, gl_ref, pgt_ref, qg_ref, kc_ref, vct_ref,
                     ksa_ref, vst_ref, kw_ref, vwt_ref, bc_ref, bs_ref, bw_ref, c2st_ref,
                     o_ref, acc_sc, qsel_sc, qwin_sc, pc_sc, ssa_sc, ssb_sc, swa_sc, swb_sc,
                     score_sc, *,
                     e_sat, n_wtiles,
                     n_sel_blocks, n_select):
    tq, tk = ATT_TQ, ATT_TK
    R, dh = NSA_GROUP_SIZE, NSA_HEAD_DIM
    g_id, b_id, qi = pl.program_id(0), pl.program_id(1), pl.program_id(2)
    scale = dh ** -0.5 * LOG2E
    qoff = LANES

    gl = gl_ref[0]
    g1 = gl.astype(BF16)
    gr = gl - g1.astype(F32)
    g2 = gr.astype(BF16)
    g3 = (gr - g2.astype(F32)).astype(BF16)
    gates = _sigmoid(_dot_nt(pgt_ref[0], g1) + _dot_nt(pgt_ref[0], g2) + _dot_nt(pgt_ref[0], g3))

    qt = q_ref[0].T
    q2 = []
    for r in range(R):
        qr = qt[r * dh:(r + 1) * dh]
        ms = jnp.mean(qr * qr, axis=0, keepdims=True)
        qn = (qr * lax.rsqrt(ms + NORM_EPS) * qg_ref[...] * scale).astype(BF16)
        qwin_sc[:dh, r * tq:(r + 1) * tq] = qn
        qsel_sc[qoff:qoff + dh, r * tq:(r + 1) * tq] = qn
        q2.append(jnp.sum(qn.astype(F32) ** 2, axis=0, keepdims=True))

    k2_sel = k2m_ref[b_id, 2 * g_id]
    k2_win = k2m_ref[b_id, 2 * g_id + 1]
    q2_max = jnp.max(functools.reduce(jnp.maximum, q2))
    thr = hb_ref[NSA_HEADS]
    fast = (2 * BOUND_SLACK) ** 2 * q2_max * jnp.maximum(k2_sel, k2_win) <= thr * thr
    shift_pad = jnp.zeros((SHIFT_ROWS - 1, tq), F32)
    for r in range(R):
        bmax = hb_ref[g_id * R + r]
        for k2, q_sc, row0 in ((k2_sel, qsel_sc, qoff + dh), (k2_win, qwin_sc, dh)):
            bound = jnp.sqrt(q2[r] * k2) * BOUND_SLACK + bmax
            rows = jnp.concatenate([-bound, shift_pad], axis=0)
            q_sc[row0:row0 + SHIFT_ROWS, r * tq:(r + 1) * tq] = rows.astype(BF16)

    ones_rows = jnp.ones((DENOM_ROWS, tk), BF16)
    rr = tq // tk
    n_tiles = rr * qi + rr

    def make_sweep(k_ref, vt_ref, q_sc, b_ref, hi, e_last, sa_sc, sb_sc):
        def tile_col(d):
            dc = jnp.minimum(d, hi - 1)
            e = jnp.where(dc == 0, rr - 1, jnp.where(dc < rr, dc - 1, dc))
            return pl.multiple_of((n_tiles - 1 - e) * tk, tk), jnp.minimum(e, e_last)

        def logits(d, dst_sc):
            col, _ = tile_col(d)
            dst_sc[...] = _dot(k_ref[0, 0, pl.ds(col, tk), :], q_sc[...])

        def softmax_pv(d, src_sc, ms):
            col, bi = tile_col(d)
            vaug = jnp.concatenate([vt_ref[0, 0, :, pl.ds(col, tk)], ones_rows], axis=0)
            new_ms = []
            for r in range(R):
                s = src_sc[:, r * tq:(r + 1) * tq] + b_ref[r, bi]
                m_new = jnp.maximum(ms[r], jnp.max(s, axis=0, keepdims=True))
                alpha = jnp.exp2(ms[r] - m_new)
                p = jnp.exp2(s - m_new).astype(BF16)
                acc_sc[r] = acc_sc[r] * alpha + _dot(vaug, p)
                new_ms.append(m_new)
            return tuple(new_ms)

        def shifted_pv(d, src_sc):
            col, bi = tile_col(d)
            vaug = jnp.concatenate([vt_ref[0, 0, :, pl.ds(col, tk)], ones_rows], axis=0)
            for r in range(R):
                p = jnp.exp2(src_sc[:, r * tq:(r + 1) * tq] + b_ref[r, bi]).astype(BF16)
                acc_sc[r] += _dot(vaug, p)

        def body(dp, ms):
            d = 2 * dp
            logits(d + 1, sb_sc)
            ms = softmax_pv(d, sa_sc, ms)
            logits(d + 2, sa_sc)
            return softmax_pv(d + 1, sb_sc, ms)

        def shifted_body(dp, carry):
            d = 2 * dp
            logits(d + 1, sb_sc)
            shifted_pv(d, sa_sc)
            logits(d + 2, sa_sc)
            shifted_pv(d + 1, sb_sc)
            return carry

        def start():
            logits(0, sa_sc)

        def run():
            acc_sc[...] = jnp.zeros(acc_sc.shape, F32)

            @pl.when(fast)
            def _():
                lax.fori_loop(0, hi // 2, shifted_body, 0)

                @pl.when(hi % 2 == 1)
                def _():
                    shifted_pv(hi - 1, sa_sc)

            @pl.when(jnp.logical_not(fast))
            def _():
                ms = lax.fori_loop(0, hi // 2, body,
                                   tuple(jnp.full((1, tq), NEG, F32) for _ in range(R)))

                @pl.when(hi % 2 == 1)
                def _():
                    softmax_pv(hi - 1, sa_sc, ms)

            outs = []
            for r in range(R):
                acc = acc_sc[r]
                outs.append(acc[:dh] * (1.0 / acc[dh:dh + 1]))
            return outs

        return start, run

    win_start, win_run = make_sweep(kw_ref, vwt_ref, qwin_sc, bw_ref,
                                    jnp.minimum(n_tiles, n_wtiles), n_wtiles - 1, swa_sc, swb_sc)
    win_start()

    ncp = kc_ref.shape[2]
    n_idx = lax.broadcasted_iota(jnp.int32, (ncp, tq), 0)
    t_idx = qi * tq + lax.broadcasted_iota(jnp.int32, (ncp, tq), 1)
    vis = (n_idx * CMP_STRIDE + (CMP_BLOCK - 1)) <= t_idx
    psum = jnp.zeros((ncp, tq), F32)
    sc_all = _dot(kc_ref[0, 0], qwin_sc[:dh])
    for r in range(R):
        s = jnp.where(vis, sc_all[:, r * tq:(r + 1) * tq] + bc_ref[r], NEG)
        m = jnp.max(s, axis=0, keepdims=True)
        e = jnp.where(vis, jnp.exp2(s - m), 0.0)
        p = e * (1.0 / jnp.maximum(jnp.sum(e, axis=0, keepdims=True), 1e-30))
        psum = psum + p
        pc_sc[:, r * tq:(r + 1) * tq] = p.astype(BF16)
    oc_all = _dot(vct_ref[0, 0], pc_sc[...])
    o_cmp = [oc_all[:, r * tq:(r + 1) * tq] for r in range(R)]

    imp_t = _dot_lhs_exact(c2st_ref[...], psum)
    blk = lax.broadcasted_iota(jnp.int32, (n_sel_blocks, tq), 0)
    tpos = qi * tq + lax.broadcasted_iota(jnp.int32, (n_sel_blocks, tq), 1)
    cur = tpos // SEL_BLOCK
    forced = (blk == 0) | (blk == cur) | (blk == cur - 1)
    score = jnp.where(forced, FORCED_SCORE, jnp.where(blk <= cur, imp_t, -1.0))
    score_sc[...] = score
    per_tile = tq // SEL_BLOCK

    def rank_body(g, rank):
        for u in range(per_tile):
            s2 = g * per_tile + u
            row = score_sc[pl.ds(s2, 1), :]
            beats = (row > score) | ((row == score) & (blk > s2))
            rank = rank + jnp.where(beats, 1.0, 0.0)
        return rank

    rank = lax.fori_loop(0, qi + 1, rank_body, jnp.zeros((n_sel_blocks, tq), F32))
    negsel = jnp.where(rank < n_select, 0.0, NEG)
    if n_sel_blocks < LANES:
        negsel = jnp.concatenate([negsel, jnp.zeros((LANES - n_sel_blocks, tq), F32)], axis=0)
    negsel = negsel.astype(BF16)
    for r in range(R):
        qsel_sc[:qoff, r * tq:(r + 1) * tq] = negsel

    sel_start, sel_run = make_sweep(ksa_ref, vst_ref, qsel_sc, bs_ref, n_tiles, e_sat,
                                    ssa_sc, ssb_sc)
    sel_start()
    o_win = win_run()
    o_sel = sel_run()

    outs = []
    for r in range(R):
        outs.append(gates[3 * r:3 * r + 1] * o_cmp[r]
                    + gates[3 * r + 1:3 * r + 2] * o_sel[r]
                    + gates[3 * r + 2:3 * r + 3] * o_win[r])
    o_ref[0] = jnp.concatenate(outs, axis=0).T.astype(o_ref.dtype)


def _t5_bucket(dist):
    n = jnp.maximum(dist, 0)
    max_exact = REL_BUCKETS // 2
    nf = jnp.maximum(n, 1).astype(F32)
    large = max_exact + (jnp.log(nf / max_exact) / math.log(REL_MAX_DISTANCE / max_exact)
                         * (REL_BUCKETS - max_exact)).astype(jnp.int32)
    large = jnp.minimum(large, REL_BUCKETS - 1)
    return jnp.where(n < max_exact, n, large)


def _bias_of_dist(dist, rel_bias):
    onehot = (_t5_bucket(dist)[..., None] == jnp.arange(REL_BUCKETS)).astype(F32)
    out = jnp.einsum('...k,kh->...h', onehot, rel_bias.astype(F32), precision=HIGHEST)
    return jnp.moveaxis(out, -1, 0)


def _saturation_distance():
    max_exact = REL_BUCKETS // 2
    steps = REL_BUCKETS - max_exact
    n_sat = max_exact * (REL_MAX_DISTANCE / max_exact) ** ((steps - 1) / steps)
    return int(math.ceil(n_sat)) + 2


def nsa_attention(proj, kc, vct, ksa, vst, kw, vwt, k2m_rows, rel_bias, q_gain):
    b, t, _ = proj.shape
    G, R, dh = NSA_KV_GROUPS, NSA_GROUP_SIZE, NSA_HEAD_DIM
    H = NSA_HEADS
    tq, tk = ATT_TQ, ATT_TK
    rr = tq // tk
    ncp = kc.shape[2]
    n_sel_blocks = t // SEL_BLOCK
    n_select = min(N_SELECT, n_sel_blocks)
    assert n_sel_blocks <= LANES and n_sel_blocks % 8 == 0 and tq % tk == 0 and t % tq == 0

    e_sat = -(-(_saturation_distance() + tk - 1) // tk) + rr - 1
    n_wtiles = -(-(WINDOW + tk - 1) // tk) + rr - 1
    jj = np.arange(tk)[:, None]
    ii = np.arange(tq)[None, :]

    def tile_dist(n_e):
        return (np.arange(n_e)[:, None, None] - (rr - 1)) * tk + (ii - jj)[None]

    dist = tile_dist(e_sat + 1)
    bias_sel = jnp.where(dist >= 0, _bias_of_dist(jnp.asarray(dist), rel_bias) * LOG2E, NEG)
    dwin = tile_dist(n_wtiles)
    bias_win = jnp.where((dwin >= 0) & (dwin < WINDOW),
                         _bias_of_dist(jnp.asarray(dwin), rel_bias) * LOG2E, NEG)
    dc = np.arange(t)[None, :] - (np.arange(ncp)[:, None] * CMP_STRIDE + CMP_BLOCK - 1)
    bias_c = _bias_of_dist(jnp.asarray(dc), rel_bias) * LOG2E

    cs = np.arange(ncp) * CMP_STRIDE
    ss = np.arange(n_sel_blocks) * SEL_BLOCK
    shared = (np.minimum(cs[None, :] + CMP_BLOCK, ss[:, None] + SEL_BLOCK)
              - np.maximum(cs[None, :], ss[:, None]))
    c2st = jnp.asarray(np.clip(shared, 0, None) / CMP_BLOCK, BF16)
    pgt = np.zeros((G, LANES, LANES), np.float32)
    for g in range(G):
        for k in range(3 * R):
            pgt[g, k, 3 * R * g + k] = 1.0
    pgt = jnp.asarray(pgt, BF16)
    gate_blk = (H * dh + 6 * G * dh) // LANES
    qg = jnp.broadcast_to(q_gain.reshape(dh, 1), (dh, tq))

    rb2 = rel_bias.astype(F32) * LOG2E
    spread = jnp.max(jnp.max(rb2, axis=0) - jnp.min(rb2, axis=0))
    head_bias = jnp.concatenate([jnp.max(rb2, axis=0), jnp.maximum(SAFE_GAP - spread, 0.0)[None]])
    k2m = k2m_rows[:, :, 0]

    kernel = functools.partial(_nsa_attn_kernel, e_sat=e_sat, n_wtiles=n_wtiles,
                               n_sel_blocks=n_sel_blocks, n_select=n_select)
    return pl.pallas_call(
        kernel,
        grid=(G, b, t // tq),
        in_specs=[
            pl.BlockSpec(memory_space=pltpu.SMEM),
            pl.BlockSpec(memory_space=pltpu.SMEM),
            pl.BlockSpec((1, tq, R * dh), lambda g, bi, i: (bi, i, g)),
            pl.BlockSpec((1, tq, LANES), lambda g, bi, i: (bi, i, gate_blk)),
            pl.BlockSpec((1, LANES, LANES), lambda g, bi, i: (g, 0, 0)),
            pl.BlockSpec((dh, tq), lambda g, bi, i: (0, 0)),
            pl.BlockSpec((1, 1, ncp, dh), lambda g, bi, i: (bi, g, 0, 0)),
            pl.BlockSpec((1, 1, dh, ncp), lambda g, bi, i: (bi, g, 0, 0)),
            pl.BlockSpec((1, 1, t, LANES + dh + SHIFT_ROWS), lambda g, bi, i: (bi, g, 0, 0)),
            pl.BlockSpec((1, 1, dh, t), lambda g, bi, i: (bi, g, 0, 0)),
            pl.BlockSpec((1, 1, t, dh + SHIFT_ROWS), lambda g, bi, i: (bi, g, 0, 0)),
            pl.BlockSpec((1, 1, dh, t), lambda g, bi, i: (bi, g, 0, 0)),
            pl.BlockSpec((R, ncp, tq), lambda g, bi, i: (g, 0, i)),
            pl.BlockSpec((R, e_sat + 1, tk, tq), lambda g, bi, i: (g, 0, 0, 0)),
            pl.BlockSpec((R, n_wtiles, tk, tq), lambda g, bi, i: (g, 0, 0, 0)),
            pl.BlockSpec((n_sel_blocks, ncp), lambda g, bi, i: (0, 0)),
        ],
        out_specs=pl.BlockSpec((1, tq, R * dh), lambda g, bi, i: (bi, i, g)),
        out_shape=jax.ShapeDtypeStruct((b, t, H * dh), BF16),
        scratch_shapes=[pltpu.VMEM((R, dh + DENOM_ROWS, tq), F32),
                        pltpu.VMEM((LANES + dh + SHIFT_ROWS, R * tq), BF16),
                        pltpu.VMEM((dh + SHIFT_ROWS, R * tq), BF16),
                        pltpu.VMEM((ncp, R * tq), BF16),
                        pltpu.VMEM((tk, R * tq), F32), pltpu.VMEM((tk, R * tq), F32),
                        pltpu.VMEM((tk, R * tq), F32), pltpu.VMEM((tk, R * tq), F32),
                        pltpu.VMEM((n_sel_blocks, tq), F32)],
        compiler_params=_cparams("parallel", "parallel", "arbitrary"),
        name="nsa_attention",
    )(k2m, head_bias, proj, proj, pgt, qg, kc, vct, ksa, vst, kw, vwt, bias_c, bias_sel, bias_win,
      c2st)


def nsa_layer(x, shift, scale, gate, norm_g, rel_bias, w_in, q_gain, k_gain,
              cmp_pe, cmp_w1, cmp_w2, w_out):
    d = x.shape[-1]
    n_in = w_in.shape[1]
    n_pad = -(-n_in // LANES) * LANES
    w_in_p = jnp.pad(w_in.astype(BF16), ((0, 0), (0, n_pad - n_in)))
    proj = norm_matmul(x, norm_g, shift, scale, w_in_p)
    kc, vct, ksa, vst, kw, vwt, k2m_rows = nsa_prep(proj, k_gain, cmp_pe, cmp_w1, cmp_w2)
    o = nsa_attention(proj, kc, vct, ksa, vst, kw, vwt, k2m_rows, rel_bias, q_gain)
    return out_proj_residual(o, w_out.astype(BF16), x, gate)


def _hgrn_kernel(q_ref, f_ref, v_ref, g_ref, lb_ref, og_ref, tri_ref, ones_ref,
                 o_ref, st_sc, k_sc, c_sc):
    C, dk, S = HG_CHUNK, HGRN_DK, HG_SUB
    n_sub = C // S

    @pl.when(pl.program_id(2) == 0)
    def _():
        st_sc[...] = jnp.zeros(st_sc.shape, F32)

    row = lax.broadcasted_iota(jnp.int32, (C, C), 0)
    colm = lax.broadcasted_iota(jnp.int32, (C, C), 1)
    diag_keep = ((row // S) == (colm // S)) & ((colm % S) <= (row % S))

    def rows_bcast(ref, hh, first, period):
        return jnp.concatenate(
            [jnp.broadcast_to(ref[hh, pl.ds(g * period + first, 1), :], (period, dk))
             for g in range(C // period)], axis=0)

    heads = range(HG_HEADS)
    sls = [slice(hh * dk, (hh + 1) * dk) for hh in heads]
    qs, ks, cums, vs = [], [], [], []
    for hh in heads:
        fl2 = f_ref[0, :, sls[hh]] * LOG2E
        lb = lb_ref[0, :, sls[hh]]
        log_sig = jnp.minimum(fl2, 0.0) - jnp.log2(1.0 + jnp.exp2(-jnp.abs(fl2)))
        ta = jnp.log2(lb)
        tb = jnp.log2(1.0 - lb) + log_sig
        lf = jnp.maximum(ta, tb) + jnp.log2(1.0 + jnp.exp2(-jnp.abs(ta - tb)))
        k = 1.0 - jnp.exp2(lf)
        cum = _dot_lhs_exact(tri_ref[...], lf)
        k_sc[hh] = k
        c_sc[hh] = cum
        ks.append(k)
        cums.append(cum)
        qs.append(_silu(q_ref[0, :, sls[hh]]))
        vs.append(v_ref[0, :, sls[hh]].astype(BF16))

    attns = []
    for hh in heads:
        pieces = []
        for j in range(S):
            kj = rows_bcast(k_sc, hh, j, S)
            cj = rows_bcast(c_sc, hh, j, S)
            pieces.append((qs[hh] * kj * jnp.exp2(jnp.minimum(cums[hh] - cj, 0.0))).astype(BF16))
        attns.append(jnp.where(diag_keep,
                               _dot(jnp.concatenate(pieces, axis=1), ones_ref[...]), 0.0))

    rloc = lax.broadcasted_iota(jnp.int32, (C, dk), 0)
    m = S
    while m < C:
        upper = (rloc // m) % 2 == 1
        same = (row // (2 * m)) == (colm // (2 * m))
        for hh in heads:
            e = jnp.exp2(-jnp.abs(cums[hh] - rows_bcast(c_sc, hh, m - 1, 2 * m)))
            qm = jnp.where(upper, qs[hh] * e, 0.0).astype(BF16)
            km = jnp.where(upper, 0.0, ks[hh] * e).astype(BF16)
            attns[hh] = attns[hh] + jnp.where(same, _dot_nt(qm, km), 0.0)
        m *= 2

    for hh in heads:
        st = st_sc[hh]
        cum, k, q, v = cums[hh], ks[hh], qs[hh], vs[hh]
        o = _dot(attns[hh].astype(BF16), v)
        o = o + _dot_nt((q * jnp.exp2(cum)).astype(BF16), st.astype(BF16))
        total = cum[C - 1:C, :]
        kd = (k * jnp.exp2(total - cum)).astype(BF16)
        st_sc[hh] = st * jnp.exp2(total) + lax.dot_general(
            v, kd, (((0,), (0,)), ((), ())), preferred_element_type=F32)
        ms = jnp.mean(o * o, axis=-1, keepdims=True)
        o = o * lax.rsqrt(ms + NORM_EPS) * og_ref[...]
        o_ref[0, :, sls[hh]] = (o * _silu(g_ref[0, :, sls[hh]])).astype(o_ref.dtype)


def hgrn_recurrence(proj, lb, out_gain):
    b, t, four_d = proj.shape
    d = four_d // 4
    dk = HGRN_DK
    C, S, hps = HG_CHUNK, HG_SUB, HG_HEADS
    w = hps * dk
    nhp = d // w
    tri = jnp.asarray(np.tril(np.ones((C, C), np.float32)), BF16)
    ones = jnp.asarray(np.arange(S * dk)[:, None] // dk == (np.arange(C)[None, :] % S), BF16)
    return pl.pallas_call(
        _hgrn_kernel,
        grid=(b, nhp, t // C),
        in_specs=[
            pl.BlockSpec((1, C, w), lambda bi, h, c: (bi, c, h)),
            pl.BlockSpec((1, C, w), lambda bi, h, c: (bi, c, nhp + h)),
            pl.BlockSpec((1, C, w), lambda bi, h, c: (bi, c, 2 * nhp + h)),
            pl.BlockSpec((1, C, w), lambda bi, h, c: (bi, c, 3 * nhp + h)),
            pl.BlockSpec((1, 1, w), lambda bi, h, c: (h, 0, 0)),
            pl.BlockSpec((1, dk), lambda bi, h, c: (0, 0)),
            pl.BlockSpec((C, C), lambda bi, h, c: (0, 0)),
            pl.BlockSpec((S * dk, C), lambda bi, h, c: (0, 0)),
        ],
        out_specs=pl.BlockSpec((1, C, w), lambda bi, h, c: (bi, c, h)),
        out_shape=jax.ShapeDtypeStruct((b, t, d), BF16),
        scratch_shapes=[pltpu.VMEM((hps, dk, dk), F32), pltpu.VMEM((hps, C, dk), F32),
                        pltpu.VMEM((hps, C, dk), F32)],
        compiler_params=_cparams("parallel", "parallel", "arbitrary"),
        name="hgrn_recurrence",
    )(proj, proj, proj, proj, lb.reshape(nhp, 1, w), out_gain.reshape(1, dk), tri, ones)


def hgrn_layer(x, shift, scale, gate, norm_g, lb, w_in, out_gain, w_out):
    proj = norm_matmul(x, norm_g, shift, scale, w_in.astype(BF16))
    o = hgrn_recurrence(proj, lb, out_gain)
    return out_proj_residual(o, w_out.astype(BF16), x, gate)


def _router_kernel(x_ref, g_ref, sh_ref, sc_ref, wr_ref, u_ref,
                   h_ref, eid_ref, ew_ref, pos_ref, cnt_ref, run_sc):
    @pl.when((pl.program_id(0) == 0) & (pl.program_id(1) == 0))
    def _():
        run_sc[...] = jnp.zeros(run_sc.shape, F32)

    h = _modulated_norm(x_ref[0], g_ref[...], sh_ref[0], sc_ref[0])
    h_ref[0] = h.astype(BF16)
    lt = _dot_bf16x3(h, wr_ref[...]).T
    NG, EPG = MOE_GROUPS, MOE_EPG

    def softmax_rows(rows):
        mx = functools.reduce(jnp.maximum, rows)
        es = [jnp.exp(r - mx) for r in rows]
        tot = functools.reduce(lambda a, c: a + c, es)
        return [e / tot for e in es]

    def argmax_rows(rows):
        best, idx = rows[0], jnp.zeros(rows[0].shape, jnp.int32)
        for i in range(1, len(rows)):
            better = rows[i] > best
            best = jnp.where(better, rows[i], best)
            idx = jnp.where(better, i, idx)
        return best, idx

    pg = softmax_rows([lt[i:i + 1] for i in range(NG)])
    p_grp, grp = argmax_rows(pg)
    el = []
    for j in range(EPG):
        acc = lt[NG + j:NG + j + 1]
        for gi in range(1, NG):
            acc = jnp.where(grp == gi, lt[NG + gi * EPG + j:NG + gi * EPG + j + 1], acc)
        el.append(acc)
    pe = softmax_rows(el)
    p1, i1 = argmax_rows(pe)
    p2, i2 = argmax_rows([jnp.where(i1 == j, -1.0, pe[j]) for j in range(EPG)])
    den = p1 + p2
    e1 = grp * EPG + i1
    e2 = grp * EPG + i2
    eid_ref[0:1, :] = e1
    eid_ref[1:2, :] = e2
    ew_ref[0:1, :] = p_grp * p1 / den
    ew_ref[1:2, :] = p_grp * p2 / den

    tm = e1.shape[1]
    ex = lax.broadcasted_iota(jnp.int32, (MOE_EXPERTS, tm), 0)
    oh1 = jnp.where(ex == e1, 1.0, 0.0)
    oh2 = jnp.where(ex == e2, 1.0, 0.0)
    before1 = _dot(oh1.astype(BF16), u_ref[...])
    before2 = _dot(oh2.astype(BF16), u_ref[...])
    tot1 = jnp.sum(oh1, axis=1, keepdims=True)
    tot2 = jnp.sum(oh2, axis=1, keepdims=True)
    run = run_sc[...]
    pos1 = jnp.sum(oh1 * (before1 + run), axis=0, keepdims=True)
    pos2 = jnp.sum(oh2 * (before2 + (run + tot1)), axis=0, keepdims=True)
    pos_ref[0:1, :] = pos1.astype(jnp.int32)
    pos_ref[1:2, :] = pos2.astype(jnp.int32)
    run = run + tot1 + tot2
    run_sc[...] = run
    cnt_ref[...] = jnp.broadcast_to(run, cnt_ref.shape)


def moe_router(x, gain, shift, scale, w_group, w_expert, tm=512):
    b, t, d = x.shape
    tm = min(tm, t)
    nt = t // tm
    wr = jnp.concatenate([w_group, w_expert], axis=1)
    wr = jnp.pad(wr, ((0, 0), (0, LANES - wr.shape[1])))
    upper = jnp.asarray(np.triu(np.ones((tm, tm), np.float32), 1), BF16)
    return pl.pallas_call(
        _router_kernel,
        grid=(b, nt),
        in_specs=[
            pl.BlockSpec((1, tm, d), lambda bi, i: (bi, i, 0)),
            pl.BlockSpec((1, d), lambda bi, i: (0, 0)),
            pl.BlockSpec((1, 1, d), lambda bi, i: (bi, 0, 0)),
            pl.BlockSpec((1, 1, d), lambda bi, i: (bi, 0, 0)),
            pl.BlockSpec((d, LANES), lambda bi, i: (0, 0)),
            pl.BlockSpec((tm, tm), lambda bi, i: (0, 0)),
        ],
        out_specs=[
            pl.BlockSpec((1, tm, d), lambda bi, i: (bi, i, 0)),
            pl.BlockSpec((MOE_TOP_K, tm), lambda bi, i: (0, bi * nt + i)),
            pl.BlockSpec((MOE_TOP_K, tm), lambda bi, i: (0, bi * nt + i)),
            pl.BlockSpec((MOE_TOP_K, tm), lambda bi, i: (0, bi * nt + i)),
            pl.BlockSpec((MOE_EXPERTS, LANES), lambda bi, i: (0, 0)),
        ],
        out_shape=[
            jax.ShapeDtypeStruct((b, t, d), BF16),
            jax.ShapeDtypeStruct((MOE_TOP_K, b * t), jnp.int32),
            jax.ShapeDtypeStruct((MOE_TOP_K, b * t), F32),
            jax.ShapeDtypeStruct((MOE_TOP_K, b * t), jnp.int32),
            jax.ShapeDtypeStruct((MOE_EXPERTS, LANES), F32),
        ],
        scratch_shapes=[pltpu.VMEM((MOE_EXPERTS, 1), F32)],
        compiler_params=_cparams("arbitrary", "arbitrary"),
        name="moe_router",
    )(x, gain.reshape(1, d), shift.reshape(b, 1, d), scale.reshape(b, 1, d), wr, upper)


def _expert_ffn_kernel(be_ref, nu_ref, xb_ref, w1_ref, w3_ref, w2_ref, yb_ref,
                       w1_sc, w3_sc, w2_sc):
    i = pl.program_id(0)

    @pl.when((i == 0) | (be_ref[i] != be_ref[jnp.maximum(i - 1, 0)]))
    def _():
        w1_sc[...] = w1_ref[0].astype(BF16)
        w3_sc[...] = w3_ref[0].astype(BF16)
        w2_sc[...] = w2_ref[0].astype(BF16)

    @pl.when(i < nu_ref[0])
    def _():
        xb = xb_ref[...]
        a = _dot(xb, w1_sc[...])
        g = _dot(xb, w3_sc[...])
        yb_ref[...] = _dot((_silu(a) * g).astype(BF16), w2_sc[...]).astype(yb_ref.dtype)

    @pl.when(i >= nu_ref[0])
    def _():
        yb_ref[...] = jnp.zeros(yb_ref.shape, yb_ref.dtype)


def expert_ffn(xb, blk_e, n_used, w1, w3, w2):
    p, d = xb.shape
    ff = w1.shape[2]
    rb = MOE_ROW_BLOCK
    grid_spec = pltpu.PrefetchScalarGridSpec(
        num_scalar_prefetch=2,
        grid=(p // rb,),
        in_specs=[
            pl.BlockSpec((rb, d), lambda i, be, nu: (i, 0)),
            pl.BlockSpec((1, d, ff), lambda i, be, nu: (be[i], 0, 0)),
            pl.BlockSpec((1, d, ff), lambda i, be, nu: (be[i], 0, 0)),
            pl.BlockSpec((1, ff, d), lambda i, be, nu: (be[i], 0, 0)),
        ],
        out_specs=pl.BlockSpec((rb, d), lambda i, be, nu: (i, 0)),
        scratch_shapes=[pltpu.VMEM((d, ff), BF16), pltpu.VMEM((d, ff), BF16),
                        pltpu.VMEM((ff, d), BF16)],
    )
    return pl.pallas_call(
        _expert_ffn_kernel,
        grid_spec=grid_spec,
        out_shape=jax.ShapeDtypeStruct((p, d), BF16),
        compiler_params=_cparams("arbitrary"),
        name="expert_ffn",
    )(blk_e, n_used, xb, w1, w3, w2)


def _combine_kernel(x_ref, gt_ref, y1_ref, y2_ref, w_ref, o_ref):
    w = w_ref[0]
    y = w[:, 0:1] * y1_ref[0].astype(F32) + w[:, 1:2] * y2_ref[0].astype(F32)
    o_ref[0] = x_ref[0] + gt_ref[0] * y


def moe_combine(x, gate, y1, y2, w, tm=512):
    b, t, d = x.shape
    tm = min(tm, t)
    spec = pl.BlockSpec((1, tm, d), lambda bi, i: (bi, i, 0))
    return pl.pallas_call(
        _combine_kernel,
        grid=(b, t // tm),
        in_specs=[spec, pl.BlockSpec((1, 1, d), lambda bi, i: (bi, 0, 0)), spec, spec,
                  pl.BlockSpec((1, tm, MOE_TOP_K), lambda bi, i: (bi, i, 0))],
        out_specs=spec,
        out_shape=jax.ShapeDtypeStruct((b, t, d), F32),
        compiler_params=_cparams("parallel", "parallel"),
        name="moe_combine",
    )(x, gate.reshape(b, 1, d), y1, y2, w)


def moe_layer(x, shift, scale, gate, norm_g, w_group, w_expert, w1, w3, w2):
    b, t, d = x.shape
    n = b * t
    a = n * MOE_TOP_K
    rb = MOE_ROW_BLOCK
    h, eid, ew, pos, cnt = moe_router(x, norm_g, shift, scale, w_group, w_expert)
    counts = cnt[:, 0].astype(jnp.int32)
    padded = (counts + rb - 1) // rb * rb
    pad_end = jnp.cumsum(padded)
    pad_start = pad_end - padded
    is_e = eid[..., None] == jnp.arange(MOE_EXPERTS, dtype=jnp.int32)
    dest = jnp.sum(jnp.where(is_e, pad_start, 0), axis=-1) + pos
    n_blk = -(-(a + MOE_EXPERTS * (rb - 1)) // rb)
    p = n_blk * rb
    blk_start = jnp.arange(n_blk, dtype=jnp.int32) * rb
    blk_e = jnp.minimum(jnp.sum((pad_end[None, :] <= blk_start[:, None]).astype(jnp.int32), axis=1),
                        MOE_EXPERTS - 1)
    n_used = (pad_end[-1:] // rb).astype(jnp.int32)
    tok = jnp.tile(jnp.arange(n, dtype=jnp.int32), MOE_TOP_K)
    _, tok_by_slot = lax.sort_key_val(dest.reshape(a), tok)
    seg_shift = pad_start - (jnp.cumsum(counts) - counts)
    slot = jnp.arange(p, dtype=jnp.int32)
    e_of_slot = jnp.repeat(blk_e, rb)
    shift = jnp.sum(jnp.where(e_of_slot[:, None] == jnp.arange(MOE_EXPERTS, dtype=jnp.int32),
                              seg_shift, 0), axis=-1)
    idx = slot - shift
    buf_t = tok_by_slot[jnp.where(idx < a, idx, slot % a)]
    xb = h.reshape(n, d)[buf_t]
    yb = expert_ffn(xb, blk_e, n_used, w1, w3, w2)
    y1 = yb[dest[0]].reshape(b, t, d)
    y2 = yb[dest[1]].reshape(b, t, d)
    return moe_combine(x, gate, y1, y2, ew.T.reshape(b, t, MOE_TOP_K))


def kernel(x, c, ada_w, ada_b, norm_g, rel_bias, nsa_w_in, nsa_q_gain, nsa_k_gain, nsa_cmp_pe,
           nsa_cmp_w1, nsa_cmp_w2, nsa_w_out, hgrn_w_in, hgrn_lower_bounds, hgrn_out_gain,
           hgrn_w_out, moe_router_group, moe_router_expert, moe_w1, moe_w3, moe_w2):
    depth = ada_w.shape[0]
    d = x.shape[-1]
    lb_soft = jax.nn.softmax(hgrn_lower_bounds.astype(F32), axis=0)
    lb_all = jnp.cumsum(lb_soft, axis=0) - lb_soft[0]
    mod = adaln_mod(c, ada_w, ada_b)
    for layer in range(depth):
        j = layer // 2
        shift, scale, gate = (mod[layer, 0, :, i * d:(i + 1) * d] for i in range(3))
        if layer % 2 == 0:
            x = nsa_layer(x, shift, scale, gate, norm_g[layer, 0], rel_bias, nsa_w_in[j],
                          nsa_q_gain[j], nsa_k_gain[j], nsa_cmp_pe[j], nsa_cmp_w1[j],
                          nsa_cmp_w2[j], nsa_w_out[j])
        else:
            x = hgrn_layer(x, shift, scale, gate, norm_g[layer, 0], lb_all[layer],
                           hgrn_w_in[j], hgrn_out_gain[j], hgrn_w_out[j])
        shift, scale, gate = (mod[layer, 1, :, i * d:(i + 1) * d] for i in range(3))
        x = moe_layer(x, shift, scale, gate, norm_g[layer, 1], moe_router_group[layer],
                      moe_router_expert[layer], moe_w1[layer], moe_w3[layer], moe_w2[layer])
    return x
```

```python
import functools
import math

import numpy as np
import jax
import jax.numpy as jnp
from jax import lax
from jax.experimental import pallas as pl
from jax.experimental.pallas import tpu as pltpu

F32 = jnp.float32
BF16 = jnp.bfloat16
HIGHEST = lax.Precision.HIGHEST

NSA_HEADS = 16
NSA_KV_GROUPS = 4
NSA_GROUP_SIZE = NSA_HEADS // NSA_KV_GROUPS
NSA_HEAD_DIM = 64
CMP_BLOCK = 32
CMP_STRIDE = 16
SEL_BLOCK = 64
N_SELECT = 8
WINDOW = 512
FORCED_SCORE = 1.0e4
REL_BUCKETS = 32
REL_MAX_DISTANCE = 1024
HGRN_DK = 128
MOE_GROUPS = 4
MOE_EPG = 4
MOE_EXPERTS = MOE_GROUPS * MOE_EPG
MOE_TOP_K = 2
MOE_ROW_BLOCK = 512
NORM_EPS = 1e-6

LANES = 128
NEG = -1.0e30
LOG2E = math.log2(math.e)
VMEM_LIMIT = 48 * 1024 * 1024

ATT_TQ = 256
ATT_TK = 256
SHIFT_ROWS = 16
SAFE_GAP = 60.0
BOUND_SLACK = 1.01
DENOM_ROWS = 16
HG_CHUNK = 128
HG_SUB = 8
HG_HEADS = 8


def _cparams(*sem):
    return pltpu.CompilerParams(dimension_semantics=sem, vmem_limit_bytes=VMEM_LIMIT)


def _dot(a, b):
    return jnp.dot(a, b, preferred_element_type=F32)


def _dot_exact(a, b):
    return jnp.dot(a, b, preferred_element_type=F32, precision=HIGHEST)


def _dot_lhs_exact(a, x):
    x1 = x.astype(BF16)
    r1 = x - x1.astype(F32)
    x2 = r1.astype(BF16)
    x3 = (r1 - x2.astype(F32)).astype(BF16)
    return _dot(a, x1) + _dot(a, x2) + _dot(a, x3)


def _dot_bf16x3(a, b):
    a1 = a.astype(BF16)
    a2 = (a - a1.astype(F32)).astype(BF16)
    b1 = b.astype(BF16)
    b2 = (b - b1.astype(F32)).astype(BF16)
    return _dot(a1, b1) + _dot(a1, b2) + _dot(a2, b1)


def _dot_nt(a, b):
    return lax.dot_general(a, b, (((1,), (1,)), ((), ())), preferred_element_type=F32)


def _sigmoid(x):
    return 0.5 * jnp.tanh(0.5 * x) + 0.5


def _silu(x):
    return x * _sigmoid(x)


def _adaln_kernel(c_ref, w_ref, b_ref, o_ref):
    cond = _silu(c_ref[...])
    o_ref[0] = _dot_exact(cond, w_ref[0]) + b_ref[0]


def adaln_mod(c, ada_w, ada_b):
    depth, two, d, d3 = ada_w.shape
    b = c.shape[0]
    ls = depth * two
    tn = 1024
    out = pl.pallas_call(
        _adaln_kernel,
        grid=(ls, d3 // tn),
        in_specs=[
            pl.BlockSpec((b, d), lambda i, j: (0, 0)),
            pl.BlockSpec((1, d, tn), lambda i, j: (i, 0, j)),
            pl.BlockSpec((1, 1, tn), lambda i, j: (i, 0, j)),
        ],
        out_specs=pl.BlockSpec((1, b, tn), lambda i, j: (i, 0, j)),
        out_shape=jax.ShapeDtypeStruct((ls, b, d3), F32),
        compiler_params=_cparams("parallel", "parallel"),
        name="adaln_mod",
    )(c, ada_w.reshape(ls, d, d3), ada_b.reshape(ls, 1, d3))
    return out.reshape(depth, two, b, d3)


def _modulated_norm(x, gain, shift, scale):
    ms = jnp.mean(x * x, axis=-1, keepdims=True)
    y = x * lax.rsqrt(ms + NORM_EPS) * gain
    return y * (1.0 + scale) + shift


def _norm_matmul_kernel(x_ref, g_ref, sh_ref, sc_ref, w_ref, o_ref):
    h = _modulated_norm(x_ref[0], g_ref[...], sh_ref[0], sc_ref[0])
    o_ref[0] = _dot(h.astype(BF16), w_ref[...])


def norm_matmul(x, gain, shift, scale, w, tm=512):
    b, t, d = x.shape
    n = w.shape[1]
    tm = min(tm, t)
    return pl.pallas_call(
        _norm_matmul_kernel,
        grid=(b, t // tm),
        in_specs=[
            pl.BlockSpec((1, tm, d), lambda bi, i: (bi, i, 0)),
            pl.BlockSpec((1, d), lambda bi, i: (0, 0)),
            pl.BlockSpec((1, 1, d), lambda bi, i: (bi, 0, 0)),
            pl.BlockSpec((1, 1, d), lambda bi, i: (bi, 0, 0)),
            pl.BlockSpec((d, n), lambda bi, i: (0, 0)),
        ],
        out_specs=pl.BlockSpec((1, tm, n), lambda bi, i: (bi, i, 0)),
        out_shape=jax.ShapeDtypeStruct((b, t, n), F32),
        compiler_params=_cparams("parallel", "parallel"),
        name="norm_matmul",
    )(x, gain.reshape(1, d), shift.reshape(b, 1, d), scale.reshape(b, 1, d), w)


def _out_proj_kernel(o_ref, w_ref, x_ref, gt_ref, y_ref):
    y = _dot(o_ref[0], w_ref[...])
    y_ref[0] = x_ref[0] + gt_ref[0] * y


def out_proj_residual(o, w, x, gate, tm=512):
    b, t, d = x.shape
    k = o.shape[-1]
    tm = min(tm, t)
    return pl.pallas_call(
        _out_proj_kernel,
        grid=(b, t // tm),
        in_specs=[
            pl.BlockSpec((1, tm, k), lambda bi, i: (bi, i, 0)),
            pl.BlockSpec((k, d), lambda bi, i: (0, 0)),
            pl.BlockSpec((1, tm, d), lambda bi, i: (bi, i, 0)),
            pl.BlockSpec((1, 1, d), lambda bi, i: (bi, 0, 0)),
        ],
        out_specs=pl.BlockSpec((1, tm, d), lambda bi, i: (bi, i, 0)),
        out_shape=jax.ShapeDtypeStruct((b, t, d), F32),
        compiler_params=_cparams("parallel", "parallel"),
        name="out_proj_residual",
    )(o, w, x, gate.reshape(b, 1, d))


def _pair_rms(x, gain2):
    lane = lax.broadcasted_iota(jnp.int32, x.shape, 1)
    lo = lane < NSA_HEAD_DIM
    x2 = x * x
    s_lo = jnp.sum(jnp.where(lo, x2, 0.0), axis=-1, keepdims=True)
    s_hi = jnp.sum(jnp.where(lo, 0.0, x2), axis=-1, keepdims=True)
    inv = jnp.where(lo, lax.rsqrt(s_lo / NSA_HEAD_DIM + NORM_EPS),
                    lax.rsqrt(s_hi / NSA_HEAD_DIM + NORM_EPS))
    return x * inv * gain2


def _nsa_prep_kernel(kc0_ref, kc1_ref, vc0_ref, vc1_ref, ks_ref, vs_ref, kw_ref, vw_ref,
                     kg_ref, pe_ref, w1_ref, w1big_ref, w2big_ref, ex_ref,
                     kco_ref, vct_ref, kso_ref, vst_ref, kwo_ref, vwt_ref, k2m_ref):
    t = ks_ref.shape[1]
    nblk = t // CMP_STRIDE
    dh = NSA_HEAD_DIM
    G = NSA_KV_GROUPS

    lane = lax.broadcasted_iota(jnp.int32, (t, LANES), 1)
    ones_cols = jnp.ones((t, SHIFT_ROWS), BF16)
    for br, (src, dst, gi, off) in enumerate(((ks_ref, kso_ref, 1, LANES), (kw_ref, kwo_ref, 2, 0))):
        g2 = kg_ref[gi:gi + 1, :]
        for gp in range(G // 2):
            kn = _pair_rms(src[0, :, gp * LANES:(gp + 1) * LANES], g2).astype(BF16)
            dst[0, 2 * gp, :, off:off + dh] = kn[:, :dh]
            dst[0, 2 * gp + 1, :, off:off + dh] = kn[:, dh:]
            k2 = kn.astype(F32) ** 2
            for half, keep in enumerate((lane < dh, lane >= dh)):
                n2 = jnp.sum(jnp.where(keep, k2, 0.0), axis=-1, keepdims=True)
                row = 2 * (2 * gp + half) + br
                k2m_ref[0, row:row + 1, :] = jnp.broadcast_to(
                    jnp.max(n2, axis=0, keepdims=True), (1, LANES))
        for g in range(G):
            dst[0, g, :, off + dh:off + dh + SHIFT_ROWS] = ones_cols
    for g in range(G):
        kso_ref[0, g, :, :LANES] = ex_ref[...]
    for src, dst in ((vs_ref, vst_ref), (vw_ref, vwt_ref)):
        for gp in range(G // 2):
            vt = src[0, :, gp * LANES:(gp + 1) * LANES].T.astype(BF16)
            dst[0, 2 * gp] = vt[:dh]
            dst[0, 2 * gp + 1] = vt[dh:]

    for ci, (srcs, is_key) in enumerate((((kc0_ref, kc1_ref), True), ((vc0_ref, vc1_ref), False))):
        pe1 = _dot(pe_ref[ci].astype(BF16), w1_ref[ci])[0:1]
        pe2 = jnp.concatenate([pe1, pe1], axis=1)
        for gp, src in enumerate(srcs):
            parts = [src[0, pl.ds(l, nblk, stride=CMP_STRIDE), :] for l in range(CMP_STRIDE)]
            r = jnp.concatenate(parts, axis=1).astype(BF16)
            ab = _dot(r, w1big_ref[ci])
            second = ab[:, LANES:]
            shifted = jnp.concatenate([second[1:], jnp.zeros((1, LANES), F32)], axis=0)
            pre = ab[:, :LANES] + shifted + pe2
            hid = _dot(_silu(pre).astype(BF16), w2big_ref[ci])
            if is_key:
                kn = _pair_rms(hid, kg_ref[0:1, :]).astype(BF16)
                kco_ref[0, 2 * gp] = kn[:, :dh]
                kco_ref[0, 2 * gp + 1] = kn[:, dh:]
            else:
                vt = hid.T.astype(BF16)
                vct_ref[0, 2 * gp] = vt[:dh]
                vct_ref[0, 2 * gp + 1] = vt[dh:]


def nsa_prep(proj, k_gain, cmp_pe, cmp_w1, cmp_w2):
    b, t, _ = proj.shape
    G, dh = NSA_KV_GROUPS, NSA_HEAD_DIM
    kvw = G * dh
    q_blocks = (NSA_HEADS * dh) // kvw
    nblk = t // CMP_STRIDE
    kg2 = jnp.concatenate([k_gain, k_gain], axis=1)
    pe_flat = jnp.broadcast_to(cmp_pe.reshape(2, 1, CMP_BLOCK * dh), (2, 8, CMP_BLOCK * dh))

    w1r = cmp_w1.reshape(2, 2, CMP_STRIDE, dh, dh)
    eye2 = jnp.eye(2, dtype=cmp_w1.dtype)
    w1big = jnp.einsum('chlde,gk->clgdhke', w1r, eye2).reshape(2, CMP_STRIDE * 2 * dh, 4 * dh)
    w2big = jnp.einsum('cde,gk->cgdke', cmp_w2, eye2).reshape(2, 2 * dh, 2 * dh)
    expand = jnp.asarray((np.arange(t)[:, None] // SEL_BLOCK) == np.arange(LANES)[None, :], BF16)

    def col(i):
        return pl.BlockSpec((1, t, kvw), lambda bi, i=i: (bi, 0, q_blocks + i))

    def col_pair(i, gp):
        return pl.BlockSpec((1, t, LANES), lambda bi: (bi, 0, (q_blocks + i) * (kvw // LANES) + gp))

    def full(shape):
        return pl.BlockSpec(shape, lambda bi: (0,) * len(shape))

    def per_b(shape):
        return pl.BlockSpec((1,) + shape, lambda bi: (bi,) + (0,) * len(shape))

    return pl.pallas_call(
        _nsa_prep_kernel,
        grid=(b,),
        in_specs=[col_pair(0, 0), col_pair(0, 1), col_pair(1, 0), col_pair(1, 1),
                  col(2), col(3), col(4), col(5),
                  full((3, 2 * dh)), full((2, 8, CMP_BLOCK * dh)),
                  full((2, CMP_BLOCK * dh, dh)), full((2, 2 * CMP_STRIDE * dh, 4 * dh)),
                  full((2, 2 * dh, 2 * dh)), full((t, LANES))],
        out_specs=[per_b((G, nblk, dh)), per_b((G, dh, nblk)),
                   per_b((G, t, LANES + dh + SHIFT_ROWS)), per_b((G, dh, t)),
                   per_b((G, t, dh + SHIFT_ROWS)), per_b((G, dh, t)),
                   per_b((2 * G, LANES))],
        out_shape=[jax.ShapeDtypeStruct((b, G, nblk, dh), BF16),
                   jax.ShapeDtypeStruct((b, G, dh, nblk), BF16),
                   jax.ShapeDtypeStruct((b, G, t, LANES + dh + SHIFT_ROWS), BF16),
                   jax.ShapeDtypeStruct((b, G, dh, t), BF16),
                   jax.ShapeDtypeStruct((b, G, t, dh + SHIFT_ROWS), BF16),
                   jax.ShapeDtypeStruct((b, G, dh, t), BF16),
                   jax.ShapeDtypeStruct((b, 2 * G, LANES), F32)],
        compiler_params=_cparams("parallel"),
        name="nsa_prep",
    )(proj, proj, proj, proj, proj, proj, proj, proj, kg2, pe_flat,
      cmp_w1.astype(BF16), w1big.astype(BF16), w2big.astype(BF16), expand)


def _nsa_attn_kernel(k2m_ref, hb_ref, q_ref, gl_ref, pgt_ref, qg_ref, kc_ref, vct_ref,
                     ksa_ref, vst_ref, kw_ref, vwt_ref, bc_ref, bs_ref, bw_ref, c2st_ref,
                     o_ref, acc_sc, qsel_sc, qwin_sc, pc_sc, ssa_sc, ssb_sc, swa_sc, swb_sc,
                     score_sc, *,
                     e_sat, n_wtiles,
                     n_sel_blocks, n_select):
    tq, tk = ATT_TQ, ATT_TK
    R, dh = NSA_GROUP_SIZE, NSA_HEAD_DIM
    g_id, b_id, qi = pl.program_id(0), pl.program_id(1), pl.program_id(2)
    scale = dh ** -0.5 * LOG2E
    qoff = LANES

    gl = gl_ref[0]
    g1 = gl.astype(BF16)
    gr = gl - g1.astype(F32)
    g2 = gr.astype(BF16)
    g3 = (gr - g2.astype(F32)).astype(BF16)
    gates = _sigmoid(_dot_nt(pgt_ref[0], g1) + _dot_nt(pgt_ref[0], g2) + _dot_nt(pgt_ref[0], g3))

    qt = q_ref[0].T
    q2 = []
    for r in range(R):
        qr = qt[r * dh:(r + 1) * dh]
        ms = jnp.mean(qr * qr, axis=0, keepdims=True)
        qn = (qr * lax.rsqrt(ms + NORM_EPS) * qg_ref[...] * scale).astype(BF16)
        qwin_sc[:dh, r * tq:(r + 1) * tq] = qn
        qsel_sc[qoff:qoff + dh, r * tq:(r + 1) * tq] = qn
        q2.append(jnp.sum(qn.astype(F32) ** 2, axis=0, keepdims=True))

    k2_sel = k2m_ref[b_id, 2 * g_id]
    k2_win = k2m_ref[b_id, 2 * g_id + 1]
    q2_max = jnp.max(functools.reduce(jnp.maximum, q2))
    thr = hb_ref[NSA_HEADS]
    fast = (2 * BOUND_SLACK) ** 2 * q2_max * jnp.maximum(k2_sel, k2_win) <= thr * thr
    shift_pad = jnp.zeros((SHIFT_ROWS - 1, tq), F32)
    for r in range(R):
        bmax = hb_ref[g_id * R + r]
        for k2, q_sc, row0 in ((k2_sel, qsel_sc, qoff + dh), (k2_win, qwin_sc, dh)):
            bound = jnp.sqrt(q2[r] * k2) * BOUND_SLACK + bmax
            rows = jnp.concatenate([-bound, shift_pad], axis=0)
            q_sc[row0:row0 + SHIFT_ROWS, r * tq:(r + 1) * tq] = rows.astype(BF16)

    ones_rows = jnp.ones((DENOM_ROWS, tk), BF16)
    rr = tq // tk
    n_tiles = rr * qi + rr

    def make_sweep(k_ref, vt_ref, q_sc, b_ref, hi, e_last, sa_sc, sb_sc):
        def tile_col(d):
            dc = jnp.minimum(d, hi - 1)
            e = jnp.where(dc == 0, rr - 1, jnp.where(dc < rr, dc - 1, dc))
            return pl.multiple_of((n_tiles - 1 - e) * tk, tk), jnp.minimum(e, e_last)

        def logits(d, dst_sc):
            col, _ = tile_col(d)
            dst_sc[...] = _dot(k_ref[0, 0, pl.ds(col, tk), :], q_sc[...])

        def softmax_pv(d, src_sc, ms):
            col, bi = tile_col(d)
            vaug = jnp.concatenate([vt_ref[0, 0, :, pl.ds(col, tk)], ones_rows], axis=0)
            new_ms = []
            for r in range(R):
                s = src_sc[:, r * tq:(r + 1) * tq] + b_ref[r, bi]
                m_new = jnp.maximum(ms[r], jnp.max(s, axis=0, keepdims=True))
                alpha = jnp.exp2(ms[r] - m_new)
                p = jnp.exp2(s - m_new).astype(BF16)
                acc_sc[r] = acc_sc[r] * alpha + _dot(vaug, p)
                new_ms.append(m_new)
            return tuple(new_ms)

        def shifted_pv(d, src_sc):
            col, bi = tile_col(d)
            vaug = jnp.concatenate([vt_ref[0, 0, :, pl.ds(col, tk)], ones_rows], axis=0)
            for r in range(R):
                p = jnp.exp2(src_sc[:, r * tq:(r + 1) * tq] + b_ref[r, bi]).astype(BF16)
                acc_sc[r] += _dot(vaug, p)

        def body(dp, ms):
            d = 2 * dp
            logits(d + 1, sb_sc)
            ms = softmax_pv(d, sa_sc, ms)
            logits(d + 2, sa_sc)
            return softmax_pv(d + 1, sb_sc, ms)

        def shifted_body(dp, carry):
            d = 2 * dp
            logits(d + 1, sb_sc)
            shifted_pv(d, sa_sc)
            logits(d + 2, sa_sc)
            shifted_pv(d + 1, sb_sc)
            return carry

        def start():
            logits(0, sa_sc)

        def run():
            acc_sc[...] = jnp.zeros(acc_sc.shape, F32)

            @pl.when(fast)
            def _():
                lax.fori_loop(0, hi // 2, shifted_body, 0)

                @pl.when(hi % 2 == 1)
                def _():
                    shifted_pv(hi - 1, sa_sc)

            @pl.when(jnp.logical_not(fast))
            def _():
                ms = lax.fori_loop(0, hi // 2, body,
                                   tuple(jnp.full((1, tq), NEG, F32) for _ in range(R)))

                @pl.when(hi % 2 == 1)
                def _():
                    softmax_pv(hi - 1, sa_sc, ms)

            outs = []
            for r in range(R):
                acc = acc_sc[r]
                outs.append(acc[:dh] * (1.0 / acc[dh:dh + 1]))
            return outs

        return start, run

    win_start, win_run = make_sweep(kw_ref, vwt_ref, qwin_sc, bw_ref,
                                    jnp.minimum(n_tiles, n_wtiles), n_wtiles - 1, swa_sc, swb_sc)
    win_start()

    ncp = kc_ref.shape[2]
    n_idx = lax.broadcasted_iota(jnp.int32, (ncp, tq), 0)
    t_idx = qi * tq + lax.broadcasted_iota(jnp.int32, (ncp, tq), 1)
    vis = (n_idx * CMP_STRIDE + (CMP_BLOCK - 1)) <= t_idx
    psum = jnp.zeros((ncp, tq), F32)
    sc_all = _dot(kc_ref[0, 0], qwin_sc[:dh])
    for r in range(R):
        s = jnp.where(vis, sc_all[:, r * tq:(r + 1) * tq] + bc_ref[r], NEG)
        m = jnp.max(s, axis=0, keepdims=True)
        e = jnp.where(vis, jnp.exp2(s - m), 0.0)
        p = e * (1.0 / jnp.maximum(jnp.sum(e, axis=0, keepdims=True), 1e-30))
        psum = psum + p
        pc_sc[:, r * tq:(r + 1) * tq] = p.astype(BF16)
    oc_all = _dot(vct_ref[0, 0], pc_sc[...])
    o_cmp = [oc_all[:, r * tq:(r + 1) * tq] for r in range(R)]

    imp_t = _dot_lhs_exact(c2st_ref[...], psum)
    blk = lax.broadcasted_iota(jnp.int32, (n_sel_blocks, tq), 0)
    tpos = qi * tq + lax.broadcasted_iota(jnp.int32, (n_sel_blocks, tq), 1)
    cur = tpos // SEL_BLOCK
    forced = (blk == 0) | (blk == cur) | (blk == cur - 1)
    score = jnp.where(forced, FORCED_SCORE, jnp.where(blk <= cur, imp_t, -1.0))
    score_sc[...] = score
    per_tile = tq // SEL_BLOCK

    def rank_body(g, rank):
        for u in range(per_tile):
            s2 = g * per_tile + u
            row = score_sc[pl.ds(s2, 1), :]
            beats = (row > score) | ((row == score) & (blk > s2))
            rank = rank + jnp.where(beats, 1.0, 0.0)
        return rank

    rank = lax.fori_loop(0, qi + 1, rank_body, jnp.zeros((n_sel_blocks, tq), F32))
    negsel = jnp.where(rank < n_select, 0.0, NEG)
    if n_sel_blocks < LANES:
        negsel = jnp.concatenate([negsel, jnp.zeros((LANES - n_sel_blocks, tq), F32)], axis=0)
    negsel = negsel.astype(BF16)
    for r in range(R):
        qsel_sc[:qoff, r * tq:(r + 1) * tq] = negsel

    sel_start, sel_run = make_sweep(ksa_ref, vst_ref, qsel_sc, bs_ref, n_tiles, e_sat,
                                    ssa_sc, ssb_sc)
    sel_start()
    o_win = win_run()
    o_sel = sel_run()

    outs = []
    for r in range(R):
        outs.append(gates[3 * r:3 * r + 1] * o_cmp[r]
                    + gates[3 * r + 1:3 * r + 2] * o_sel[r]
                    + gates[3 * r + 2:3 * r + 3] * o_win[r])
    o_ref[0] = jnp.concatenate(outs, axis=0).T.astype(o_ref.dtype)


def _t5_bucket(dist):
    n = jnp.maximum(dist, 0)
    max_exact = REL_BUCKETS // 2
    nf = jnp.maximum(n, 1).astype(F32)
    large = max_exact + (jnp.log(nf / max_exact) / math.log(REL_MAX_DISTANCE / max_exact)
                         * (REL_BUCKETS - max_exact)).astype(jnp.int32)
    large = jnp.minimum(large, REL_BUCKETS - 1)
    return jnp.where(n < max_exact, n, large)


def _bias_of_dist(dist, rel_bias):
    onehot = (_t5_bucket(dist)[..., None] == jnp.arange(REL_BUCKETS)).astype(F32)
    out = jnp.einsum('...k,kh->...h', onehot, rel_bias.astype(F32), precision=HIGHEST)
    return jnp.moveaxis(out, -1, 0)


def _saturation_distance():
    max_exact = REL_BUCKETS // 2
    steps = REL_BUCKETS - max_exact
    n_sat = max_exact * (REL_MAX_DISTANCE / max_exact) ** ((steps - 1) / steps)
    return int(math.ceil(n_sat)) + 2


def nsa_attention(proj, kc, vct, ksa, vst, kw, vwt, k2m_rows, rel_bias, q_gain):
    b, t, _ = proj.shape
    G, R, dh = NSA_KV_GROUPS, NSA_GROUP_SIZE, NSA_HEAD_DIM
    H = NSA_HEADS
    tq, tk = ATT_TQ, ATT_TK
    rr = tq // tk
    ncp = kc.shape[2]
    n_sel_blocks = t // SEL_BLOCK
    n_select = min(N_SELECT, n_sel_blocks)
    assert n_sel_blocks <= LANES and n_sel_blocks % 8 == 0 and tq % tk == 0 and t % tq == 0

    e_sat = -(-(_saturation_distance() + tk - 1) // tk) + rr - 1
    n_wtiles = -(-(WINDOW + tk - 1) // tk) + rr - 1
    jj = np.arange(tk)[:, None]
    ii = np.arange(tq)[None, :]

    def tile_dist(n_e):
        return (np.arange(n_e)[:, None, None] - (rr - 1)) * tk + (ii - jj)[None]

    dist = tile_dist(e_sat + 1)
    bias_sel = jnp.where(dist >= 0, _bias_of_dist(jnp.asarray(dist), rel_bias) * LOG2E, NEG)
    dwin = tile_dist(n_wtiles)
    bias_win = jnp.where((dwin >= 0) & (dwin < WINDOW),
                         _bias_of_dist(jnp.asarray(dwin), rel_bias) * LOG2E, NEG)
    dc = np.arange(t)[None, :] - (np.arange(ncp)[:, None] * CMP_STRIDE + CMP_BLOCK - 1)
    bias_c = _bias_of_dist(jnp.asarray(dc), rel_bias) * LOG2E

    cs = np.arange(ncp) * CMP_STRIDE
    ss = np.arange(n_sel_blocks) * SEL_BLOCK
    shared = (np.minimum(cs[None, :] + CMP_BLOCK, ss[:, None] + SEL_BLOCK)
              - np.maximum(cs[None, :], ss[:, None]))
    c2st = jnp.asarray(np.clip(shared, 0, None) / CMP_BLOCK, BF16)
    pgt = np.zeros((G, LANES, LANES), np.float32)
    for g in range(G):
        for k in range(3 * R):
            pgt[g, k, 3 * R * g + k] = 1.0
    pgt = jnp.asarray(pgt, BF16)
    gate_blk = (H * dh + 6 * G * dh) // LANES
    qg = jnp.broadcast_to(q_gain.reshape(dh, 1), (dh, tq))

    rb2 = rel_bias.astype(F32) * LOG2E
    spread = jnp.max(jnp.max(rb2, axis=0) - jnp.min(rb2, axis=0))
    head_bias = jnp.concatenate([jnp.max(rb2, axis=0), jnp.maximum(SAFE_GAP - spread, 0.0)[None]])
    k2m = k2m_rows[:, :, 0]

    kernel = functools.partial(_nsa_attn_kernel, e_sat=e_sat, n_wtiles=n_wtiles,
                               n_sel_blocks=n_sel_blocks, n_select=n_select)
    return pl.pallas_call(
        kernel,
        grid=(G, b, t // tq),
        in_specs=[
            pl.BlockSpec(memory_space=pltpu.SMEM),
            pl.BlockSpec(memory_space=pltpu.SMEM),
            pl.BlockSpec((1, tq, R * dh), lambda g, bi, i: (bi, i, g)),
            pl.BlockSpec((1, tq, LANES), lambda g, bi, i: (bi, i, gate_blk)),
            pl.BlockSpec((1, LANES, LANES), lambda g, bi, i: (g, 0, 0)),
            pl.BlockSpec((dh, tq), lambda g, bi, i: (0, 0)),
            pl.BlockSpec((1, 1, ncp, dh), lambda g, bi, i: (bi, g, 0, 0)),
            pl.BlockSpec((1, 1, dh, ncp), lambda g, bi, i: (bi, g, 0, 0)),
            pl.BlockSpec((1, 1, t, LANES + dh + SHIFT_ROWS), lambda g, bi, i: (bi, g, 0, 0)),
            pl.BlockSpec((1, 1, dh, t), lambda g, bi, i: (bi, g, 0, 0)),
            pl.BlockSpec((1, 1, t, dh + SHIFT_ROWS), lambda g, bi, i: (bi, g, 0, 0)),
            pl.BlockSpec((1, 1, dh, t), lambda g, bi, i: (bi, g, 0, 0)),
            pl.BlockSpec((R, ncp, tq), lambda g, bi, i: (g, 0, i)),
            pl.BlockSpec((R, e_sat + 1, tk, tq), lambda g, bi, i: (g, 0, 0, 0)),
            pl.BlockSpec((R, n_wtiles, tk, tq), lambda g, bi, i: (g, 0, 0, 0)),
            pl.BlockSpec((n_sel_blocks, ncp), lambda g, bi, i: (0, 0)),
        ],
        out_specs=pl.BlockSpec((1, tq, R * dh), lambda g, bi, i: (bi, i, g)),
        out_shape=jax.ShapeDtypeStruct((b, t, H * dh), BF16),
        scratch_shapes=[pltpu.VMEM((R, dh + DENOM_ROWS, tq), F32),
                        pltpu.VMEM((LANES + dh + SHIFT_ROWS, R * tq), BF16),
                        pltpu.VMEM((dh + SHIFT_ROWS, R * tq), BF16),
                        pltpu.VMEM((ncp, R * tq), BF16),
                        pltpu.VMEM((tk, R * tq), F32), pltpu.VMEM((tk, R * tq), F32),
                        pltpu.VMEM((tk, R * tq), F32), pltpu.VMEM((tk, R * tq), F32),
                        pltpu.VMEM((n_sel_blocks, tq), F32)],
        compiler_params=_cparams("parallel", "parallel", "arbitrary"),
        name="nsa_attention",
    )(k2m, head_bias, proj, proj, pgt, qg, kc, vct, ksa, vst, kw, vwt, bias_c, bias_sel, bias_win,
      c2st)


def nsa_layer(x, shift, scale, gate, norm_g, rel_bias, w_in, q_gain, k_gain,
              cmp_pe, cmp_w1, cmp_w2, w_out):
    d = x.shape[-1]
    n_in = w_in.shape[1]
    n_pad = -(-n_in // LANES) * LANES
    w_in_p = jnp.pad(w_in.astype(BF16), ((0, 0), (0, n_pad - n_in)))
    proj = norm_matmul(x, norm_g, shift, scale, w_in_p)
    kc, vct, ksa, vst, kw, vwt, k2m_rows = nsa_prep(proj, k_gain, cmp_pe, cmp_w1, cmp_w2)
    o = nsa_attention(proj, kc, vct, ksa, vst, kw, vwt, k2m_rows, rel_bias, q_gain)
    return out_proj_residual(o, w_out.astype(BF16), x, gate)


def _hgrn_kernel(q_ref, f_ref, v_ref, g_ref, lb_ref, og_ref, tri_ref, ones_ref,
                 o_ref, st_sc, k_sc, c_sc):
    C, dk, S = HG_CHUNK, HGRN_DK, HG_SUB
    n_sub = C // S

    @pl.when(pl.program_id(2) == 0)
    def _():
        st_sc[...] = jnp.zeros(st_sc.shape, F32)

    row = lax.broadcasted_iota(jnp.int32, (C, C), 0)
    colm = lax.broadcasted_iota(jnp.int32, (C, C), 1)
    diag_keep = ((row // S) == (colm // S)) & ((colm % S) <= (row % S))

    def rows_bcast(ref, hh, first, period):
        return jnp.concatenate(
            [jnp.broadcast_to(ref[hh, pl.ds(g * period + first, 1), :], (period, dk))
             for g in range(C // period)], axis=0)

    heads = range(HG_HEADS)
    sls = [slice(hh * dk, (hh + 1) * dk) for hh in heads]
    qs, ks, cums, vs = [], [], [], []
    for hh in heads:
        fl2 = f_ref[0, :, sls[hh]] * LOG2E
        lb = lb_ref[0, :, sls[hh]]
        log_sig = jnp.minimum(fl2, 0.0) - jnp.log2(1.0 + jnp.exp2(-jnp.abs(fl2)))
        ta = jnp.log2(lb)
        tb = jnp.log2(1.0 - lb) + log_sig
        lf = jnp.maximum(ta, tb) + jnp.log2(1.0 + jnp.exp2(-jnp.abs(ta - tb)))
        k = 1.0 - jnp.exp2(lf)
        cum = _dot_lhs_exact(tri_ref[...], lf)
        k_sc[hh] = k
        c_sc[hh] = cum
        ks.append(k)
        cums.append(cum)
        qs.append(_silu(q_ref[0, :, sls[hh]]))
        vs.append(v_ref[0, :, sls[hh]].astype(BF16))

    attns = []
    for hh in heads:
        pieces = []
        for j in range(S):
            kj = rows_bcast(k_sc, hh, j, S)
            cj = rows_bcast(c_sc, hh, j, S)
            pieces.append((qs[hh] * kj * jnp.exp2(jnp.minimum(cums[hh] - cj, 0.0))).astype(BF16))
        attns.append(jnp.where(diag_keep,
                               _dot(jnp.concatenate(pieces, axis=1), ones_ref[...]), 0.0))

    rloc = lax.broadcasted_iota(jnp.int32, (C, dk), 0)
    m = S
    while m < C:
        upper = (rloc // m) % 2 == 1
        same = (row // (2 * m)) == (colm // (2 * m))
        for hh in heads:
            e = jnp.exp2(-jnp.abs(cums[hh] - rows_bcast(c_sc, hh, m - 1, 2 * m)))
            qm = jnp.where(upper, qs[hh] * e, 0.0).astype(BF16)
            km = jnp.where(upper, 0.0, ks[hh] * e).astype(BF16)
            attns[hh] = attns[hh] + jnp.where(same, _dot_nt(qm, km), 0.0)
        m *= 2

    for hh in heads:
        st = st_sc[hh]
        cum, k, q, v = cums[hh], ks[hh], qs[hh], vs[hh]
        o = _dot(attns[hh].astype(BF16), v)
        o = o + _dot_nt((q * jnp.exp2(cum)).astype(BF16), st.astype(BF16))
        total = cum[C - 1:C, :]
        kd = (k * jnp.exp2(total - cum)).astype(BF16)
        st_sc[hh] = st * jnp.exp2(total) + lax.dot_general(
            v, kd, (((0,), (0,)), ((), ())), preferred_element_type=F32)
        ms = jnp.mean(o * o, axis=-1, keepdims=True)
        o = o * lax.rsqrt(ms + NORM_EPS) * og_ref[...]
        o_ref[0, :, sls[hh]] = (o * _silu(g_ref[0, :, sls[hh]])).astype(o_ref.dtype)


def hgrn_recurrence(proj, lb, out_gain):
    b, t, four_d = proj.shape
    d = four_d // 4
    dk = HGRN_DK
    C, S, hps = HG_CHUNK, HG_SUB, HG_HEADS
    w = hps * dk
    nhp = d // w
    tri = jnp.asarray(np.tril(np.ones((C, C), np.float32)), BF16)
    ones = jnp.asarray(np.arange(S * dk)[:, None] // dk == (np.arange(C)[None, :] % S), BF16)
    return pl.pallas_call(
        _hgrn_kernel,
        grid=(b, nhp, t // C),
        in_specs=[
            pl.BlockSpec((1, C, w), lambda bi, h, c: (bi, c, h)),
            pl.BlockSpec((1, C, w), lambda bi, h, c: (bi, c, nhp + h)),
            pl.BlockSpec((1, C, w), lambda bi, h, c: (bi, c, 2 * nhp + h)),
            pl.BlockSpec((1, C, w), lambda bi, h, c: (bi, c, 3 * nhp + h)),
            pl.BlockSpec((1, 1, w), lambda bi, h, c: (h, 0, 0)),
            pl.BlockSpec((1, dk), lambda bi, h, c: (0, 0)),
            pl.BlockSpec((C, C), lambda bi, h, c: (0, 0)),
            pl.BlockSpec((S * dk, C), lambda bi, h, c: (0, 0)),
        ],
        out_specs=pl.BlockSpec((1, C, w), lambda bi, h, c: (bi, c, h)),
        out_shape=jax.ShapeDtypeStruct((b, t, d), BF16),
        scratch_shapes=[pltpu.VMEM((hps, dk, dk), F32), pltpu.VMEM((hps, C, dk), F32),
                        pltpu.VMEM((hps, C, dk), F32)],
        compiler_params=_cparams("parallel", "parallel", "arbitrary"),
        name="hgrn_recurrence",
    )(proj, proj, proj, proj, lb.reshape(nhp, 1, w), out_gain.reshape(1, dk), tri, ones)


def hgrn_layer(x, shift, scale, gate, norm_g, lb, w_in, out_gain, w_out):
    proj = norm_matmul(x, norm_g, shift, scale, w_in.astype(BF16))
    o = hgrn_recurrence(proj, lb, out_gain)
    return out_proj_residual(o, w_out.astype(BF16), x, gate)


def _router_kernel(x_ref, g_ref, sh_ref, sc_ref, wr_ref, u_ref,
                   h_ref, eid_ref, ew_ref, pos_ref, cnt_ref, run_sc):
    @pl.when((pl.program_id(0) == 0) & (pl.program_id(1) == 0))
    def _():
        run_sc[...] = jnp.zeros(run_sc.shape, F32)

    h = _modulated_norm(x_ref[0], g_ref[...], sh_ref[0], sc_ref[0])
    h_ref[0] = h.astype(BF16)
    lt = _dot_bf16x3(h, wr_ref[...]).T
    NG, EPG = MOE_GROUPS, MOE_EPG

    def softmax_rows(rows):
        mx = functools.reduce(jnp.maximum, rows)
        es = [jnp.exp(r - mx) for r in rows]
        tot = functools.reduce(lambda a, c: a + c, es)
        return [e / tot for e in es]

    def argmax_rows(rows):
        best, idx = rows[0], jnp.zeros(rows[0].shape, jnp.int32)
        for i in range(1, len(rows)):
            better = rows[i] > best
            best = jnp.where(better, rows[i], best)
            idx = jnp.where(better, i, idx)
        return best, idx

    pg = softmax_rows([lt[i:i + 1] for i in range(NG)])
    p_grp, grp = argmax_rows(pg)
    el = []
    for j in range(EPG):
        acc = lt[NG + j:NG + j + 1]
        for gi in range(1, NG):
            acc = jnp.where(grp == gi, lt[NG + gi * EPG + j:NG + gi * EPG + j + 1], acc)
        el.append(acc)
    pe = softmax_rows(el)
    p1, i1 = argmax_rows(pe)
    p2, i2 = argmax_rows([jnp.where(i1 == j, -1.0, pe[j]) for j in range(EPG)])
    den = p1 + p2
    e1 = grp * EPG + i1
    e2 = grp * EPG + i2
    eid_ref[0:1, :] = e1
    eid_ref[1:2, :] = e2
    ew_ref[0:1, :] = p_grp * p1 / den
    ew_ref[1:2, :] = p_grp * p2 / den

    tm = e1.shape[1]
    ex = lax.broadcasted_iota(jnp.int32, (MOE_EXPERTS, tm), 0)
    oh1 = jnp.where(ex == e1, 1.0, 0.0)
    oh2 = jnp.where(ex == e2, 1.0, 0.0)
    before1 = _dot(oh1.astype(BF16), u_ref[...])
    before2 = _dot(oh2.astype(BF16), u_ref[...])
    tot1 = jnp.sum(oh1, axis=1, keepdims=True)
    tot2 = jnp.sum(oh2, axis=1, keepdims=True)
    run = run_sc[...]
    pos1 = jnp.sum(oh1 * (before1 + run), axis=0, keepdims=True)
    pos2 = jnp.sum(oh2 * (before2 + (run + tot1)), axis=0, keepdims=True)
    pos_ref[0:1, :] = pos1.astype(jnp.int32)
    pos_ref[1:2, :] = pos2.astype(jnp.int32)
    run = run + tot1 + tot2
    run_sc[...] = run
    cnt_ref[...] = jnp.broadcast_to(run, cnt_ref.shape)


def moe_router(x, gain, shift, scale, w_group, w_expert, tm=512):
    b, t, d = x.shape
    tm = min(tm, t)
    nt = t // tm
    wr = jnp.concatenate([w_group, w_expert], axis=1)
    wr = jnp.pad(wr, ((0, 0), (0, LANES - wr.shape[1])))
    upper = jnp.asarray(np.triu(np.ones((tm, tm), np.float32), 1), BF16)
    return pl.pallas_call(
        _router_kernel,
        grid=(b, nt),
        in_specs=[
            pl.BlockSpec((1, tm, d), lambda bi, i: (bi, i, 0)),
            pl.BlockSpec((1, d), lambda bi, i: (0, 0)),
            pl.BlockSpec((1, 1, d), lambda bi, i: (bi, 0, 0)),
            pl.BlockSpec((1, 1, d), lambda bi, i: (bi, 0, 0)),
            pl.BlockSpec((d, LANES), lambda bi, i: (0, 0)),
            pl.BlockSpec((tm, tm), lambda bi, i: (0, 0)),
        ],
        out_specs=[
            pl.BlockSpec((1, tm, d), lambda bi, i: (bi, i, 0)),
            pl.BlockSpec((MOE_TOP_K, tm), lambda bi, i: (0, bi * nt + i)),
            pl.BlockSpec((MOE_TOP_K, tm), lambda bi, i: (0, bi * nt + i)),
            pl.BlockSpec((MOE_TOP_K, tm), lambda bi, i: (0, bi * nt + i)),
            pl.BlockSpec((MOE_EXPERTS, LANES), lambda bi, i: (0, 0)),
        ],
        out_shape=[
            jax.ShapeDtypeStruct((b, t, d), BF16),
            jax.ShapeDtypeStruct((MOE_TOP_K, b * t), jnp.int32),
            jax.ShapeDtypeStruct((MOE_TOP_K, b * t), F32),
            jax.ShapeDtypeStruct((MOE_TOP_K, b * t), jnp.int32),
            jax.ShapeDtypeStruct((MOE_EXPERTS, LANES), F32),
        ],
        scratch_shapes=[pltpu.VMEM((MOE_EXPERTS, 1), F32)],
        compiler_params=_cparams("arbitrary", "arbitrary"),
        name="moe_router",
    )(x, gain.reshape(1, d), shift.reshape(b, 1, d), scale.reshape(b, 1, d), wr, upper)


def _expert_ffn_kernel(be_ref, nu_ref, xb_ref, w1_ref, w3_ref, w2_ref, yb_ref,
                       w1_sc, w3_sc, w2_sc):
    i = pl.program_id(0)

    @pl.when((i == 0) | (be_ref[i] != be_ref[jnp.maximum(i - 1, 0)]))
    def _():
        w1_sc[...] = w1_ref[0].astype(BF16)
        w3_sc[...] = w3_ref[0].astype(BF16)
        w2_sc[...] = w2_ref[0].astype(BF16)

    @pl.when(i < nu_ref[0])
    def _():
        xb = xb_ref[...]
        a = _dot(xb, w1_sc[...])
        g = _dot(xb, w3_sc[...])
        yb_ref[...] = _dot((_silu(a) * g).astype(BF16), w2_sc[...]).astype(yb_ref.dtype)

    @pl.when(i >= nu_ref[0])
    def _():
        yb_ref[...] = jnp.zeros(yb_ref.shape, yb_ref.dtype)


def expert_ffn(xb, blk_e, n_used, w1, w3, w2):
    p, d = xb.shape
    ff = w1.shape[2]
    rb = MOE_ROW_BLOCK
    grid_spec = pltpu.PrefetchScalarGridSpec(
        num_scalar_prefetch=2,
        grid=(p // rb,),
        in_specs=[
            pl.BlockSpec((rb, d), lambda i, be, nu: (i, 0)),
            pl.BlockSpec((1, d, ff), lambda i, be, nu: (be[i], 0, 0)),
            pl.BlockSpec((1, d, ff), lambda i, be, nu: (be[i], 0, 0)),
            pl.BlockSpec((1, ff, d), lambda i, be, nu: (be[i], 0, 0)),
        ],
        out_specs=pl.BlockSpec((rb, d), lambda i, be, nu: (i, 0)),
        scratch_shapes=[pltpu.VMEM((d, ff), BF16), pltpu.VMEM((d, ff), BF16),
                        pltpu.VMEM((ff, d), BF16)],
    )
    return pl.pallas_call(
        _expert_ffn_kernel,
        grid_spec=grid_spec,
        out_shape=jax.ShapeDtypeStruct((p, d), BF16),
        compiler_params=_cparams("arbitrary"),
        name="expert_ffn",
    )(blk_e, n_used, xb, w1, w3, w2)


def _combine_kernel(x_ref, gt_ref, y1_ref, y2_ref, w_ref, o_ref):
    w = w_ref[0]
    y = w[:, 0:1] * y1_ref[0].astype(F32) + w[:, 1:2] * y2_ref[0].astype(F32)
    o_ref[0] = x_ref[0] + gt_ref[0] * y


def moe_combine(x, gate, y1, y2, w, tm=512):
    b, t, d = x.shape
    tm = min(tm, t)
    spec = pl.BlockSpec((1, tm, d), lambda bi, i: (bi, i, 0))
    return pl.pallas_call(
        _combine_kernel,
        grid=(b, t // tm),
        in_specs=[spec, pl.BlockSpec((1, 1, d), lambda bi, i: (bi, 0, 0)), spec, spec,
                  pl.BlockSpec((1, tm, MOE_TOP_K), lambda bi, i: (bi, i, 0))],
        out_specs=spec,
        out_shape=jax.ShapeDtypeStruct((b, t, d), F32),
        compiler_params=_cparams("parallel", "parallel"),
        name="moe_combine",
    )(x, gate.reshape(b, 1, d), y1, y2, w)


def moe_layer(x, shift, scale, gate, norm_g, w_group, w_expert, w1, w3, w2):
    b, t, d = x.shape
    n = b * t
    a = n * MOE_TOP_K
    rb = MOE_ROW_BLOCK
    h, eid, ew, pos, cnt = moe_router(x, norm_g, shift, scale, w_group, w_expert)
    counts = cnt[:, 0].astype(jnp.int32)
    padded = (counts + rb - 1) // rb * rb
    pad_end = jnp.cumsum(padded)
    pad_start = pad_end - padded
    is_e = eid[..., None] == jnp.arange(MOE_EXPERTS, dtype=jnp.int32)
    dest = jnp.sum(jnp.where(is_e, pad_start, 0), axis=-1) + pos
    n_blk = -(-(a + MOE_EXPERTS * (rb - 1)) // rb)
    p = n_blk * rb
    blk_start = jnp.arange(n_blk, dtype=jnp.int32) * rb
    blk_e = jnp.minimum(jnp.sum((pad_end[None, :] <= blk_start[:, None]).astype(jnp.int32), axis=1),
                        MOE_EXPERTS - 1)
    n_used = (pad_end[-1:] // rb).astype(jnp.int32)
    tok = jnp.tile(jnp.arange(n, dtype=jnp.int32), MOE_TOP_K)
    _, tok_by_slot = lax.sort_key_val(dest.reshape(a), tok)
    seg_shift = pad_start - (jnp.cumsum(counts) - counts)
    slot = jnp.arange(p, dtype=jnp.int32)
    e_of_slot = jnp.repeat(blk_e, rb)
    shift = jnp.sum(jnp.where(e_of_slot[:, None] == jnp.arange(MOE_EXPERTS, dtype=jnp.int32),
                              seg_shift, 0), axis=-1)
    idx = slot - shift
    buf_t = tok_by_slot[jnp.where(idx < a, idx, slot % a)]
    xb = h.reshape(n, d)[buf_t]
    yb = expert_ffn(xb, blk_e, n_used, w1, w3, w2)
    y1 = yb[dest[0]].reshape(b, t, d)
    y2 = yb[dest[1]].reshape(b, t, d)
    return moe_combine(x, gate, y1, y2, ew.T.reshape(b, t, MOE_TOP_K))


def kernel(x, c, ada_w, ada_b, norm_g, rel_bias, nsa_w_in, nsa_q_gain, nsa_k_gain, nsa_cmp_pe,
           nsa_cmp_w1, nsa_cmp_w2, nsa_w_out, hgrn_w_in, hgrn_lower_bounds, hgrn_out_gain,
           hgrn_w_out, moe_router_group, moe_router_expert, moe_w1, moe_w3, moe_w2):
    depth = ada_w.shape[0]
    d = x.shape[-1]
    lb_soft = jax.nn.softmax(hgrn_lower_bounds.astype(F32), axis=0)
    lb_all = jnp.cumsum(lb_soft, axis=0) - lb_soft[0]
    mod = adaln_mod(c, ada_w, ada_b)
    for layer in range(depth):
        j = layer // 2
        shift, scale, gate = (mod[layer, 0, :, i * d:(i + 1) * d] for i in range(3))
        if layer % 2 == 0:
            x = nsa_layer(x, shift, scale, gate, norm_g[layer, 0], rel_bias, nsa_w_in[j],
                          nsa_q_gain[j], nsa_k_gain[j], nsa_cmp_pe[j], nsa_cmp_w1[j],
                          nsa_cmp_w2[j], nsa_w_out[j])
        else:
            x = hgrn_layer(x, shift, scale, gate, norm_g[layer, 0], lb_all[layer],
                           hgrn_w_in[j], hgrn_out_gain[j], hgrn_w_out[j])
        shift, scale, gate = (mod[layer, 1, :, i * d:(i + 1) * d] for i in range(3))
        x = moe_layer(x, shift, scale, gate, norm_g[layer, 1], moe_router_group[layer],
                      moe_router_expert[layer], moe_w1[layer], moe_w3[layer], moe_w2[layer])
    return x
```

```python
import functools
import math

import numpy as np
import jax
import jax.numpy as jnp
from jax import lax
from jax.experimental import pallas as pl
from jax.experimental.pallas import tpu as pltpu

F32 = jnp.float32
BF16 = jnp.bfloat16
HIGHEST = lax.Precision.HIGHEST

NSA_HEADS = 16
NSA_KV_GROUPS = 4
NSA_GROUP_SIZE = NSA_HEADS // NSA_KV_GROUPS
NSA_HEAD_DIM = 64
CMP_BLOCK = 32
CMP_STRIDE = 16
SEL_BLOCK = 64
N_SELECT = 8
WINDOW = 512
FORCED_SCORE = 1.0e4
REL_BUCKETS = 32
REL_MAX_DISTANCE = 1024
HGRN_DK = 128
MOE_GROUPS = 4
MOE_EPG = 4
MOE_EXPERTS = MOE_GROUPS * MOE_EPG
MOE_TOP_K = 2
MOE_ROW_BLOCK = 512
NORM_EPS = 1e-6

LANES = 128
NEG = -1.0e30
LOG2E = math.log2(math.e)
VMEM_LIMIT = 48 * 1024 * 1024

ATT_TQ = 256
ATT_TK = 256
SHIFT_ROWS = 16
SAFE_GAP = 60.0
BOUND_SLACK = 1.01
DENOM_ROWS = 16
HG_CHUNK = 128
HG_SUB = 8
HG_HEADS = 8


def _cparams(*sem):
    return pltpu.CompilerParams(dimension_semantics=sem, vmem_limit_bytes=VMEM_LIMIT)


def _dot(a, b):
    return jnp.dot(a, b, preferred_element_type=F32)


def _dot_exact(a, b):
    return jnp.dot(a, b, preferred_element_type=F32, precision=HIGHEST)


def _dot_lhs_exact(a, x):
    x1 = x.astype(BF16)
    r1 = x - x1.astype(F32)
    x2 = r1.astype(BF16)
    x3 = (r1 - x2.astype(F32)).astype(BF16)
    return _dot(a, x1) + _dot(a, x2) + _dot(a, x3)


def _dot_bf16x3(a, b):
    a1 = a.astype(BF16)
    a2 = (a - a1.astype(F32)).astype(BF16)
    b1 = b.astype(BF16)
    b2 = (b - b1.astype(F32)).astype(BF16)
    return _dot(a1, b1) + _dot(a1, b2) + _dot(a2, b1)


def _dot_nt(a, b):
    return lax.dot_general(a, b, (((1,), (1,)), ((), ())), preferred_element_type=F32)


def _sigmoid(x):
    return 0.5 * jnp.tanh(0.5 * x) + 0.5


def _silu(x):
    return x * _sigmoid(x)


def _adaln_kernel(c_ref, w_ref, b_ref, o_ref):
    cond = _silu(c_ref[...])
    o_ref[0] = _dot_exact(cond, w_ref[0]) + b_ref[0]


def adaln_mod(c, ada_w, ada_b):
    depth, two, d, d3 = ada_w.shape
    b = c.shape[0]
    ls = depth * two
    tn = 1024
    out = pl.pallas_call(
        _adaln_kernel,
        grid=(ls, d3 // tn),
        in_specs=[
            pl.BlockSpec((b, d), lambda i, j: (0, 0)),
            pl.BlockSpec((1, d, tn), lambda i, j: (i, 0, j)),
            pl.BlockSpec((1, 1, tn), lambda i, j: (i, 0, j)),
        ],
        out_specs=pl.BlockSpec((1, b, tn), lambda i, j: (i, 0, j)),
        out_shape=jax.ShapeDtypeStruct((ls, b, d3), F32),
        compiler_params=_cparams("parallel", "parallel"),
        name="adaln_mod",
    )(c, ada_w.reshape(ls, d, d3), ada_b.reshape(ls, 1, d3))
    return out.reshape(depth, two, b, d3)


def _modulated_norm(x, gain, shift, scale):
    ms = jnp.mean(x * x, axis=-1, keepdims=True)
    y = x * lax.rsqrt(ms + NORM_EPS) * gain
    return y * (1.0 + scale) + shift


def _norm_matmul_kernel(x_ref, g_ref, sh_ref, sc_ref, w_ref, o_ref):
    h = _modulated_norm(x_ref[0], g_ref[...], sh_ref[0], sc_ref[0])
    o_ref[0] = _dot(h.astype(BF16), w_ref[...])


def norm_matmul(x, gain, shift, scale, w, tm=1024):
    b, t, d = x.shape
    n = w.shape[1]
    tm = min(tm, t)
    need = (w.size * w.dtype.itemsize + 2 * tm * (d + n) * 4) + 4 * 1024 * 1024
    return pl.pallas_call(
        _norm_matmul_kernel,
        grid=(b, t // tm),
        in_specs=[
            pl.BlockSpec((1, tm, d), lambda bi, i: (bi, i, 0)),
            pl.BlockSpec((1, d), lambda bi, i: (0, 0)),
            pl.BlockSpec((1, 1, d), lambda bi, i: (bi, 0, 0)),
            pl.BlockSpec((1, 1, d), lambda bi, i: (bi, 0, 0)),
            pl.BlockSpec((d, n), lambda bi, i: (0, 0), pipeline_mode=pl.Buffered(1)),
        ],
        out_specs=pl.BlockSpec((1, tm, n), lambda bi, i: (bi, i, 0)),
        out_shape=jax.ShapeDtypeStruct((b, t, n), F32),
        compiler_params=pltpu.CompilerParams(dimension_semantics=("parallel", "parallel"),
                                             vmem_limit_bytes=max(need, VMEM_LIMIT)),
        name="norm_matmul",
    )(x, gain.reshape(1, d), shift.reshape(b, 1, d), scale.reshape(b, 1, d), w)


def _out_proj_kernel(o_ref, w_ref, x_ref, gt_ref, y_ref):
    y = _dot(o_ref[0], w_ref[...])
    y_ref[0] = x_ref[0] + gt_ref[0] * y


def out_proj_residual(o, w, x, gate, tm=512):
    b, t, d = x.shape
    k = o.shape[-1]
    tm = min(tm, t)
    return pl.pallas_call(
        _out_proj_kernel,
        grid=(b, t // tm),
        in_specs=[
            pl.BlockSpec((1, tm, k), lambda bi, i: (bi, i, 0)),
            pl.BlockSpec((k, d), lambda bi, i: (0, 0)),
            pl.BlockSpec((1, tm, d), lambda bi, i: (bi, i, 0)),
            pl.BlockSpec((1, 1, d), lambda bi, i: (bi, 0, 0)),
        ],
        out_specs=pl.BlockSpec((1, tm, d), lambda bi, i: (bi, i, 0)),
        out_shape=jax.ShapeDtypeStruct((b, t, d), F32),
        compiler_params=_cparams("parallel", "parallel"),
        name="out_proj_residual",
    )(o, w, x, gate.reshape(b, 1, d))


def _pair_rms(x, gain2):
    lane = lax.broadcasted_iota(jnp.int32, x.shape, 1)
    lo = lane < NSA_HEAD_DIM
    x2 = x * x
    s_lo = jnp.sum(jnp.where(lo, x2, 0.0), axis=-1, keepdims=True)
    s_hi = jnp.sum(jnp.where(lo, 0.0, x2), axis=-1, keepdims=True)
    inv = jnp.where(lo, lax.rsqrt(s_lo / NSA_HEAD_DIM + NORM_EPS),
                    lax.rsqrt(s_hi / NSA_HEAD_DIM + NORM_EPS))
    return x * inv * gain2


def _nsa_prep_kernel(kc0_ref, kc1_ref, vc0_ref, vc1_ref, ks_ref, vs_ref, kw_ref, vw_ref,
                     kg_ref, pe_ref, w1_ref, w1big_ref, w2big_ref, ex_ref,
                     kco_ref, vct_ref, kso_ref, vst_ref, kwo_ref, vwt_ref, k2m_ref):
    t = ks_ref.shape[1]
    nblk = t // CMP_STRIDE
    dh = NSA_HEAD_DIM
    G = NSA_KV_GROUPS

    lane = lax.broadcasted_iota(jnp.int32, (t, LANES), 1)
    ones_cols = jnp.ones((t, SHIFT_ROWS), BF16)
    for br, (src, dst, gi, off) in enumerate(((ks_ref, kso_ref, 1, LANES), (kw_ref, kwo_ref, 2, 0))):
        g2 = kg_ref[gi:gi + 1, :]
        for gp in range(G // 2):
            kn = _pair_rms(src[0, :, gp * LANES:(gp + 1) * LANES], g2).astype(BF16)
            dst[0, 2 * gp, :, off:off + dh] = kn[:, :dh]
            dst[0, 2 * gp + 1, :, off:off + dh] = kn[:, dh:]
            k2 = kn.astype(F32) ** 2
            for half, keep in enumerate((lane < dh, lane >= dh)):
                n2 = jnp.sum(jnp.where(keep, k2, 0.0), axis=-1, keepdims=True)
                row = 2 * (2 * gp + half) + br
                k2m_ref[0, row:row + 1, :] = jnp.broadcast_to(
                    jnp.max(n2, axis=0, keepdims=True), (1, LANES))
        for g in range(G):
            dst[0, g, :, off + dh:off + dh + SHIFT_ROWS] = ones_cols
    for g in range(G):
        kso_ref[0, g, :, :LANES] = ex_ref[...]
    for src, dst in ((vs_ref, vst_ref), (vw_ref, vwt_ref)):
        for gp in range(G // 2):
            vt = src[0, :, gp * LANES:(gp + 1) * LANES].T.astype(BF16)
            dst[0, 2 * gp] = vt[:dh]
            dst[0, 2 * gp + 1] = vt[dh:]

    for ci, (srcs, is_key) in enumerate((((kc0_ref, kc1_ref), True), ((vc0_ref, vc1_ref), False))):
        pe1 = _dot(pe_ref[ci].astype(BF16), w1_ref[ci])[0:1]
        pe2 = jnp.concatenate([pe1, pe1], axis=1)
        for gp, src in enumerate(srcs):
            parts = [src[0, pl.ds(l, nblk, stride=CMP_STRIDE), :] for l in range(CMP_STRIDE)]
            r = jnp.concatenate(parts, axis=1).astype(BF16)
            ab = _dot(r, w1big_ref[ci])
            second = ab[:, LANES:]
            shifted = jnp.concatenate([second[1:], jnp.zeros((1, LANES), F32)], axis=0)
            pre = ab[:, :LANES] + shifted + pe2
            hid = _dot(_silu(pre).astype(BF16), w2big_ref[ci])
            if is_key:
                kn = _pair_rms(hid, kg_ref[0:1, :]).astype(BF16)
                kco_ref[0, 2 * gp] = kn[:, :dh]
                kco_ref[0, 2 * gp + 1] = kn[:, dh:]
            else:
                vt = hid.T.astype(BF16)
                vct_ref[0, 2 * gp] = vt[:dh]
                vct_ref[0, 2 * gp + 1] = vt[dh:]


def nsa_prep(proj, k_gain, cmp_pe, cmp_w1, cmp_w2):
    b, t, _ = proj.shape
    G, dh = NSA_KV_GROUPS, NSA_HEAD_DIM
    kvw = G * dh
    q_blocks = (NSA_HEADS * dh) // kvw
    nblk = t // CMP_STRIDE
    kg2 = jnp.concatenate([k_gain, k_gain], axis=1)
    pe_flat = jnp.broadcast_to(cmp_pe.reshape(2, 1, CMP_BLOCK * dh), (2, 8, CMP_BLOCK * dh))

    w1r = cmp_w1.reshape(2, 2, CMP_STRIDE, dh, dh)
    eye2 = jnp.eye(2, dtype=cmp_w1.dtype)
    w1big = jnp.einsum('chlde,gk->clgdhke', w1r, eye2).reshape(2, CMP_STRIDE * 2 * dh, 4 * dh)
    w2big = jnp.einsum('cde,gk->cgdke', cmp_w2, eye2).reshape(2, 2 * dh, 2 * dh)
    expand = jnp.asarray((np.arange(t)[:, None] // SEL_BLOCK) == np.arange(LANES)[None, :], BF16)

    def col(i):
        return pl.BlockSpec((1, t, kvw), lambda bi, i=i: (bi, 0, q_blocks + i))

    def col_pair(i, gp):
        return pl.BlockSpec((1, t, LANES), lambda bi: (bi, 0, (q_blocks + i) * (kvw // LANES) + gp))

    def full(shape):
        return pl.BlockSpec(shape, lambda bi: (0,) * len(shape))

    def per_b(shape):
        return pl.BlockSpec((1,) + shape, lambda bi: (bi,) + (0,) * len(shape))

    return pl.pallas_call(
        _nsa_prep_kernel,
        grid=(b,),
        in_specs=[col_pair(0, 0), col_pair(0, 1), col_pair(1, 0), col_pair(1, 1),
                  col(2), col(3), col(4), col(5),
                  full((3, 2 * dh)), full((2, 8, CMP_BLOCK * dh)),
                  full((2, CMP_BLOCK * dh, dh)), full((2, 2 * CMP_STRIDE * dh, 4 * dh)),
                  full((2, 2 * dh, 2 * dh)), full((t, LANES))],
        out_specs=[per_b((G, nblk, dh)), per_b((G, dh, nblk)),
                   per_b((G, t, LANES + dh + SHIFT_ROWS)), per_b((G, dh, t)),
                   per_b((G, t, dh + SHIFT_ROWS)), per_b((G, dh, t)),
                   per_b((2 * G, LANES))],
        out_shape=[jax.ShapeDtypeStruct((b, G, nblk, dh), BF16),
                   jax.ShapeDtypeStruct((b, G, dh, nblk), BF16),
                   jax.ShapeDtypeStruct((b, G, t, LANES + dh + SHIFT_ROWS), BF16),
                   jax.ShapeDtypeStruct((b, G, dh, t), BF16),
                   jax.ShapeDtypeStruct((b, G, t, dh + SHIFT_ROWS), BF16),
                   jax.ShapeDtypeStruct((b, G, dh, t), BF16),
                   jax.ShapeDtypeStruct((b, 2 * G, LANES), F32)],
        compiler_params=_cparams("parallel"),
        name="nsa_prep",
    )(proj, proj, proj, proj, proj, proj, proj, proj, kg2, pe_flat,
      cmp_w1.astype(BF16), w1big.astype(BF16), w2big.astype(BF16), expand)


def _nsa_attn_kernel(k2m_ref, hb_ref, q_ref, gl_ref, pgt_ref, qg_ref, kc_ref, vct_ref,
                     ksa_ref, vst_ref, kw_ref, vwt_ref, bc_ref, bs_ref, bw_ref, c2st_ref,
                     o_ref, acc_sc, qsel_sc, qwin_sc, pc_sc, ssa_sc, ssb_sc, swa_sc, swb_sc,
                     score_sc, *,
                     e_sat, n_wtiles,
                     n_sel_blocks, n_select):
    tq, tk = ATT_TQ, ATT_TK
    R, dh = NSA_GROUP_SIZE, NSA_HEAD_DIM
    g_id, b_id, qi = pl.program_id(0), pl.program_id(1), pl.program_id(2)
    scale = dh ** -0.5 * LOG2E
    qoff = LANES

    gl = gl_ref[0]
    g1 = gl.astype(BF16)
    gr = gl - g1.astype(F32)
    g2 = gr.astype(BF16)
    g3 = (gr - g2.astype(F32)).astype(BF16)
    gates = _sigmoid(_dot_nt(pgt_ref[0], g1) + _dot_nt(pgt_ref[0], g2) + _dot_nt(pgt_ref[0], g3))

    qt = q_ref[0].T
    q2 = []
    for r in range(R):
        qr = qt[r * dh:(r + 1) * dh]
        ms = jnp.mean(qr * qr, axis=0, keepdims=True)
        qn = (qr * lax.rsqrt(ms + NORM_EPS) * qg_ref[...] * scale).astype(BF16)
        qwin_sc[:dh, r * tq:(r + 1) * tq] = qn
        qsel_sc[qoff:qoff + dh, r * tq:(r + 1) * tq] = qn
        q2.append(jnp.sum(qn.astype(F32) ** 2, axis=0, keepdims=True))

    k2_sel = k2m_ref[b_id, 2 * g_id]
    k2_win = k2m_ref[b_id, 2 * g_id + 1]
    q2_max = jnp.max(functools.reduce(jnp.maximum, q2))
    thr = hb_ref[NSA_HEADS]
    fast = (2 * BOUND_SLACK) ** 2 * q2_max * jnp.maximum(k2_sel, k2_win) <= thr * thr
    shift_pad = jnp.zeros((SHIFT_ROWS - 1, tq), F32)
    for r in range(R):
        bmax = hb_ref[g_id * R + r]
        for k2, q_sc, row0 in ((k2_sel, qsel_sc, qoff + dh), (k2_win, qwin_sc, dh)):
            bound = jnp.sqrt(q2[r] * k2) * BOUND_SLACK + bmax
            rows = jnp.concatenate([-bound, shift_pad], axis=0)
            q_sc[row0:row0 + SHIFT_ROWS, r * tq:(r + 1) * tq] = rows.astype(BF16)

    ones_rows = jnp.ones((DENOM_ROWS, tk), BF16)
    rr = tq // tk
    n_tiles = rr * qi + rr

    def make_sweep(k_ref, vt_ref, q_sc, b_ref, hi, e_last, sa_sc, sb_sc):
        def tile_col(d):
            dc = jnp.minimum(d, hi - 1)
            e = jnp.where(dc == 0, rr - 1, jnp.where(dc < rr, dc - 1, dc))
            return pl.multiple_of((n_tiles - 1 - e) * tk, tk), jnp.minimum(e, e_last)

        def logits(d, dst_sc):
            col, _ = tile_col(d)
            dst_sc[...] = _dot(k_ref[0, 0, pl.ds(col, tk), :], q_sc[...])

        def softmax_pv(d, src_sc, ms):
            col, bi = tile_col(d)
            vaug = jnp.concatenate([vt_ref[0, 0, :, pl.ds(col, tk)], ones_rows], axis=0)
            new_ms = []
            for r in range(R):
                s = src_sc[:, r * tq:(r + 1) * tq] + b_ref[r, bi]
                m_new = jnp.maximum(ms[r], jnp.max(s, axis=0, keepdims=True))
                alpha = jnp.exp2(ms[r] - m_new)
                p = jnp.exp2(s - m_new).astype(BF16)
                acc_sc[r] = acc_sc[r] * alpha + _dot(vaug, p)
                new_ms.append(m_new)
            return tuple(new_ms)

        def shifted_pv(d, src_sc):
            col, bi = tile_col(d)
            vaug = jnp.concatenate([vt_ref[0, 0, :, pl.ds(col, tk)], ones_rows], axis=0)
            for r in range(R):
                p = jnp.exp2(src_sc[:, r * tq:(r + 1) * tq] + b_ref[r, bi]).astype(BF16)
                acc_sc[r] += _dot(vaug, p)

        def body(dp, ms):
            d = 2 * dp
            logits(d + 1, sb_sc)
            ms = softmax_pv(d, sa_sc, ms)
            logits(d + 2, sa_sc)
            return softmax_pv(d + 1, sb_sc, ms)

        def shifted_body(dp, carry):
            d = 2 * dp
            logits(d + 1, sb_sc)
            shifted_pv(d, sa_sc)
            logits(d + 2, sa_sc)
            shifted_pv(d + 1, sb_sc)
            return carry

        def start():
            logits(0, sa_sc)

        def run():
            acc_sc[...] = jnp.zeros(acc_sc.shape, F32)

            @pl.when(fast)
            def _():
                lax.fori_loop(0, hi // 2, shifted_body, 0)

                @pl.when(hi % 2 == 1)
                def _():
                    shifted_pv(hi - 1, sa_sc)

            @pl.when(jnp.logical_not(fast))
            def _():
                ms = lax.fori_loop(0, hi // 2, body,
                                   tuple(jnp.full((1, tq), NEG, F32) for _ in range(R)))

                @pl.when(hi % 2 == 1)
                def _():
                    softmax_pv(hi - 1, sa_sc, ms)

            outs = []
            for r in range(R):
                acc = acc_sc[r]
                outs.append(acc[:dh] * (1.0 / acc[dh:dh + 1]))
            return outs

        return start, run

    win_start, win_run = make_sweep(kw_ref, vwt_ref, qwin_sc, bw_ref,
                                    jnp.minimum(n_tiles, n_wtiles), n_wtiles - 1, swa_sc, swb_sc)
    win_start()

    ncp = kc_ref.shape[2]
    n_idx = lax.broadcasted_iota(jnp.int32, (ncp, tq), 0)
    t_idx = qi * tq + lax.broadcasted_iota(jnp.int32, (ncp, tq), 1)
    vis = (n_idx * CMP_STRIDE + (CMP_BLOCK - 1)) <= t_idx
    psum = jnp.zeros((ncp, tq), F32)
    sc_all = _dot(kc_ref[0, 0], qwin_sc[:dh])
    for r in range(R):
        s = jnp.where(vis, sc_all[:, r * tq:(r + 1) * tq] + bc_ref[r], NEG)
        m = jnp.max(s, axis=0, keepdims=True)
        e = jnp.where(vis, jnp.exp2(s - m), 0.0)
        p = e * (1.0 / jnp.maximum(jnp.sum(e, axis=0, keepdims=True), 1e-30))
        psum = psum + p
        pc_sc[:, r * tq:(r + 1) * tq] = p.astype(BF16)
    oc_all = _dot(vct_ref[0, 0], pc_sc[...])
    o_cmp = [oc_all[:, r * tq:(r + 1) * tq] for r in range(R)]

    imp_t = _dot_lhs_exact(c2st_ref[...], psum)
    blk = lax.broadcasted_iota(jnp.int32, (n_sel_blocks, tq), 0)
    tpos = qi * tq + lax.broadcasted_iota(jnp.int32, (n_sel_blocks, tq), 1)
    cur = tpos // SEL_BLOCK
    forced = (blk == 0) | (blk == cur) | (blk == cur - 1)
    score = jnp.where(forced, FORCED_SCORE, jnp.where(blk <= cur, imp_t, -1.0))
    score_sc[...] = score
    per_tile = tq // SEL_BLOCK

    def rank_body(g, rank):
        for u in range(per_tile):
            s2 = g * per_tile + u
            row = score_sc[pl.ds(s2, 1), :]
            beats = (row > score) | ((row == score) & (blk > s2))
            rank = rank + jnp.where(beats, 1.0, 0.0)
        return rank

    rank = lax.fori_loop(0, qi + 1, rank_body, jnp.zeros((n_sel_blocks, tq), F32))
    negsel = jnp.where(rank < n_select, 0.0, NEG)
    if n_sel_blocks < LANES:
        negsel = jnp.concatenate([negsel, jnp.zeros((LANES - n_sel_blocks, tq), F32)], axis=0)
    negsel = negsel.astype(BF16)
    for r in range(R):
        qsel_sc[:qoff, r * tq:(r + 1) * tq] = negsel

    sel_start, sel_run = make_sweep(ksa_ref, vst_ref, qsel_sc, bs_ref, n_tiles, e_sat,
                                    ssa_sc, ssb_sc)
    sel_start()
    o_win = win_run()
    o_sel = sel_run()

    outs = []
    for r in range(R):
        outs.append(gates[3 * r:3 * r + 1] * o_cmp[r]
                    + gates[3 * r + 1:3 * r + 2] * o_sel[r]
                    + gates[3 * r + 2:3 * r + 3] * o_win[r])
    o_ref[0] = jnp.concatenate(outs, axis=0).T.astype(o_ref.dtype)


def _t5_bucket(dist):
    n = jnp.maximum(dist, 0)
    max_exact = REL_BUCKETS // 2
    nf = jnp.maximum(n, 1).astype(F32)
    large = max_exact + (jnp.log(nf / max_exact) / math.log(REL_MAX_DISTANCE / max_exact)
                         * (REL_BUCKETS - max_exact)).astype(jnp.int32)
    large = jnp.minimum(large, REL_BUCKETS - 1)
    return jnp.where(n < max_exact, n, large)


def _bias_of_dist(dist, rel_bias):
    onehot = (_t5_bucket(dist)[..., None] == jnp.arange(REL_BUCKETS)).astype(F32)
    out = jnp.einsum('...k,kh->...h', onehot, rel_bias.astype(F32), precision=HIGHEST)
    return jnp.moveaxis(out, -1, 0)


def _saturation_distance():
    max_exact = REL_BUCKETS // 2
    steps = REL_BUCKETS - max_exact
    n_sat = max_exact * (REL_MAX_DISTANCE / max_exact) ** ((steps - 1) / steps)
    return int(math.ceil(n_sat)) + 2


def nsa_attention(proj, kc, vct, ksa, vst, kw, vwt, k2m_rows, rel_bias, q_gain):
    b, t, _ = proj.shape
    G, R, dh = NSA_KV_GROUPS, NSA_GROUP_SIZE, NSA_HEAD_DIM
    H = NSA_HEADS
    tq, tk = ATT_TQ, ATT_TK
    rr = tq // tk
    ncp = kc.shape[2]
    n_sel_blocks = t // SEL_BLOCK
    n_select = min(N_SELECT, n_sel_blocks)
    assert n_sel_blocks <= LANES and n_sel_blocks % 8 == 0 and tq % tk == 0 and t % tq == 0

    e_sat = -(-(_saturation_distance() + tk - 1) // tk) + rr - 1
    n_wtiles = -(-(WINDOW + tk - 1) // tk) + rr - 1
    jj = np.arange(tk)[:, None]
    ii = np.arange(tq)[None, :]

    def tile_dist(n_e):
        return (np.arange(n_e)[:, None, None] - (rr - 1)) * tk + (ii - jj)[None]

    dist = tile_dist(e_sat + 1)
    bias_sel = jnp.where(dist >= 0, _bias_of_dist(jnp.asarray(dist), rel_bias) * LOG2E, NEG)
    dwin = tile_dist(n_wtiles)
    bias_win = jnp.where((dwin >= 0) & (dwin < WINDOW),
                         _bias_of_dist(jnp.asarray(dwin), rel_bias) * LOG2E, NEG)
    dc = np.arange(t)[None, :] - (np.arange(ncp)[:, None] * CMP_STRIDE + CMP_BLOCK - 1)
    bias_c = _bias_of_dist(jnp.asarray(dc), rel_bias) * LOG2E

    cs = np.arange(ncp) * CMP_STRIDE
    ss = np.arange(n_sel_blocks) * SEL_BLOCK
    shared = (np.minimum(cs[None, :] + CMP_BLOCK, ss[:, None] + SEL_BLOCK)
              - np.maximum(cs[None, :], ss[:, None]))
    c2st = jnp.asarray(np.clip(shared, 0, None) / CMP_BLOCK, BF16)
    pgt = np.zeros((G, LANES, LANES), np.float32)
    for g in range(G):
        for k in range(3 * R):
            pgt[g, k, 3 * R * g + k] = 1.0
    pgt = jnp.asarray(pgt, BF16)
    gate_blk = (H * dh + 6 * G * dh) // LANES
    qg = jnp.broadcast_to(q_gain.reshape(dh, 1), (dh, tq))

    rb2 = rel_bias.astype(F32) * LOG2E
    spread = jnp.max(jnp.max(rb2, axis=0) - jnp.min(rb2, axis=0))
    head_bias = jnp.concatenate([jnp.max(rb2, axis=0), jnp.maximum(SAFE_GAP - spread, 0.0)[None]])
    k2m = k2m_rows[:, :, 0]

    kernel = functools.partial(_nsa_attn_kernel, e_sat=e_sat, n_wtiles=n_wtiles,
                               n_sel_blocks=n_sel_blocks, n_select=n_select)
    return pl.pallas_call(
        kernel,
        grid=(G, b, t // tq),
        in_specs=[
            pl.BlockSpec(memory_space=pltpu.SMEM),
            pl.BlockSpec(memory_space=pltpu.SMEM),
            pl.BlockSpec((1, tq, R * dh), lambda g, bi, i: (bi, i, g)),
            pl.BlockSpec((1, tq, LANES), lambda g, bi, i: (bi, i, gate_blk)),
            pl.BlockSpec((1, LANES, LANES), lambda g, bi, i: (g, 0, 0)),
            pl.BlockSpec((dh, tq), lambda g, bi, i: (0, 0)),
            pl.BlockSpec((1, 1, ncp, dh), lambda g, bi, i: (bi, g, 0, 0)),
            pl.BlockSpec((1, 1, dh, ncp), lambda g, bi, i: (bi, g, 0, 0)),
            pl.BlockSpec((1, 1, t, LANES + dh + SHIFT_ROWS), lambda g, bi, i: (bi, g, 0, 0)),
            pl.BlockSpec((1, 1, dh, t), lambda g, bi, i: (bi, g, 0, 0)),
            pl.BlockSpec((1, 1, t, dh + SHIFT_ROWS), lambda g, bi, i: (bi, g, 0, 0)),
            pl.BlockSpec((1, 1, dh, t), lambda g, bi, i: (bi, g, 0, 0)),
            pl.BlockSpec((R, ncp, tq), lambda g, bi, i: (g, 0, i)),
            pl.BlockSpec((R, e_sat + 1, tk, tq), lambda g, bi, i: (g, 0, 0, 0)),
            pl.BlockSpec((R, n_wtiles, tk, tq), lambda g, bi, i: (g, 0, 0, 0)),
            pl.BlockSpec((n_sel_blocks, ncp), lambda g, bi, i: (0, 0)),
        ],
        out_specs=pl.BlockSpec((1, tq, R * dh), lambda g, bi, i: (bi, i, g)),
        out_shape=jax.ShapeDtypeStruct((b, t, H * dh), BF16),
        scratch_shapes=[pltpu.VMEM((R, dh + DENOM_ROWS, tq), F32),
                        pltpu.VMEM((LANES + dh + SHIFT_ROWS, R * tq), BF16),
                        pltpu.VMEM((dh + SHIFT_ROWS, R * tq), BF16),
                        pltpu.VMEM((ncp, R * tq), BF16),
                        pltpu.VMEM((tk, R * tq), F32), pltpu.VMEM((tk, R * tq), F32),
                        pltpu.VMEM((tk, R * tq), F32), pltpu.VMEM((tk, R * tq), F32),
                        pltpu.VMEM((n_sel_blocks, tq), F32)],
        compiler_params=_cparams("parallel", "parallel", "arbitrary"),
        name="nsa_attention",
    )(k2m, head_bias, proj, proj, pgt, qg, kc, vct, ksa, vst, kw, vwt, bias_c, bias_sel, bias_win,
      c2st)


def nsa_layer(x, shift, scale, gate, norm_g, rel_bias, w_in, q_gain, k_gain,
              cmp_pe, cmp_w1, cmp_w2, w_out):
    d = x.shape[-1]
    n_in = w_in.shape[1]
    n_pad = -(-n_in // LANES) * LANES
    w_in_p = jnp.pad(w_in.astype(BF16), ((0, 0), (0, n_pad - n_in)))
    proj = norm_matmul(x, norm_g, shift, scale, w_in_p)
    kc, vct, ksa, vst, kw, vwt, k2m_rows = nsa_prep(proj, k_gain, cmp_pe, cmp_w1, cmp_w2)
    o = nsa_attention(proj, kc, vct, ksa, vst, kw, vwt, k2m_rows, rel_bias, q_gain)
    return out_proj_residual(o, w_out.astype(BF16), x, gate)


def _hgrn_kernel(q_ref, f_ref, v_ref, g_ref, lb_ref, og_ref, tri_ref, ones_ref,
                 o_ref, st_sc, k_sc, c_sc):
    C, dk, S = HG_CHUNK, HGRN_DK, HG_SUB
    n_sub = C // S

    @pl.when(pl.program_id(2) == 0)
    def _():
        st_sc[...] = jnp.zeros(st_sc.shape, F32)

    row = lax.broadcasted_iota(jnp.int32, (C, C), 0)
    colm = lax.broadcasted_iota(jnp.int32, (C, C), 1)
    diag_keep = ((row // S) == (colm // S)) & ((colm % S) <= (row % S))

    def rows_bcast(ref, hh, first, period):
        return jnp.concatenate(
            [jnp.broadcast_to(ref[hh, pl.ds(g * period + first, 1), :], (period, dk))
             for g in range(C // period)], axis=0)

    heads = range(HG_HEADS)
    sls = [slice(hh * dk, (hh + 1) * dk) for hh in heads]
    qs, ks, cums, vs = [], [], [], []
    for hh in heads:
        fl2 = f_ref[0, :, sls[hh]] * LOG2E
        lb = lb_ref[0, :, sls[hh]]
        log_sig = jnp.minimum(fl2, 0.0) - jnp.log2(1.0 + jnp.exp2(-jnp.abs(fl2)))
        ta = jnp.log2(lb)
        tb = jnp.log2(1.0 - lb) + log_sig
        lf = jnp.maximum(ta, tb) + jnp.log2(1.0 + jnp.exp2(-jnp.abs(ta - tb)))
        k = 1.0 - jnp.exp2(lf)
        cum = _dot_lhs_exact(tri_ref[...], lf)
        k_sc[hh] = k
        c_sc[hh] = cum
        ks.append(k)
        cums.append(cum)
        qs.append(_silu(q_ref[0, :, sls[hh]]))
        vs.append(v_ref[0, :, sls[hh]].astype(BF16))

    attns = []
    for hh in heads:
        pieces = []
        for j in range(S):
            kj = rows_bcast(k_sc, hh, j, S)
            cj = rows_bcast(c_sc, hh, j, S)
            pieces.append((qs[hh] * kj * jnp.exp2(jnp.minimum(cums[hh] - cj, 0.0))).astype(BF16))
        attns.append(jnp.where(diag_keep,
                               _dot(jnp.concatenate(pieces, axis=1), ones_ref[...]), 0.0))

    rloc = lax.broadcasted_iota(jnp.int32, (C, dk), 0)
    m = S
    while m < C:
        upper = (rloc // m) % 2 == 1
        same = (row // (2 * m)) == (colm // (2 * m))
        for hh in heads:
            e = jnp.exp2(-jnp.abs(cums[hh] - rows_bcast(c_sc, hh, m - 1, 2 * m)))
            qm = jnp.where(upper, qs[hh] * e, 0.0).astype(BF16)
            km = jnp.where(upper, 0.0, ks[hh] * e).astype(BF16)
            attns[hh] = attns[hh] + jnp.where(same, _dot_nt(qm, km), 0.0)
        m *= 2

    for hh in heads:
        st = st_sc[hh]
        cum, k, q, v = cums[hh], ks[hh], qs[hh], vs[hh]
        o = _dot(attns[hh].astype(BF16), v)
        o = o + _dot_nt((q * jnp.exp2(cum)).astype(BF16), st.astype(BF16))
        total = cum[C - 1:C, :]
        kd = (k * jnp.exp2(total - cum)).astype(BF16)
        st_sc[hh] = st * jnp.exp2(total) + lax.dot_general(
            v, kd, (((0,), (0,)), ((), ())), preferred_element_type=F32)
        ms = jnp.mean(o * o, axis=-1, keepdims=True)
        o = o * lax.rsqrt(ms + NORM_EPS) * og_ref[...]
        o_ref[0, :, sls[hh]] = (o * _silu(g_ref[0, :, sls[hh]])).astype(o_ref.dtype)


def hgrn_recurrence(proj, lb, out_gain):
    b, t, four_d = proj.shape
    d = four_d // 4
    dk = HGRN_DK
    C, S, hps = HG_CHUNK, HG_SUB, HG_HEADS
    w = hps * dk
    nhp = d // w
    tri = jnp.asarray(np.tril(np.ones((C, C), np.float32)), BF16)
    ones = jnp.asarray(np.arange(S * dk)[:, None] // dk == (np.arange(C)[None, :] % S), BF16)
    return pl.pallas_call(
        _hgrn_kernel,
        grid=(b, nhp, t // C),
        in_specs=[
            pl.BlockSpec((1, C, w), lambda bi, h, c: (bi, c, h)),
            pl.BlockSpec((1, C, w), lambda bi, h, c: (bi, c, nhp + h)),
            pl.BlockSpec((1, C, w), lambda bi, h, c: (bi, c, 2 * nhp + h)),
            pl.BlockSpec((1, C, w), lambda bi, h, c: (bi, c, 3 * nhp + h)),
            pl.BlockSpec((1, 1, w), lambda bi, h, c: (h, 0, 0)),
            pl.BlockSpec((1, dk), lambda bi, h, c: (0, 0)),
            pl.BlockSpec((C, C), lambda bi, h, c: (0, 0)),
            pl.BlockSpec((S * dk, C), lambda bi, h, c: (0, 0)),
        ],
        out_specs=pl.BlockSpec((1, C, w), lambda bi, h, c: (bi, c, h)),
        out_shape=jax.ShapeDtypeStruct((b, t, d), BF16),
        scratch_shapes=[pltpu.VMEM((hps, dk, dk), F32), pltpu.VMEM((hps, C, dk), F32),
                        pltpu.VMEM((hps, C, dk), F32)],
        compiler_params=_cparams("parallel", "parallel", "arbitrary"),
        name="hgrn_recurrence",
    )(proj, proj, proj, proj, lb.reshape(nhp, 1, w), out_gain.reshape(1, dk), tri, ones)


def hgrn_layer(x, shift, scale, gate, norm_g, lb, w_in, out_gain, w_out):
    proj = norm_matmul(x, norm_g, shift, scale, w_in.astype(BF16))
    o = hgrn_recurrence(proj, lb, out_gain)
    return out_proj_residual(o, w_out.astype(BF16), x, gate)


def _router_kernel(x_ref, g_ref, sh_ref, sc_ref, wr_ref, u_ref,
                   h_ref, eid_ref, ew_ref, pos_ref, cnt_ref, run_sc):
    @pl.when((pl.program_id(0) == 0) & (pl.program_id(1) == 0))
    def _():
        run_sc[...] = jnp.zeros(run_sc.shape, F32)

    h = _modulated_norm(x_ref[0], g_ref[...], sh_ref[0], sc_ref[0])
    h_ref[0] = h.astype(BF16)
    lt = _dot_bf16x3(h, wr_ref[...]).T
    NG, EPG = MOE_GROUPS, MOE_EPG

    def softmax_rows(rows):
        mx = functools.reduce(jnp.maximum, rows)
        es = [jnp.exp(r - mx) for r in rows]
        tot = functools.reduce(lambda a, c: a + c, es)
        return [e / tot for e in es]

    def argmax_rows(rows):
        best, idx = rows[0], jnp.zeros(rows[0].shape, jnp.int32)
        for i in range(1, len(rows)):
            better = rows[i] > best
            best = jnp.where(better, rows[i], best)
            idx = jnp.where(better, i, idx)
        return best, idx

    pg = softmax_rows([lt[i:i + 1] for i in range(NG)])
    p_grp, grp = argmax_rows(pg)
    el = []
    for j in range(EPG):
        acc = lt[NG + j:NG + j + 1]
        for gi in range(1, NG):
            acc = jnp.where(grp == gi, lt[NG + gi * EPG + j:NG + gi * EPG + j + 1], acc)
        el.append(acc)
    pe = softmax_rows(el)
    p1, i1 = argmax_rows(pe)
    p2, i2 = argmax_rows([jnp.where(i1 == j, -1.0, pe[j]) for j in range(EPG)])
    den = p1 + p2
    e1 = grp * EPG + i1
    e2 = grp * EPG + i2
    eid_ref[0:1, :] = e1
    eid_ref[1:2, :] = e2
    ew_ref[0:1, :] = p_grp * p1 / den
    ew_ref[1:2, :] = p_grp * p2 / den

    tm = e1.shape[1]
    ex = lax.broadcasted_iota(jnp.int32, (MOE_EXPERTS, tm), 0)
    oh1 = jnp.where(ex == e1, 1.0, 0.0)
    oh2 = jnp.where(ex == e2, 1.0, 0.0)
    before1 = _dot(oh1.astype(BF16), u_ref[...])
    before2 = _dot(oh2.astype(BF16), u_ref[...])
    tot1 = jnp.sum(oh1, axis=1, keepdims=True)
    tot2 = jnp.sum(oh2, axis=1, keepdims=True)
    run = run_sc[...]
    pos1 = jnp.sum(oh1 * (before1 + run), axis=0, keepdims=True)
    pos2 = jnp.sum(oh2 * (before2 + (run + tot1)), axis=0, keepdims=True)
    pos_ref[0:1, :] = pos1.astype(jnp.int32)
    pos_ref[1:2, :] = pos2.astype(jnp.int32)
    run = run + tot1 + tot2
    run_sc[...] = run
    cnt_ref[...] = jnp.broadcast_to(run, cnt_ref.shape)


def moe_router(x, gain, shift, scale, w_group, w_expert, tm=512):
    b, t, d = x.shape
    tm = min(tm, t)
    nt = t // tm
    wr = jnp.concatenate([w_group, w_expert], axis=1)
    wr = jnp.pad(wr, ((0, 0), (0, LANES - wr.shape[1])))
    upper = jnp.asarray(np.triu(np.ones((tm, tm), np.float32), 1), BF16)
    return pl.pallas_call(
        _router_kernel,
        grid=(b, nt),
        in_specs=[
            pl.BlockSpec((1, tm, d), lambda bi, i: (bi, i, 0)),
            pl.BlockSpec((1, d), lambda bi, i: (0, 0)),
            pl.BlockSpec((1, 1, d), lambda bi, i: (bi, 0, 0)),
            pl.BlockSpec((1, 1, d), lambda bi, i: (bi, 0, 0)),
            pl.BlockSpec((d, LANES), lambda bi, i: (0, 0)),
            pl.BlockSpec((tm, tm), lambda bi, i: (0, 0)),
        ],
        out_specs=[
            pl.BlockSpec((1, tm, d), lambda bi, i: (bi, i, 0)),
            pl.BlockSpec((MOE_TOP_K, tm), lambda bi, i: (0, bi * nt + i)),
            pl.BlockSpec((MOE_TOP_K, tm), lambda bi, i: (0, bi * nt + i)),
            pl.BlockSpec((MOE_TOP_K, tm), lambda bi, i: (0, bi * nt + i)),
            pl.BlockSpec((MOE_EXPERTS, LANES), lambda bi, i: (0, 0)),
        ],
        out_shape=[
            jax.ShapeDtypeStruct((b, t, d), BF16),
            jax.ShapeDtypeStruct((MOE_TOP_K, b * t), jnp.int32),
            jax.ShapeDtypeStruct((MOE_TOP_K, b * t), F32),
            jax.ShapeDtypeStruct((MOE_TOP_K, b * t), jnp.int32),
            jax.ShapeDtypeStruct((MOE_EXPERTS, LANES), F32),
        ],
        scratch_shapes=[pltpu.VMEM((MOE_EXPERTS, 1), F32)],
        compiler_params=_cparams("arbitrary", "arbitrary"),
        name="moe_router",
    )(x, gain.reshape(1, d), shift.reshape(b, 1, d), scale.reshape(b, 1, d), wr, upper)


def _expert_ffn_kernel(be_ref, nu_ref, xb_ref, w1_ref, w3_ref, w2_ref, yb_ref,
                       w1_sc, w3_sc, w2_sc):
    i = pl.program_id(0)

    @pl.when((i == 0) | (be_ref[i] != be_ref[jnp.maximum(i - 1, 0)]))
    def _():
        w1_sc[...] = w1_ref[0].astype(BF16)
        w3_sc[...] = w3_ref[0].astype(BF16)
        w2_sc[...] = w2_ref[0].astype(BF16)

    @pl.when(i < nu_ref[0])
    def _():
        xb = xb_ref[...]
        a = _dot(xb, w1_sc[...])
        g = _dot(xb, w3_sc[...])
        yb_ref[...] = _dot((_silu(a) * g).astype(BF16), w2_sc[...]).astype(yb_ref.dtype)

    @pl.when(i >= nu_ref[0])
    def _():
        yb_ref[...] = jnp.zeros(yb_ref.shape, yb_ref.dtype)


def expert_ffn(xb, blk_e, n_used, w1, w3, w2):
    p, d = xb.shape
    ff = w1.shape[2]
    rb = MOE_ROW_BLOCK
    grid_spec = pltpu.PrefetchScalarGridSpec(
        num_scalar_prefetch=2,
        grid=(p // rb,),
        in_specs=[
            pl.BlockSpec((rb, d), lambda i, be, nu: (i, 0)),
            pl.BlockSpec((1, d, ff), lambda i, be, nu: (be[i], 0, 0)),
            pl.BlockSpec((1, d, ff), lambda i, be, nu: (be[i], 0, 0)),
            pl.BlockSpec((1, ff, d), lambda i, be, nu: (be[i], 0, 0)),
        ],
        out_specs=pl.BlockSpec((rb, d), lambda i, be, nu: (i, 0)),
        scratch_shapes=[pltpu.VMEM((d, ff), BF16), pltpu.VMEM((d, ff), BF16),
                        pltpu.VMEM((ff, d), BF16)],
    )
    return pl.pallas_call(
        _expert_ffn_kernel,
        grid_spec=grid_spec,
        out_shape=jax.ShapeDtypeStruct((p, d), BF16),
        compiler_params=_cparams("arbitrary"),
        name="expert_ffn",
    )(blk_e, n_used, xb, w1, w3, w2)


def _combine_kernel(x_ref, gt_ref, y1_ref, y2_ref, w_ref, o_ref):
    w = w_ref[0]
    y = w[:, 0:1] * y1_ref[0].astype(F32) + w[:, 1:2] * y2_ref[0].astype(F32)
    o_ref[0] = x_ref[0] + gt_ref[0] * y


def moe_combine(x, gate, y1, y2, w, tm=512):
    b, t, d = x.shape
    tm = min(tm, t)
    spec = pl.BlockSpec((1, tm, d), lambda bi, i: (bi, i, 0))
    return pl.pallas_call(
        _combine_kernel,
        grid=(b, t // tm),
        in_specs=[spec, pl.BlockSpec((1, 1, d), lambda bi, i: (bi, 0, 0)), spec, spec,
                  pl.BlockSpec((1, tm, MOE_TOP_K), lambda bi, i: (bi, i, 0))],
        out_specs=spec,
        out_shape=jax.ShapeDtypeStruct((b, t, d), F32),
        compiler_params=_cparams("parallel", "parallel"),
        name="moe_combine",
    )(x, gate.reshape(b, 1, d), y1, y2, w)


def moe_layer(x, shift, scale, gate, norm_g, w_group, w_expert, w1, w3, w2):
    b, t, d = x.shape
    n = b * t
    a = n * MOE_TOP_K
    rb = MOE_ROW_BLOCK
    h, eid, ew, pos, cnt = moe_router(x, norm_g, shift, scale, w_group, w_expert)
    counts = cnt[:, 0].astype(jnp.int32)
    padded = (counts + rb - 1) // rb * rb
    pad_end = jnp.cumsum(padded)
    pad_start = pad_end - padded
    is_e = eid[..., None] == jnp.arange(MOE_EXPERTS, dtype=jnp.int32)
    dest = jnp.sum(jnp.where(is_e, pad_start, 0), axis=-1) + pos
    n_blk = -(-(a + MOE_EXPERTS * (rb - 1)) // rb)
    p = n_blk * rb
    blk_start = jnp.arange(n_blk, dtype=jnp.int32) * rb
    blk_e = jnp.minimum(jnp.sum((pad_end[None, :] <= blk_start[:, None]).astype(jnp.int32), axis=1),
                        MOE_EXPERTS - 1)
    n_used = (pad_end[-1:] // rb).astype(jnp.int32)
    tok = jnp.tile(jnp.arange(n, dtype=jnp.int32), MOE_TOP_K)
    _, tok_by_slot = lax.sort_key_val(dest.reshape(a), tok)
    seg_shift = pad_start - (jnp.cumsum(counts) - counts)
    slot = jnp.arange(p, dtype=jnp.int32)
    e_of_slot = jnp.repeat(blk_e, rb)
    shift = jnp.sum(jnp.where(e_of_slot[:, None] == jnp.arange(MOE_EXPERTS, dtype=jnp.int32),
                              seg_shift, 0), axis=-1)
    idx = slot - shift
    buf_t = tok_by_slot[jnp.where(idx < a, idx, slot % a)]
    xb = h.reshape(n, d)[buf_t]
    yb = expert_ffn(xb, blk_e, n_used, w1, w3, w2)
    y1 = yb[dest[0]].reshape(b, t, d)
    y2 = yb[dest[1]].reshape(b, t, d)
    return moe_combine(x, gate, y1, y2, ew.T.reshape(b, t, MOE_TOP_K))


def kernel(x, c, ada_w, ada_b, norm_g, rel_bias, nsa_w_in, nsa_q_gain, nsa_k_gain, nsa_cmp_pe,
           nsa_cmp_w1, nsa_cmp_w2, nsa_w_out, hgrn_w_in, hgrn_lower_bounds, hgrn_out_gain,
           hgrn_w_out, moe_router_group, moe_router_expert, moe_w1, moe_w3, moe_w2):
    depth = ada_w.shape[0]
    d = x.shape[-1]
    lb_soft = jax.nn.softmax(hgrn_lower_bounds.astype(F32), axis=0)
    lb_all = jnp.cumsum(lb_soft, axis=0) - lb_soft[0]
    mod = adaln_mod(c, ada_w, ada_b)
    for layer in range(depth):
        j = layer // 2
        shift, scale, gate = (mod[layer, 0, :, i * d:(i + 1) * d] for i in range(3))
        if layer % 2 == 0:
            x = nsa_layer(x, shift, scale, gate, norm_g[layer, 0], rel_bias, nsa_w_in[j],
                          nsa_q_gain[j], nsa_k_gain[j], nsa_cmp_pe[j], nsa_cmp_w1[j],
                          nsa_cmp_w2[j], nsa_w_out[j])
        else:
            x = hgrn_layer(x, shift, scale, gate, norm_g[layer, 0], lb_all[layer],
                           hgrn_w_in[j], hgrn_out_gain[j], hgrn_w_out[j])
        shift, scale, gate = (mod[layer, 1, :, i * d:(i + 1) * d] for i in range(3))
        x = moe_layer(x, shift, scale, gate, norm_g[layer, 1], moe_router_group[layer],
                      moe_router_expert[layer], moe_w1[layer], moe_w3[layer], moe_w2[layer])
    return x
```
